```python
import math
import jax
import jax.numpy as jnp
from jax import lax
import numpy as np

D_MODEL = 1024
BATCH = 32
SEQ = 256
DEPTH = 2
DEC_BATCH = 8
DEC_SEQ = 1024
PAST_LEN = 256

GRID_W = 64
N_EVEN = (DEPTH + 1) // 2
N_ODD = DEPTH // 2
MIX_A = D_MODEL // 2
H_B = D_MODEL // 256
DH_B = 64
W_B = H_B * 2 * DH_B
H_C = D_MODEL // 128
DH_C = 64
W_C = H_C * DH_C
H_D = D_MODEL // 128
KV_D = H_D // 4
DH_D = 64
W_D = H_D * DH_D
MIX_WIDTH = MIX_A + W_B
IN_EVEN = 3 * MIX_A + 3 * W_B
IN_ODD = 3 * W_C + W_D + 2 * KV_D * DH_D
D_FF = 4 * D_MODEL
CONV_W = 3
NA_WIN_R = 8
NA_WIN_C = 16
SWA_WINDOW = 128
SWA_BLOCK = 128
Q_BLOCK = 128
DENSE_KEY_LIMIT = 2048
ROPE_BASE = 10000.0
EPS = 1e-6
NEG = -1e30

kernel_name = 'hybrid_diffusion_ctx_prefix_step'


def rmsnorm(x, g):
    xf = x.astype(jnp.float32)
    y = xf * lax.rsqrt(jnp.mean(xf * xf, axis=-1, keepdims=True) + EPS)
    return (y * g.astype(jnp.float32)).astype(x.dtype)


def to_heads(x, n):
    b, t, _ = x.shape
    return x.reshape(b, t, n, -1).transpose(0, 2, 1, 3)


def from_heads(x):
    b, h, t, d = x.shape
    return x.transpose(0, 2, 1, 3).reshape(b, t, h * d)


def axial_rope(x):
    t_len, dh = x.shape[1], x.shape[-1]
    q4 = dh // 4
    t = jnp.arange(t_len)
    rows = (t // GRID_W).astype(jnp.float32)
    cols = (t % GRID_W).astype(jnp.float32)
    inv = 1.0 / (ROPE_BASE ** (jnp.arange(q4, dtype=jnp.float32) / q4))
    ang = jnp.stack([rows[:, None] * inv, cols[:, None] * inv], axis=1)
    cos = jnp.cos(ang)[None, :, None].astype(x.dtype)
    sin = jnp.sin(ang)[None, :, None].astype(x.dtype)
    xs = x.reshape(x.shape[:-1] + (2, 2, q4))
    a, b = xs[..., 0, :], xs[..., 1, :]
    return jnp.stack([a * cos - b * sin, b * cos + a * sin], axis=-2).reshape(x.shape)


def short_conv(u, w):
    return lax.conv_general_dilated(u, w[:, None, :].astype(u.dtype), window_strides=(1,), padding=[(1, 1)], dimension_numbers=('NWC', 'WIO', 'NWC'), feature_group_count=u.shape[-1])


def sweep_queries(core, q, q_axis, out_axis, n_keys):
    if n_keys < DENSE_KEY_LIMIT:
        return core(q)
    t_q = q.shape[q_axis]
    nb = t_q // Q_BLOCK
    qb = q.reshape(q.shape[:q_axis] + (nb, Q_BLOCK) + q.shape[q_axis + 1:])
    ob = jnp.moveaxis(lax.map(core, jnp.moveaxis(qb, q_axis, 0)), 0, out_axis)
    return ob.reshape(ob.shape[:out_axis] + (t_q,) + ob.shape[out_axis + 2:])


def diff_attention(q, k, v, lam, lam_init, subln):
    scale = q.shape[-1] ** -0.5

    def core(qb):
        s = jnp.einsum('bhmqd,bhmkd->bhmqk', qb, k).astype(jnp.float32) * scale
        p = jax.nn.softmax(s, axis=-1)
        pd = (p[:, :, 0] - lam * p[:, :, 1]).astype(v.dtype)
        return rmsnorm(jnp.einsum('bhqk,bhkd->bhqd', pd, v), subln) * (1.0 - lam_init)

    return sweep_queries(core, q, 3, 2, k.shape[3])


def full_attention(q, k, v, sink):
    scale = q.shape[-1] ** -0.5

    def core(qb):
        s = jnp.einsum('bgmqd,bgkd->bgmqk', qb, k).astype(jnp.float32) * scale
        if sink is None:
            p = jax.nn.softmax(s, axis=-1)
        else:
            sk = jnp.broadcast_to(sink.astype(jnp.float32).reshape(1, s.shape[1], s.shape[2], 1, 1), s.shape[:-1] + (1,))
            p = jax.nn.softmax(jnp.concatenate([s, sk], axis=-1), axis=-1)[..., :-1]
        return jnp.einsum('bgmqk,bgkd->bgmqd', p.astype(v.dtype), v)

    return sweep_queries(core, q, 3, 3, k.shape[2])


def neighbourhood_attention(q, k, v, kc, vc, rpb):
    bn, h, t, dh = q.shape
    rows = t // GRID_W
    wr = min(NA_WIN_R, rows)
    wc = NA_WIN_C
    scale = dh ** -0.5
    qg = q.reshape(bn, h, rows, GRID_W, dh)
    r = jnp.arange(rows)
    row_idx = jnp.clip(r - wr // 2, 0, rows - wr)[:, None] + jnp.arange(wr)
    k_rows = k.reshape(bn, h, rows, GRID_W, dh)[:, :, row_idx]
    v_rows = v.reshape(bn, h, rows, GRID_W, dh)[:, :, row_idx]
    col = jnp.arange(GRID_W)
    col_start = jnp.clip(col - wc // 2, 0, GRID_W - wc)
    col_ok = (col[None, :] >= col_start[:, None]) & (col[None, :] < col_start[:, None] + wc)
    dr = row_idx - r[:, None] + NA_WIN_R - 1
    dc = jnp.clip(col[None, :] - col[:, None] + NA_WIN_C - 1, 0, 2 * NA_WIN_C - 2)
    bias = rpb[:, dr[:, None, :, None], dc[None, :, None, :]].astype(jnp.float32)
    s = jnp.einsum('bhrqd,bhrjkd->bhrqjk', qg, k_rows).astype(jnp.float32) * scale + bias[None]
    s = jnp.where(col_ok[:, None, :], s, NEG).reshape(bn, h, rows, GRID_W, wr * GRID_W)
    sc = jnp.einsum('bhrqd,bhcd->bhrqc', qg, kc).astype(jnp.float32) * scale
    p = jax.nn.softmax(jnp.concatenate([s, sc], axis=-1), axis=-1)
    n_lat = wr * GRID_W
    p_lat = p[..., :n_lat].reshape(bn, h, rows, GRID_W, wr, GRID_W).astype(v.dtype)
    o = jnp.einsum('bhrqjk,bhrjkd->bhrqd', p_lat, v_rows) + jnp.einsum('bhrqc,bhcd->bhrqd', p[..., n_lat:].astype(v.dtype), vc)
    return o.reshape(bn, h, t, dh)


def window_attention(q, k, v, kc, vc, sink):
    bn, g, m, t, dh = q.shape
    nb = t // SWA_BLOCK
    scale = dh ** -0.5
    qb = q.reshape(bn, g, m, nb, SWA_BLOCK, dh)

    def band(x):
        xb = jnp.pad(x, ((0, 0), (0, 0), (SWA_BLOCK, SWA_BLOCK), (0, 0))).reshape(bn, g, nb + 2, SWA_BLOCK, dh)
        return jnp.concatenate([xb[:, :, 0:nb], xb[:, :, 1:nb + 1], xb[:, :, 2:nb + 2]], axis=3)

    kb, vb = band(k), band(v)
    blk = jnp.arange(nb)
    qpos = blk[:, None] * SWA_BLOCK + jnp.arange(SWA_BLOCK)
    kpos = (blk[:, None] - 1) * SWA_BLOCK + jnp.arange(3 * SWA_BLOCK)
    ok = (jnp.abs(qpos[:, :, None] - kpos[:, None, :]) <= SWA_WINDOW) & (kpos[:, None, :] >= 0) & (kpos[:, None, :] < t)
    s = jnp.einsum('bgmnqd,bgnkd->bgmnqk', qb, kb).astype(jnp.float32) * scale
    s = jnp.where(ok, s, NEG)
    sc = jnp.einsum('bgmnqd,bgcd->bgmnqc', qb, kc).astype(jnp.float32) * scale
    sk = jnp.broadcast_to(sink.astype(jnp.float32).reshape(1, g, m, 1, 1, 1), s.shape[:-1] + (1,))
    p = jax.nn.softmax(jnp.concatenate([s, sc, sk], axis=-1), axis=-1)
    n_lat = 3 * SWA_BLOCK
    n_ctx = kc.shape[2]
    o = jnp.einsum('bgmnqk,bgnkd->bgmnqd', p[..., :n_lat].astype(v.dtype), vb) + jnp.einsum('bgmnqc,bgcd->bgmnqd', p[..., n_lat:n_lat + n_ctx].astype(v.dtype), vc)
    return o.reshape(bn, g, m, t, dh)


def even_mixer(h, w_in, conv_w, lam, lam_init, subln, w_out, ctx_kv):
    bn, t, _ = h.shape
    a_b, a_c, a_x, q, k, v = jnp.split(h @ w_in, [MIX_A, 2 * MIX_A, 3 * MIX_A, 3 * MIX_A + W_B, 3 * MIX_A + 2 * W_B], axis=-1)
    y_a = a_b * short_conv(a_c * a_x, conv_w)
    q = q.reshape(bn, t, 2 * H_B, DH_B)
    k = k.reshape(bn, t, 2 * H_B, DH_B)
    if ctx_kv is not None:
        q, k = axial_rope(q), axial_rope(k)
    q = q.reshape(bn, t, H_B, 2, DH_B).transpose(0, 2, 3, 1, 4)
    k = k.reshape(bn, t, H_B, 2, DH_B).transpose(0, 2, 3, 1, 4)
    v = to_heads(v, H_B)
    if ctx_kv is None:
        k_all, v_all = k, v
    else:
        k_all = jnp.concatenate([k, ctx_kv[0]], axis=3)
        v_all = jnp.concatenate([v, ctx_kv[1]], axis=2)
    y_b = diff_attention(q, k_all, v_all, lam, lam_init, subln)
    y = jnp.concatenate([y_a, from_heads(y_b)], axis=-1) @ w_out
    return y, (k, v)


def odd_mixer(h, w_in, rpb, sink, w_out, ctx_kv):
    bn, t, _ = h.shape
    cq, ck, cv, dq, dk, dv = jnp.split(h @ w_in, [W_C, 2 * W_C, 3 * W_C, 3 * W_C + W_D, 3 * W_C + W_D + KV_D * DH_D], axis=-1)
    cq, ck, cv = to_heads(cq, H_C), to_heads(ck, H_C), to_heads(cv, H_C)
    dq = dq.reshape(bn, t, H_D, DH_D)
    dk = dk.reshape(bn, t, KV_D, DH_D)
    dv = to_heads(dv, KV_D)
    if ctx_kv is None:
        dk = dk.transpose(0, 2, 1, 3)
        y_c = full_attention(cq[:, :, None], ck, cv, None)[:, :, 0]
        y_d = full_attention(dq.transpose(0, 2, 1, 3).reshape(bn, KV_D, H_D // KV_D, t, DH_D), dk, dv, sink)
    else:
        ck_ctx, cv_ctx, dk_ctx, dv_ctx = ctx_kv
        y_c = neighbourhood_attention(cq, ck, cv, ck_ctx, cv_ctx, rpb)
        dq = axial_rope(dq)
        dk = axial_rope(dk).transpose(0, 2, 1, 3)
        y_d = window_attention(dq.transpose(0, 2, 1, 3).reshape(bn, KV_D, H_D // KV_D, t, DH_D), dk, dv, dk_ctx, dv_ctx, sink)
    y_d = y_d.reshape(bn, H_D, t, DH_D)
    y = jnp.concatenate([from_heads(y_c), from_heads(y_d)], axis=-1) @ w_out
    return y, (ck, cv, dk, dv)


def setup_inputs(seed: int = 0) -> dict:
    key = jax.random.key(seed)
    ks = iter(jax.random.split(key, 40))

    def nrm(shape, scale):
        return jax.random.normal(next(ks), shape, jnp.float32) * scale

    def gain(shape):
        return 1.0 + nrm(shape, 0.05)

    return {
        'x_prompt': nrm((BATCH, SEQ, D_MODEL), 1.0),
        'x_sample': nrm((DEC_BATCH, DEC_SEQ, D_MODEL), 1.0),
        'cache_diff_k': nrm((DEC_BATCH, N_EVEN, H_B, 2, PAST_LEN, DH_B), 1.0),
        'cache_diff_v': nrm((DEC_BATCH, N_EVEN, H_B, PAST_LEN, 2 * DH_B), 1.0),
        'cache_na_k': nrm((DEC_BATCH, N_ODD, H_C, PAST_LEN, DH_C), 1.0),
        'cache_na_v': nrm((DEC_BATCH, N_ODD, H_C, PAST_LEN, DH_C), 1.0),
        'cache_swa_k': nrm((DEC_BATCH, N_ODD, KV_D, PAST_LEN, DH_D), 1.0),
        'cache_swa_v': nrm((DEC_BATCH, N_ODD, KV_D, PAST_LEN, DH_D), 1.0),
        'c': nrm((DEC_BATCH, D_MODEL), 1.0),
        'c_ctx': nrm((D_MODEL,), 1.0),
        'mod_w': nrm((DEPTH, D_MODEL, 6 * D_MODEL), D_MODEL ** -0.5),
        'mod_b': nrm((DEPTH, 6 * D_MODEL), 0.02),
        'norm_mix_pre': gain((DEPTH, D_MODEL)),
        'norm_mix_post': gain((DEPTH, D_MODEL)),
        'norm_mlp_pre': gain((DEPTH, D_MODEL)),
        'norm_mlp_post': gain((DEPTH, D_MODEL)),
        'w_in_even': nrm((N_EVEN, D_MODEL, IN_EVEN), D_MODEL ** -0.5),
        'conv_w': nrm((N_EVEN, CONV_W, MIX_A), CONV_W ** -0.5),
        'lambda_q1': nrm((N_EVEN, DH_B), 0.1),
        'lambda_k1': nrm((N_EVEN, DH_B), 0.1),
        'lambda_q2': nrm((N_EVEN, DH_B), 0.1),
        'lambda_k2': nrm((N_EVEN, DH_B), 0.1),
        'subln': gain((N_EVEN, 2 * DH_B)),
        'w_in_odd': nrm((N_ODD, D_MODEL, IN_ODD), D_MODEL ** -0.5),
        'rpb': nrm((N_ODD, H_C, 2 * NA_WIN_R - 1, 2 * NA_WIN_C - 1), 0.1),
        'sink': nrm((N_ODD, H_D), 0.5),
        'w_out': nrm((DEPTH, MIX_WIDTH, D_MODEL), MIX_WIDTH ** -0.5),
        'mlp_w1': nrm((DEPTH, D_MODEL, D_FF), D_MODEL ** -0.5),
        'mlp_w2': nrm((DEPTH, D_FF, D_MODEL), D_FF ** -0.5),
    }


def reference(x_prompt, x_sample, cache_diff_k, cache_diff_v, cache_na_k, cache_na_v, cache_swa_k, cache_swa_v, c, c_ctx, mod_w, mod_b, norm_mix_pre, norm_mix_post, norm_mlp_pre, norm_mlp_post, w_in_even, conv_w, lambda_q1, lambda_k1, lambda_q2, lambda_k2, subln, w_in_odd, rpb, sink, w_out, mlp_w1, mlp_w2):
    def layer(x, cond, li, ctx_kv):
        mod = (jax.nn.silu(cond) @ mod_w[li] + mod_b[li])[:, None, :]
        sh1, sc1, g1, sh2, sc2, g2 = jnp.split(mod, 6, axis=-1)
        h = rmsnorm(x, norm_mix_pre[li]) * (1.0 + sc1) + sh1
        if li % 2 == 0:
            e = li // 2
            lam_init = 0.8 - 0.6 * math.exp(-0.3 * li)
            lam = (jnp.exp(jnp.sum(lambda_q1[e].astype(jnp.float32) * lambda_k1[e].astype(jnp.float32)))
                   - jnp.exp(jnp.sum(lambda_q2[e].astype(jnp.float32) * lambda_k2[e].astype(jnp.float32))) + lam_init)
            y, kv = even_mixer(h, w_in_even[e], conv_w[e], lam, lam_init, subln[e], w_out[li], ctx_kv)
        else:
            o = li // 2
            y, kv = odd_mixer(h, w_in_odd[o], rpb[o], sink[o], w_out[li], ctx_kv)
        x = x + g1 * rmsnorm(y, norm_mix_post[li])
        h = rmsnorm(x, norm_mlp_pre[li]) * (1.0 + sc2) + sh2
        y = jnp.square(jax.nn.relu(h @ mlp_w1[li])) @ mlp_w2[li]
        x = x + g2 * rmsnorm(y, norm_mlp_post[li])
        return x, kv

    xp = x_prompt
    diff_k, diff_v, na_k, na_v, swa_k, swa_v = [], [], [], [], [], []
    for li in range(DEPTH):
        xp, kv = layer(xp, c_ctx[None, :], li, None)
        if li % 2 == 0:
            diff_k.append(kv[0])
            diff_v.append(kv[1])
        else:
            na_k.append(kv[0])
            na_v.append(kv[1])
            swa_k.append(kv[2])
            swa_v.append(kv[3])
    y_prompt = xp
    new_diff_k = jnp.stack(diff_k, axis=1)
    new_diff_v = jnp.stack(diff_v, axis=1)
    new_na_k = jnp.stack(na_k, axis=1)
    new_na_v = jnp.stack(na_v, axis=1)
    new_swa_k = jnp.stack(swa_k, axis=1)
    new_swa_v = jnp.stack(swa_v, axis=1)

    xs = x_sample
    for li in range(DEPTH):
        if li % 2 == 0:
            e = li // 2
            ctx = (cache_diff_k[:, e], cache_diff_v[:, e])
        else:
            o = li // 2
            ctx = (cache_na_k[:, o], cache_na_v[:, o], cache_swa_k[:, o], cache_swa_v[:, o])
        xs, _ = layer(xs, c, li, ctx)
    y_sample = xs

    return (y_prompt, y_sample, new_diff_k, new_diff_v, new_na_k, new_na_v, new_swa_k, new_swa_v)
```

```python
import functools
import math

import jax
import jax.numpy as jnp
from jax import lax
from jax.experimental import pallas as pl
from jax.experimental.pallas import tpu as pltpu

F32 = jnp.float32
BF16 = jnp.bfloat16

D_MODEL = 1024
BATCH = 32
SEQ = 256
DEPTH = 2
DEC_BATCH = 8
DEC_SEQ = 1024
PAST_LEN = 256
GRID_W = 64
GRID_H = DEC_SEQ // GRID_W
MIX_A = 512
H_B = 4
DH = 64
H_C = 8
H_D = 8
KV_D = 2
GQA = H_D // KV_D
IN_EVEN = 3072
IN_ODD = 2304
D_FF = 4096
NA_WIN_R = 8
NA_WIN_C = 16
SWA_BLOCK = 128
ROPE_BASE = 10000.0
EPS = 1e-6
NEG = -1e30
QK_SCALE = DH ** -0.5

MOD_ROWS = 16
CTX_ROW = DEC_BATCH
V7X_VMEM_BYTES = 64 * 1024 * 1024
VMEM_LIMIT = V7X_VMEM_BYTES - 8 * 1024 * 1024

TM = 512
FF_CHUNK = 1024
MOD_TN = 1536
NA_HALF = 512
NA_KEYS = 768


def _cparams(n_axes):
    return pltpu.CompilerParams(dimension_semantics=("arbitrary",) * n_axes, vmem_limit_bytes=VMEM_LIMIT)


def _rms(x, g):
    return x * lax.rsqrt(jnp.mean(x * x, axis=-1, keepdims=True) + EPS) * g


def _dot(a, b):
    return jnp.dot(a, b, preferred_element_type=F32)


def _qk(q, k):
    return lax.dot_general(q, k, (((1,), (1,)), ((), ())), preferred_element_type=F32)


def _resident(shape):
    return pl.BlockSpec(shape, lambda *_: (0,) * len(shape), pipeline_mode=pl.Buffered(1))


def _mod_kernel(cond_ref, w_ref, b_ref, o_ref):
    c = cond_ref[...]
    s = c * (1.0 / (1.0 + jnp.exp(-c)))
    o_ref[0] = _dot(s.astype(BF16), w_ref[0].astype(BF16)) + b_ref[0]


def _modulation(cond, mod_w, mod_b):
    return pl.pallas_call(
        _mod_kernel,
        grid=(DEPTH, 6 * D_MODEL // MOD_TN),
        in_specs=[
            pl.BlockSpec((MOD_ROWS, D_MODEL), lambda l, j: (0, 0)),
            pl.BlockSpec((1, D_MODEL, MOD_TN), lambda l, j: (l, 0, j)),
            pl.BlockSpec((1, 1, MOD_TN), lambda l, j: (l, 0, j)),
        ],
        out_specs=pl.BlockSpec((1, MOD_ROWS, MOD_TN), lambda l, j: (l, 0, j)),
        out_shape=jax.ShapeDtypeStruct((DEPTH, MOD_ROWS, 6 * D_MODEL), F32),
        compiler_params=_cparams(2),
        name="adaln_modulation",
    )(cond, mod_w, mod_b.reshape(DEPTH, 1, 6 * D_MODEL))


def _norm_mod(x, g, mod, shift_col):
    sh = mod[:, shift_col:shift_col + D_MODEL]
    sc = mod[:, shift_col + D_MODEL:shift_col + 2 * D_MODEL]
    return _rms(x, g) * (1.0 + sc) + sh


def _rope(z, cos, sin):
    outs = []
    for j in range(z.shape[1] // 128):
        zj = z[:, j * 128:(j + 1) * 128]
        lane = lax.broadcasted_iota(jnp.int32, zj.shape, 1)
        partner = jnp.where((lane & 16) == 0, pltpu.roll(zj, 128 - 16, axis=1), pltpu.roll(zj, 16, axis=1))
        outs.append(zj * cos + partner * sin)
    return outs[0] if len(outs) == 1 else jnp.concatenate(outs, axis=1)


def _k1_ctx_even(x_ref, mod_ref, g_ref, w_ref, proj_ref, dk_ref, dv_ref):
    h = _norm_mod(x_ref[...], g_ref[...], mod_ref[0], 0).astype(BF16)
    for c in range(IN_EVEN // 512):
        pc = _dot(h, w_ref[:, c * 512:(c + 1) * 512])
        if c == 3:
            proj_ref[:, c * 512:(c + 1) * 512] = (pc * QK_SCALE).astype(BF16)
        else:
            proj_ref[:, c * 512:(c + 1) * 512] = pc.astype(BF16)
        for bi in range(TM // SEQ):
            rows = slice(bi * SEQ, (bi + 1) * SEQ)
            if c == 4:
                for hm in range(2 * H_B):
                    dk_ref[bi, 0, hm // 2, hm % 2] = pc[rows, hm * DH:(hm + 1) * DH]
            if c == 5:
                for hh in range(H_B):
                    dv_ref[bi, 0, hh] = pc[rows, hh * 2 * DH:(hh + 1) * 2 * DH]


def _k1_lat_even(x_ref, mod_ref, g_ref, w_ref, cos_ref, sin_ref, proj_ref):
    h = _norm_mod(x_ref[...], g_ref[...], mod_ref[0], 0).astype(BF16)
    for c in range(IN_EVEN // 512):
        pc = _dot(h, w_ref[:, c * 512:(c + 1) * 512])
        if c == 3:
            pc = _rope(pc * QK_SCALE, cos_ref[...], sin_ref[...])
        elif c == 4:
            pc = _rope(pc, cos_ref[...], sin_ref[...])
        proj_ref[:, c * 512:(c + 1) * 512] = pc.astype(BF16)


def _k1_ctx_odd(x_ref, mod_ref, g_ref, w_ref, proj_ref, nk_ref, nv_ref, sk_ref, sv_ref):
    h = _norm_mod(x_ref[...], g_ref[...], mod_ref[0], 0).astype(BF16)
    for c in range(4):
        pc = _dot(h, w_ref[:, c * 512:(c + 1) * 512])
        if c == 0 or c == 3:
            proj_ref[:, c * 512:(c + 1) * 512] = (pc * QK_SCALE).astype(BF16)
        else:
            proj_ref[:, c * 512:(c + 1) * 512] = pc.astype(BF16)
        if c == 1 or c == 2:
            dst = nk_ref if c == 1 else nv_ref
            for bi in range(TM // SEQ):
                for hh in range(H_C):
                    dst[bi, 0, hh] = pc[bi * SEQ:(bi + 1) * SEQ, hh * DH:(hh + 1) * DH]
    pc = _dot(h, w_ref[:, 2048:IN_ODD])
    proj_ref[:, 2048:IN_ODD] = pc.astype(BF16)
    for bi in range(TM // SEQ):
        for g in range(KV_D):
            sk_ref[bi, 0, g] = pc[bi * SEQ:(bi + 1) * SEQ, g * DH:(g + 1) * DH]
            sv_ref[bi, 0, g] = pc[bi * SEQ:(bi + 1) * SEQ, 128 + g * DH:128 + (g + 1) * DH]


def _k1_lat_odd(x_ref, mod_ref, g_ref, w_ref, cos_ref, sin_ref, proj_ref):
    h = _norm_mod(x_ref[...], g_ref[...], mod_ref[0], 0).astype(BF16)
    for c in range(4):
        pc = _dot(h, w_ref[:, c * 512:(c + 1) * 512])
        if c == 0:
            pc = pc * QK_SCALE
        elif c == 3:
            pc = _rope(pc * QK_SCALE, cos_ref[...], sin_ref[...])
        proj_ref[:, c * 512:(c + 1) * 512] = pc.astype(BF16)
    pc = _dot(h, w_ref[:, 2048:IN_ODD])
    proj_ref[:, 2048:2176] = _rope(pc[:, 0:128], cos_ref[...], sin_ref[...]).astype(BF16)
    proj_ref[:, 2176:IN_ODD] = pc[:, 128:256].astype(BF16)


def _pre_mixer(x, mod_l, gain, w, *, latent, even, rope):
    n_tok = x.shape[0]
    n_in = w.shape[1]
    tiles_per_seq = DEC_SEQ // TM
    if latent:
        mod_map = lambda i: (i // tiles_per_seq, 0, 0)
    else:
        mod_map = lambda i: (CTX_ROW, 0, 0)
    in_specs = [
        pl.BlockSpec((TM, D_MODEL), lambda i: (i, 0)),
        pl.BlockSpec((1, 1, 6 * D_MODEL), mod_map),
        pl.BlockSpec((1, D_MODEL), lambda i: (0, 0)),
        _resident((D_MODEL, n_in)),
    ]
    args = [x, mod_l, gain, w]
    proj_spec = pl.BlockSpec((TM, n_in), lambda i: (i, 0))
    proj_shape = jax.ShapeDtypeStruct((n_tok, n_in), BF16)
    if latent:
        in_specs += [pl.BlockSpec((TM, 128), lambda i: (i % tiles_per_seq, 0))] * 2
        args += list(rope)
        body = _k1_lat_even if even else _k1_lat_odd
        out_specs, out_shape = proj_spec, proj_shape
    else:
        nb = TM // SEQ
        if even:
            body = _k1_ctx_even
            out_specs = [proj_spec,
                         pl.BlockSpec((nb, 1, H_B, 2, SEQ, DH), lambda i: (i, 0, 0, 0, 0, 0)),
                         pl.BlockSpec((nb, 1, H_B, SEQ, 2 * DH), lambda i: (i, 0, 0, 0, 0))]
            out_shape = [proj_shape,
                         jax.ShapeDtypeStruct((BATCH, 1, H_B, 2, SEQ, DH), F32),
                         jax.ShapeDtypeStruct((BATCH, 1, H_B, SEQ, 2 * DH), F32)]
        else:
            body = _k1_ctx_odd
            c_spec = pl.BlockSpec((nb, 1, H_C, SEQ, DH), lambda i: (i, 0, 0, 0, 0))
            d_spec = pl.BlockSpec((nb, 1, KV_D, SEQ, DH), lambda i: (i, 0, 0, 0, 0))
            c_shape = jax.ShapeDtypeStruct((BATCH, 1, H_C, SEQ, DH), F32)
            d_shape = jax.ShapeDtypeStruct((BATCH, 1, KV_D, SEQ, DH), F32)
            out_specs = [proj_spec, c_spec, c_spec, d_spec, d_spec]
            out_shape = [proj_shape, c_shape, c_shape, d_shape, d_shape]
    return pl.pallas_call(
        body,
        grid=(n_tok // TM,),
        in_specs=in_specs,
        out_specs=out_specs,
        out_shape=out_shape,
        compiler_params=_cparams(1),
        name=f"pre_mixer_{'lat' if latent else 'ctx'}_{'even' if even else 'odd'}",
    )(*args)


def _even_mixer_kernel(*refs, seq, tq, has_ctx, lam_init):
    if has_ctx:
        (proj_ref, cw_ref, lq1_ref, lk1_ref, lq2_ref, lk2_ref, subln_ref, ck_ref, cv_ref, ya_ref, yb_ref) = refs
    else:
        (proj_ref, cw_ref, lq1_ref, lk1_ref, lq2_ref, lk2_ref, subln_ref, ya_ref, yb_ref) = refs
        ck_ref = cv_ref = None

    row = lax.broadcasted_iota(jnp.int32, (seq, 128), 0)
    for j in range(MIX_A // 128):
        cols = slice(j * 128, (j + 1) * 128)
        a_b = proj_ref[:, j * 128:(j + 1) * 128].astype(F32)
        u = (proj_ref[:, MIX_A + j * 128:MIX_A + (j + 1) * 128].astype(F32)
             * proj_ref[:, 2 * MIX_A + j * 128:2 * MIX_A + (j + 1) * 128].astype(F32))
        u_prev = jnp.where(row == 0, 0.0, pltpu.roll(u, 1, axis=0))
        u_next = jnp.where(row == seq - 1, 0.0, pltpu.roll(u, seq - 1, axis=0))
        w = cw_ref[:, cols]
        ya_ref[:, cols] = (a_b * (w[0:1] * u_prev + w[1:2] * u + w[2:3] * u_next)).astype(BF16)

    lam = (jnp.exp(jnp.sum(lq1_ref[...] * lk1_ref[...], axis=-1, keepdims=True))
           - jnp.exp(jnp.sum(lq2_ref[...] * lk2_ref[...], axis=-1, keepdims=True)) + lam_init)
    subln = subln_ref[...]
    q_col, k_col, v_col = 3 * MIX_A, 3 * MIX_A + 512, 3 * MIX_A + 1024

    def q_tile(qrows):
        for h in range(H_B):
            v_h = proj_ref[:, v_col + h * 2 * DH:v_col + (h + 1) * 2 * DH]
            lat_parts, ctx_parts = [], []
            for m in range(2):
                c = (2 * h + m) * DH
                q = proj_ref[qrows, q_col + c:q_col + c + DH]
                k = proj_ref[:, k_col + c:k_col + c + DH]
                s1 = _qk(q, k)
                mx = jnp.max(s1, axis=-1, keepdims=True)
                if has_ctx:
                    s2 = _qk(q, ck_ref[0, 0, h, m].astype(BF16))
                    mx = jnp.maximum(mx, jnp.max(s2, axis=-1, keepdims=True))
                e1 = jnp.exp(s1 - mx)
                den = jnp.sum(e1, axis=-1, keepdims=True)
                if has_ctx:
                    e2 = jnp.exp(s2 - mx)
                    den = den + jnp.sum(e2, axis=-1, keepdims=True)
                r = (1.0 / den) if m == 0 else (lam / den)
                lat_parts.append(e1 * r)
                if has_ctx:
                    ctx_parts.append(e2 * r)
            o = _dot((lat_parts[0] - lat_parts[1]).astype(BF16), v_h)
            if has_ctx:
                o = o + _dot((ctx_parts[0] - ctx_parts[1]).astype(BF16), cv_ref[0, 0, h].astype(BF16))
            o = _rms(o, subln) * (1.0 - lam_init)
            yb_ref[qrows, h * 2 * DH:(h + 1) * 2 * DH] = o.astype(BF16)

    if seq == tq:
        q_tile(slice(0, seq))
    else:
        def body(i, carry):
            q_tile(pl.ds(pl.multiple_of(i * tq, tq), tq))
            return carry
        lax.fori_loop(0, seq // tq, body, 0)


def _even_mixer(proj, conv_w, lq1, lk1, lq2, lk2, subln, cache_k, cache_v, *, seq, lam_init):
    n_tok = proj.shape[0]
    has_ctx = cache_k is not None
    small = lambda n: pl.BlockSpec((1, n), lambda b: (0, 0))
    in_specs = [pl.BlockSpec((seq, IN_EVEN), lambda b: (b, 0)),
                pl.BlockSpec((3, MIX_A), lambda b: (0, 0)),
                small(DH), small(DH), small(DH), small(DH), small(2 * DH)]
    args = [proj, conv_w, lq1, lk1, lq2, lk2, subln]
    if has_ctx:
        in_specs += [pl.BlockSpec((1, 1, H_B, 2, PAST_LEN, DH), lambda b: (b, 0, 0, 0, 0, 0)),
                     pl.BlockSpec((1, 1, H_B, PAST_LEN, 2 * DH), lambda b: (b, 0, 0, 0, 0))]
        args += [cache_k, cache_v]
    y_spec = pl.BlockSpec((seq, 512), lambda b: (b, 0))
    y_shape = jax.ShapeDtypeStruct((n_tok, 512), BF16)
    return pl.pallas_call(
        functools.partial(_even_mixer_kernel, seq=seq, tq=min(seq, 256), has_ctx=has_ctx, lam_init=lam_init),
        grid=(n_tok // seq,),
        in_specs=in_specs,
        out_specs=[y_spec, y_spec],
        out_shape=[y_shape, y_shape],
        compiler_params=_cparams(1),
        name=f"even_mixer_{'lat' if has_ctx else 'ctx'}",
    )(*args)


def _odd_ctx_kernel(proj_ref, sink_ref, y_ref):
    for h in range(H_C):
        q = proj_ref[:, h * DH:(h + 1) * DH]
        k = proj_ref[:, 512 + h * DH:512 + (h + 1) * DH]
        v = proj_ref[:, 1024 + h * DH:1024 + (h + 1) * DH]
        s = _qk(q, k)
        e = jnp.exp(s - jnp.max(s, axis=-1, keepdims=True))
        den = jnp.sum(e, axis=-1, keepdims=True)
        y_ref[:, h * DH:(h + 1) * DH] = (_dot(e.astype(BF16), v) * (1.0 / den)).astype(BF16)
    for g in range(KV_D):
        k = proj_ref[:, 2048 + g * DH:2048 + (g + 1) * DH]
        v = proj_ref[:, 2176 + g * DH:2176 + (g + 1) * DH]
        q4 = jnp.concatenate(
            [proj_ref[:, 1536 + (g * GQA + m) * DH:1536 + (g * GQA + m + 1) * DH] for m in range(GQA)], axis=0)
        sk = jnp.concatenate(
            [jnp.broadcast_to(sink_ref[g * GQA + m:g * GQA + m + 1, :], (SEQ, 1)) for m in range(GQA)], axis=0)
        s = _qk(q4, k)
        mx = jnp.maximum(jnp.max(s, axis=-1, keepdims=True), sk)
        e = jnp.exp(s - mx)
        den = jnp.sum(e, axis=-1, keepdims=True) + jnp.exp(sk - mx)
        o = _dot(e.astype(BF16), v) * (1.0 / den)
        for m in range(GQA):
            c = 512 + (g * GQA + m) * DH
            y_ref[:, c:c + DH] = o[m * SEQ:(m + 1) * SEQ].astype(BF16)


def _odd_ctx_mixer(proj, sink):
    n_tok = proj.shape[0]
    return pl.pallas_call(
        _odd_ctx_kernel,
        grid=(n_tok // SEQ,),
        in_specs=[pl.BlockSpec((SEQ, IN_ODD), lambda b: (b, 0)),
                  pl.BlockSpec((H_D, 1), lambda b: (0, 0))],
        out_specs=pl.BlockSpec((SEQ, D_MODEL), lambda b: (b, 0)),
        out_shape=jax.ShapeDtypeStruct((n_tok, D_MODEL), BF16),
        compiler_params=_cparams(1),
        name="odd_mixer_ctx",
    )(proj, sink)


def _na_bias_kernel(rpb_ref, o_ref):
    h = pl.program_id(0)
    qc = lax.broadcasted_iota(jnp.int32, (GRID_W, GRID_W), 0)
    kc = lax.broadcasted_iota(jnp.int32, (GRID_W, GRID_W), 1)
    col_start = jnp.clip(qc - NA_WIN_C // 2, 0, GRID_W - NA_WIN_C)
    col_ok = (kc >= col_start) & (kc < col_start + NA_WIN_C)
    dc = kc - qc + NA_WIN_C - 1
    n_dr, n_dc = 2 * NA_WIN_R - 1, 2 * NA_WIN_C - 1
    neg = jnp.full((GRID_W, GRID_W), NEG, F32)
    toeplitz = []
    for dr in range(n_dr):
        t = neg
        for d in range(n_dc):
            t = jnp.where(dc == d, rpb_ref[(h * n_dr + dr) * n_dc + d], t)
        toeplitz.append(jnp.where(col_ok, t, NEG))
    rows_per_half = NA_HALF // GRID_W
    key_rows = NA_KEYS // GRID_W
    for half in range(2):
        for rl in range(rows_per_half):
            r = half * rows_per_half + rl
            r_start = min(max(r - NA_WIN_R // 2, 0), GRID_H - NA_WIN_R)
            blocks = []
            for j in range(key_rows):
                rk = half * (GRID_H - key_rows) + j
                inside = r_start <= rk < r_start + NA_WIN_R
                blocks.append(toeplitz[rk - r + NA_WIN_R - 1] if inside else neg)
            o_ref[0, half, rl * GRID_W:(rl + 1) * GRID_W, :] = jnp.concatenate(blocks, axis=1)


def _na_bias(rpb):
    return pl.pallas_call(
        _na_bias_kernel,
        grid=(H_C,),
        in_specs=[pl.BlockSpec(memory_space=pltpu.SMEM)],
        out_specs=pl.BlockSpec((1, 2, NA_HALF, NA_KEYS), lambda h: (h, 0, 0, 0)),
        out_shape=jax.ShapeDtypeStruct((H_C, 2, NA_HALF, NA_KEYS), F32),
        compiler_params=_cparams(1),
        name="na_bias",
    )(rpb.reshape(-1))


def _na_kernel(q_ref, k_ref, v_ref, kc_ref, vc_ref, nb_ref, y_ref):
    for half in range(2):
        w0 = half * (DEC_SEQ - NA_KEYS)
        outs = []
        for hh in range(2):
            cols = slice(hh * DH, (hh + 1) * DH)
            q = q_ref[half * NA_HALF:(half + 1) * NA_HALF, cols]
            s1 = _qk(q, k_ref[w0:w0 + NA_KEYS, cols]) + nb_ref[hh, half]
            s2 = _qk(q, kc_ref[0, 0, hh].astype(BF16))
            mx = jnp.maximum(jnp.max(s1, axis=-1, keepdims=True), jnp.max(s2, axis=-1, keepdims=True))
            e1 = jnp.exp(s1 - mx)
            e2 = jnp.exp(s2 - mx)
            den = jnp.sum(e1, axis=-1, keepdims=True) + jnp.sum(e2, axis=-1, keepdims=True)
            o = _dot(e1.astype(BF16), v_ref[w0:w0 + NA_KEYS, cols]) + _dot(e2.astype(BF16), vc_ref[0, 0, hh].astype(BF16))
            outs.append(o * (1.0 / den))
        y_ref[half * NA_HALF:(half + 1) * NA_HALF, :] = jnp.concatenate(outs, axis=1).astype(BF16)


def _na_mixer(proj, cache_k, cache_v, na_bias):
    n_tok = proj.shape[0]
    pairs = H_C // 2
    col = lambda base: pl.BlockSpec((DEC_SEQ, 128), lambda hp, b: (b, base + hp))
    cache = pl.BlockSpec((1, 1, 2, PAST_LEN, DH), lambda hp, b: (b, 0, hp, 0, 0))
    return pl.pallas_call(
        _na_kernel,
        grid=(pairs, DEC_BATCH),
        in_specs=[col(0), col(pairs), col(2 * pairs), cache, cache,
                  pl.BlockSpec((2, 2, NA_HALF, NA_KEYS), lambda hp, b: (hp, 0, 0, 0))],
        out_specs=pl.BlockSpec((DEC_SEQ, 128), lambda hp, b: (b, hp)),
        out_shape=jax.ShapeDtypeStruct((n_tok, 512), BF16),
        compiler_params=_cparams(2),
        name="na_mixer",
    )(proj, proj, proj, cache_k, cache_v, na_bias)


def _swa_kernel(q_ref, kv_ref, kc_ref, vc_ref, sink_ref, y_ref):
    nb = DEC_SEQ // SWA_BLOCK
    band = 3 * SWA_BLOCK
    ql = lax.broadcasted_iota(jnp.int32, (SWA_BLOCK, band), 0)
    kj = lax.broadcasted_iota(jnp.int32, (SWA_BLOCK, band), 1)

    def band_bias(first_key_minus_block_start):
        ok = jnp.abs(kj + first_key_minus_block_start - ql) <= SWA_BLOCK
        return jnp.concatenate([jnp.where(ok, 0.0, NEG)] * GQA, axis=0)

    bias_first, bias_mid, bias_last = band_bias(0), band_bias(-SWA_BLOCK), band_bias(-2 * SWA_BLOCK)
    for g in range(KV_D):
        kc = kc_ref[0, 0, g].astype(BF16)
        vc = vc_ref[0, 0, g].astype(BF16)
        sk = jnp.concatenate(
            [jnp.broadcast_to(sink_ref[g * GQA + m:g * GQA + m + 1, :], (SWA_BLOCK, 1)) for m in range(GQA)], axis=0)
        for n in range(nb):
            start = min(max((n - 1) * SWA_BLOCK, 0), DEC_SEQ - band)
            bias = bias_first if n == 0 else (bias_last if n == nb - 1 else bias_mid)
            rows = slice(n * SWA_BLOCK, (n + 1) * SWA_BLOCK)
            q4 = jnp.concatenate(
                [q_ref[rows, (g * GQA + m) * DH:(g * GQA + m + 1) * DH] for m in range(GQA)], axis=0)
            s1 = _qk(q4, kv_ref[start:start + band, g * DH:(g + 1) * DH]) + bias
            s2 = _qk(q4, kc)
            mx = jnp.maximum(jnp.maximum(jnp.max(s1, axis=-1, keepdims=True), jnp.max(s2, axis=-1, keepdims=True)), sk)
            e1 = jnp.exp(s1 - mx)
            e2 = jnp.exp(s2 - mx)
            den = jnp.sum(e1, axis=-1, keepdims=True) + jnp.sum(e2, axis=-1, keepdims=True) + jnp.exp(sk - mx)
            o = _dot(e1.astype(BF16), kv_ref[start:start + band, 128 + g * DH:128 + (g + 1) * DH]) + _dot(e2.astype(BF16), vc)
            o = o * (1.0 / den)
            for m in range(GQA):
                c = (g * GQA + m) * DH
                y_ref[rows, c:c + DH] = o[m * SWA_BLOCK:(m + 1) * SWA_BLOCK].astype(BF16)


def _swa_mixer(proj, cache_k, cache_v, sink):
    n_tok = proj.shape[0]
    cache = pl.BlockSpec((1, 1, KV_D, PAST_LEN, DH), lambda b: (b, 0, 0, 0, 0))
    return pl.pallas_call(
        _swa_kernel,
        grid=(DEC_BATCH,),
        in_specs=[pl.BlockSpec((DEC_SEQ, 512), lambda b: (b, 3)),
                  pl.BlockSpec((DEC_SEQ, 256), lambda b: (b, 8)),
                  cache, cache,
                  pl.BlockSpec((H_D, 1), lambda b: (0, 0))],
        out_specs=pl.BlockSpec((DEC_SEQ, 512), lambda b: (b, 0)),
        out_shape=jax.ShapeDtypeStruct((n_tok, 512), BF16),
        compiler_params=_cparams(1),
        name="swa_mixer",
    )(proj, proj, cache_k, cache_v, sink)


def _post_mixer_kernel(ma_ref, mb_ref, x_ref, mod_ref, gpost_ref, gpre_ref, gmlp_ref, wo_ref, w1_ref, w2_ref, o_ref):
    half = D_MODEL // 2
    mod = mod_ref[0]
    y = _dot(ma_ref[...], wo_ref[0:half, :]) + _dot(mb_ref[...], wo_ref[half:D_MODEL, :])
    x1 = x_ref[...] + mod[:, 2 * D_MODEL:3 * D_MODEL] * _rms(y, gpost_ref[...])
    h = _norm_mod(x1, gpre_ref[...], mod, 3 * D_MODEL).astype(BF16)
    acc = jnp.zeros((TM, D_MODEL), F32)
    for c in range(D_FF // FF_CHUNK):
        f = _dot(h, w1_ref[:, c * FF_CHUNK:(c + 1) * FF_CHUNK])
        f = jnp.square(jnp.maximum(f, 0.0)).astype(BF16)
        acc = acc + _dot(f, w2_ref[c * FF_CHUNK:(c + 1) * FF_CHUNK, :])
    o_ref[...] = x1 + mod[:, 5 * D_MODEL:6 * D_MODEL] * _rms(acc, gmlp_ref[...])


def _post_mixer(mix_a, a_blk, mix_b, b_blk, x, mod_l, g_post, g_pre, g_mlp, w_out, w1, w2, *, latent):
    n_tok = x.shape[0]
    half = D_MODEL // 2
    tiles_per_seq = DEC_SEQ // TM
    if latent:
        mod_map = lambda i: (i // tiles_per_seq, 0, 0)
    else:
        mod_map = lambda i: (CTX_ROW, 0, 0)
    gain = pl.BlockSpec((1, D_MODEL), lambda i: (0, 0))
    return pl.pallas_call(
        _post_mixer_kernel,
        grid=(n_tok // TM,),
        in_specs=[pl.BlockSpec((TM, half), lambda i: (i, a_blk)),
                  pl.BlockSpec((TM, half), lambda i: (i, b_blk)),
                  pl.BlockSpec((TM, D_MODEL), lambda i: (i, 0)),
                  pl.BlockSpec((1, 1, 6 * D_MODEL), mod_map),
                  gain, gain, gain,
                  _resident((D_MODEL, D_MODEL)), _resident((D_MODEL, D_FF)), _resident((D_FF, D_MODEL))],
        out_specs=pl.BlockSpec((TM, D_MODEL), lambda i: (i, 0)),
        out_shape=jax.ShapeDtypeStruct((n_tok, D_MODEL), F32),
        compiler_params=_cparams(1),
        name=f"post_mixer_{'lat' if latent else 'ctx'}",
    )(mix_a, mix_b, x, mod_l, g_post, g_pre, g_mlp, w_out, w1, w2)


def _rope_tables():
    t = jnp.arange(DEC_SEQ)
    rows = (t // GRID_W).astype(F32)
    cols = (t % GRID_W).astype(F32)
    q4 = DH // 4
    inv = 1.0 / (ROPE_BASE ** (jnp.arange(q4, dtype=F32) / q4))
    ar, ac = rows[:, None] * inv, cols[:, None] * inv
    cos64 = jnp.concatenate([jnp.cos(ar), jnp.cos(ar), jnp.cos(ac), jnp.cos(ac)], axis=1)
    sin64 = jnp.concatenate([-jnp.sin(ar), jnp.sin(ar), -jnp.sin(ac), jnp.sin(ac)], axis=1)
    return jnp.tile(cos64, (1, 2)), jnp.tile(sin64, (1, 2))


def kernel(x_prompt, x_sample, cache_diff_k, cache_diff_v, cache_na_k, cache_na_v, cache_swa_k, cache_swa_v, c, c_ctx, mod_w, mod_b, norm_mix_pre, norm_mix_post, norm_mlp_pre, norm_mlp_post, w_in_even, conv_w, lambda_q1, lambda_k1, lambda_q2, lambda_k2, subln, w_in_odd, rpb, sink, w_out, mlp_w1, mlp_w2):
    cond = jnp.zeros((MOD_ROWS, D_MODEL), F32).at[:DEC_BATCH].set(c).at[CTX_ROW].set(c_ctx)
    mod = _modulation(cond, mod_w, mod_b).reshape(DEPTH, MOD_ROWS, 1, 6 * D_MODEL)
    rope = _rope_tables()
    na_bias = _na_bias(rpb[0])

    w_in = [w_in_even[0].astype(BF16), w_in_odd[0].astype(BF16)]
    w_o, w_1, w_2 = w_out.astype(BF16), mlp_w1.astype(BF16), mlp_w2.astype(BF16)
    row = lambda a, i: a[i].reshape(1, -1)

    xp = x_prompt.reshape(BATCH * SEQ, D_MODEL)
    xs = x_sample.reshape(DEC_BATCH * DEC_SEQ, D_MODEL)
    lam_init0 = 0.8 - 0.6 * math.exp(-0.3 * 0)
    lam_args = (row(lambda_q1, 0), row(lambda_k1, 0), row(lambda_q2, 0), row(lambda_k2, 0), row(subln, 0))
    sink_col = sink[0].reshape(H_D, 1)

    def post(li, ma, a_blk, mb, b_blk, x, latent):
        return _post_mixer(ma, a_blk, mb, b_blk, x, mod[li], row(norm_mix_post, li), row(norm_mlp_pre, li),
                           row(norm_mlp_post, li), w_o[li], w_1[li], w_2[li], latent=latent)

    proj, new_diff_k, new_diff_v = _pre_mixer(xp, mod[0], row(norm_mix_pre, 0), w_in[0], latent=False, even=True, rope=None)
    ya, yb = _even_mixer(proj, conv_w[0], *lam_args, None, None, seq=SEQ, lam_init=lam_init0)
    xp = post(0, ya, 0, yb, 0, xp, False)
    proj = _pre_mixer(xs, mod[0], row(norm_mix_pre, 0), w_in[0], latent=True, even=True, rope=rope)
    ya, yb = _even_mixer(proj, conv_w[0], *lam_args, cache_diff_k, cache_diff_v, seq=DEC_SEQ, lam_init=lam_init0)
    xs = post(0, ya, 0, yb, 0, xs, True)

    proj, new_na_k, new_na_v, new_swa_k, new_swa_v = _pre_mixer(
        xp, mod[1], row(norm_mix_pre, 1), w_in[1], latent=False, even=False, rope=None)
    y = _odd_ctx_mixer(proj, sink_col)
    xp = post(1, y, 0, y, 1, xp, False)
    proj = _pre_mixer(xs, mod[1], row(norm_mix_pre, 1), w_in[1], latent=True, even=False, rope=rope)
    yc = _na_mixer(proj, cache_na_k, cache_na_v, na_bias)
    yd = _swa_mixer(proj, cache_swa_k, cache_swa_v, sink_col)
    xs = post(1, yc, 0, yd, 0, xs, True)

    return (xp.reshape(BATCH, SEQ, D_MODEL), xs.reshape(DEC_BATCH, DEC_SEQ, D_MODEL),
            new_diff_k, new_diff_v, new_na_k, new_na_v, new_swa_k, new_swa_v)
```

```python
import functools
import math

import jax
import jax.numpy as jnp
from jax import lax
from jax.experimental import pallas as pl
from jax.experimental.pallas import tpu as pltpu

F32 = jnp.float32
BF16 = jnp.bfloat16

D_MODEL = 1024
BATCH = 32
SEQ = 256
DEPTH = 2
DEC_BATCH = 8
DEC_SEQ = 1024
PAST_LEN = 256
GRID_W = 64
GRID_H = DEC_SEQ // GRID_W
MIX_A = 512
H_B = 4
DH = 64
PAIR = 2 * DH
H_C = 8
H_D = 8
KV_D = 2
GQA = H_D // KV_D
IN_EVEN = 3072
IN_ODD = 2304
D_FF = 4096
NA_WIN_R = 8
NA_WIN_C = 16
SWA_BLOCK = 128
SWA_WINDOW = 128
ROPE_BASE = 10000.0
EPS = 1e-6
NEG = -1e30
LOG2E = math.log2(math.e)
Q_SCALE = DH ** -0.5 * LOG2E

MOD_ROWS = 16
CTX_ROW = DEC_BATCH
V7X_VMEM_BYTES = 64 * 1024 * 1024
VMEM_LIMIT = V7X_VMEM_BYTES - 8 * 1024 * 1024

TM = 512
FF_CHUNK = 1024
MOD_TN = 1536
NA_HALF = 512
NA_KEYS = 768
SWA_QROWS = 2 * SWA_BLOCK
SWA_BAND = 4 * SWA_BLOCK
LAT_ODD_COLS = IN_ODD + 2 * PAIR


def _cparams(n_axes):
    return pltpu.CompilerParams(dimension_semantics=("arbitrary",) * n_axes, vmem_limit_bytes=VMEM_LIMIT)


def _rms(x, g):
    return x * lax.rsqrt(jnp.mean(x * x, axis=-1, keepdims=True) + EPS) * g


def _dot(a, b):
    return jnp.dot(a, b, preferred_element_type=F32)


def _dot_nt(a, b):
    return lax.dot_general(a, b, (((1,), (1,)), ((), ())), preferred_element_type=F32)


def _layer_block(shape, li):
    zeros = (0,) * len(shape)
    return pl.BlockSpec((1,) + tuple(shape), lambda *_: (li,) + zeros, pipeline_mode=pl.Buffered(1))


def _low_half(shape):
    return (lax.broadcasted_iota(jnp.int32, shape, len(shape) - 1) & DH) == 0


def _keep_mask_bf16(rows, hh):
    lane = lax.broadcasted_iota(jnp.int32, (rows, PAIR), 1)
    half = (lane & DH).astype(F32).astype(BF16)
    return (half == 0) if hh == 0 else (half != 0)


def _pad_rows(t, hh, fill):
    other = jnp.full(t.shape, fill, t.dtype)
    return jnp.concatenate([t, other] if hh == 0 else [other, t], axis=0)


def _row_max(parts):
    mx = jnp.max(parts[0], axis=-1, keepdims=True)
    for s in parts[1:]:
        mx = jnp.maximum(mx, jnp.max(s, axis=-1, keepdims=True))
    return mx


def _merge_pair(o0, o1, extra_den=None):
    low = _low_half(o0.shape)
    num = jnp.where(low, o0, o1)
    den = pltpu.roll(jnp.where(low, o1, o0), DH, axis=1)
    if extra_den is not None:
        den = den + jnp.where(low, extra_den[0], extra_den[1])
    return num * (1.0 / den)


class _Head:
    def __init__(self, scores, finish, sink=None):
        self.scores, self.finish, self.sink = scores, finish, sink


def _attend(heads, batch):
    batches = [heads[i:i + batch] for i in range(0, len(heads), batch)]
    cur = [hd.scores() for hd in batches[0]]
    for bi, group in enumerate(batches):
        nxt = [hd.scores() for hd in batches[bi + 1]] if bi + 1 < len(batches) else None
        exps = []
        for hd, parts in zip(group, cur):
            mx = _row_max(parts)
            term = None
            if hd.sink is not None:
                mx = jnp.maximum(mx, hd.sink)
                term = jnp.exp2(hd.sink - mx)
            exps.append(([jnp.exp2(s - mx).astype(BF16) for s in parts], term))
        for hd, (e_parts, term) in zip(group, exps):
            hd.finish(e_parts, term)
        cur = nxt


class _PairSink:
    def __init__(self, y_ref, rows, cols):
        self.y_ref, self.rows, self.cols, self.first = y_ref, rows, cols, None

    def put(self, hh, o, term):
        if hh == 0:
            self.first = (o, term)
        else:
            o0, term0 = self.first
            extra = None if term is None else (term0, term)
            self.y_ref[self.rows, self.cols] = _merge_pair(o0, o, extra).astype(BF16)


def _mod_kernel(cond_ref, w_ref, b_ref, o_ref):
    c = cond_ref[...]
    s = c * (1.0 / (1.0 + jnp.exp(-c)))
    o_ref[0] = _dot(s.astype(BF16), w_ref[0].astype(BF16)) + b_ref[0]


def _modulation(cond, mod_w, mod_b):
    return pl.pallas_call(
        _mod_kernel,
        grid=(DEPTH, 6 * D_MODEL // MOD_TN),
        in_specs=[
            pl.BlockSpec((MOD_ROWS, D_MODEL), lambda l, j: (0, 0)),
            pl.BlockSpec((1, D_MODEL, MOD_TN), lambda l, j: (l, 0, j)),
            pl.BlockSpec((1, 1, MOD_TN), lambda l, j: (l, 0, j)),
        ],
        out_specs=pl.BlockSpec((1, MOD_ROWS, MOD_TN), lambda l, j: (l, 0, j)),
        out_shape=jax.ShapeDtypeStruct((DEPTH, MOD_ROWS, 6 * D_MODEL), F32),
        compiler_params=_cparams(2),
        name="adaln_modulation",
    )(cond, mod_w, mod_b.reshape(DEPTH, 1, 6 * D_MODEL))


def _norm_mod(x, g, mod, shift_col):
    sh = mod[:, shift_col:shift_col + D_MODEL]
    sc = mod[:, shift_col + D_MODEL:shift_col + 2 * D_MODEL]
    return _rms(x, g) * (1.0 + sc) + sh


def _rope(z, cos, sin):
    outs = []
    for j in range(z.shape[1] // 128):
        zj = z[:, j * 128:(j + 1) * 128]
        lane = lax.broadcasted_iota(jnp.int32, zj.shape, 1)
        partner = jnp.where((lane & 16) == 0, pltpu.roll(zj, 128 - 16, axis=1), pltpu.roll(zj, 16, axis=1))
        outs.append(zj * cos + partner * sin)
    return outs[0] if len(outs) == 1 else jnp.concatenate(outs, axis=1)


def _store_heads_transposed(dst, bi, pc_rows, n_pairs):
    for p in range(n_pairs):
        t = pc_rows[:, p * PAIR:(p + 1) * PAIR].T
        dst[bi, 0, 2 * p] = t[0:DH]
        dst[bi, 0, 2 * p + 1] = t[DH:PAIR]


def _k1_ctx_even(x_ref, mod_ref, g_ref, w_ref, proj_ref, dkt_ref, dv_ref):
    h = _norm_mod(x_ref[...], g_ref[0], mod_ref[0, 0], 0).astype(BF16)
    for c in range(IN_EVEN // 512):
        pc = _dot(h, w_ref[0, :, c * 512:(c + 1) * 512])
        if c < 3:
            proj_ref[:, c * 512:(c + 1) * 512] = pc.astype(BF16)
        elif c == 3:
            proj_ref[:, c * 512:(c + 1) * 512] = (pc * Q_SCALE).astype(BF16)
        for bi in range(TM // SEQ):
            rows = slice(bi * SEQ, (bi + 1) * SEQ)
            if c == 4:
                for hh in range(H_B):
                    t = pc[rows, hh * PAIR:(hh + 1) * PAIR].T
                    dkt_ref[bi, 0, hh, 0] = t[0:DH]
                    dkt_ref[bi, 0, hh, 1] = t[DH:PAIR]
            if c == 5:
                for hh in range(H_B):
                    dv_ref[bi, 0, hh] = pc[rows, hh * PAIR:(hh + 1) * PAIR]


def _k1_lat_even(x_ref, mod_ref, g_ref, w_ref, cos_ref, sin_ref, proj_ref):
    h = _norm_mod(x_ref[...], g_ref[0], mod_ref[0, 0], 0).astype(BF16)
    for c in range(IN_EVEN // 512):
        pc = _dot(h, w_ref[0, :, c * 512:(c + 1) * 512])
        if c == 3:
            pc = _rope(pc * Q_SCALE, cos_ref[...], sin_ref[...])
        elif c == 4:
            pc = _rope(pc, cos_ref[...], sin_ref[...])
        proj_ref[:, c * 512:(c + 1) * 512] = pc.astype(BF16)


def _k1_ctx_odd(x_ref, mod_ref, g_ref, w_ref, q_ref, nkt_ref, nvt_ref, skt_ref, svt_ref):
    h = _norm_mod(x_ref[...], g_ref[0], mod_ref[0, 0], 0).astype(BF16)
    for c in range(4):
        pc = _dot(h, w_ref[0, :, c * 512:(c + 1) * 512])
        if c == 0 or c == 3:
            q_ref[:, (c // 3) * 512:(c // 3 + 1) * 512] = (pc * Q_SCALE).astype(BF16)
        else:
            dst = nkt_ref if c == 1 else nvt_ref
            for bi in range(TM // SEQ):
                _store_heads_transposed(dst, bi, pc[bi * SEQ:(bi + 1) * SEQ], H_C // 2)
    pc = _dot(h, w_ref[0, :, 2048:IN_ODD])
    for bi in range(TM // SEQ):
        _store_heads_transposed(skt_ref, bi, pc[bi * SEQ:(bi + 1) * SEQ, 0:PAIR], 1)
        _store_heads_transposed(svt_ref, bi, pc[bi * SEQ:(bi + 1) * SEQ, PAIR:2 * PAIR], 1)


def _k1_lat_odd(x_ref, mod_ref, g_ref, w_ref, cos_ref, sin_ref, proj_ref):
    h = _norm_mod(x_ref[...], g_ref[0], mod_ref[0, 0], 0).astype(BF16)
    for c in range(4):
        pc = _dot(h, w_ref[0, :, c * 512:(c + 1) * 512])
        if c == 0:
            pc = pc * Q_SCALE
        elif c == 3:
            pc = _rope(pc * Q_SCALE, cos_ref[...], sin_ref[...])
        proj_ref[:, c * 512:(c + 1) * 512] = pc.astype(BF16)
    pc = _dot(h, w_ref[0, :, 2048:IN_ODD])
    low = _low_half((TM, PAIR))
    for j, z in enumerate((_rope(pc[:, 0:PAIR], cos_ref[...], sin_ref[...]), pc[:, PAIR:2 * PAIR])):
        zr = pltpu.roll(z, DH, axis=1)
        base = 2048 + j * 2 * PAIR
        proj_ref[:, base:base + PAIR] = jnp.where(low, z, zr).astype(BF16)
        proj_ref[:, base + PAIR:base + 2 * PAIR] = jnp.where(low, zr, z).astype(BF16)


def _pre_mixer(x, mod, gains, w, li, *, latent, even, rope):
    n_tok = x.shape[0]
    n_in = w.shape[2]
    tiles_per_seq = DEC_SEQ // TM
    if latent:
        mod_map = lambda i: (li, i // tiles_per_seq, 0, 0)
    else:
        mod_map = lambda i: (li, CTX_ROW, 0, 0)
    in_specs = [
        pl.BlockSpec((TM, D_MODEL), lambda i: (i, 0)),
        pl.BlockSpec((1, 1, 1, 6 * D_MODEL), mod_map),
        pl.BlockSpec((1, 1, D_MODEL), lambda i: (li, 0, 0)),
        _layer_block((D_MODEL, n_in), 0),
    ]
    args = [x, mod, gains, w]
    nb = TM // SEQ
    if latent:
        in_specs += [pl.BlockSpec((TM, 128), lambda i: (i % tiles_per_seq, 0))] * 2
        args += list(rope)
        body = _k1_lat_even if even else _k1_lat_odd
        n_out = IN_EVEN if even else LAT_ODD_COLS
        out_specs = pl.BlockSpec((TM, n_out), lambda i: (i, 0))
        out_shape = jax.ShapeDtypeStruct((n_tok, n_out), BF16)
    elif even:
        body = _k1_ctx_even
        out_specs = [pl.BlockSpec((TM, 4 * MIX_A), lambda i: (i, 0)),
                     pl.BlockSpec((nb, 1, H_B, 2, DH, SEQ), lambda i: (i, 0, 0, 0, 0, 0)),
                     pl.BlockSpec((nb, 1, H_B, SEQ, PAIR), lambda i: (i, 0, 0, 0, 0))]
        out_shape = [jax.ShapeDtypeStruct((n_tok, 4 * MIX_A), BF16),
                     jax.ShapeDtypeStruct((BATCH, 1, H_B, 2, DH, SEQ), F32),
                     jax.ShapeDtypeStruct((BATCH, 1, H_B, SEQ, PAIR), F32)]
    else:
        body = _k1_ctx_odd
        c_spec = pl.BlockSpec((nb, 1, H_C, DH, SEQ), lambda i: (i, 0, 0, 0, 0))
        d_spec = pl.BlockSpec((nb, 1, KV_D, DH, SEQ), lambda i: (i, 0, 0, 0, 0))
        c_shape = jax.ShapeDtypeStruct((BATCH, 1, H_C, DH, SEQ), F32)
        d_shape = jax.ShapeDtypeStruct((BATCH, 1, KV_D, DH, SEQ), F32)
        out_specs = [pl.BlockSpec((TM, D_MODEL), lambda i: (i, 0)), c_spec, c_spec, d_spec, d_spec]
        out_shape = [jax.ShapeDtypeStruct((n_tok, D_MODEL), BF16), c_shape, c_shape, d_shape, d_shape]
    return pl.pallas_call(
        body,
        grid=(n_tok // TM,),
        in_specs=in_specs,
        out_specs=out_specs,
        out_shape=out_shape,
        compiler_params=_cparams(1),
        name=f"pre_mixer_{'lat' if latent else 'ctx'}_{'even' if even else 'odd'}",
    )(*args)


def _even_mixer_kernel(*refs, seq, tq, own_keys, lam_init):
    proj_ref, cw_ref, lq1_ref, lk1_ref, lq2_ref, lk2_ref, subln_ref, ckt_ref, cv_ref, ya_ref, yb_ref = refs

    row = lax.broadcasted_iota(jnp.int32, (seq, 128), 0)
    for j in range(MIX_A // 128):
        cols = slice(j * 128, (j + 1) * 128)
        a_b = proj_ref[:, j * 128:(j + 1) * 128].astype(F32)
        u = (proj_ref[:, MIX_A + j * 128:MIX_A + (j + 1) * 128].astype(F32)
             * proj_ref[:, 2 * MIX_A + j * 128:2 * MIX_A + (j + 1) * 128].astype(F32))
        u_prev = jnp.where(row == 0, 0.0, pltpu.roll(u, 1, axis=0))
        u_next = jnp.where(row == seq - 1, 0.0, pltpu.roll(u, seq - 1, axis=0))
        w = cw_ref[0, :, cols]
        ya_ref[:, cols] = (a_b * (w[0:1] * u_prev + w[1:2] * u + w[2:3] * u_next)).astype(BF16)

    lam = (jnp.exp(jnp.sum(lq1_ref[...] * lk1_ref[...], axis=-1, keepdims=True))
           - jnp.exp(jnp.sum(lq2_ref[...] * lk2_ref[...], axis=-1, keepdims=True)) + lam_init)
    subln = subln_ref[...]
    q_col, k_col, v_col = 3 * MIX_A, 3 * MIX_A + 512, 3 * MIX_A + 1024
    ones = jnp.ones((PAST_LEN, PAIR), BF16)

    def scores(qrows, h, m):
        qp = proj_ref[qrows, q_col + h * PAIR:q_col + (h + 1) * PAIR]
        parts = []
        if own_keys:
            kp = proj_ref[:, k_col + h * PAIR:k_col + (h + 1) * PAIR]
            parts.append(_dot_nt(qp, jnp.where(_keep_mask_bf16(seq, m), kp, jnp.zeros_like(kp))))
        parts.append(_dot(qp, _pad_rows(ckt_ref[0, 0, h, m].astype(BF16), m, 0.0)))
        return parts

    first_map = {}

    def pv(qrows, h, m, e_parts, _):
        v_ops = []
        if own_keys:
            v_own = proj_ref[:, v_col + h * PAIR:v_col + (h + 1) * PAIR]
            v_ops.append(jnp.concatenate([v_own, jnp.ones((seq, PAIR), BF16)], axis=1))
        v_ops.append(jnp.concatenate([cv_ref[0, 0, h].astype(BF16), ones], axis=1))
        o = None
        for e, v_op in zip(e_parts, v_ops):
            t = _dot(e, v_op)
            o = t if o is None else o + t
        attn = o[:, 0:PAIR] * (1.0 / o[:, PAIR:2 * PAIR])
        if m == 0:
            first_map[(qrows.start, h)] = attn
        else:
            y = _rms(first_map.pop((qrows.start, h)) - lam * attn, subln) * (1.0 - lam_init)
            yb_ref[qrows, h * PAIR:(h + 1) * PAIR] = y.astype(BF16)

    tiles = [slice(i * tq, (i + 1) * tq) for i in range(seq // tq)]
    heads = [_Head(functools.partial(scores, qrows, h, m), functools.partial(pv, qrows, h, m))
             for qrows in tiles for h in range(H_B) for m in range(2)]
    _attend(heads, batch=4)


def _even_mixer(proj, conv_w, lq1, lk1, lq2, lk2, subln, kt, v, *, seq, own_keys, lam_init):
    n_tok = proj.shape[0]
    small = lambda a: pl.BlockSpec((1, a.shape[1]), lambda b: (0, 0))
    in_specs = [pl.BlockSpec((seq, proj.shape[1]), lambda b: (b, 0)),
                pl.BlockSpec((1, 3, MIX_A), lambda b: (0, 0, 0)),
                small(lq1), small(lk1), small(lq2), small(lk2), small(subln),
                pl.BlockSpec((1, 1, H_B, 2, DH, PAST_LEN), lambda b: (b, 0, 0, 0, 0, 0)),
                pl.BlockSpec((1, 1, H_B, PAST_LEN, PAIR), lambda b: (b, 0, 0, 0, 0))]
    y_spec = pl.BlockSpec((seq, 512), lambda b: (b, 0))
    y_shape = jax.ShapeDtypeStruct((n_tok, 512), BF16)
    return pl.pallas_call(
        functools.partial(_even_mixer_kernel, seq=seq, tq=min(seq, 256), own_keys=own_keys, lam_init=lam_init),
        grid=(n_tok // seq,),
        in_specs=in_specs,
        out_specs=[y_spec, y_spec],
        out_shape=[y_shape, y_shape],
        compiler_params=_cparams(1),
        name=f"even_mixer_{'lat' if own_keys else 'ctx'}",
    )(proj, conv_w, lq1, lk1, lq2, lk2, subln, kt, v)


def _odd_ctx_kernel(q_ref, nkt_ref, nvt_ref, skt_ref, svt_ref, sink_ref, y_ref):
    def kv_refs(head):
        if head < H_C:
            return nkt_ref.at[0, 0, head], nvt_ref.at[0, 0, head]
        g = (head - H_C) // GQA
        return skt_ref.at[0, 0, g], svt_ref.at[0, 0, g]

    def scores(p, hh):
        kt = kv_refs(2 * p + hh)[0][...].astype(BF16)
        return [_dot(q_ref[:, p * PAIR:(p + 1) * PAIR], _pad_rows(kt, hh, 0.0))]

    def finish(out, p, hh, e_parts, term):
        vt = kv_refs(2 * p + hh)[1][...].astype(BF16)
        out.put(hh, _dot_nt(e_parts[0], _pad_rows(vt, hh, 1.0)), term)

    heads = []
    for p in range((H_C + H_D) // 2):
        out = _PairSink(y_ref, slice(0, SEQ), slice(p * PAIR, (p + 1) * PAIR))
        for hh in range(2):
            d_head = 2 * p + hh - H_C
            sink = sink_ref[d_head:d_head + 1, :] * LOG2E if d_head >= 0 else None
            heads.append(_Head(functools.partial(scores, p, hh), functools.partial(finish, out, p, hh), sink))
    _attend(heads, batch=8)


def _odd_ctx_mixer(q, nkt, nvt, skt, svt, sink):
    n_tok = q.shape[0]
    c_spec = pl.BlockSpec((1, 1, H_C, DH, SEQ), lambda b: (b, 0, 0, 0, 0))
    d_spec = pl.BlockSpec((1, 1, KV_D, DH, SEQ), lambda b: (b, 0, 0, 0, 0))
    return pl.pallas_call(
        _odd_ctx_kernel,
        grid=(n_tok // SEQ,),
        in_specs=[pl.BlockSpec((SEQ, D_MODEL), lambda b: (b, 0)), c_spec, c_spec, d_spec, d_spec,
                  pl.BlockSpec((H_D, 1), lambda b: (0, 0))],
        out_specs=pl.BlockSpec((SEQ, D_MODEL), lambda b: (b, 0)),
        out_shape=jax.ShapeDtypeStruct((n_tok, D_MODEL), BF16),
        compiler_params=_cparams(1),
        name="odd_mixer_ctx",
    )(q, nkt, nvt, skt, svt, sink)


def _na_bias_kernel(rpb_ref, o_ref):
    h = pl.program_id(0)
    qc = lax.broadcasted_iota(jnp.int32, (GRID_W, GRID_W), 0)
    kc = lax.broadcasted_iota(jnp.int32, (GRID_W, GRID_W), 1)
    col_start = jnp.clip(qc - NA_WIN_C // 2, 0, GRID_W - NA_WIN_C)
    col_ok = (kc >= col_start) & (kc < col_start + NA_WIN_C)
    dc = kc - qc + NA_WIN_C - 1
    n_dr, n_dc = 2 * NA_WIN_R - 1, 2 * NA_WIN_C - 1
    neg = jnp.full((GRID_W, GRID_W), NEG, F32)
    toeplitz = []
    for dr in range(n_dr):
        t = neg
        for d in range(n_dc):
            t = jnp.where(dc == d, rpb_ref[(h * n_dr + dr) * n_dc + d] * LOG2E, t)
        toeplitz.append(jnp.where(col_ok, t, NEG))
    rows_per_half = NA_HALF // GRID_W
    key_rows = NA_KEYS // GRID_W
    for half in range(2):
        for rl in range(rows_per_half):
            r = half * rows_per_half + rl
            r_start = min(max(r - NA_WIN_R // 2, 0), GRID_H - NA_WIN_R)
            blocks = []
            for j in range(key_rows):
                rk = half * (GRID_H - key_rows) + j
                inside = r_start <= rk < r_start + NA_WIN_R
                blocks.append(toeplitz[rk - r + NA_WIN_R - 1] if inside else neg)
            o_ref[0, half, rl * GRID_W:(rl + 1) * GRID_W, :] = jnp.concatenate(blocks, axis=1)


def _na_bias(rpb):
    return pl.pallas_call(
        _na_bias_kernel,
        grid=(H_C,),
        in_specs=[pl.BlockSpec(memory_space=pltpu.SMEM)],
        out_specs=pl.BlockSpec((1, 2, NA_HALF, NA_KEYS), lambda h: (h, 0, 0, 0)),
        out_shape=jax.ShapeDtypeStruct((H_C, 2, NA_HALF, NA_KEYS), F32),
        compiler_params=_cparams(1),
        name="na_bias",
    )(rpb.reshape(-1))


def _na_kernel(q_ref, k_ref, v_ref, kct_ref, vct_ref, nb_ref, y_ref):
    def scores(half, hh):
        w0 = half * (DEC_SEQ - NA_KEYS)
        qp = q_ref[half * NA_HALF:(half + 1) * NA_HALF, :]
        kw = k_ref[w0:w0 + NA_KEYS, :]
        s1 = _dot_nt(qp, jnp.where(_keep_mask_bf16(NA_KEYS, hh), kw, jnp.zeros_like(kw))) + nb_ref[hh, half]
        s2 = _dot(qp, _pad_rows(kct_ref[0, 0, hh].astype(BF16), hh, 0.0))
        return [s1, s2]

    def finish(out, half, hh, e_parts, term):
        w0 = half * (DEC_SEQ - NA_KEYS)
        vw = v_ref[w0:w0 + NA_KEYS, :]
        o = (_dot(e_parts[0], jnp.where(_keep_mask_bf16(NA_KEYS, hh), vw, jnp.ones_like(vw)))
             + _dot_nt(e_parts[1], _pad_rows(vct_ref[0, 0, hh].astype(BF16), hh, 1.0)))
        out.put(hh, o, term)

    heads = []
    for half in range(2):
        out = _PairSink(y_ref, slice(half * NA_HALF, (half + 1) * NA_HALF), slice(0, PAIR))
        for hh in range(2):
            heads.append(_Head(functools.partial(scores, half, hh), functools.partial(finish, out, half, hh)))
    _attend(heads, batch=2)


def _na_mixer(proj, cache_kt, cache_vt, na_bias):
    n_tok = proj.shape[0]
    pairs = H_C // 2
    col = lambda base: pl.BlockSpec((DEC_SEQ, PAIR), lambda hp, b: (b, base + hp))
    cache = pl.BlockSpec((1, 1, 2, DH, PAST_LEN), lambda hp, b: (b, 0, hp, 0, 0))
    return pl.pallas_call(
        _na_kernel,
        grid=(pairs, DEC_BATCH),
        in_specs=[col(0), col(pairs), col(2 * pairs), cache, cache,
                  pl.BlockSpec((2, 2, NA_HALF, NA_KEYS), lambda hp, b: (hp, 0, 0, 0))],
        out_specs=pl.BlockSpec((DEC_SEQ, PAIR), lambda hp, b: (b, hp)),
        out_shape=jax.ShapeDtypeStruct((n_tok, 512), BF16),
        compiler_params=_cparams(2),
        name="na_mixer",
    )(proj, proj, proj, cache_kt, cache_vt, na_bias)


def _swa_kernel(q_ref, kv_ref, kct_ref, vct_ref, sink_ref, y_ref):
    n_groups = DEC_SEQ // SWA_QROWS
    ql = lax.broadcasted_iota(jnp.int32, (SWA_QROWS, SWA_BAND), 0)
    kj = lax.broadcasted_iota(jnp.int32, (SWA_QROWS, SWA_BAND), 1)

    def band_bias(first_key_minus_first_query):
        return jnp.where(jnp.abs(kj + first_key_minus_first_query - ql) <= SWA_WINDOW, 0.0, NEG)

    bias_first, bias_mid, bias_last = band_bias(0), band_bias(-SWA_BLOCK), band_bias(-2 * SWA_BLOCK)
    def band_start(grp):
        return min(max(grp * SWA_QROWS - SWA_BLOCK, 0), DEC_SEQ - SWA_BAND)

    def scores(g, grp, p, hh):
        start = band_start(grp)
        bias = bias_first if grp == 0 else (bias_last if grp == n_groups - 1 else bias_mid)
        qp = q_ref[grp * SWA_QROWS:(grp + 1) * SWA_QROWS, p * PAIR:(p + 1) * PAIR]
        kb = kv_ref[start:start + SWA_BAND, g * PAIR:(g + 1) * PAIR]
        s1 = _dot_nt(qp, jnp.where(_keep_mask_bf16(SWA_BAND, hh), kb, jnp.zeros_like(kb))) + bias
        s2 = _dot(qp, _pad_rows(kct_ref[0, 0, g].astype(BF16), hh, 0.0))
        return [s1, s2]

    def finish(out, g, grp, hh, e_parts, term):
        start = band_start(grp)
        vb = kv_ref[start:start + SWA_BAND, (KV_D + g) * PAIR:(KV_D + g + 1) * PAIR]
        o = (_dot(e_parts[0], jnp.where(_keep_mask_bf16(SWA_BAND, hh), vb, jnp.ones_like(vb)))
             + _dot_nt(e_parts[1], _pad_rows(vct_ref[0, 0, g].astype(BF16), hh, 1.0)))
        out.put(hh, o, term)

    heads = []
    for g in range(KV_D):
        for grp in range(n_groups):
            for j in range(GQA // 2):
                p = g * (GQA // 2) + j
                out = _PairSink(y_ref, slice(grp * SWA_QROWS, (grp + 1) * SWA_QROWS), slice(p * PAIR, (p + 1) * PAIR))
                for hh in range(2):
                    sink = sink_ref[2 * p + hh:2 * p + hh + 1, :] * LOG2E
                    heads.append(_Head(functools.partial(scores, g, grp, p, hh),
                                       functools.partial(finish, out, g, grp, hh), sink))
    _attend(heads, batch=4)


def _swa_mixer(proj, cache_kt, cache_vt, sink):
    n_tok = proj.shape[0]
    cache = pl.BlockSpec((1, 1, KV_D, DH, PAST_LEN), lambda b: (b, 0, 0, 0, 0))
    return pl.pallas_call(
        _swa_kernel,
        grid=(DEC_BATCH,),
        in_specs=[pl.BlockSpec((DEC_SEQ, 512), lambda b: (b, 3)),
                  pl.BlockSpec((DEC_SEQ, 512), lambda b: (b, 4)),
                  cache, cache,
                  pl.BlockSpec((H_D, 1), lambda b: (0, 0))],
        out_specs=pl.BlockSpec((DEC_SEQ, 512), lambda b: (b, 0)),
        out_shape=jax.ShapeDtypeStruct((n_tok, 512), BF16),
        compiler_params=_cparams(1),
        name="swa_mixer",
    )(proj, proj, cache_kt, cache_vt, sink)


def _post_mixer_kernel(ma_ref, mb_ref, x_ref, mod_ref, gpost_ref, gpre_ref, gmlp_ref, wo_ref, w1_ref, w2_ref, o_ref):
    half = D_MODEL // 2
    mod = mod_ref[0, 0]
    y = _dot(ma_ref[...], wo_ref[0, 0:half, :]) + _dot(mb_ref[...], wo_ref[0, half:D_MODEL, :])
    x1 = x_ref[...] + mod[:, 2 * D_MODEL:3 * D_MODEL] * _rms(y, gpost_ref[0])
    h = _norm_mod(x1, gpre_ref[0], mod, 3 * D_MODEL).astype(BF16)
    acc = jnp.zeros((TM, D_MODEL), F32)
    for c in range(D_FF // FF_CHUNK):
        f = _dot(h, w1_ref[0, :, c * FF_CHUNK:(c + 1) * FF_CHUNK])
        f = jnp.square(jnp.maximum(f, 0.0)).astype(BF16)
        acc = acc + _dot(f, w2_ref[0, c * FF_CHUNK:(c + 1) * FF_CHUNK, :])
    o_ref[...] = x1 + mod[:, 5 * D_MODEL:6 * D_MODEL] * _rms(acc, gmlp_ref[0])


def _post_mixer(mix_a, a_blk, mix_b, b_blk, x, mod, g_post, g_pre, g_mlp, w_out, w1, w2, li, *, latent):
    n_tok = x.shape[0]
    half = D_MODEL // 2
    tiles_per_seq = DEC_SEQ // TM
    if latent:
        mod_map = lambda i: (li, i // tiles_per_seq, 0, 0)
    else:
        mod_map = lambda i: (li, CTX_ROW, 0, 0)
    gain = pl.BlockSpec((1, 1, D_MODEL), lambda i: (li, 0, 0))
    return pl.pallas_call(
        _post_mixer_kernel,
        grid=(n_tok // TM,),
        in_specs=[pl.BlockSpec((TM, half), lambda i: (i, a_blk)),
                  pl.BlockSpec((TM, half), lambda i: (i, b_blk)),
                  pl.BlockSpec((TM, D_MODEL), lambda i: (i, 0)),
                  pl.BlockSpec((1, 1, 1, 6 * D_MODEL), mod_map),
                  gain, gain, gain,
                  _layer_block((D_MODEL, D_MODEL), li), _layer_block((D_MODEL, D_FF), li),
                  _layer_block((D_FF, D_MODEL), li)],
        out_specs=pl.BlockSpec((TM, D_MODEL), lambda i: (i, 0)),
        out_shape=jax.ShapeDtypeStruct((n_tok, D_MODEL), F32),
        compiler_params=_cparams(1),
        name=f"post_mixer_{'lat' if latent else 'ctx'}",
    )(mix_a, mix_b, x, mod, g_post, g_pre, g_mlp, w_out, w1, w2)


def _rope_tables():
    t = jnp.arange(DEC_SEQ)
    rows = (t // GRID_W).astype(F32)
    cols = (t % GRID_W).astype(F32)
    q4 = DH // 4
    inv = 1.0 / (ROPE_BASE ** (jnp.arange(q4, dtype=F32) / q4))
    ar, ac = rows[:, None] * inv, cols[:, None] * inv
    cos64 = jnp.concatenate([jnp.cos(ar), jnp.cos(ar), jnp.cos(ac), jnp.cos(ac)], axis=1)
    sin64 = jnp.concatenate([-jnp.sin(ar), jnp.sin(ar), -jnp.sin(ac), jnp.sin(ac)], axis=1)
    return jnp.tile(cos64, (1, 2)), jnp.tile(sin64, (1, 2))


def kernel(x_prompt, x_sample, cache_diff_k, cache_diff_v, cache_na_k, cache_na_v, cache_swa_k, cache_swa_v, c, c_ctx, mod_w, mod_b, norm_mix_pre, norm_mix_post, norm_mlp_pre, norm_mlp_post, w_in_even, conv_w, lambda_q1, lambda_k1, lambda_q2, lambda_k2, subln, w_in_odd, rpb, sink, w_out, mlp_w1, mlp_w2):
    cond = jnp.zeros((MOD_ROWS, D_MODEL), F32).at[:DEC_BATCH].set(c).at[CTX_ROW].set(c_ctx)
    mod = _modulation(cond, mod_w, mod_b).reshape(DEPTH, MOD_ROWS, 1, 6 * D_MODEL)
    rope = _rope_tables()
    na_bias = _na_bias(rpb[0])

    t_minor = lambda a: jnp.swapaxes(a, -1, -2)

    w_in = [w_in_even.astype(BF16), w_in_odd.astype(BF16)]
    w_o, w_1, w_2 = w_out.astype(BF16), mlp_w1.astype(BF16), mlp_w2.astype(BF16)
    gains = lambda a: a.reshape(DEPTH, 1, D_MODEL)
    g_pre_mix, g_post_mix, g_pre_mlp, g_post_mlp = (gains(norm_mix_pre), gains(norm_mix_post), gains(norm_mlp_pre),
                                                    gains(norm_mlp_post))

    xp = x_prompt.reshape(BATCH * SEQ, D_MODEL)
    xs = x_sample.reshape(DEC_BATCH * DEC_SEQ, D_MODEL)
    lam_init0 = 0.8 - 0.6 * math.exp(-0.3 * 0)
    lam_args = (conv_w, lambda_q1, lambda_k1, lambda_q2, lambda_k2, subln)
    sink_col = sink[0].reshape(H_D, 1)

    def post(li, ma, a_blk, mb, b_blk, x, latent):
        return _post_mixer(ma, a_blk, mb, b_blk, x, mod, g_post_mix, g_pre_mlp, g_post_mlp, w_o, w_1, w_2, li,
                           latent=latent)

    proj, new_diff_kt, new_diff_v = _pre_mixer(xp, mod, g_pre_mix, w_in[0], 0, latent=False, even=True, rope=None)
    ya, yb = _even_mixer(proj, *lam_args, new_diff_kt, new_diff_v, seq=SEQ, own_keys=False, lam_init=lam_init0)
    xp = post(0, ya, 0, yb, 0, xp, False)
    proj = _pre_mixer(xs, mod, g_pre_mix, w_in[0], 0, latent=True, even=True, rope=rope)
    ya, yb = _even_mixer(proj, *lam_args, t_minor(cache_diff_k), cache_diff_v, seq=DEC_SEQ, own_keys=True,
                         lam_init=lam_init0)
    xs = post(0, ya, 0, yb, 0, xs, True)

    q, new_na_kt, new_na_vt, new_swa_kt, new_swa_vt = _pre_mixer(
        xp, mod, g_pre_mix, w_in[1], 1, latent=False, even=False, rope=None)
    y = _odd_ctx_mixer(q, new_na_kt, new_na_vt, new_swa_kt, new_swa_vt, sink_col)
    xp = post(1, y, 0, y, 1, xp, False)
    proj = _pre_mixer(xs, mod, g_pre_mix, w_in[1], 1, latent=True, even=False, rope=rope)
    yc = _na_mixer(proj, t_minor(cache_na_k), t_minor(cache_na_v), na_bias)
    yd = _swa_mixer(proj, t_minor(cache_swa_k), t_minor(cache_swa_v), sink_col)
    xs = post(1, yc, 0, yd, 0, xs, True)

    return (xp.reshape(BATCH, SEQ, D_MODEL), xs.reshape(DEC_BATCH, DEC_SEQ, D_MODEL),
            t_minor(new_diff_kt), new_diff_v, t_minor(new_na_kt), t_minor(new_na_vt),
            t_minor(new_swa_kt), t_minor(new_swa_vt))
```

```python
import functools
import math

import jax
import jax.numpy as jnp
from jax import lax
from jax.experimental import pallas as pl
from jax.experimental.pallas import tpu as pltpu

F32 = jnp.float32
BF16 = jnp.bfloat16

D_MODEL = 1024
BATCH = 32
SEQ = 256
DEPTH = 2
DEC_BATCH = 8
DEC_SEQ = 1024
PAST_LEN = 256
GRID_W = 64
GRID_H = DEC_SEQ // GRID_W
MIX_A = 512
H_B = 4
DH = 64
PAIR = 2 * DH
H_C = 8
H_D = 8
KV_D = 2
GQA = H_D // KV_D
IN_EVEN = 3072
IN_ODD = 2304
D_FF = 4096
NA_WIN_R = 8
NA_WIN_C = 16
SWA_BLOCK = 128
SWA_WINDOW = 128
ROPE_BASE = 10000.0
EPS = 1e-6
NEG = -1e30
LOG2E = math.log2(math.e)
Q_SCALE = DH ** -0.5 * LOG2E

MOD_ROWS = 16
CTX_ROW = DEC_BATCH
V7X_VMEM_BYTES = 64 * 1024 * 1024
VMEM_LIMIT = V7X_VMEM_BYTES - 8 * 1024 * 1024

TM = 512
FF_CHUNK = 1024
MOD_TN = 1536
NA_HALF = 512
NA_KEYS = 768
SWA_QROWS = 2 * SWA_BLOCK
SWA_BAND = 4 * SWA_BLOCK
LAT_ODD_COLS = IN_ODD + 2 * PAIR


def _cparams(n_axes):
    return pltpu.CompilerParams(dimension_semantics=("arbitrary",) * n_axes, vmem_limit_bytes=VMEM_LIMIT)


def _rms(x, g):
    return x * lax.rsqrt(jnp.mean(x * x, axis=-1, keepdims=True) + EPS) * g


def _dot(a, b):
    return jnp.dot(a, b, preferred_element_type=F32)


def _dot_nt(a, b):
    return lax.dot_general(a, b, (((1,), (1,)), ((), ())), preferred_element_type=F32)


def _layer_block(shape, li):
    zeros = (0,) * len(shape)
    return pl.BlockSpec((1,) + tuple(shape), lambda *_: (li,) + zeros, pipeline_mode=pl.Buffered(1))


def _low_half(shape):
    return (lax.broadcasted_iota(jnp.int32, shape, len(shape) - 1) & DH) == 0


def _keep_mask_bf16(rows, hh):
    lane = lax.broadcasted_iota(jnp.int32, (rows, PAIR), 1)
    half = (lane & DH).astype(F32).astype(BF16)
    return (half == 0) if hh == 0 else (half != 0)


def _pad_rows(t, hh, fill):
    other = jnp.full(t.shape, fill, t.dtype)
    return jnp.concatenate([t, other] if hh == 0 else [other, t], axis=0)


class _Memo(dict):
    def get_or_build(self, key, build):
        if key not in self:
            self[key] = build()
        return self[key]


def _merge_pair(o0, o1, extra_den=None):
    low = _low_half(o0.shape)
    num = jnp.where(low, o0, o1)
    den = pltpu.roll(jnp.where(low, o1, o0), DH, axis=1)
    if extra_den is not None:
        den = den + jnp.where(low, extra_den[0], extra_den[1])
    return num * (1.0 / den)


class _Head:
    def __init__(self, scores, finish, sink=None):
        self.scores, self.finish, self.sink = scores, finish, sink


def _attend(heads, batch):
    batches = [heads[i:i + batch] for i in range(0, len(heads), batch)]
    cur = [hd.scores() for hd in batches[0]]
    for bi, group in enumerate(batches):
        nxt = [hd.scores() for hd in batches[bi + 1]] if bi + 1 < len(batches) else None
        exps = []
        for hd, parts in zip(group, cur):
            chunks = [s[:, j * 128:(j + 1) * 128] for s in parts for j in range(s.shape[1] // 128)]
            m_el = functools.reduce(jnp.maximum, chunks)
            if hd.sink is not None:
                m_el = jnp.maximum(m_el, hd.sink)
            mx = jnp.broadcast_to(jnp.max(m_el, axis=-1, keepdims=True), m_el.shape)
            term = None if hd.sink is None else jnp.exp2(hd.sink - mx)
            e_parts = [jnp.concatenate([jnp.exp2(s[:, j * 128:(j + 1) * 128] - mx)
                                        for j in range(s.shape[1] // 128)], axis=1).astype(BF16) for s in parts]
            exps.append((e_parts, term))
        for hd, (e_parts, term) in zip(group, exps):
            hd.finish(e_parts, term)
        cur = nxt


class _PairSink:
    def __init__(self, y_ref, rows, cols):
        self.y_ref, self.rows, self.cols, self.first = y_ref, rows, cols, None

    def put(self, hh, o, term):
        if hh == 0:
            self.first = (o, term)
        else:
            o0, term0 = self.first
            extra = None if term is None else (term0, term)
            self.y_ref[self.rows, self.cols] = _merge_pair(o0, o, extra).astype(BF16)


def _mod_kernel(cond_ref, w_ref, b_ref, o_ref):
    c = cond_ref[...]
    s = c * (1.0 / (1.0 + jnp.exp(-c)))
    o_ref[0] = _dot(s.astype(BF16), w_ref[0].astype(BF16)) + b_ref[0]


def _modulation(cond, mod_w, mod_b):
    return pl.pallas_call(
        _mod_kernel,
        grid=(DEPTH, 6 * D_MODEL // MOD_TN),
        in_specs=[
            pl.BlockSpec((MOD_ROWS, D_MODEL), lambda l, j: (0, 0)),
            pl.BlockSpec((1, D_MODEL, MOD_TN), lambda l, j: (l, 0, j)),
            pl.BlockSpec((1, 1, MOD_TN), lambda l, j: (l, 0, j)),
        ],
        out_specs=pl.BlockSpec((1, MOD_ROWS, MOD_TN), lambda l, j: (l, 0, j)),
        out_shape=jax.ShapeDtypeStruct((DEPTH, MOD_ROWS, 6 * D_MODEL), F32),
        compiler_params=_cparams(2),
        name="adaln_modulation",
    )(cond, mod_w, mod_b.reshape(DEPTH, 1, 6 * D_MODEL))


def _norm_mod(x, g, mod, shift_col):
    sh = mod[:, shift_col:shift_col + D_MODEL]
    sc = mod[:, shift_col + D_MODEL:shift_col + 2 * D_MODEL]
    return _rms(x, g) * (1.0 + sc) + sh


def _rope(z, cos, sin):
    outs = []
    for j in range(z.shape[1] // 128):
        zj = z[:, j * 128:(j + 1) * 128]
        lane = lax.broadcasted_iota(jnp.int32, zj.shape, 1)
        partner = jnp.where((lane & 16) == 0, pltpu.roll(zj, 128 - 16, axis=1), pltpu.roll(zj, 16, axis=1))
        outs.append(zj * cos + partner * sin)
    return outs[0] if len(outs) == 1 else jnp.concatenate(outs, axis=1)


def _store_heads_transposed(dst, bi, pc_rows, n_pairs):
    for p in range(n_pairs):
        t = pc_rows[:, p * PAIR:(p + 1) * PAIR].T
        dst[bi, 0, 2 * p] = t[0:DH]
        dst[bi, 0, 2 * p + 1] = t[DH:PAIR]


def _k1_ctx_even(x_ref, mod_ref, g_ref, w_ref, proj_ref, dkt_ref, dv_ref):
    h = _norm_mod(x_ref[...], g_ref[0], mod_ref[0, 0], 0).astype(BF16)
    for c in range(IN_EVEN // 512):
        pc = _dot(h, w_ref[0, :, c * 512:(c + 1) * 512])
        if c < 3:
            proj_ref[:, c * 512:(c + 1) * 512] = pc.astype(BF16)
        elif c == 3:
            proj_ref[:, c * 512:(c + 1) * 512] = (pc * Q_SCALE).astype(BF16)
        for bi in range(TM // SEQ):
            rows = slice(bi * SEQ, (bi + 1) * SEQ)
            if c == 4:
                for hh in range(H_B):
                    t = pc[rows, hh * PAIR:(hh + 1) * PAIR].T
                    dkt_ref[bi, 0, hh, 0] = t[0:DH]
                    dkt_ref[bi, 0, hh, 1] = t[DH:PAIR]
            if c == 5:
                for hh in range(H_B):
                    dv_ref[bi, 0, hh] = pc[rows, hh * PAIR:(hh + 1) * PAIR]


def _k1_lat_even(x_ref, mod_ref, g_ref, w_ref, cos_ref, sin_ref, proj_ref):
    h = _norm_mod(x_ref[...], g_ref[0], mod_ref[0, 0], 0).astype(BF16)
    for c in range(IN_EVEN // 512):
        pc = _dot(h, w_ref[0, :, c * 512:(c + 1) * 512])
        if c == 3:
            pc = _rope(pc * Q_SCALE, cos_ref[...], sin_ref[...])
        elif c == 4:
            pc = _rope(pc, cos_ref[...], sin_ref[...])
        proj_ref[:, c * 512:(c + 1) * 512] = pc.astype(BF16)


def _k1_ctx_odd(x_ref, mod_ref, g_ref, w_ref, q_ref, nkt_ref, nvt_ref, skt_ref, svt_ref):
    h = _norm_mod(x_ref[...], g_ref[0], mod_ref[0, 0], 0).astype(BF16)
    for c in range(4):
        pc = _dot(h, w_ref[0, :, c * 512:(c + 1) * 512])
        if c == 0 or c == 3:
            q_ref[:, (c // 3) * 512:(c // 3 + 1) * 512] = (pc * Q_SCALE).astype(BF16)
        else:
            dst = nkt_ref if c == 1 else nvt_ref
            for bi in range(TM // SEQ):
                _store_heads_transposed(dst, bi, pc[bi * SEQ:(bi + 1) * SEQ], H_C // 2)
    pc = _dot(h, w_ref[0, :, 2048:IN_ODD])
    for bi in range(TM // SEQ):
        _store_heads_transposed(skt_ref, bi, pc[bi * SEQ:(bi + 1) * SEQ, 0:PAIR], 1)
        _store_heads_transposed(svt_ref, bi, pc[bi * SEQ:(bi + 1) * SEQ, PAIR:2 * PAIR], 1)


def _k1_lat_odd(x_ref, mod_ref, g_ref, w_ref, cos_ref, sin_ref, proj_ref):
    h = _norm_mod(x_ref[...], g_ref[0], mod_ref[0, 0], 0).astype(BF16)
    for c in range(4):
        pc = _dot(h, w_ref[0, :, c * 512:(c + 1) * 512])
        if c == 0:
            pc = pc * Q_SCALE
        elif c == 3:
            pc = _rope(pc * Q_SCALE, cos_ref[...], sin_ref[...])
        proj_ref[:, c * 512:(c + 1) * 512] = pc.astype(BF16)
    pc = _dot(h, w_ref[0, :, 2048:IN_ODD])
    low = _low_half((TM, PAIR))
    for j, z in enumerate((_rope(pc[:, 0:PAIR], cos_ref[...], sin_ref[...]), pc[:, PAIR:2 * PAIR])):
        zr = pltpu.roll(z, DH, axis=1)
        base = 2048 + j * 2 * PAIR
        proj_ref[:, base:base + PAIR] = jnp.where(low, z, zr).astype(BF16)
        proj_ref[:, base + PAIR:base + 2 * PAIR] = jnp.where(low, zr, z).astype(BF16)


def _pre_mixer(x, mod, gains, w, li, *, latent, even, rope):
    n_tok = x.shape[0]
    n_in = w.shape[2]
    tiles_per_seq = DEC_SEQ // TM
    if latent:
        mod_map = lambda i: (li, i // tiles_per_seq, 0, 0)
    else:
        mod_map = lambda i: (li, CTX_ROW, 0, 0)
    in_specs = [
        pl.BlockSpec((TM, D_MODEL), lambda i: (i, 0)),
        pl.BlockSpec((1, 1, 1, 6 * D_MODEL), mod_map),
        pl.BlockSpec((1, 1, D_MODEL), lambda i: (li, 0, 0)),
        _layer_block((D_MODEL, n_in), 0),
    ]
    args = [x, mod, gains, w]
    nb = TM // SEQ
    if latent:
        in_specs += [pl.BlockSpec((TM, 128), lambda i: (i % tiles_per_seq, 0))] * 2
        args += list(rope)
        body = _k1_lat_even if even else _k1_lat_odd
        n_out = IN_EVEN if even else LAT_ODD_COLS
        out_specs = pl.BlockSpec((TM, n_out), lambda i: (i, 0))
        out_shape = jax.ShapeDtypeStruct((n_tok, n_out), BF16)
    elif even:
        body = _k1_ctx_even
        out_specs = [pl.BlockSpec((TM, 4 * MIX_A), lambda i: (i, 0)),
                     pl.BlockSpec((nb, 1, H_B, 2, DH, SEQ), lambda i: (i, 0, 0, 0, 0, 0)),
                     pl.BlockSpec((nb, 1, H_B, SEQ, PAIR), lambda i: (i, 0, 0, 0, 0))]
        out_shape = [jax.ShapeDtypeStruct((n_tok, 4 * MIX_A), BF16),
                     jax.ShapeDtypeStruct((BATCH, 1, H_B, 2, DH, SEQ), F32),
                     jax.ShapeDtypeStruct((BATCH, 1, H_B, SEQ, PAIR), F32)]
    else:
        body = _k1_ctx_odd
        c_spec = pl.BlockSpec((nb, 1, H_C, DH, SEQ), lambda i: (i, 0, 0, 0, 0))
        d_spec = pl.BlockSpec((nb, 1, KV_D, DH, SEQ), lambda i: (i, 0, 0, 0, 0))
        c_shape = jax.ShapeDtypeStruct((BATCH, 1, H_C, DH, SEQ), F32)
        d_shape = jax.ShapeDtypeStruct((BATCH, 1, KV_D, DH, SEQ), F32)
        out_specs = [pl.BlockSpec((TM, D_MODEL), lambda i: (i, 0)), c_spec, c_spec, d_spec, d_spec]
        out_shape = [jax.ShapeDtypeStruct((n_tok, D_MODEL), BF16), c_shape, c_shape, d_shape, d_shape]
    return pl.pallas_call(
        body,
        grid=(n_tok // TM,),
        in_specs=in_specs,
        out_specs=out_specs,
        out_shape=out_shape,
        compiler_params=_cparams(1),
        name=f"pre_mixer_{'lat' if latent else 'ctx'}_{'even' if even else 'odd'}",
    )(*args)


def _even_mixer_kernel(*refs, seq, tq, own_keys, lam_init):
    proj_ref, cw_ref, lq1_ref, lk1_ref, lq2_ref, lk2_ref, subln_ref, ckt_ref, cv_ref, ya_ref, yb_ref = refs

    row = lax.broadcasted_iota(jnp.int32, (seq, 128), 0)
    for j in range(MIX_A // 128):
        cols = slice(j * 128, (j + 1) * 128)
        a_b = proj_ref[:, j * 128:(j + 1) * 128].astype(F32)
        u = (proj_ref[:, MIX_A + j * 128:MIX_A + (j + 1) * 128].astype(F32)
             * proj_ref[:, 2 * MIX_A + j * 128:2 * MIX_A + (j + 1) * 128].astype(F32))
        u_prev = jnp.where(row == 0, 0.0, pltpu.roll(u, 1, axis=0))
        u_next = jnp.where(row == seq - 1, 0.0, pltpu.roll(u, seq - 1, axis=0))
        w = cw_ref[0, :, cols]
        ya_ref[:, cols] = (a_b * (w[0:1] * u_prev + w[1:2] * u + w[2:3] * u_next)).astype(BF16)

    lam = (jnp.exp(jnp.sum(lq1_ref[...] * lk1_ref[...], axis=-1, keepdims=True))
           - jnp.exp(jnp.sum(lq2_ref[...] * lk2_ref[...], axis=-1, keepdims=True)) + lam_init)
    subln = subln_ref[...]
    q_col, k_col, v_col = 3 * MIX_A, 3 * MIX_A + 512, 3 * MIX_A + 1024
    ones = jnp.ones((PAST_LEN, PAIR), BF16)

    memo = _Memo()

    def own_k(h, m):
        kp = proj_ref[:, k_col + h * PAIR:k_col + (h + 1) * PAIR]
        return jnp.where(_keep_mask_bf16(seq, m), kp, jnp.zeros_like(kp))

    def value_ops(h):
        v_ops = []
        if own_keys:
            v_own = proj_ref[:, v_col + h * PAIR:v_col + (h + 1) * PAIR]
            v_ops.append(jnp.concatenate([v_own, jnp.ones((seq, PAIR), BF16)], axis=1))
        v_ops.append(jnp.concatenate([cv_ref[0, 0, h].astype(BF16), ones], axis=1))
        return v_ops

    def scores(qrows, h, m):
        qp = proj_ref[qrows, q_col + h * PAIR:q_col + (h + 1) * PAIR]
        parts = []
        if own_keys:
            parts.append(_dot_nt(qp, memo.get_or_build(("k", h, m), functools.partial(own_k, h, m))))
        ckt = memo.get_or_build(("ck", h, m), lambda: _pad_rows(ckt_ref[0, 0, h, m].astype(BF16), m, 0.0))
        parts.append(_dot(qp, ckt))
        return parts

    first_map = {}

    def pv(qrows, h, m, e_parts, _):
        v_ops = memo.get_or_build(("v", h), functools.partial(value_ops, h))
        o = None
        for e, v_op in zip(e_parts, v_ops):
            t = _dot(e, v_op)
            o = t if o is None else o + t
        attn = o[:, 0:PAIR] * (1.0 / o[:, PAIR:2 * PAIR])
        if m == 0:
            first_map[(qrows.start, h)] = attn
        else:
            y = _rms(first_map.pop((qrows.start, h)) - lam * attn, subln) * (1.0 - lam_init)
            yb_ref[qrows, h * PAIR:(h + 1) * PAIR] = y.astype(BF16)

    tiles = [slice(i * tq, (i + 1) * tq) for i in range(seq // tq)]
    heads = [_Head(functools.partial(scores, qrows, h, m), functools.partial(pv, qrows, h, m))
             for qrows in tiles for h in range(H_B) for m in range(2)]
    _attend(heads, batch=4)


def _even_mixer(proj, conv_w, lq1, lk1, lq2, lk2, subln, kt, v, *, seq, own_keys, lam_init):
    n_tok = proj.shape[0]
    small = lambda a: pl.BlockSpec((1, a.shape[1]), lambda b: (0, 0))
    in_specs = [pl.BlockSpec((seq, proj.shape[1]), lambda b: (b, 0)),
                pl.BlockSpec((1, 3, MIX_A), lambda b: (0, 0, 0)),
                small(lq1), small(lk1), small(lq2), small(lk2), small(subln),
                pl.BlockSpec((1, 1, H_B, 2, DH, PAST_LEN), lambda b: (b, 0, 0, 0, 0, 0)),
                pl.BlockSpec((1, 1, H_B, PAST_LEN, PAIR), lambda b: (b, 0, 0, 0, 0))]
    y_spec = pl.BlockSpec((seq, 512), lambda b: (b, 0))
    y_shape = jax.ShapeDtypeStruct((n_tok, 512), BF16)
    return pl.pallas_call(
        functools.partial(_even_mixer_kernel, seq=seq, tq=min(seq, 256), own_keys=own_keys, lam_init=lam_init),
        grid=(n_tok // seq,),
        in_specs=in_specs,
        out_specs=[y_spec, y_spec],
        out_shape=[y_shape, y_shape],
        compiler_params=_cparams(1),
        name=f"even_mixer_{'lat' if own_keys else 'ctx'}",
    )(proj, conv_w, lq1, lk1, lq2, lk2, subln, kt, v)


def _odd_ctx_kernel(q_ref, nkt_ref, nvt_ref, skt_ref, svt_ref, sink_ref, y_ref):
    def kv_refs(head):
        if head < H_C:
            return nkt_ref.at[0, 0, head], nvt_ref.at[0, 0, head]
        g = (head - H_C) // GQA
        return skt_ref.at[0, 0, g], svt_ref.at[0, 0, g]

    memo = _Memo()

    def kv_key(head):
        return head if head < H_C else H_C + (head - H_C) // GQA

    def scores(p, hh):
        head = 2 * p + hh
        kt = memo.get_or_build(("k", kv_key(head), hh),
                               lambda: _pad_rows(kv_refs(head)[0][...].astype(BF16), hh, 0.0))
        return [_dot(q_ref[:, p * PAIR:(p + 1) * PAIR], kt)]

    def finish(out, p, hh, e_parts, term):
        head = 2 * p + hh
        vt = memo.get_or_build(("v", kv_key(head), hh),
                               lambda: _pad_rows(kv_refs(head)[1][...].astype(BF16), hh, 1.0))
        out.put(hh, _dot_nt(e_parts[0], vt), term)

    heads = []
    for p in range((H_C + H_D) // 2):
        out = _PairSink(y_ref, slice(0, SEQ), slice(p * PAIR, (p + 1) * PAIR))
        for hh in range(2):
            d_head = 2 * p + hh - H_C
            sink = sink_ref[d_head:d_head + 1, :] * LOG2E if d_head >= 0 else None
            heads.append(_Head(functools.partial(scores, p, hh), functools.partial(finish, out, p, hh), sink))
    _attend(heads, batch=8)


def _odd_ctx_mixer(q, nkt, nvt, skt, svt, sink):
    n_tok = q.shape[0]
    c_spec = pl.BlockSpec((1, 1, H_C, DH, SEQ), lambda b: (b, 0, 0, 0, 0))
    d_spec = pl.BlockSpec((1, 1, KV_D, DH, SEQ), lambda b: (b, 0, 0, 0, 0))
    return pl.pallas_call(
        _odd_ctx_kernel,
        grid=(n_tok // SEQ,),
        in_specs=[pl.BlockSpec((SEQ, D_MODEL), lambda b: (b, 0)), c_spec, c_spec, d_spec, d_spec,
                  pl.BlockSpec((H_D, 1), lambda b: (0, 0))],
        out_specs=pl.BlockSpec((SEQ, D_MODEL), lambda b: (b, 0)),
        out_shape=jax.ShapeDtypeStruct((n_tok, D_MODEL), BF16),
        compiler_params=_cparams(1),
        name="odd_mixer_ctx",
    )(q, nkt, nvt, skt, svt, sink)


def _na_bias_kernel(rpb_ref, o_ref):
    h = pl.program_id(0)
    qc = lax.broadcasted_iota(jnp.int32, (GRID_W, GRID_W), 0)
    kc = lax.broadcasted_iota(jnp.int32, (GRID_W, GRID_W), 1)
    col_start = jnp.clip(qc - NA_WIN_C // 2, 0, GRID_W - NA_WIN_C)
    col_ok = (kc >= col_start) & (kc < col_start + NA_WIN_C)
    dc = kc - qc + NA_WIN_C - 1
    n_dr, n_dc = 2 * NA_WIN_R - 1, 2 * NA_WIN_C - 1
    neg = jnp.full((GRID_W, GRID_W), NEG, F32)
    toeplitz = []
    for dr in range(n_dr):
        t = neg
        for d in range(n_dc):
            t = jnp.where(dc == d, rpb_ref[(h * n_dr + dr) * n_dc + d] * LOG2E, t)
        toeplitz.append(jnp.where(col_ok, t, NEG))
    rows_per_half = NA_HALF // GRID_W
    key_rows = NA_KEYS // GRID_W
    for half in range(2):
        for rl in range(rows_per_half):
            r = half * rows_per_half + rl
            r_start = min(max(r - NA_WIN_R // 2, 0), GRID_H - NA_WIN_R)
            blocks = []
            for j in range(key_rows):
                rk = half * (GRID_H - key_rows) + j
                inside = r_start <= rk < r_start + NA_WIN_R
                blocks.append(toeplitz[rk - r + NA_WIN_R - 1] if inside else neg)
            o_ref[0, half, rl * GRID_W:(rl + 1) * GRID_W, :] = jnp.concatenate(blocks, axis=1)


def _na_bias(rpb):
    return pl.pallas_call(
        _na_bias_kernel,
        grid=(H_C,),
        in_specs=[pl.BlockSpec(memory_space=pltpu.SMEM)],
        out_specs=pl.BlockSpec((1, 2, NA_HALF, NA_KEYS), lambda h: (h, 0, 0, 0)),
        out_shape=jax.ShapeDtypeStruct((H_C, 2, NA_HALF, NA_KEYS), F32),
        compiler_params=_cparams(1),
        name="na_bias",
    )(rpb.reshape(-1))


def _na_kernel(q_ref, k_ref, v_ref, kct_ref, vct_ref, nb_ref, y_ref):
    def scores(half, hh):
        w0 = half * (DEC_SEQ - NA_KEYS)
        qp = q_ref[half * NA_HALF:(half + 1) * NA_HALF, :]
        kw = k_ref[w0:w0 + NA_KEYS, :]
        s1 = _dot_nt(qp, jnp.where(_keep_mask_bf16(NA_KEYS, hh), kw, jnp.zeros_like(kw))) + nb_ref[hh, half]
        s2 = _dot(qp, _pad_rows(kct_ref[0, 0, hh].astype(BF16), hh, 0.0))
        return [s1, s2]

    def finish(out, half, hh, e_parts, term):
        w0 = half * (DEC_SEQ - NA_KEYS)
        vw = v_ref[w0:w0 + NA_KEYS, :]
        o = (_dot(e_parts[0], jnp.where(_keep_mask_bf16(NA_KEYS, hh), vw, jnp.ones_like(vw)))
             + _dot_nt(e_parts[1], _pad_rows(vct_ref[0, 0, hh].astype(BF16), hh, 1.0)))
        out.put(hh, o, term)

    heads = []
    for half in range(2):
        out = _PairSink(y_ref, slice(half * NA_HALF, (half + 1) * NA_HALF), slice(0, PAIR))
        for hh in range(2):
            heads.append(_Head(functools.partial(scores, half, hh), functools.partial(finish, out, half, hh)))
    _attend(heads, batch=2)


def _na_mixer(proj, cache_kt, cache_vt, na_bias):
    n_tok = proj.shape[0]
    pairs = H_C // 2
    col = lambda base: pl.BlockSpec((DEC_SEQ, PAIR), lambda hp, b: (b, base + hp))
    cache = pl.BlockSpec((1, 1, 2, DH, PAST_LEN), lambda hp, b: (b, 0, hp, 0, 0))
    return pl.pallas_call(
        _na_kernel,
        grid=(pairs, DEC_BATCH),
        in_specs=[col(0), col(pairs), col(2 * pairs), cache, cache,
                  pl.BlockSpec((2, 2, NA_HALF, NA_KEYS), lambda hp, b: (hp, 0, 0, 0))],
        out_specs=pl.BlockSpec((DEC_SEQ, PAIR), lambda hp, b: (b, hp)),
        out_shape=jax.ShapeDtypeStruct((n_tok, 512), BF16),
        compiler_params=_cparams(2),
        name="na_mixer",
    )(proj, proj, proj, cache_kt, cache_vt, na_bias)


def _swa_kernel(q_ref, kv_ref, kct_ref, vct_ref, sink_ref, y_ref):
    n_groups = DEC_SEQ // SWA_QROWS
    ql = lax.broadcasted_iota(jnp.int32, (SWA_QROWS, SWA_BAND), 0)
    kj = lax.broadcasted_iota(jnp.int32, (SWA_QROWS, SWA_BAND), 1)

    def band_bias(first_key_minus_first_query):
        return jnp.where(jnp.abs(kj + first_key_minus_first_query - ql) <= SWA_WINDOW, 0.0, NEG)

    bias_first, bias_mid, bias_last = band_bias(0), band_bias(-SWA_BLOCK), band_bias(-2 * SWA_BLOCK)
    def band_start(grp):
        return min(max(grp * SWA_QROWS - SWA_BLOCK, 0), DEC_SEQ - SWA_BAND)

    memo = _Memo()

    def band_k(g, grp, hh):
        start = band_start(grp)
        kb = kv_ref[start:start + SWA_BAND, g * PAIR:(g + 1) * PAIR]
        return jnp.where(_keep_mask_bf16(SWA_BAND, hh), kb, jnp.zeros_like(kb))

    def band_v(g, grp, hh):
        start = band_start(grp)
        vb = kv_ref[start:start + SWA_BAND, (KV_D + g) * PAIR:(KV_D + g + 1) * PAIR]
        return jnp.where(_keep_mask_bf16(SWA_BAND, hh), vb, jnp.ones_like(vb))

    def scores(g, grp, p, hh):
        bias = bias_first if grp == 0 else (bias_last if grp == n_groups - 1 else bias_mid)
        qp = q_ref[grp * SWA_QROWS:(grp + 1) * SWA_QROWS, p * PAIR:(p + 1) * PAIR]
        s1 = _dot_nt(qp, memo.get_or_build(("k", g, grp, hh), functools.partial(band_k, g, grp, hh))) + bias
        ckt = memo.get_or_build(("ck", g, hh), lambda: _pad_rows(kct_ref[0, 0, g].astype(BF16), hh, 0.0))
        return [s1, _dot(qp, ckt)]

    def finish(out, g, grp, hh, e_parts, term):
        cvt = memo.get_or_build(("cv", g, hh), lambda: _pad_rows(vct_ref[0, 0, g].astype(BF16), hh, 1.0))
        o = (_dot(e_parts[0], memo.get_or_build(("v", g, grp, hh), functools.partial(band_v, g, grp, hh)))
             + _dot_nt(e_parts[1], cvt))
        out.put(hh, o, term)

    heads = []
    for g in range(KV_D):
        for grp in range(n_groups):
            for j in range(GQA // 2):
                p = g * (GQA // 2) + j
                out = _PairSink(y_ref, slice(grp * SWA_QROWS, (grp + 1) * SWA_QROWS), slice(p * PAIR, (p + 1) * PAIR))
                for hh in range(2):
                    sink = sink_ref[2 * p + hh:2 * p + hh + 1, :] * LOG2E
                    heads.append(_Head(functools.partial(scores, g, grp, p, hh),
                                       functools.partial(finish, out, g, grp, hh), sink))
    _attend(heads, batch=4)


def _swa_mixer(proj, cache_kt, cache_vt, sink):
    n_tok = proj.shape[0]
    cache = pl.BlockSpec((1, 1, KV_D, DH, PAST_LEN), lambda b: (b, 0, 0, 0, 0))
    return pl.pallas_call(
        _swa_kernel,
        grid=(DEC_BATCH,),
        in_specs=[pl.BlockSpec((DEC_SEQ, 512), lambda b: (b, 3)),
                  pl.BlockSpec((DEC_SEQ, 512), lambda b: (b, 4)),
                  cache, cache,
                  pl.BlockSpec((H_D, 1), lambda b: (0, 0))],
        out_specs=pl.BlockSpec((DEC_SEQ, 512), lambda b: (b, 0)),
        out_shape=jax.ShapeDtypeStruct((n_tok, 512), BF16),
        compiler_params=_cparams(1),
        name="swa_mixer",
    )(proj, proj, cache_kt, cache_vt, sink)


def _post_mixer_kernel(ma_ref, mb_ref, x_ref, mod_ref, gpost_ref, gpre_ref, gmlp_ref, wo_ref, w1_ref, w2_ref, o_ref):
    half = D_MODEL // 2
    mod = mod_ref[0, 0]
    y = _dot(ma_ref[...], wo_ref[0, 0:half, :]) + _dot(mb_ref[...], wo_ref[0, half:D_MODEL, :])
    x1 = x_ref[...] + mod[:, 2 * D_MODEL:3 * D_MODEL] * _rms(y, gpost_ref[0])
    h = _norm_mod(x1, gpre_ref[0], mod, 3 * D_MODEL).astype(BF16)
    acc = jnp.zeros((TM, D_MODEL), F32)
    for c in range(D_FF // FF_CHUNK):
        f = _dot(h, w1_ref[0, :, c * FF_CHUNK:(c + 1) * FF_CHUNK])
        f = jnp.square(jnp.maximum(f, 0.0)).astype(BF16)
        acc = acc + _dot(f, w2_ref[0, c * FF_CHUNK:(c + 1) * FF_CHUNK, :])
    o_ref[...] = x1 + mod[:, 5 * D_MODEL:6 * D_MODEL] * _rms(acc, gmlp_ref[0])


def _post_mixer(mix_a, a_blk, mix_b, b_blk, x, mod, g_post, g_pre, g_mlp, w_out, w1, w2, li, *, latent):
    n_tok = x.shape[0]
    half = D_MODEL // 2
    tiles_per_seq = DEC_SEQ // TM
    if latent:
        mod_map = lambda i: (li, i // tiles_per_seq, 0, 0)
    else:
        mod_map = lambda i: (li, CTX_ROW, 0, 0)
    gain = pl.BlockSpec((1, 1, D_MODEL), lambda i: (li, 0, 0))
    return pl.pallas_call(
        _post_mixer_kernel,
        grid=(n_tok // TM,),
        in_specs=[pl.BlockSpec((TM, half), lambda i: (i, a_blk)),
                  pl.BlockSpec((TM, half), lambda i: (i, b_blk)),
                  pl.BlockSpec((TM, D_MODEL), lambda i: (i, 0)),
                  pl.BlockSpec((1, 1, 1, 6 * D_MODEL), mod_map),
                  gain, gain, gain,
                  _layer_block((D_MODEL, D_MODEL), li), _layer_block((D_MODEL, D_FF), li),
                  _layer_block((D_FF, D_MODEL), li)],
        out_specs=pl.BlockSpec((TM, D_MODEL), lambda i: (i, 0)),
        out_shape=jax.ShapeDtypeStruct((n_tok, D_MODEL), F32),
        compiler_params=_cparams(1),
        name=f"post_mixer_{'lat' if latent else 'ctx'}",
    )(mix_a, mix_b, x, mod, g_post, g_pre, g_mlp, w_out, w1, w2)


def _rope_tables():
    t = jnp.arange(DEC_SEQ)
    rows = (t // GRID_W).astype(F32)
    cols = (t % GRID_W).astype(F32)
    q4 = DH // 4
    inv = 1.0 / (ROPE_BASE ** (jnp.arange(q4, dtype=F32) / q4))
    ar, ac = rows[:, None] * inv, cols[:, None] * inv
    cos64 = jnp.concatenate([jnp.cos(ar), jnp.cos(ar), jnp.cos(ac), jnp.cos(ac)], axis=1)
    sin64 = jnp.concatenate([-jnp.sin(ar), jnp.sin(ar), -jnp.sin(ac), jnp.sin(ac)], axis=1)
    return jnp.tile(cos64, (1, 2)), jnp.tile(sin64, (1, 2))


def kernel(x_prompt, x_sample, cache_diff_k, cache_diff_v, cache_na_k, cache_na_v, cache_swa_k, cache_swa_v, c, c_ctx, mod_w, mod_b, norm_mix_pre, norm_mix_post, norm_mlp_pre, norm_mlp_post, w_in_even, conv_w, lambda_q1, lambda_k1, lambda_q2, lambda_k2, subln, w_in_odd, rpb, sink, w_out, mlp_w1, mlp_w2):
    cond = jnp.zeros((MOD_ROWS, D_MODEL), F32).at[:DEC_BATCH].set(c).at[CTX_ROW].set(c_ctx)
    mod = _modulation(cond, mod_w, mod_b).reshape(DEPTH, MOD_ROWS, 1, 6 * D_MODEL)
    rope = _rope_tables()
    na_bias = _na_bias(rpb[0])

    t_minor = lambda a: jnp.swapaxes(a, -1, -2)

    w_in = [w_in_even.astype(BF16), w_in_odd.astype(BF16)]
    w_o, w_1, w_2 = w_out.astype(BF16), mlp_w1.astype(BF16), mlp_w2.astype(BF16)
    gains = lambda a: a.reshape(DEPTH, 1, D_MODEL)
    g_pre_mix, g_post_mix, g_pre_mlp, g_post_mlp = (gains(norm_mix_pre), gains(norm_mix_post), gains(norm_mlp_pre),
                                                    gains(norm_mlp_post))

    xp = x_prompt.reshape(BATCH * SEQ, D_MODEL)
    xs = x_sample.reshape(DEC_BATCH * DEC_SEQ, D_MODEL)
    lam_init0 = 0.8 - 0.6 * math.exp(-0.3 * 0)
    lam_args = (conv_w, lambda_q1, lambda_k1, lambda_q2, lambda_k2, subln)
    sink_col = sink[0].reshape(H_D, 1)

    def post(li, ma, a_blk, mb, b_blk, x, latent):
        return _post_mixer(ma, a_blk, mb, b_blk, x, mod, g_post_mix, g_pre_mlp, g_post_mlp, w_o, w_1, w_2, li,
                           latent=latent)

    proj, new_diff_kt, new_diff_v = _pre_mixer(xp, mod, g_pre_mix, w_in[0], 0, latent=False, even=True, rope=None)
    ya, yb = _even_mixer(proj, *lam_args, new_diff_kt, new_diff_v, seq=SEQ, own_keys=False, lam_init=lam_init0)
    xp = post(0, ya, 0, yb, 0, xp, False)
    proj = _pre_mixer(xs, mod, g_pre_mix, w_in[0], 0, latent=True, even=True, rope=rope)
    ya, yb = _even_mixer(proj, *lam_args, t_minor(cache_diff_k), cache_diff_v, seq=DEC_SEQ, own_keys=True,
                         lam_init=lam_init0)
    xs = post(0, ya, 0, yb, 0, xs, True)

    q, new_na_kt, new_na_vt, new_swa_kt, new_swa_vt = _pre_mixer(
        xp, mod, g_pre_mix, w_in[1], 1, latent=False, even=False, rope=None)
    y = _odd_ctx_mixer(q, new_na_kt, new_na_vt, new_swa_kt, new_swa_vt, sink_col)
    xp = post(1, y, 0, y, 1, xp, False)
    proj = _pre_mixer(xs, mod, g_pre_mix, w_in[1], 1, latent=True, even=False, rope=rope)
    yc = _na_mixer(proj, t_minor(cache_na_k), t_minor(cache_na_v), na_bias)
    yd = _swa_mixer(proj, t_minor(cache_swa_k), t_minor(cache_swa_v), sink_col)
    xs = post(1, yc, 0, yd, 0, xs, True)

    return (xp.reshape(BATCH, SEQ, D_MODEL), xs.reshape(DEC_BATCH, DEC_SEQ, D_MODEL),
            t_minor(new_diff_kt), new_diff_v, t_minor(new_na_kt), t_minor(new_na_vt),
            t_minor(new_swa_kt), t_minor(new_swa_vt))
```

```python
import functools
import math

import jax
import jax.numpy as jnp
from jax import lax
from jax.experimental import pallas as pl
from jax.experimental.pallas import tpu as pltpu

F32 = jnp.float32
BF16 = jnp.bfloat16

D_MODEL = 1024
BATCH = 32
SEQ = 256
DEPTH = 2
DEC_BATCH = 8
DEC_SEQ = 1024
PAST_LEN = 256
GRID_W = 64
GRID_H = DEC_SEQ // GRID_W
MIX_A = 512
H_B = 4
DH = 64
PAIR = 2 * DH
H_C = 8
H_D = 8
KV_D = 2
GQA = H_D // KV_D
IN_EVEN = 3072
IN_ODD = 2304
D_FF = 4096
NA_WIN_R = 8
NA_WIN_C = 16
SWA_BLOCK = 128
SWA_WINDOW = 128
ROPE_BASE = 10000.0
EPS = 1e-6
NEG = -1e30
LOG2E = math.log2(math.e)
Q_SCALE = DH ** -0.5 * LOG2E

MOD_ROWS = 16
CTX_ROW = DEC_BATCH
V7X_VMEM_BYTES = 64 * 1024 * 1024
VMEM_LIMIT = V7X_VMEM_BYTES - 8 * 1024 * 1024

TM = 512
FF_CHUNK = 1024
POST_SPLIT = 2
CTX_SEQS_PER_STEP = 4
MOD_TN = 1536
NA_HALF = 512
NA_KEYS = 768
SWA_QROWS = 2 * SWA_BLOCK
SWA_BAND = 4 * SWA_BLOCK
LAT_ODD_COLS = IN_ODD + 2 * PAIR


def _cparams(n_axes):
    return pltpu.CompilerParams(dimension_semantics=("arbitrary",) * n_axes, vmem_limit_bytes=VMEM_LIMIT)


def _rms(x, g):
    return x * lax.rsqrt(jnp.mean(x * x, axis=-1, keepdims=True) + EPS) * g


def _dot(a, b):
    return jnp.dot(a, b, preferred_element_type=F32)


def _dot_nt(a, b):
    return lax.dot_general(a, b, (((1,), (1,)), ((), ())), preferred_element_type=F32)


def _layer_block(shape, li):
    zeros = (0,) * len(shape)
    return pl.BlockSpec((1,) + tuple(shape), lambda *_: (li,) + zeros, pipeline_mode=pl.Buffered(1))


def _low_half(shape):
    return (lax.broadcasted_iota(jnp.int32, shape, len(shape) - 1) & DH) == 0


def _keep_mask_bf16(rows, hh):
    lane = lax.broadcasted_iota(jnp.int32, (rows, PAIR), 1)
    half = (lane & DH).astype(F32).astype(BF16)
    return (half == 0) if hh == 0 else (half != 0)


def _pad_rows(t, hh, fill):
    other = jnp.full(t.shape, fill, t.dtype)
    return jnp.concatenate([t, other] if hh == 0 else [other, t], axis=0)


class _Memo(dict):
    def get_or_build(self, key, build):
        if key not in self:
            self[key] = build()
        return self[key]


def _merge_pair(o0, o1, extra_den=None):
    low = _low_half(o0.shape)
    num = jnp.where(low, o0, o1)
    den = pltpu.roll(jnp.where(low, o1, o0), DH, axis=1)
    if extra_den is not None:
        den = den + jnp.where(low, extra_den[0], extra_den[1])
    return num * (1.0 / den)


class _Head:
    def __init__(self, scores, finish, sink=None):
        self.scores, self.finish, self.sink = scores, finish, sink


def _attend(heads, batch):
    batches = [heads[i:i + batch] for i in range(0, len(heads), batch)]
    cur = [hd.scores() for hd in batches[0]]
    for bi, group in enumerate(batches):
        nxt = [hd.scores() for hd in batches[bi + 1]] if bi + 1 < len(batches) else None
        exps = []
        for hd, parts in zip(group, cur):
            chunks = [s[:, j * 128:(j + 1) * 128] for s in parts for j in range(s.shape[1] // 128)]
            m_el = functools.reduce(jnp.maximum, chunks)
            if hd.sink is not None:
                m_el = jnp.maximum(m_el, hd.sink)
            mx = jnp.broadcast_to(jnp.max(m_el, axis=-1, keepdims=True), m_el.shape)
            term = None if hd.sink is None else jnp.exp2(hd.sink - mx)
            e_parts = [jnp.concatenate([jnp.exp2(s[:, j * 128:(j + 1) * 128] - mx)
                                        for j in range(s.shape[1] // 128)], axis=1).astype(BF16) for s in parts]
            exps.append((e_parts, term))
        for hd, (e_parts, term) in zip(group, exps):
            hd.finish(e_parts, term)
        cur = nxt


class _PairSink:
    def __init__(self, y_ref, rows, cols):
        self.y_ref, self.rows, self.cols, self.first = y_ref, rows, cols, None

    def put(self, hh, o, term):
        if hh == 0:
            self.first = (o, term)
        else:
            o0, term0 = self.first
            extra = None if term is None else (term0, term)
            self.y_ref[self.rows, self.cols] = _merge_pair(o0, o, extra).astype(BF16)


def _mod_kernel(cond_ref, w_ref, b_ref, o_ref):
    c = cond_ref[...]
    s = c * (1.0 / (1.0 + jnp.exp(-c)))
    o_ref[0] = _dot(s.astype(BF16), w_ref[0].astype(BF16)) + b_ref[0]


def _modulation(cond, mod_w, mod_b):
    return pl.pallas_call(
        _mod_kernel,
        grid=(DEPTH, 6 * D_MODEL // MOD_TN),
        in_specs=[
            pl.BlockSpec((MOD_ROWS, D_MODEL), lambda l, j: (0, 0)),
            pl.BlockSpec((1, D_MODEL, MOD_TN), lambda l, j: (l, 0, j)),
            pl.BlockSpec((1, 1, MOD_TN), lambda l, j: (l, 0, j)),
        ],
        out_specs=pl.BlockSpec((1, MOD_ROWS, MOD_TN), lambda l, j: (l, 0, j)),
        out_shape=jax.ShapeDtypeStruct((DEPTH, MOD_ROWS, 6 * D_MODEL), F32),
        compiler_params=_cparams(2),
        name="adaln_modulation",
    )(cond, mod_w, mod_b.reshape(DEPTH, 1, 6 * D_MODEL))


def _norm_mod(x, g, mod, shift_col):
    sh = mod[:, shift_col:shift_col + D_MODEL]
    sc = mod[:, shift_col + D_MODEL:shift_col + 2 * D_MODEL]
    return _rms(x, g) * (1.0 + sc) + sh


def _rope(z, cos, sin):
    outs = []
    for j in range(z.shape[1] // 128):
        zj = z[:, j * 128:(j + 1) * 128]
        lane = lax.broadcasted_iota(jnp.int32, zj.shape, 1)
        partner = jnp.where((lane & 16) == 0, pltpu.roll(zj, 128 - 16, axis=1), pltpu.roll(zj, 16, axis=1))
        outs.append(zj * cos + partner * sin)
    return outs[0] if len(outs) == 1 else jnp.concatenate(outs, axis=1)


def _store_heads_transposed(dst, bi, pc_rows, n_pairs):
    for p in range(n_pairs):
        t = pc_rows[:, p * PAIR:(p + 1) * PAIR].T
        dst[bi, 0, 2 * p] = t[0:DH]
        dst[bi, 0, 2 * p + 1] = t[DH:PAIR]


def _normed_groups(x_ref, mod_ref, g_ref):
    rows = [slice(i * SEQ, (i + 1) * SEQ) for i in range(TM // SEQ)]
    return rows, [_norm_mod(x_ref[r, :], g_ref[0], mod_ref[0, 0], 0).astype(BF16) for r in rows]


def _k1_ctx_even(x_ref, mod_ref, g_ref, w_ref, proj_ref, dkt_ref, dv_ref):
    rows, hs = _normed_groups(x_ref, mod_ref, g_ref)
    for c in range(IN_EVEN // 512):
        for bi, (r, h) in enumerate(zip(rows, hs)):
            pc = _dot(h, w_ref[0, :, c * 512:(c + 1) * 512])
            if c < 3:
                proj_ref[r, c * 512:(c + 1) * 512] = pc.astype(BF16)
            elif c == 3:
                proj_ref[r, c * 512:(c + 1) * 512] = (pc * Q_SCALE).astype(BF16)
            elif c == 4:
                for hh in range(H_B):
                    t = pc[:, hh * PAIR:(hh + 1) * PAIR].T
                    dkt_ref[bi, 0, hh, 0] = t[0:DH]
                    dkt_ref[bi, 0, hh, 1] = t[DH:PAIR]
            else:
                for hh in range(H_B):
                    dv_ref[bi, 0, hh] = pc[:, hh * PAIR:(hh + 1) * PAIR]


def _k1_lat_even(x_ref, mod_ref, g_ref, w_ref, cos_ref, sin_ref, proj_ref):
    rows, hs = _normed_groups(x_ref, mod_ref, g_ref)
    for c in range(IN_EVEN // 512):
        for r, h in zip(rows, hs):
            pc = _dot(h, w_ref[0, :, c * 512:(c + 1) * 512])
            if c == 3:
                pc = _rope(pc * Q_SCALE, cos_ref[r, :], sin_ref[r, :])
            elif c == 4:
                pc = _rope(pc, cos_ref[r, :], sin_ref[r, :])
            proj_ref[r, c * 512:(c + 1) * 512] = pc.astype(BF16)


def _k1_ctx_odd(x_ref, mod_ref, g_ref, w_ref, q_ref, nkt_ref, nvt_ref, skt_ref, svt_ref):
    rows, hs = _normed_groups(x_ref, mod_ref, g_ref)
    for c in range(4):
        for bi, (r, h) in enumerate(zip(rows, hs)):
            pc = _dot(h, w_ref[0, :, c * 512:(c + 1) * 512])
            if c == 0 or c == 3:
                q_ref[r, (c // 3) * 512:(c // 3 + 1) * 512] = (pc * Q_SCALE).astype(BF16)
            else:
                _store_heads_transposed(nkt_ref if c == 1 else nvt_ref, bi, pc, H_C // 2)
    for bi, h in enumerate(hs):
        pc = _dot(h, w_ref[0, :, 2048:IN_ODD])
        _store_heads_transposed(skt_ref, bi, pc[:, 0:PAIR], 1)
        _store_heads_transposed(svt_ref, bi, pc[:, PAIR:2 * PAIR], 1)


def _k1_lat_odd(x_ref, mod_ref, g_ref, w_ref, cos_ref, sin_ref, proj_ref):
    rows, hs = _normed_groups(x_ref, mod_ref, g_ref)
    for c in range(4):
        for r, h in zip(rows, hs):
            pc = _dot(h, w_ref[0, :, c * 512:(c + 1) * 512])
            if c == 0:
                pc = pc * Q_SCALE
            elif c == 3:
                pc = _rope(pc * Q_SCALE, cos_ref[r, :], sin_ref[r, :])
            proj_ref[r, c * 512:(c + 1) * 512] = pc.astype(BF16)
    low = _low_half((SEQ, PAIR))
    for r, h in zip(rows, hs):
        pc = _dot(h, w_ref[0, :, 2048:IN_ODD])
        for j, z in enumerate((_rope(pc[:, 0:PAIR], cos_ref[r, :], sin_ref[r, :]), pc[:, PAIR:2 * PAIR])):
            zr = pltpu.roll(z, DH, axis=1)
            base = 2048 + j * 2 * PAIR
            proj_ref[r, base:base + PAIR] = jnp.where(low, z, zr).astype(BF16)
            proj_ref[r, base + PAIR:base + 2 * PAIR] = jnp.where(low, zr, z).astype(BF16)


def _pre_mixer(x, mod, gains, w, li, *, latent, even, rope):
    n_tok = x.shape[0]
    n_in = w.shape[2]
    tiles_per_seq = DEC_SEQ // TM
    if latent:
        mod_map = lambda i: (li, i // tiles_per_seq, 0, 0)
    else:
        mod_map = lambda i: (li, CTX_ROW, 0, 0)
    in_specs = [
        pl.BlockSpec((TM, D_MODEL), lambda i: (i, 0)),
        pl.BlockSpec((1, 1, 1, 6 * D_MODEL), mod_map),
        pl.BlockSpec((1, 1, D_MODEL), lambda i: (li, 0, 0)),
        _layer_block((D_MODEL, n_in), 0),
    ]
    args = [x, mod, gains, w]
    nb = TM // SEQ
    if latent:
        in_specs += [pl.BlockSpec((TM, 128), lambda i: (i % tiles_per_seq, 0))] * 2
        args += list(rope)
        body = _k1_lat_even if even else _k1_lat_odd
        n_out = IN_EVEN if even else LAT_ODD_COLS
        out_specs = pl.BlockSpec((TM, n_out), lambda i: (i, 0))
        out_shape = jax.ShapeDtypeStruct((n_tok, n_out), BF16)
    elif even:
        body = _k1_ctx_even
        out_specs = [pl.BlockSpec((TM, 4 * MIX_A), lambda i: (i, 0)),
                     pl.BlockSpec((nb, 1, H_B, 2, DH, SEQ), lambda i: (i, 0, 0, 0, 0, 0)),
                     pl.BlockSpec((nb, 1, H_B, SEQ, PAIR), lambda i: (i, 0, 0, 0, 0))]
        out_shape = [jax.ShapeDtypeStruct((n_tok, 4 * MIX_A), BF16),
                     jax.ShapeDtypeStruct((BATCH, 1, H_B, 2, DH, SEQ), F32),
                     jax.ShapeDtypeStruct((BATCH, 1, H_B, SEQ, PAIR), F32)]
    else:
        body = _k1_ctx_odd
        c_spec = pl.BlockSpec((nb, 1, H_C, DH, SEQ), lambda i: (i, 0, 0, 0, 0))
        d_spec = pl.BlockSpec((nb, 1, KV_D, DH, SEQ), lambda i: (i, 0, 0, 0, 0))
        c_shape = jax.ShapeDtypeStruct((BATCH, 1, H_C, DH, SEQ), F32)
        d_shape = jax.ShapeDtypeStruct((BATCH, 1, KV_D, DH, SEQ), F32)
        out_specs = [pl.BlockSpec((TM, D_MODEL), lambda i: (i, 0)), c_spec, c_spec, d_spec, d_spec]
        out_shape = [jax.ShapeDtypeStruct((n_tok, D_MODEL), BF16), c_shape, c_shape, d_shape, d_shape]
    return pl.pallas_call(
        body,
        grid=(n_tok // TM,),
        in_specs=in_specs,
        out_specs=out_specs,
        out_shape=out_shape,
        compiler_params=_cparams(1),
        name=f"pre_mixer_{'lat' if latent else 'ctx'}_{'even' if even else 'odd'}",
    )(*args)


def _even_mixer_kernel(*refs, seq, n_seq, tq, own_keys, lam_init):
    proj_ref, cw_ref, lq1_ref, lk1_ref, lq2_ref, lk2_ref, subln_ref, ckt_ref, cv_ref, ya_ref, yb_ref = refs
    n_rows = n_seq * seq

    pos = lax.broadcasted_iota(jnp.int32, (n_rows, 128), 0) % seq
    for j in range(MIX_A // 128):
        cols = slice(j * 128, (j + 1) * 128)
        a_b = proj_ref[:, j * 128:(j + 1) * 128].astype(F32)
        u = (proj_ref[:, MIX_A + j * 128:MIX_A + (j + 1) * 128].astype(F32)
             * proj_ref[:, 2 * MIX_A + j * 128:2 * MIX_A + (j + 1) * 128].astype(F32))
        u_prev = jnp.where(pos == 0, 0.0, pltpu.roll(u, 1, axis=0))
        u_next = jnp.where(pos == seq - 1, 0.0, pltpu.roll(u, n_rows - 1, axis=0))
        w = cw_ref[0, :, cols]
        ya_ref[:, cols] = (a_b * (w[0:1] * u_prev + w[1:2] * u + w[2:3] * u_next)).astype(BF16)

    lam = (jnp.exp(jnp.sum(lq1_ref[...] * lk1_ref[...], axis=-1, keepdims=True))
           - jnp.exp(jnp.sum(lq2_ref[...] * lk2_ref[...], axis=-1, keepdims=True)) + lam_init)
    subln = subln_ref[...]
    q_col, k_col, v_col = 3 * MIX_A, 3 * MIX_A + 512, 3 * MIX_A + 1024
    ones = jnp.ones((PAST_LEN, PAIR), BF16)

    memo = _Memo()

    def own_k(b, h, m):
        kp = proj_ref[b * seq:(b + 1) * seq, k_col + h * PAIR:k_col + (h + 1) * PAIR]
        return jnp.where(_keep_mask_bf16(seq, m), kp, jnp.zeros_like(kp))

    def value_ops(b, h):
        v_ops = []
        if own_keys:
            v_own = proj_ref[b * seq:(b + 1) * seq, v_col + h * PAIR:v_col + (h + 1) * PAIR]
            v_ops.append(jnp.concatenate([v_own, jnp.ones((seq, PAIR), BF16)], axis=1))
        v_ops.append(jnp.concatenate([cv_ref[b, 0, h].astype(BF16), ones], axis=1))
        return v_ops

    def scores(b, qrows, h, m):
        qp = proj_ref[qrows, q_col + h * PAIR:q_col + (h + 1) * PAIR]
        parts = []
        if own_keys:
            parts.append(_dot_nt(qp, memo.get_or_build(("k", b, h, m), functools.partial(own_k, b, h, m))))
        ckt = memo.get_or_build(("ck", b, h, m), lambda: _pad_rows(ckt_ref[b, 0, h, m].astype(BF16), m, 0.0))
        parts.append(_dot(qp, ckt))
        return parts

    first_map = {}

    def pv(b, qrows, h, m, e_parts, _):
        v_ops = memo.get_or_build(("v", b, h), functools.partial(value_ops, b, h))
        o = None
        for e, v_op in zip(e_parts, v_ops):
            t = _dot(e, v_op)
            o = t if o is None else o + t
        attn = o[:, 0:PAIR] * (1.0 / o[:, PAIR:2 * PAIR])
        if m == 0:
            first_map[(qrows.start, h)] = attn
        else:
            y = _rms(first_map.pop((qrows.start, h)) - lam * attn, subln) * (1.0 - lam_init)
            yb_ref[qrows, h * PAIR:(h + 1) * PAIR] = y.astype(BF16)

    heads = []
    for b in range(n_seq):
        for i in range(seq // tq):
            qrows = slice(b * seq + i * tq, b * seq + (i + 1) * tq)
            for h in range(H_B):
                for m in range(2):
                    heads.append(_Head(functools.partial(scores, b, qrows, h, m),
                                       functools.partial(pv, b, qrows, h, m)))
    _attend(heads, batch=4)


def _even_mixer(proj, conv_w, lq1, lk1, lq2, lk2, subln, kt, v, *, seq, n_seq, own_keys, lam_init):
    n_tok = proj.shape[0]
    rows = n_seq * seq
    small = lambda a: pl.BlockSpec((1, a.shape[1]), lambda b: (0, 0))
    in_specs = [pl.BlockSpec((rows, proj.shape[1]), lambda b: (b, 0)),
                pl.BlockSpec((1, 3, MIX_A), lambda b: (0, 0, 0)),
                small(lq1), small(lk1), small(lq2), small(lk2), small(subln),
                pl.BlockSpec((n_seq, 1, H_B, 2, DH, PAST_LEN), lambda b: (b, 0, 0, 0, 0, 0)),
                pl.BlockSpec((n_seq, 1, H_B, PAST_LEN, PAIR), lambda b: (b, 0, 0, 0, 0))]
    y_spec = pl.BlockSpec((rows, 512), lambda b: (b, 0))
    y_shape = jax.ShapeDtypeStruct((n_tok, 512), BF16)
    return pl.pallas_call(
        functools.partial(_even_mixer_kernel, seq=seq, n_seq=n_seq, tq=min(seq, 256), own_keys=own_keys,
                          lam_init=lam_init),
        grid=(n_tok // rows,),
        in_specs=in_specs,
        out_specs=[y_spec, y_spec],
        out_shape=[y_shape, y_shape],
        compiler_params=_cparams(1),
        name=f"even_mixer_{'lat' if own_keys else 'ctx'}",
    )(proj, conv_w, lq1, lk1, lq2, lk2, subln, kt, v)


def _odd_ctx_kernel(q_ref, nkt_ref, nvt_ref, skt_ref, svt_ref, sink_ref, y_ref):
    def kv_refs(b, head):
        if head < H_C:
            return nkt_ref.at[b, 0, head], nvt_ref.at[b, 0, head]
        g = (head - H_C) // GQA
        return skt_ref.at[b, 0, g], svt_ref.at[b, 0, g]

    memo = _Memo()

    def kv_key(head):
        return head if head < H_C else H_C + (head - H_C) // GQA

    def scores(b, p, hh):
        head = 2 * p + hh
        kt = memo.get_or_build(("k", b, kv_key(head), hh),
                               lambda: _pad_rows(kv_refs(b, head)[0][...].astype(BF16), hh, 0.0))
        return [_dot(q_ref[b * SEQ:(b + 1) * SEQ, p * PAIR:(p + 1) * PAIR], kt)]

    def finish(out, b, p, hh, e_parts, term):
        head = 2 * p + hh
        vt = memo.get_or_build(("v", b, kv_key(head), hh),
                               lambda: _pad_rows(kv_refs(b, head)[1][...].astype(BF16), hh, 1.0))
        out.put(hh, _dot_nt(e_parts[0], vt), term)

    heads = []
    for b in range(CTX_SEQS_PER_STEP):
        for p in range((H_C + H_D) // 2):
            out = _PairSink(y_ref, slice(b * SEQ, (b + 1) * SEQ), slice(p * PAIR, (p + 1) * PAIR))
            for hh in range(2):
                d_head = 2 * p + hh - H_C
                sink = sink_ref[d_head:d_head + 1, :] * LOG2E if d_head >= 0 else None
                heads.append(_Head(functools.partial(scores, b, p, hh), functools.partial(finish, out, b, p, hh), sink))
    _attend(heads, batch=8)


def _odd_ctx_mixer(q, nkt, nvt, skt, svt, sink):
    n_tok = q.shape[0]
    n_seq = CTX_SEQS_PER_STEP
    c_spec = pl.BlockSpec((n_seq, 1, H_C, DH, SEQ), lambda b: (b, 0, 0, 0, 0))
    d_spec = pl.BlockSpec((n_seq, 1, KV_D, DH, SEQ), lambda b: (b, 0, 0, 0, 0))
    return pl.pallas_call(
        _odd_ctx_kernel,
        grid=(n_tok // (n_seq * SEQ),),
        in_specs=[pl.BlockSpec((n_seq * SEQ, D_MODEL), lambda b: (b, 0)), c_spec, c_spec, d_spec, d_spec,
                  pl.BlockSpec((H_D, 1), lambda b: (0, 0))],
        out_specs=pl.BlockSpec((n_seq * SEQ, D_MODEL), lambda b: (b, 0)),
        out_shape=jax.ShapeDtypeStruct((n_tok, D_MODEL), BF16),
        compiler_params=_cparams(1),
        name="odd_mixer_ctx",
    )(q, nkt, nvt, skt, svt, sink)


def _na_bias_kernel(rpb_ref, o_ref):
    h = pl.program_id(0)
    qc = lax.broadcasted_iota(jnp.int32, (GRID_W, GRID_W), 0)
    kc = lax.broadcasted_iota(jnp.int32, (GRID_W, GRID_W), 1)
    col_start = jnp.clip(qc - NA_WIN_C // 2, 0, GRID_W - NA_WIN_C)
    col_ok = (kc >= col_start) & (kc < col_start + NA_WIN_C)
    dc = kc - qc + NA_WIN_C - 1
    n_dr, n_dc = 2 * NA_WIN_R - 1, 2 * NA_WIN_C - 1
    neg = jnp.full((GRID_W, GRID_W), NEG, F32)
    toeplitz = []
    for dr in range(n_dr):
        t = neg
        for d in range(n_dc):
            t = jnp.where(dc == d, rpb_ref[(h * n_dr + dr) * n_dc + d] * LOG2E, t)
        toeplitz.append(jnp.where(col_ok, t, NEG))
    rows_per_half = NA_HALF // GRID_W
    key_rows = NA_KEYS // GRID_W
    for half in range(2):
        for rl in range(rows_per_half):
            r = half * rows_per_half + rl
            r_start = min(max(r - NA_WIN_R // 2, 0), GRID_H - NA_WIN_R)
            blocks = []
            for j in range(key_rows):
                rk = half * (GRID_H - key_rows) + j
                inside = r_start <= rk < r_start + NA_WIN_R
                blocks.append(toeplitz[rk - r + NA_WIN_R - 1] if inside else neg)
            o_ref[0, half, rl * GRID_W:(rl + 1) * GRID_W, :] = jnp.concatenate(blocks, axis=1)


def _na_bias(rpb):
    return pl.pallas_call(
        _na_bias_kernel,
        grid=(H_C,),
        in_specs=[pl.BlockSpec(memory_space=pltpu.SMEM)],
        out_specs=pl.BlockSpec((1, 2, NA_HALF, NA_KEYS), lambda h: (h, 0, 0, 0)),
        out_shape=jax.ShapeDtypeStruct((H_C, 2, NA_HALF, NA_KEYS), F32),
        compiler_params=_cparams(1),
        name="na_bias",
    )(rpb.reshape(-1))


def _na_kernel(q_ref, k_ref, v_ref, kct_ref, vct_ref, nb_ref, y_ref):
    def scores(half, hh):
        w0 = half * (DEC_SEQ - NA_KEYS)
        qp = q_ref[half * NA_HALF:(half + 1) * NA_HALF, :]
        kw = k_ref[w0:w0 + NA_KEYS, :]
        s1 = _dot_nt(qp, jnp.where(_keep_mask_bf16(NA_KEYS, hh), kw, jnp.zeros_like(kw))) + nb_ref[hh, half]
        s2 = _dot(qp, _pad_rows(kct_ref[0, 0, hh].astype(BF16), hh, 0.0))
        return [s1, s2]

    def finish(out, half, hh, e_parts, term):
        w0 = half * (DEC_SEQ - NA_KEYS)
        vw = v_ref[w0:w0 + NA_KEYS, :]
        o = (_dot(e_parts[0], jnp.where(_keep_mask_bf16(NA_KEYS, hh), vw, jnp.ones_like(vw)))
             + _dot_nt(e_parts[1], _pad_rows(vct_ref[0, 0, hh].astype(BF16), hh, 1.0)))
        out.put(hh, o, term)

    heads = []
    for half in range(2):
        out = _PairSink(y_ref, slice(half * NA_HALF, (half + 1) * NA_HALF), slice(0, PAIR))
        for hh in range(2):
            heads.append(_Head(functools.partial(scores, half, hh), functools.partial(finish, out, half, hh)))
    _attend(heads, batch=2)


def _na_mixer(proj, cache_kt, cache_vt, na_bias):
    n_tok = proj.shape[0]
    pairs = H_C // 2
    col = lambda base: pl.BlockSpec((DEC_SEQ, PAIR), lambda hp, b: (b, base + hp))
    cache = pl.BlockSpec((1, 1, 2, DH, PAST_LEN), lambda hp, b: (b, 0, hp, 0, 0))
    return pl.pallas_call(
        _na_kernel,
        grid=(pairs, DEC_BATCH),
        in_specs=[col(0), col(pairs), col(2 * pairs), cache, cache,
                  pl.BlockSpec((2, 2, NA_HALF, NA_KEYS), lambda hp, b: (hp, 0, 0, 0))],
        out_specs=pl.BlockSpec((DEC_SEQ, PAIR), lambda hp, b: (b, hp)),
        out_shape=jax.ShapeDtypeStruct((n_tok, 512), BF16),
        compiler_params=_cparams(2),
        name="na_mixer",
    )(proj, proj, proj, cache_kt, cache_vt, na_bias)


def _swa_kernel(q_ref, kv_ref, kct_ref, vct_ref, sink_ref, y_ref):
    n_groups = DEC_SEQ // SWA_QROWS
    ql = lax.broadcasted_iota(jnp.int32, (SWA_QROWS, SWA_BAND), 0)
    kj = lax.broadcasted_iota(jnp.int32, (SWA_QROWS, SWA_BAND), 1)

    def band_bias(first_key_minus_first_query):
        return jnp.where(jnp.abs(kj + first_key_minus_first_query - ql) <= SWA_WINDOW, 0.0, NEG)

    bias_first, bias_mid, bias_last = band_bias(0), band_bias(-SWA_BLOCK), band_bias(-2 * SWA_BLOCK)
    def band_start(grp):
        return min(max(grp * SWA_QROWS - SWA_BLOCK, 0), DEC_SEQ - SWA_BAND)

    memo = _Memo()

    def band_k(g, grp, hh):
        start = band_start(grp)
        kb = kv_ref[start:start + SWA_BAND, g * PAIR:(g + 1) * PAIR]
        return jnp.where(_keep_mask_bf16(SWA_BAND, hh), kb, jnp.zeros_like(kb))

    def band_v(g, grp, hh):
        start = band_start(grp)
        vb = kv_ref[start:start + SWA_BAND, (KV_D + g) * PAIR:(KV_D + g + 1) * PAIR]
        return jnp.where(_keep_mask_bf16(SWA_BAND, hh), vb, jnp.ones_like(vb))

    def scores(g, grp, p, hh):
        bias = bias_first if grp == 0 else (bias_last if grp == n_groups - 1 else bias_mid)
        qp = q_ref[grp * SWA_QROWS:(grp + 1) * SWA_QROWS, p * PAIR:(p + 1) * PAIR]
        s1 = _dot_nt(qp, memo.get_or_build(("k", g, grp, hh), functools.partial(band_k, g, grp, hh))) + bias
        ckt = memo.get_or_build(("ck", g, hh), lambda: _pad_rows(kct_ref[0, 0, g].astype(BF16), hh, 0.0))
        return [s1, _dot(qp, ckt)]

    def finish(out, g, grp, hh, e_parts, term):
        cvt = memo.get_or_build(("cv", g, hh), lambda: _pad_rows(vct_ref[0, 0, g].astype(BF16), hh, 1.0))
        o = (_dot(e_parts[0], memo.get_or_build(("v", g, grp, hh), functools.partial(band_v, g, grp, hh)))
             + _dot_nt(e_parts[1], cvt))
        out.put(hh, o, term)

    heads = []
    for g in range(KV_D):
        for grp in range(n_groups):
            for j in range(GQA // 2):
                p = g * (GQA // 2) + j
                out = _PairSink(y_ref, slice(grp * SWA_QROWS, (grp + 1) * SWA_QROWS), slice(p * PAIR, (p + 1) * PAIR))
                for hh in range(2):
                    sink = sink_ref[2 * p + hh:2 * p + hh + 1, :] * LOG2E
                    heads.append(_Head(functools.partial(scores, g, grp, p, hh),
                                       functools.partial(finish, out, g, grp, hh), sink))
    _attend(heads, batch=4)


def _swa_mixer(proj, cache_kt, cache_vt, sink):
    n_tok = proj.shape[0]
    cache = pl.BlockSpec((1, 1, KV_D, DH, PAST_LEN), lambda b: (b, 0, 0, 0, 0))
    return pl.pallas_call(
        _swa_kernel,
        grid=(DEC_BATCH,),
        in_specs=[pl.BlockSpec((DEC_SEQ, 512), lambda b: (b, 3)),
                  pl.BlockSpec((DEC_SEQ, 512), lambda b: (b, 4)),
                  cache, cache,
                  pl.BlockSpec((H_D, 1), lambda b: (0, 0))],
        out_specs=pl.BlockSpec((DEC_SEQ, 512), lambda b: (b, 0)),
        out_shape=jax.ShapeDtypeStruct((n_tok, 512), BF16),
        compiler_params=_cparams(1),
        name="swa_mixer",
    )(proj, proj, cache_kt, cache_vt, sink)


def _post_mixer_kernel(ma_ref, mb_ref, x_ref, mod_ref, gpost_ref, gpre_ref, gmlp_ref, wo_ref, w1_ref, w2_ref, o_ref):
    half = D_MODEL // 2
    mod = mod_ref[0, 0]
    rows = [slice(i * (TM // POST_SPLIT), (i + 1) * (TM // POST_SPLIT)) for i in range(POST_SPLIT)]
    ys = [_dot(ma_ref[r, :], wo_ref[0, 0:half, :]) + _dot(mb_ref[r, :], wo_ref[0, half:D_MODEL, :]) for r in rows]
    x1 = [x_ref[r, :] + mod[:, 2 * D_MODEL:3 * D_MODEL] * _rms(y, gpost_ref[0]) for r, y in zip(rows, ys)]
    h = [_norm_mod(x, gpre_ref[0], mod, 3 * D_MODEL).astype(BF16) for x in x1]
    acc = [None] * POST_SPLIT
    pending = []

    def second_matmul(c, i, f):
        t = _dot(f, w2_ref[0, c * FF_CHUNK:(c + 1) * FF_CHUNK, :])
        acc[i] = t if acc[i] is None else acc[i] + t

    for c in range(D_FF // FF_CHUNK):
        for i in range(POST_SPLIT):
            f = _dot(h[i], w1_ref[0, :, c * FF_CHUNK:(c + 1) * FF_CHUNK])
            pending.append((c, i, jnp.square(jnp.maximum(f, 0.0)).astype(BF16)))
            if len(pending) > 1:
                second_matmul(*pending.pop(0))
    second_matmul(*pending.pop(0))
    for i, r in enumerate(rows):
        o_ref[r, :] = x1[i] + mod[:, 5 * D_MODEL:6 * D_MODEL] * _rms(acc[i], gmlp_ref[0])


def _post_mixer(mix_a, a_blk, mix_b, b_blk, x, mod, g_post, g_pre, g_mlp, w_out, w1, w2, li, *, latent):
    n_tok = x.shape[0]
    half = D_MODEL // 2
    tiles_per_seq = DEC_SEQ // TM
    if latent:
        mod_map = lambda i: (li, i // tiles_per_seq, 0, 0)
    else:
        mod_map = lambda i: (li, CTX_ROW, 0, 0)
    gain = pl.BlockSpec((1, 1, D_MODEL), lambda i: (li, 0, 0))
    return pl.pallas_call(
        _post_mixer_kernel,
        grid=(n_tok // TM,),
        in_specs=[pl.BlockSpec((TM, half), lambda i: (i, a_blk)),
                  pl.BlockSpec((TM, half), lambda i: (i, b_blk)),
                  pl.BlockSpec((TM, D_MODEL), lambda i: (i, 0)),
                  pl.BlockSpec((1, 1, 1, 6 * D_MODEL), mod_map),
                  gain, gain, gain,
                  _layer_block((D_MODEL, D_MODEL), li), _layer_block((D_MODEL, D_FF), li),
                  _layer_block((D_FF, D_MODEL), li)],
        out_specs=pl.BlockSpec((TM, D_MODEL), lambda i: (i, 0)),
        out_shape=jax.ShapeDtypeStruct((n_tok, D_MODEL), F32),
        compiler_params=_cparams(1),
        name=f"post_mixer_{'lat' if latent else 'ctx'}",
    )(mix_a, mix_b, x, mod, g_post, g_pre, g_mlp, w_out, w1, w2)


def _rope_tables():
    t = jnp.arange(DEC_SEQ)
    rows = (t // GRID_W).astype(F32)
    cols = (t % GRID_W).astype(F32)
    q4 = DH // 4
    inv = 1.0 / (ROPE_BASE ** (jnp.arange(q4, dtype=F32) / q4))
    ar, ac = rows[:, None] * inv, cols[:, None] * inv
    cos64 = jnp.concatenate([jnp.cos(ar), jnp.cos(ar), jnp.cos(ac), jnp.cos(ac)], axis=1)
    sin64 = jnp.concatenate([-jnp.sin(ar), jnp.sin(ar), -jnp.sin(ac), jnp.sin(ac)], axis=1)
    return jnp.tile(cos64, (1, 2)), jnp.tile(sin64, (1, 2))


def kernel(x_prompt, x_sample, cache_diff_k, cache_diff_v, cache_na_k, cache_na_v, cache_swa_k, cache_swa_v, c, c_ctx, mod_w, mod_b, norm_mix_pre, norm_mix_post, norm_mlp_pre, norm_mlp_post, w_in_even, conv_w, lambda_q1, lambda_k1, lambda_q2, lambda_k2, subln, w_in_odd, rpb, sink, w_out, mlp_w1, mlp_w2):
    cond = jnp.zeros((MOD_ROWS, D_MODEL), F32).at[:DEC_BATCH].set(c).at[CTX_ROW].set(c_ctx)
    mod = _modulation(cond, mod_w, mod_b).reshape(DEPTH, MOD_ROWS, 1, 6 * D_MODEL)
    rope = _rope_tables()
    na_bias = _na_bias(rpb[0])

    t_minor = lambda a: jnp.swapaxes(a, -1, -2)

    w_in = [w_in_even.astype(BF16), w_in_odd.astype(BF16)]
    w_o, w_1, w_2 = w_out.astype(BF16), mlp_w1.astype(BF16), mlp_w2.astype(BF16)
    gains = lambda a: a.reshape(DEPTH, 1, D_MODEL)
    g_pre_mix, g_post_mix, g_pre_mlp, g_post_mlp = (gains(norm_mix_pre), gains(norm_mix_post), gains(norm_mlp_pre),
                                                    gains(norm_mlp_post))

    xp = x_prompt.reshape(BATCH * SEQ, D_MODEL)
    xs = x_sample.reshape(DEC_BATCH * DEC_SEQ, D_MODEL)
    lam_init0 = 0.8 - 0.6 * math.exp(-0.3 * 0)
    lam_args = (conv_w, lambda_q1, lambda_k1, lambda_q2, lambda_k2, subln)
    sink_col = sink[0].reshape(H_D, 1)

    def post(li, ma, a_blk, mb, b_blk, x, latent):
        return _post_mixer(ma, a_blk, mb, b_blk, x, mod, g_post_mix, g_pre_mlp, g_post_mlp, w_o, w_1, w_2, li,
                           latent=latent)

    proj, new_diff_kt, new_diff_v = _pre_mixer(xp, mod, g_pre_mix, w_in[0], 0, latent=False, even=True, rope=None)
    ya, yb = _even_mixer(proj, *lam_args, new_diff_kt, new_diff_v, seq=SEQ, n_seq=CTX_SEQS_PER_STEP // 2,
                         own_keys=False, lam_init=lam_init0)
    xp = post(0, ya, 0, yb, 0, xp, False)
    proj = _pre_mixer(xs, mod, g_pre_mix, w_in[0], 0, latent=True, even=True, rope=rope)
    ya, yb = _even_mixer(proj, *lam_args, t_minor(cache_diff_k), cache_diff_v, seq=DEC_SEQ, n_seq=1, own_keys=True,
                         lam_init=lam_init0)
    xs = post(0, ya, 0, yb, 0, xs, True)

    q, new_na_kt, new_na_vt, new_swa_kt, new_swa_vt = _pre_mixer(
        xp, mod, g_pre_mix, w_in[1], 1, latent=False, even=False, rope=None)
    y = _odd_ctx_mixer(q, new_na_kt, new_na_vt, new_swa_kt, new_swa_vt, sink_col)
    xp = post(1, y, 0, y, 1, xp, False)
    proj = _pre_mixer(xs, mod, g_pre_mix, w_in[1], 1, latent=True, even=False, rope=rope)
    yc = _na_mixer(proj, t_minor(cache_na_k), t_minor(cache_na_v), na_bias)
    yd = _swa_mixer(proj, t_minor(cache_swa_k), t_minor(cache_swa_v), sink_col)
    xs = post(1, yc, 0, yd, 0, xs, True)

    return (xp.reshape(BATCH, SEQ, D_MODEL), xs.reshape(DEC_BATCH, DEC_SEQ, D_MODEL),
            t_minor(new_diff_kt), new_diff_v, t_minor(new_na_kt), t_minor(new_na_vt),
            t_minor(new_swa_kt), t_minor(new_swa_vt))
```

```python
import functools
import math

import jax
import jax.numpy as jnp
from jax import lax
from jax.experimental import pallas as pl
from jax.experimental.pallas import tpu as pltpu

F32 = jnp.float32
BF16 = jnp.bfloat16

D_MODEL = 1024
BATCH = 32
SEQ = 256
DEPTH = 2
DEC_BATCH = 8
DEC_SEQ = 1024
PAST_LEN = 256
GRID_W = 64
GRID_H = DEC_SEQ // GRID_W
MIX_A = 512
H_B = 4
DH = 64
PAIR = 2 * DH
H_C = 8
H_D = 8
KV_D = 2
GQA = H_D // KV_D
IN_EVEN = 3072
IN_ODD = 2304
D_FF = 4096
NA_WIN_R = 8
NA_WIN_C = 16
SWA_BLOCK = 128
SWA_WINDOW = 128
ROPE_BASE = 10000.0
EPS = 1e-6
NEG = -1e30
LOG2E = math.log2(math.e)
Q_SCALE = DH ** -0.5 * LOG2E

MOD_ROWS = 16
CTX_ROW = DEC_BATCH
V7X_VMEM_BYTES = 64 * 1024 * 1024
VMEM_LIMIT = V7X_VMEM_BYTES - 8 * 1024 * 1024
BF16_SUBLANES = 16

TM = 512
FF_CHUNK = 1024
POST_SPLIT = 2
CTX_SEQS_EVEN = 2
CTX_SEQS_ODD = 4
MOD_TN = 1536
TQ = 256
NA_HALF = 512
NA_KEYS = 768
SWA_BAND = 4 * SWA_BLOCK
CTX_EVEN_COLS = 3 * MIX_A
LAT_EVEN_COLS = 3 * MIX_A + 512
LAT_ODD_COLS = 512 + 2 * PAIR
LAT_ODD_VROWS = 512 + PAIR


def _cparams(n_axes):
    return pltpu.CompilerParams(dimension_semantics=("arbitrary",) * n_axes, vmem_limit_bytes=VMEM_LIMIT)


def _rms(x, g):
    return x * lax.rsqrt(jnp.mean(x * x, axis=-1, keepdims=True) + EPS) * g


def _dot(a, b):
    return jnp.dot(a, b, preferred_element_type=F32)


def _layer_block(shape, li):
    zeros = (0,) * len(shape)
    return pl.BlockSpec((1,) + tuple(shape), lambda *_: (li,) + zeros, pipeline_mode=pl.Buffered(1))


def _low_half(shape):
    return (lax.broadcasted_iota(jnp.int32, shape, len(shape) - 1) & DH) == 0


def _keep_head_rows(qt, hh):
    zeros = jnp.zeros((DH, qt.shape[1]), qt.dtype)
    return jnp.concatenate([qt[0:DH], zeros] if hh == 0 else [zeros, qt[DH:PAIR]], axis=0)


def _with_ones_rows(vt):
    return jnp.concatenate([vt, jnp.ones((BF16_SUBLANES, vt.shape[1]), vt.dtype)], axis=0)


def _keys_from_transposed(kt_a, kt_b):
    return jnp.concatenate([kt_a, kt_b], axis=0).T.astype(BF16)


class _Memo(dict):
    def get_or_build(self, key, build):
        if key not in self:
            self[key] = build()
        return self[key]


class _Head:
    def __init__(self, scores, finish, sink=None):
        self.scores, self.finish, self.sink = scores, finish, sink


def _attend(heads, batch):
    batches = [heads[i:i + batch] for i in range(0, len(heads), batch)]
    cur = [hd.scores() for hd in batches[0]]
    for bi, group in enumerate(batches):
        nxt = [hd.scores() for hd in batches[bi + 1]] if bi + 1 < len(batches) else None
        exps = []
        for hd, parts in zip(group, cur):
            mx = functools.reduce(jnp.maximum, [jnp.max(s, axis=0, keepdims=True) for s in parts])
            term = None
            if hd.sink is not None:
                mx = jnp.maximum(mx, hd.sink)
                term = jnp.exp2(hd.sink - mx)
            exps.append(([jnp.exp2(s - mx).astype(BF16) for s in parts], term))
        for hd, (e_parts, term) in zip(group, exps):
            hd.finish(e_parts, term)
        cur = nxt


class _PairOut:
    def __init__(self, y_ref, rows, cols):
        self.y_ref, self.rows, self.cols, self.first = y_ref, rows, cols, None

    def put(self, hh, o, term):
        den = o[DH:DH + 1, :]
        if term is not None:
            den = den + term
        n = o[0:DH, :] * (1.0 / den)
        if hh == 0:
            self.first = n
        else:
            self.y_ref[self.rows, self.cols] = jnp.concatenate([self.first, n], axis=0).T.astype(BF16)


def _mod_kernel(cond_ref, w_ref, b_ref, o_ref):
    c = cond_ref[...]
    s = c * (1.0 / (1.0 + jnp.exp(-c)))
    o_ref[0] = _dot(s.astype(BF16), w_ref[0].astype(BF16)) + b_ref[0]


def _modulation(cond, mod_w, mod_b):
    return pl.pallas_call(
        _mod_kernel,
        grid=(DEPTH, 6 * D_MODEL // MOD_TN),
        in_specs=[
            pl.BlockSpec((MOD_ROWS, D_MODEL), lambda l, j: (0, 0)),
            pl.BlockSpec((1, D_MODEL, MOD_TN), lambda l, j: (l, 0, j)),
            pl.BlockSpec((1, 1, MOD_TN), lambda l, j: (l, 0, j)),
        ],
        out_specs=pl.BlockSpec((1, MOD_ROWS, MOD_TN), lambda l, j: (l, 0, j)),
        out_shape=jax.ShapeDtypeStruct((DEPTH, MOD_ROWS, 6 * D_MODEL), F32),
        compiler_params=_cparams(2),
        name="adaln_modulation",
    )(cond, mod_w, mod_b.reshape(DEPTH, 1, 6 * D_MODEL))


def _norm_mod(x, g, mod, shift_col):
    sh = mod[:, shift_col:shift_col + D_MODEL]
    sc = mod[:, shift_col + D_MODEL:shift_col + 2 * D_MODEL]
    return _rms(x, g) * (1.0 + sc) + sh


def _rope(z, cos, sin):
    outs = []
    for j in range(z.shape[1] // 128):
        zj = z[:, j * 128:(j + 1) * 128]
        lane = lax.broadcasted_iota(jnp.int32, zj.shape, 1)
        partner = jnp.where((lane & 16) == 0, pltpu.roll(zj, 128 - 16, axis=1), pltpu.roll(zj, 16, axis=1))
        outs.append(zj * cos + partner * sin)
    return outs[0] if len(outs) == 1 else jnp.concatenate(outs, axis=1)


def _store_transposed(dst, lead, row0, cols, pc):
    for j in range(pc.shape[1] // 128):
        dst[lead, row0 + j * 128:row0 + (j + 1) * 128, cols] = pc[:, j * 128:(j + 1) * 128].T.astype(BF16)


def _store_heads_transposed(dst, bi, pc_rows, n_pairs):
    for p in range(n_pairs):
        t = pc_rows[:, p * PAIR:(p + 1) * PAIR].T
        dst[bi, 0, 2 * p] = t[0:DH]
        dst[bi, 0, 2 * p + 1] = t[DH:PAIR]


def _normed_groups(x_ref, mod_ref, g_ref):
    rows = [slice(i * SEQ, (i + 1) * SEQ) for i in range(TM // SEQ)]
    return rows, [_norm_mod(x_ref[r, :], g_ref[0], mod_ref[0, 0], 0).astype(BF16) for r in rows]


def _k1_ctx_even(x_ref, mod_ref, g_ref, w_ref, proj_ref, qt_ref, dkt_ref, dv_ref):
    rows, hs = _normed_groups(x_ref, mod_ref, g_ref)
    for c in range(IN_EVEN // 512):
        for bi, (r, h) in enumerate(zip(rows, hs)):
            pc = _dot(h, w_ref[0, :, c * 512:(c + 1) * 512])
            if c < 3:
                proj_ref[r, c * 512:(c + 1) * 512] = pc.astype(BF16)
            elif c == 3:
                _store_transposed(qt_ref, bi, 0, slice(0, SEQ), pc * Q_SCALE)
            elif c == 4:
                for hh in range(H_B):
                    t = pc[:, hh * PAIR:(hh + 1) * PAIR].T
                    dkt_ref[bi, 0, hh, 0] = t[0:DH]
                    dkt_ref[bi, 0, hh, 1] = t[DH:PAIR]
            else:
                for hh in range(H_B):
                    dv_ref[bi, 0, hh] = pc[:, hh * PAIR:(hh + 1) * PAIR]


def _k1_lat_even(x_ref, mod_ref, g_ref, w_ref, cos_ref, sin_ref, proj_ref, qt_ref, vt_ref):
    rows, hs = _normed_groups(x_ref, mod_ref, g_ref)
    for c in range(IN_EVEN // 512):
        for r, h in zip(rows, hs):
            pc = _dot(h, w_ref[0, :, c * 512:(c + 1) * 512])
            if c < 3:
                proj_ref[r, c * 512:(c + 1) * 512] = pc.astype(BF16)
            elif c == 3:
                _store_transposed(qt_ref, 0, 0, r, _rope(pc * Q_SCALE, cos_ref[r, :], sin_ref[r, :]))
            elif c == 4:
                proj_ref[r, 3 * MIX_A:3 * MIX_A + 512] = _rope(pc, cos_ref[r, :], sin_ref[r, :]).astype(BF16)
            else:
                _store_transposed(vt_ref, 0, 0, r, pc)


def _k1_ctx_odd(x_ref, mod_ref, g_ref, w_ref, qt_ref, nkt_ref, nvt_ref, skt_ref, svt_ref):
    rows, hs = _normed_groups(x_ref, mod_ref, g_ref)
    for c in range(4):
        for bi, h in enumerate(hs):
            pc = _dot(h, w_ref[0, :, c * 512:(c + 1) * 512])
            if c == 0 or c == 3:
                _store_transposed(qt_ref, bi, (c // 3) * 512, slice(0, SEQ), pc * Q_SCALE)
            else:
                _store_heads_transposed(nkt_ref if c == 1 else nvt_ref, bi, pc, H_C // 2)
    for bi, h in enumerate(hs):
        pc = _dot(h, w_ref[0, :, 2048:IN_ODD])
        _store_heads_transposed(skt_ref, bi, pc[:, 0:PAIR], 1)
        _store_heads_transposed(svt_ref, bi, pc[:, PAIR:2 * PAIR], 1)


def _k1_lat_odd(x_ref, mod_ref, g_ref, w_ref, cos_ref, sin_ref, proj_ref, qt_ref, vt_ref):
    rows, hs = _normed_groups(x_ref, mod_ref, g_ref)
    for c in range(4):
        for r, h in zip(rows, hs):
            pc = _dot(h, w_ref[0, :, c * 512:(c + 1) * 512])
            if c == 0:
                _store_transposed(qt_ref, 0, 0, r, pc * Q_SCALE)
            elif c == 1:
                proj_ref[r, 0:512] = pc.astype(BF16)
            elif c == 2:
                _store_transposed(vt_ref, 0, 0, r, pc)
            else:
                _store_transposed(qt_ref, 0, 512, r, _rope(pc * Q_SCALE, cos_ref[r, :], sin_ref[r, :]))
    low = _low_half((SEQ, PAIR))
    for r, h in zip(rows, hs):
        pc = _dot(h, w_ref[0, :, 2048:IN_ODD])
        z = _rope(pc[:, 0:PAIR], cos_ref[r, :], sin_ref[r, :])
        zr = pltpu.roll(z, DH, axis=1)
        proj_ref[r, 512:512 + PAIR] = jnp.where(low, z, zr).astype(BF16)
        proj_ref[r, 512 + PAIR:512 + 2 * PAIR] = jnp.where(low, zr, z).astype(BF16)
        _store_transposed(vt_ref, 0, 512, r, pc[:, PAIR:2 * PAIR])


def _pre_mixer(x, mod, gains, w, li, *, latent, even, rope):
    n_tok = x.shape[0]
    n_in = w.shape[2]
    tiles_per_seq = DEC_SEQ // TM
    if latent:
        mod_map = lambda i: (li, i // tiles_per_seq, 0, 0)
    else:
        mod_map = lambda i: (li, CTX_ROW, 0, 0)
    in_specs = [
        pl.BlockSpec((TM, D_MODEL), lambda i: (i, 0)),
        pl.BlockSpec((1, 1, 1, 6 * D_MODEL), mod_map),
        pl.BlockSpec((1, 1, D_MODEL), lambda i: (li, 0, 0)),
        _layer_block((D_MODEL, n_in), 0),
    ]
    args = [x, mod, gains, w]
    nb = TM // SEQ
    q_rows = 512 if even else 1024
    if latent:
        in_specs += [pl.BlockSpec((TM, 128), lambda i: (i % tiles_per_seq, 0))] * 2
        args += list(rope)
        body = _k1_lat_even if even else _k1_lat_odd
        n_out = LAT_EVEN_COLS if even else LAT_ODD_COLS
        v_rows = 512 if even else LAT_ODD_VROWS
        t_spec = lambda n: pl.BlockSpec((1, n, TM), lambda i: (i // tiles_per_seq, 0, i % tiles_per_seq))
        out_specs = [pl.BlockSpec((TM, n_out), lambda i: (i, 0)), t_spec(q_rows), t_spec(v_rows)]
        out_shape = [jax.ShapeDtypeStruct((n_tok, n_out), BF16),
                     jax.ShapeDtypeStruct((DEC_BATCH, q_rows, DEC_SEQ), BF16),
                     jax.ShapeDtypeStruct((DEC_BATCH, v_rows, DEC_SEQ), BF16)]
    else:
        qt_spec = pl.BlockSpec((nb, q_rows, SEQ), lambda i: (i, 0, 0))
        qt_shape = jax.ShapeDtypeStruct((BATCH, q_rows, SEQ), BF16)
        if even:
            body = _k1_ctx_even
            out_specs = [pl.BlockSpec((TM, CTX_EVEN_COLS), lambda i: (i, 0)), qt_spec,
                         pl.BlockSpec((nb, 1, H_B, 2, DH, SEQ), lambda i: (i, 0, 0, 0, 0, 0)),
                         pl.BlockSpec((nb, 1, H_B, SEQ, PAIR), lambda i: (i, 0, 0, 0, 0))]
            out_shape = [jax.ShapeDtypeStruct((n_tok, CTX_EVEN_COLS), BF16), qt_shape,
                         jax.ShapeDtypeStruct((BATCH, 1, H_B, 2, DH, SEQ), F32),
                         jax.ShapeDtypeStruct((BATCH, 1, H_B, SEQ, PAIR), F32)]
        else:
            body = _k1_ctx_odd
            c_spec = pl.BlockSpec((nb, 1, H_C, DH, SEQ), lambda i: (i, 0, 0, 0, 0))
            d_spec = pl.BlockSpec((nb, 1, KV_D, DH, SEQ), lambda i: (i, 0, 0, 0, 0))
            c_shape = jax.ShapeDtypeStruct((BATCH, 1, H_C, DH, SEQ), F32)
            d_shape = jax.ShapeDtypeStruct((BATCH, 1, KV_D, DH, SEQ), F32)
            out_specs = [qt_spec, c_spec, c_spec, d_spec, d_spec]
            out_shape = [qt_shape, c_shape, c_shape, d_shape, d_shape]
    return pl.pallas_call(
        body,
        grid=(n_tok // TM,),
        in_specs=in_specs,
        out_specs=out_specs,
        out_shape=out_shape,
        compiler_params=_cparams(1),
        name=f"pre_mixer_{'lat' if latent else 'ctx'}_{'even' if even else 'odd'}",
    )(*args)


def _even_mixer_kernel(*refs, seq, n_seq, own_keys, lam_init):
    if own_keys:
        (proj_ref, qt_ref, vt_ref, cw_ref, lq1_ref, lk1_ref, lq2_ref, lk2_ref, subln_ref, ckt_ref, cv_ref,
         ya_ref, yb_ref) = refs
    else:
        (proj_ref, qt_ref, cw_ref, lq1_ref, lk1_ref, lq2_ref, lk2_ref, subln_ref, ckt_ref, cv_ref,
         ya_ref, yb_ref) = refs
        vt_ref = None
    n_rows = n_seq * seq

    pos = lax.broadcasted_iota(jnp.int32, (n_rows, 128), 0) % seq
    for j in range(MIX_A // 128):
        cols = slice(j * 128, (j + 1) * 128)
        a_b = proj_ref[:, j * 128:(j + 1) * 128].astype(F32)
        u = (proj_ref[:, MIX_A + j * 128:MIX_A + (j + 1) * 128].astype(F32)
             * proj_ref[:, 2 * MIX_A + j * 128:2 * MIX_A + (j + 1) * 128].astype(F32))
        u_prev = jnp.where(pos == 0, 0.0, pltpu.roll(u, 1, axis=0))
        u_next = jnp.where(pos == seq - 1, 0.0, pltpu.roll(u, n_rows - 1, axis=0))
        w = cw_ref[0, :, cols]
        ya_ref[:, cols] = (a_b * (w[0:1] * u_prev + w[1:2] * u + w[2:3] * u_next)).astype(BF16)

    lam = (jnp.exp(jnp.sum(lq1_ref[...] * lk1_ref[...], axis=-1, keepdims=True))
           - jnp.exp(jnp.sum(lq2_ref[...] * lk2_ref[...], axis=-1, keepdims=True)) + lam_init)
    subln_col = subln_ref[...]
    memo = _Memo()

    def key_ops(b, h):
        ops = []
        if own_keys:
            ops.append(proj_ref[b * seq:(b + 1) * seq, 3 * MIX_A + h * PAIR:3 * MIX_A + (h + 1) * PAIR])
        ops.append(_keys_from_transposed(ckt_ref[b, 0, h, 0], ckt_ref[b, 0, h, 1]))
        return ops

    def value_ops(b, h):
        ops = []
        if own_keys:
            ops.append(_with_ones_rows(vt_ref[b, h * PAIR:(h + 1) * PAIR, :]))
        ops.append(_with_ones_rows(cv_ref[b, 0, h].T.astype(BF16)))
        return ops

    def scores(b, qcols, h, m):
        qm = _keep_head_rows(qt_ref[b, h * PAIR:(h + 1) * PAIR, qcols], m)
        return [_dot(k, qm) for k in memo.get_or_build(("k", b, h), functools.partial(key_ops, b, h))]

    first_map = {}

    def finish(b, qcols, h, m, e_parts, _):
        v_ops = memo.get_or_build(("v", b, h), functools.partial(value_ops, b, h))
        o = functools.reduce(jnp.add, [_dot(v, e) for v, e in zip(v_ops, e_parts)])
        attn = o[0:PAIR, :] * (1.0 / o[PAIR:PAIR + 1, :])
        if m == 0:
            first_map[(b, qcols.start, h)] = attn
        else:
            y = first_map.pop((b, qcols.start, h)) - lam * attn
            y = y * lax.rsqrt(jnp.mean(y * y, axis=0, keepdims=True) + EPS) * subln_col * (1.0 - lam_init)
            yb_ref[b * seq + qcols.start:b * seq + qcols.stop, h * PAIR:(h + 1) * PAIR] = y.T.astype(BF16)

    heads = []
    for b in range(n_seq):
        for i in range(seq // TQ):
            qcols = slice(i * TQ, (i + 1) * TQ)
            for h in range(H_B):
                for m in range(2):
                    heads.append(_Head(functools.partial(scores, b, qcols, h, m),
                                       functools.partial(finish, b, qcols, h, m)))
    _attend(heads, batch=4)


def _even_mixer(proj, qt, vt, conv_w, lq1, lk1, lq2, lk2, subln_col, kt, v, *, seq, n_seq, lam_init):
    n_tok = proj.shape[0]
    rows = n_seq * seq
    own_keys = vt is not None
    whole = lambda a: pl.BlockSpec(a.shape, lambda b: (0,) * a.ndim)
    in_specs = [pl.BlockSpec((rows, proj.shape[1]), lambda b: (b, 0)),
                pl.BlockSpec((n_seq, 4 * PAIR, seq), lambda b: (b, 0, 0))]
    args = [proj, qt]
    if own_keys:
        in_specs.append(pl.BlockSpec((n_seq, 4 * PAIR, seq), lambda b: (b, 0, 0)))
        args.append(vt)
    in_specs += [whole(conv_w), whole(lq1), whole(lk1), whole(lq2), whole(lk2), whole(subln_col),
                 pl.BlockSpec((n_seq, 1, H_B, 2, DH, PAST_LEN), lambda b: (b, 0, 0, 0, 0, 0)),
                 pl.BlockSpec((n_seq, 1, H_B, PAST_LEN, PAIR), lambda b: (b, 0, 0, 0, 0))]
    args += [conv_w, lq1, lk1, lq2, lk2, subln_col, kt, v]
    y_spec = pl.BlockSpec((rows, 512), lambda b: (b, 0))
    y_shape = jax.ShapeDtypeStruct((n_tok, 512), BF16)
    return pl.pallas_call(
        functools.partial(_even_mixer_kernel, seq=seq, n_seq=n_seq, own_keys=own_keys, lam_init=lam_init),
        grid=(n_tok // rows,),
        in_specs=in_specs,
        out_specs=[y_spec, y_spec],
        out_shape=[y_shape, y_shape],
        compiler_params=_cparams(1),
        name=f"even_mixer_{'lat' if own_keys else 'ctx'}",
    )(*args)


def _odd_ctx_kernel(qt_ref, nkt_ref, nvt_ref, skt_ref, svt_ref, sink_ref, y_ref):
    memo = _Memo()

    def key_op(b, p):
        if 2 * p < H_C:
            return memo.get_or_build(("kc", b, p),
                                     lambda: _keys_from_transposed(nkt_ref[b, 0, 2 * p], nkt_ref[b, 0, 2 * p + 1]))
        g = (2 * p - H_C) // GQA
        return memo.get_or_build(("kd", b, g), lambda: _keys_from_transposed(skt_ref[b, 0, g], skt_ref[b, 0, g]))

    def value_op(b, head):
        if head < H_C:
            return _with_ones_rows(nvt_ref[b, 0, head].astype(BF16))
        g = (head - H_C) // GQA
        return memo.get_or_build(("vd", b, g), lambda: _with_ones_rows(svt_ref[b, 0, g].astype(BF16)))

    def scores(b, p, hh):
        return [_dot(key_op(b, p), _keep_head_rows(qt_ref[b, p * PAIR:(p + 1) * PAIR, :], hh))]

    def finish(out, b, p, hh, e_parts, term):
        out.put(hh, _dot(value_op(b, 2 * p + hh), e_parts[0]), term)

    heads = []
    for b in range(CTX_SEQS_ODD):
        for p in range((H_C + H_D) // 2):
            out = _PairOut(y_ref, slice(b * SEQ, (b + 1) * SEQ), slice(p * PAIR, (p + 1) * PAIR))
            for hh in range(2):
                d_head = 2 * p + hh - H_C
                sink = sink_ref[d_head:d_head + 1, :] * LOG2E if d_head >= 0 else None
                heads.append(_Head(functools.partial(scores, b, p, hh), functools.partial(finish, out, b, p, hh), sink))
    _attend(heads, batch=8)


def _odd_ctx_mixer(qt, nkt, nvt, skt, svt, sink):
    n_seq = CTX_SEQS_ODD
    c_spec = pl.BlockSpec((n_seq, 1, H_C, DH, SEQ), lambda b: (b, 0, 0, 0, 0))
    d_spec = pl.BlockSpec((n_seq, 1, KV_D, DH, SEQ), lambda b: (b, 0, 0, 0, 0))
    return pl.pallas_call(
        _odd_ctx_kernel,
        grid=(BATCH // n_seq,),
        in_specs=[pl.BlockSpec((n_seq, D_MODEL, SEQ), lambda b: (b, 0, 0)), c_spec, c_spec, d_spec, d_spec,
                  pl.BlockSpec((H_D, 1), lambda b: (0, 0))],
        out_specs=pl.BlockSpec((n_seq * SEQ, D_MODEL), lambda b: (b, 0)),
        out_shape=jax.ShapeDtypeStruct((BATCH * SEQ, D_MODEL), BF16),
        compiler_params=_cparams(1),
        name="odd_mixer_ctx",
    )(qt, nkt, nvt, skt, svt, sink)


def _na_bias_kernel(rpb_ref, o_ref):
    h = pl.program_id(0)
    kc = lax.broadcasted_iota(jnp.int32, (GRID_W, GRID_W), 0)
    qc = lax.broadcasted_iota(jnp.int32, (GRID_W, GRID_W), 1)
    col_start = jnp.clip(qc - NA_WIN_C // 2, 0, GRID_W - NA_WIN_C)
    col_ok = (kc >= col_start) & (kc < col_start + NA_WIN_C)
    dc = kc - qc + NA_WIN_C - 1
    n_dr, n_dc = 2 * NA_WIN_R - 1, 2 * NA_WIN_C - 1
    neg = jnp.full((GRID_W, GRID_W), NEG, F32)
    toeplitz = []
    for dr in range(n_dr):
        t = neg
        for d in range(n_dc):
            t = jnp.where(dc == d, rpb_ref[(h * n_dr + dr) * n_dc + d] * LOG2E, t)
        toeplitz.append(jnp.where(col_ok, t, NEG))
    rows_per_half = NA_HALF // GRID_W
    key_rows = NA_KEYS // GRID_W
    for half in range(2):
        for j in range(key_rows):
            rk = half * (GRID_H - key_rows) + j
            blocks = []
            for rl in range(rows_per_half):
                r = half * rows_per_half + rl
                r_start = min(max(r - NA_WIN_R // 2, 0), GRID_H - NA_WIN_R)
                inside = r_start <= rk < r_start + NA_WIN_R
                blocks.append(toeplitz[rk - r + NA_WIN_R - 1] if inside else neg)
            o_ref[0, half, j * GRID_W:(j + 1) * GRID_W, :] = jnp.concatenate(blocks, axis=1)


def _na_bias(rpb):
    return pl.pallas_call(
        _na_bias_kernel,
        grid=(H_C,),
        in_specs=[pl.BlockSpec(memory_space=pltpu.SMEM)],
        out_specs=pl.BlockSpec((1, 2, NA_KEYS, NA_HALF), lambda h: (h, 0, 0, 0)),
        out_shape=jax.ShapeDtypeStruct((H_C, 2, NA_KEYS, NA_HALF), F32),
        compiler_params=_cparams(1),
        name="na_bias",
    )(rpb.reshape(-1))


def _na_kernel(k_ref, qt_ref, vt_ref, kct_ref, vct_ref, nb_ref, y_ref):
    memo = _Memo()

    def scores(half, hh):
        w0 = half * (DEC_SEQ - NA_KEYS)
        qm = _keep_head_rows(qt_ref[0, :, half * NA_HALF:(half + 1) * NA_HALF], hh)
        ctx_keys = memo.get_or_build("kc", lambda: _keys_from_transposed(kct_ref[0, 0, 0], kct_ref[0, 0, 1]))
        return [_dot(k_ref[w0:w0 + NA_KEYS, :], qm) + nb_ref[hh, half], _dot(ctx_keys, qm)]

    def finish(out, half, hh, e_parts, term):
        w0 = half * (DEC_SEQ - NA_KEYS)
        o = (_dot(_with_ones_rows(vt_ref[0, hh * DH:(hh + 1) * DH, w0:w0 + NA_KEYS]), e_parts[0])
             + _dot(_with_ones_rows(vct_ref[0, 0, hh].astype(BF16)), e_parts[1]))
        out.put(hh, o, term)

    heads = []
    for half in range(2):
        out = _PairOut(y_ref, slice(half * NA_HALF, (half + 1) * NA_HALF), slice(0, PAIR))
        for hh in range(2):
            heads.append(_Head(functools.partial(scores, half, hh), functools.partial(finish, out, half, hh)))
    _attend(heads, batch=2)


def _na_mixer(proj, qt, vt, cache_kt, cache_vt, na_bias):
    pairs = H_C // 2
    t_spec = pl.BlockSpec((1, PAIR, DEC_SEQ), lambda hp, b: (b, hp, 0))
    cache = pl.BlockSpec((1, 1, 2, DH, PAST_LEN), lambda hp, b: (b, 0, hp, 0, 0))
    return pl.pallas_call(
        _na_kernel,
        grid=(pairs, DEC_BATCH),
        in_specs=[pl.BlockSpec((DEC_SEQ, PAIR), lambda hp, b: (b, hp)), t_spec, t_spec, cache, cache,
                  pl.BlockSpec((2, 2, NA_KEYS, NA_HALF), lambda hp, b: (hp, 0, 0, 0))],
        out_specs=pl.BlockSpec((DEC_SEQ, PAIR), lambda hp, b: (b, hp)),
        out_shape=jax.ShapeDtypeStruct((DEC_BATCH * DEC_SEQ, 512), BF16),
        compiler_params=_cparams(2),
        name="na_mixer",
    )(proj, qt, vt, cache_kt, cache_vt, na_bias)


def _swa_kernel(kk_ref, qt_ref, vt_ref, kct_ref, vct_ref, sink_ref, y_ref):
    n_groups = DEC_SEQ // TQ
    kj = lax.broadcasted_iota(jnp.int32, (SWA_BAND, TQ), 0)
    ql = lax.broadcasted_iota(jnp.int32, (SWA_BAND, TQ), 1)

    def band_bias(first_key_minus_first_query):
        return jnp.where(jnp.abs(kj + first_key_minus_first_query - ql) <= SWA_WINDOW, 0.0, NEG)

    bias_first, bias_mid, bias_last = band_bias(0), band_bias(-SWA_BLOCK), band_bias(-2 * SWA_BLOCK)
    memo = _Memo()

    def band_start(grp):
        return min(max(grp * TQ - SWA_BLOCK, 0), DEC_SEQ - SWA_BAND)

    def scores(g, grp, p, hh):
        start = band_start(grp)
        bias = bias_first if grp == 0 else (bias_last if grp == n_groups - 1 else bias_mid)
        qm = _keep_head_rows(qt_ref[0, p * PAIR:(p + 1) * PAIR, grp * TQ:(grp + 1) * TQ], hh)
        ctx_keys = memo.get_or_build(("kc", g), lambda: _keys_from_transposed(kct_ref[0, 0, g], kct_ref[0, 0, g]))
        return [_dot(kk_ref[start:start + SWA_BAND, g * PAIR:(g + 1) * PAIR], qm) + bias, _dot(ctx_keys, qm)]

    def finish(out, g, grp, hh, e_parts, term):
        start = band_start(grp)
        v_band = memo.get_or_build(("v", g, grp),
                                   lambda: _with_ones_rows(vt_ref[0, g * DH:(g + 1) * DH, start:start + SWA_BAND]))
        v_ctx = memo.get_or_build(("vc", g), lambda: _with_ones_rows(vct_ref[0, 0, g].astype(BF16)))
        out.put(hh, _dot(v_band, e_parts[0]) + _dot(v_ctx, e_parts[1]), term)

    heads = []
    for g in range(KV_D):
        for grp in range(n_groups):
            for j in range(GQA // 2):
                p = g * (GQA // 2) + j
                out = _PairOut(y_ref, slice(grp * TQ, (grp + 1) * TQ), slice(p * PAIR, (p + 1) * PAIR))
                for hh in range(2):
                    sink = sink_ref[2 * p + hh:2 * p + hh + 1, :] * LOG2E
                    heads.append(_Head(functools.partial(scores, g, grp, p, hh),
                                       functools.partial(finish, out, g, grp, hh), sink))
    _attend(heads, batch=4)


def _swa_mixer(proj, qt, vt, cache_kt, cache_vt, sink):
    cache = pl.BlockSpec((1, 1, KV_D, DH, PAST_LEN), lambda b: (b, 0, 0, 0, 0))
    return pl.pallas_call(
        _swa_kernel,
        grid=(DEC_BATCH,),
        in_specs=[pl.BlockSpec((DEC_SEQ, 2 * PAIR), lambda b: (b, 2)),
                  pl.BlockSpec((1, 512, DEC_SEQ), lambda b: (b, 1, 0)),
                  pl.BlockSpec((1, PAIR, DEC_SEQ), lambda b: (b, 4, 0)),
                  cache, cache,
                  pl.BlockSpec((H_D, 1), lambda b: (0, 0))],
        out_specs=pl.BlockSpec((DEC_SEQ, 512), lambda b: (b, 0)),
        out_shape=jax.ShapeDtypeStruct((DEC_BATCH * DEC_SEQ, 512), BF16),
        compiler_params=_cparams(1),
        name="swa_mixer",
    )(proj, qt, vt, cache_kt, cache_vt, sink)


def _post_mixer_kernel(ma_ref, mb_ref, x_ref, mod_ref, gpost_ref, gpre_ref, gmlp_ref, wo_ref, w1_ref, w2_ref, o_ref):
    half = D_MODEL // 2
    mod = mod_ref[0, 0]
    rows = [slice(i * (TM // POST_SPLIT), (i + 1) * (TM // POST_SPLIT)) for i in range(POST_SPLIT)]
    ys = [_dot(ma_ref[r, :], wo_ref[0, 0:half, :]) + _dot(mb_ref[r, :], wo_ref[0, half:D_MODEL, :]) for r in rows]
    x1 = [x_ref[r, :] + mod[:, 2 * D_MODEL:3 * D_MODEL] * _rms(y, gpost_ref[0]) for r, y in zip(rows, ys)]
    h = [_norm_mod(x, gpre_ref[0], mod, 3 * D_MODEL).astype(BF16) for x in x1]
    acc = [None] * POST_SPLIT
    pending = []

    def second_matmul(c, i, f):
        t = _dot(f, w2_ref[0, c * FF_CHUNK:(c + 1) * FF_CHUNK, :])
        acc[i] = t if acc[i] is None else acc[i] + t

    for c in range(D_FF // FF_CHUNK):
        for i in range(POST_SPLIT):
            f = _dot(h[i], w1_ref[0, :, c * FF_CHUNK:(c + 1) * FF_CHUNK])
            pending.append((c, i, jnp.square(jnp.maximum(f, 0.0)).astype(BF16)))
            if len(pending) > 1:
                second_matmul(*pending.pop(0))
    second_matmul(*pending.pop(0))
    for i, r in enumerate(rows):
        o_ref[r, :] = x1[i] + mod[:, 5 * D_MODEL:6 * D_MODEL] * _rms(acc[i], gmlp_ref[0])


def _post_mixer(mix_a, a_blk, mix_b, b_blk, x, mod, g_post, g_pre, g_mlp, w_out, w1, w2, li, *, latent):
    n_tok = x.shape[0]
    half = D_MODEL // 2
    tiles_per_seq = DEC_SEQ // TM
    if latent:
        mod_map = lambda i: (li, i // tiles_per_seq, 0, 0)
    else:
        mod_map = lambda i: (li, CTX_ROW, 0, 0)
    gain = pl.BlockSpec((1, 1, D_MODEL), lambda i: (li, 0, 0))
    return pl.pallas_call(
        _post_mixer_kernel,
        grid=(n_tok // TM,),
        in_specs=[pl.BlockSpec((TM, half), lambda i: (i, a_blk)),
                  pl.BlockSpec((TM, half), lambda i: (i, b_blk)),
                  pl.BlockSpec((TM, D_MODEL), lambda i: (i, 0)),
                  pl.BlockSpec((1, 1, 1, 6 * D_MODEL), mod_map),
                  gain, gain, gain,
                  _layer_block((D_MODEL, D_MODEL), li), _layer_block((D_MODEL, D_FF), li),
                  _layer_block((D_FF, D_MODEL), li)],
        out_specs=pl.BlockSpec((TM, D_MODEL), lambda i: (i, 0)),
        out_shape=jax.ShapeDtypeStruct((n_tok, D_MODEL), F32),
        compiler_params=_cparams(1),
        name=f"post_mixer_{'lat' if latent else 'ctx'}",
    )(mix_a, mix_b, x, mod, g_post, g_pre, g_mlp, w_out, w1, w2)


def _rope_tables():
    t = jnp.arange(DEC_SEQ)
    rows = (t // GRID_W).astype(F32)
    cols = (t % GRID_W).astype(F32)
    q4 = DH // 4
    inv = 1.0 / (ROPE_BASE ** (jnp.arange(q4, dtype=F32) / q4))
    ar, ac = rows[:, None] * inv, cols[:, None] * inv
    cos64 = jnp.concatenate([jnp.cos(ar), jnp.cos(ar), jnp.cos(ac), jnp.cos(ac)], axis=1)
    sin64 = jnp.concatenate([-jnp.sin(ar), jnp.sin(ar), -jnp.sin(ac), jnp.sin(ac)], axis=1)
    return jnp.tile(cos64, (1, 2)), jnp.tile(sin64, (1, 2))


def kernel(x_prompt, x_sample, cache_diff_k, cache_diff_v, cache_na_k, cache_na_v, cache_swa_k, cache_swa_v, c, c_ctx, mod_w, mod_b, norm_mix_pre, norm_mix_post, norm_mlp_pre, norm_mlp_post, w_in_even, conv_w, lambda_q1, lambda_k1, lambda_q2, lambda_k2, subln, w_in_odd, rpb, sink, w_out, mlp_w1, mlp_w2):
    cond = jnp.zeros((MOD_ROWS, D_MODEL), F32).at[:DEC_BATCH].set(c).at[CTX_ROW].set(c_ctx)
    mod = _modulation(cond, mod_w, mod_b).reshape(DEPTH, MOD_ROWS, 1, 6 * D_MODEL)
    rope = _rope_tables()
    na_bias = _na_bias(rpb[0])

    t_minor = lambda a: jnp.swapaxes(a, -1, -2)

    w_in = [w_in_even.astype(BF16), w_in_odd.astype(BF16)]
    w_o, w_1, w_2 = w_out.astype(BF16), mlp_w1.astype(BF16), mlp_w2.astype(BF16)
    gains = lambda a: a.reshape(DEPTH, 1, D_MODEL)
    g_pre_mix, g_post_mix, g_pre_mlp, g_post_mlp = (gains(norm_mix_pre), gains(norm_mix_post), gains(norm_mlp_pre),
                                                    gains(norm_mlp_post))

    xp = x_prompt.reshape(BATCH * SEQ, D_MODEL)
    xs = x_sample.reshape(DEC_BATCH * DEC_SEQ, D_MODEL)
    lam_init0 = 0.8 - 0.6 * math.exp(-0.3 * 0)
    lam_args = (conv_w, lambda_q1, lambda_k1, lambda_q2, lambda_k2, subln.reshape(2 * DH, 1))
    sink_col = sink[0].reshape(H_D, 1)

    def post(li, ma, a_blk, mb, b_blk, x, latent):
        return _post_mixer(ma, a_blk, mb, b_blk, x, mod, g_post_mix, g_pre_mlp, g_post_mlp, w_o, w_1, w_2, li,
                           latent=latent)

    proj, qt, new_diff_kt, new_diff_v = _pre_mixer(xp, mod, g_pre_mix, w_in[0], 0, latent=False, even=True, rope=None)
    ya, yb = _even_mixer(proj, qt, None, *lam_args, new_diff_kt, new_diff_v, seq=SEQ, n_seq=CTX_SEQS_EVEN,
                         lam_init=lam_init0)
    xp = post(0, ya, 0, yb, 0, xp, False)
    proj, qt, vt = _pre_mixer(xs, mod, g_pre_mix, w_in[0], 0, latent=True, even=True, rope=rope)
    ya, yb = _even_mixer(proj, qt, vt, *lam_args, t_minor(cache_diff_k), cache_diff_v, seq=DEC_SEQ, n_seq=1,
                         lam_init=lam_init0)
    xs = post(0, ya, 0, yb, 0, xs, True)

    qt, new_na_kt, new_na_vt, new_swa_kt, new_swa_vt = _pre_mixer(
        xp, mod, g_pre_mix, w_in[1], 1, latent=False, even=False, rope=None)
    y = _odd_ctx_mixer(qt, new_na_kt, new_na_vt, new_swa_kt, new_swa_vt, sink_col)
    xp = post(1, y, 0, y, 1, xp, False)
    proj, qt, vt = _pre_mixer(xs, mod, g_pre_mix, w_in[1], 1, latent=True, even=False, rope=rope)
    yc = _na_mixer(proj, qt, vt, t_minor(cache_na_k), t_minor(cache_na_v), na_bias)
    yd = _swa_mixer(proj, qt, vt, t_minor(cache_swa_k), t_minor(cache_swa_v), sink_col)
    xs = post(1, yc, 0, yd, 0, xs, True)

    return (xp.reshape(BATCH, SEQ, D_MODEL), xs.reshape(DEC_BATCH, DEC_SEQ, D_MODEL),
            t_minor(new_diff_kt), new_diff_v, t_minor(new_na_kt), t_minor(new_na_vt),
            t_minor(new_swa_kt), t_minor(new_swa_vt))
```

```python
import functools
import math

import jax
import jax.numpy as jnp
from jax import lax
from jax.experimental import pallas as pl
from jax.experimental.pallas import tpu as pltpu

F32 = jnp.float32
BF16 = jnp.bfloat16

D_MODEL = 1024
BATCH = 32
SEQ = 256
DEPTH = 2
DEC_BATCH = 8
DEC_SEQ = 1024
PAST_LEN = 256
GRID_W = 64
GRID_H = DEC_SEQ // GRID_W
MIX_A = 512
H_B = 4
DH = 64
PAIR = 2 * DH
H_C = 8
H_D = 8
KV_D = 2
GQA = H_D // KV_D
IN_EVEN = 3072
IN_ODD = 2304
D_FF = 4096
NA_WIN_R = 8
NA_WIN_C = 16
SWA_BLOCK = 128
SWA_WINDOW = 128
ROPE_BASE = 10000.0
EPS = 1e-6
NEG = -1e30
LOG2E = math.log2(math.e)
Q_SCALE = DH ** -0.5 * LOG2E

MOD_ROWS = 16
CTX_ROW = DEC_BATCH
V7X_VMEM_BYTES = 64 * 1024 * 1024
VMEM_LIMIT = V7X_VMEM_BYTES - 8 * 1024 * 1024

TM = 512
FF_CHUNK = 1024
POST_SPLIT = 2
CTX_SEQS_PER_STEP = 4
MOD_TN = 1536
NA_HALF = 512
NA_KEYS = 768
SWA_QROWS = 2 * SWA_BLOCK
SWA_BAND = 4 * SWA_BLOCK
LAT_ODD_COLS = IN_ODD + 2 * PAIR


def _cparams(n_axes):
    return pltpu.CompilerParams(dimension_semantics=("arbitrary",) * n_axes, vmem_limit_bytes=VMEM_LIMIT)


def _rms(x, g):
    return x * lax.rsqrt(jnp.mean(x * x, axis=-1, keepdims=True) + EPS) * g


def _dot(a, b):
    return jnp.dot(a, b, preferred_element_type=F32)


def _dot_nt(a, b):
    return lax.dot_general(a, b, (((1,), (1,)), ((), ())), preferred_element_type=F32)


def _layer_block(shape, li):
    zeros = (0,) * len(shape)
    return pl.BlockSpec((1,) + tuple(shape), lambda *_: (li,) + zeros, pipeline_mode=pl.Buffered(1))


def _low_half(shape):
    return (lax.broadcasted_iota(jnp.int32, shape, len(shape) - 1) & DH) == 0


def _keep_mask_bf16(rows, hh):
    lane = lax.broadcasted_iota(jnp.int32, (rows, PAIR), 1)
    half = (lane & DH).astype(F32).astype(BF16)
    return (half == 0) if hh == 0 else (half != 0)


def _pad_rows(t, hh, fill):
    other = jnp.full(t.shape, fill, t.dtype)
    return jnp.concatenate([t, other] if hh == 0 else [other, t], axis=0)


class _Memo(dict):
    def get_or_build(self, key, build):
        if key not in self:
            self[key] = build()
        return self[key]


def _merge_pair(o0, o1, extra_den=None):
    low = _low_half(o0.shape)
    num = jnp.where(low, o0, o1)
    den = pltpu.roll(jnp.where(low, o1, o0), DH, axis=1)
    if extra_den is not None:
        den = den + jnp.where(low, extra_den[0], extra_den[1])
    return num * (1.0 / den)


class _Head:
    def __init__(self, scores, finish, sink=None):
        self.scores, self.finish, self.sink = scores, finish, sink


def _attend(heads, batch):
    batches = [heads[i:i + batch] for i in range(0, len(heads), batch)]
    cur = [hd.scores() for hd in batches[0]]
    for bi, group in enumerate(batches):
        nxt = [hd.scores() for hd in batches[bi + 1]] if bi + 1 < len(batches) else None
        exps = []
        for hd, parts in zip(group, cur):
            chunks = [s[:, j * 128:(j + 1) * 128] for s in parts for j in range(s.shape[1] // 128)]
            m_el = functools.reduce(jnp.maximum, chunks)
            if hd.sink is not None:
                m_el = jnp.maximum(m_el, hd.sink)
            mx = jnp.broadcast_to(jnp.max(m_el, axis=-1, keepdims=True), m_el.shape)
            term = None if hd.sink is None else jnp.exp2(hd.sink - mx)
            e_parts = [jnp.concatenate([jnp.exp2(s[:, j * 128:(j + 1) * 128] - mx)
                                        for j in range(s.shape[1] // 128)], axis=1).astype(BF16) for s in parts]
            exps.append((e_parts, term))
        for hd, (e_parts, term) in zip(group, exps):
            hd.finish(e_parts, term)
        cur = nxt


class _PairSink:
    def __init__(self, y_ref, rows, cols):
        self.y_ref, self.rows, self.cols, self.first = y_ref, rows, cols, None

    def put(self, hh, o, term):
        if hh == 0:
            self.first = (o, term)
        else:
            o0, term0 = self.first
            extra = None if term is None else (term0, term)
            self.y_ref[self.rows, self.cols] = _merge_pair(o0, o, extra).astype(BF16)


def _mod_kernel(cond_ref, w_ref, b_ref, o_ref):
    c = cond_ref[...]
    s = c * (1.0 / (1.0 + jnp.exp(-c)))
    o_ref[0] = _dot(s.astype(BF16), w_ref[0].astype(BF16)) + b_ref[pl.ds(pl.program_id(0), 1), :]


def _modulation(cond, mod_w, mod_b):
    return pl.pallas_call(
        _mod_kernel,
        grid=(DEPTH, 6 * D_MODEL // MOD_TN),
        in_specs=[
            pl.BlockSpec((MOD_ROWS, D_MODEL), lambda l, j: (0, 0)),
            pl.BlockSpec((1, D_MODEL, MOD_TN), lambda l, j: (l, 0, j)),
            pl.BlockSpec((DEPTH, MOD_TN), lambda l, j: (0, j)),
        ],
        out_specs=pl.BlockSpec((1, MOD_ROWS, MOD_TN), lambda l, j: (l, 0, j)),
        out_shape=jax.ShapeDtypeStruct((DEPTH, MOD_ROWS, 6 * D_MODEL), F32),
        compiler_params=_cparams(2),
        name="adaln_modulation",
    )(cond, mod_w, mod_b)


def _norm_mod(x, g, mod, shift_col):
    sh = mod[:, shift_col:shift_col + D_MODEL]
    sc = mod[:, shift_col + D_MODEL:shift_col + 2 * D_MODEL]
    return _rms(x, g) * (1.0 + sc) + sh


def _rope(z, cos, sin):
    outs = []
    for j in range(z.shape[1] // 128):
        zj = z[:, j * 128:(j + 1) * 128]
        lane = lax.broadcasted_iota(jnp.int32, zj.shape, 1)
        partner = jnp.where((lane & 16) == 0, pltpu.roll(zj, 128 - 16, axis=1), pltpu.roll(zj, 16, axis=1))
        outs.append(zj * cos + partner * sin)
    return outs[0] if len(outs) == 1 else jnp.concatenate(outs, axis=1)


def _store_heads_transposed(dst, bi, pc_rows, n_pairs):
    for p in range(n_pairs):
        t = pc_rows[:, p * PAIR:(p + 1) * PAIR].T
        dst[bi, 0, 2 * p] = t[0:DH]
        dst[bi, 0, 2 * p + 1] = t[DH:PAIR]


def _mod_row(mod_ref, latent):
    if latent:
        return mod_ref[0, pl.ds(pl.program_id(0) // (DEC_SEQ // TM), 1), :]
    return mod_ref[0, CTX_ROW:CTX_ROW + 1, :]


def _normed_groups(x_ref, mod_ref, g_ref, li, latent):
    rows = [slice(i * SEQ, (i + 1) * SEQ) for i in range(TM // SEQ)]
    mod, g = _mod_row(mod_ref, latent), g_ref[li:li + 1, :]
    return rows, [_norm_mod(x_ref[r, :], g, mod, 0).astype(BF16) for r in rows]


def _k1_ctx_even(x_ref, mod_ref, g_ref, w_ref, proj_ref, dkt_ref, dv_ref, *, li):
    rows, hs = _normed_groups(x_ref, mod_ref, g_ref, li, False)
    for c in range(IN_EVEN // 512):
        for bi, (r, h) in enumerate(zip(rows, hs)):
            pc = _dot(h, w_ref[0, :, c * 512:(c + 1) * 512])
            if c < 3:
                proj_ref[r, c * 512:(c + 1) * 512] = pc.astype(BF16)
            elif c == 3:
                proj_ref[r, c * 512:(c + 1) * 512] = (pc * Q_SCALE).astype(BF16)
            elif c == 4:
                for hh in range(H_B):
                    t = pc[:, hh * PAIR:(hh + 1) * PAIR].T
                    dkt_ref[bi, 0, hh, 0] = t[0:DH]
                    dkt_ref[bi, 0, hh, 1] = t[DH:PAIR]
            else:
                for hh in range(H_B):
                    dv_ref[bi, 0, hh] = pc[:, hh * PAIR:(hh + 1) * PAIR]


def _k1_lat_even(x_ref, mod_ref, g_ref, w_ref, cos_ref, sin_ref, proj_ref, *, li):
    rows, hs = _normed_groups(x_ref, mod_ref, g_ref, li, True)
    for c in range(IN_EVEN // 512):
        for r, h in zip(rows, hs):
            pc = _dot(h, w_ref[0, :, c * 512:(c + 1) * 512])
            if c == 3:
                pc = _rope(pc * Q_SCALE, cos_ref[r, :], sin_ref[r, :])
            elif c == 4:
                pc = _rope(pc, cos_ref[r, :], sin_ref[r, :])
            proj_ref[r, c * 512:(c + 1) * 512] = pc.astype(BF16)


def _k1_ctx_odd(x_ref, mod_ref, g_ref, w_ref, q_ref, nkt_ref, nvt_ref, skt_ref, svt_ref, *, li):
    rows, hs = _normed_groups(x_ref, mod_ref, g_ref, li, False)
    for c in range(4):
        for bi, (r, h) in enumerate(zip(rows, hs)):
            pc = _dot(h, w_ref[0, :, c * 512:(c + 1) * 512])
            if c == 0 or c == 3:
                q_ref[r, (c // 3) * 512:(c // 3 + 1) * 512] = (pc * Q_SCALE).astype(BF16)
            else:
                _store_heads_transposed(nkt_ref if c == 1 else nvt_ref, bi, pc, H_C // 2)
    for bi, h in enumerate(hs):
        pc = _dot(h, w_ref[0, :, 2048:IN_ODD])
        _store_heads_transposed(skt_ref, bi, pc[:, 0:PAIR], 1)
        _store_heads_transposed(svt_ref, bi, pc[:, PAIR:2 * PAIR], 1)


def _k1_lat_odd(x_ref, mod_ref, g_ref, w_ref, cos_ref, sin_ref, proj_ref, *, li):
    rows, hs = _normed_groups(x_ref, mod_ref, g_ref, li, True)
    for c in range(4):
        for r, h in zip(rows, hs):
            pc = _dot(h, w_ref[0, :, c * 512:(c + 1) * 512])
            if c == 0:
                pc = pc * Q_SCALE
            elif c == 3:
                pc = _rope(pc * Q_SCALE, cos_ref[r, :], sin_ref[r, :])
            proj_ref[r, c * 512:(c + 1) * 512] = pc.astype(BF16)
    low = _low_half((SEQ, PAIR))
    for r, h in zip(rows, hs):
        pc = _dot(h, w_ref[0, :, 2048:IN_ODD])
        for j, z in enumerate((_rope(pc[:, 0:PAIR], cos_ref[r, :], sin_ref[r, :]), pc[:, PAIR:2 * PAIR])):
            zr = pltpu.roll(z, DH, axis=1)
            base = 2048 + j * 2 * PAIR
            proj_ref[r, base:base + PAIR] = jnp.where(low, z, zr).astype(BF16)
            proj_ref[r, base + PAIR:base + 2 * PAIR] = jnp.where(low, zr, z).astype(BF16)


def _pre_mixer(x, mod, gains, w, li, *, latent, even, rope):
    n_tok = x.shape[0]
    n_in = w.shape[2]
    tiles_per_seq = DEC_SEQ // TM
    in_specs = [
        pl.BlockSpec((TM, D_MODEL), lambda i: (i, 0)),
        pl.BlockSpec((1, MOD_ROWS, 6 * D_MODEL), lambda i: (li, 0, 0)),
        pl.BlockSpec((DEPTH, D_MODEL), lambda i: (0, 0)),
        _layer_block((D_MODEL, n_in), 0),
    ]
    args = [x, mod, gains, w]
    nb = TM // SEQ
    if latent:
        in_specs += [pl.BlockSpec((TM, 128), lambda i: (i % tiles_per_seq, 0))] * 2
        args += list(rope)
        body = _k1_lat_even if even else _k1_lat_odd
        n_out = IN_EVEN if even else LAT_ODD_COLS
        out_specs = pl.BlockSpec((TM, n_out), lambda i: (i, 0))
        out_shape = jax.ShapeDtypeStruct((n_tok, n_out), BF16)
    elif even:
        body = _k1_ctx_even
        out_specs = [pl.BlockSpec((TM, 4 * MIX_A), lambda i: (i, 0)),
                     pl.BlockSpec((nb, 1, H_B, 2, DH, SEQ), lambda i: (i, 0, 0, 0, 0, 0)),
                     pl.BlockSpec((nb, 1, H_B, SEQ, PAIR), lambda i: (i, 0, 0, 0, 0))]
        out_shape = [jax.ShapeDtypeStruct((n_tok, 4 * MIX_A), BF16),
                     jax.ShapeDtypeStruct((BATCH, 1, H_B, 2, DH, SEQ), F32),
                     jax.ShapeDtypeStruct((BATCH, 1, H_B, SEQ, PAIR), F32)]
    else:
        body = _k1_ctx_odd
        c_spec = pl.BlockSpec((nb, 1, H_C, DH, SEQ), lambda i: (i, 0, 0, 0, 0))
        d_spec = pl.BlockSpec((nb, 1, KV_D, DH, SEQ), lambda i: (i, 0, 0, 0, 0))
        c_shape = jax.ShapeDtypeStruct((BATCH, 1, H_C, DH, SEQ), F32)
        d_shape = jax.ShapeDtypeStruct((BATCH, 1, KV_D, DH, SEQ), F32)
        out_specs = [pl.BlockSpec((TM, D_MODEL), lambda i: (i, 0)), c_spec, c_spec, d_spec, d_spec]
        out_shape = [jax.ShapeDtypeStruct((n_tok, D_MODEL), BF16), c_shape, c_shape, d_shape, d_shape]
    return pl.pallas_call(
        functools.partial(body, li=li),
        grid=(n_tok // TM,),
        in_specs=in_specs,
        out_specs=out_specs,
        out_shape=out_shape,
        compiler_params=_cparams(1),
        name=f"pre_mixer_{'lat' if latent else 'ctx'}_{'even' if even else 'odd'}",
    )(*args)


def _even_mixer_kernel(*refs, seq, n_seq, tq, own_keys, lam_init):
    proj_ref, cw_ref, lq1_ref, lk1_ref, lq2_ref, lk2_ref, subln_ref, ckt_ref, cv_ref, ya_ref, yb_ref = refs
    n_rows = n_seq * seq

    pos = lax.broadcasted_iota(jnp.int32, (n_rows, 128), 0) % seq
    for j in range(MIX_A // 128):
        cols = slice(j * 128, (j + 1) * 128)
        a_b = proj_ref[:, j * 128:(j + 1) * 128].astype(F32)
        u = (proj_ref[:, MIX_A + j * 128:MIX_A + (j + 1) * 128].astype(F32)
             * proj_ref[:, 2 * MIX_A + j * 128:2 * MIX_A + (j + 1) * 128].astype(F32))
        u_prev = jnp.where(pos == 0, 0.0, pltpu.roll(u, 1, axis=0))
        u_next = jnp.where(pos == seq - 1, 0.0, pltpu.roll(u, n_rows - 1, axis=0))
        w = cw_ref[0, :, cols]
        ya_ref[:, cols] = (a_b * (w[0:1] * u_prev + w[1:2] * u + w[2:3] * u_next)).astype(BF16)

    lam = (jnp.exp(jnp.sum(lq1_ref[...] * lk1_ref[...], axis=-1, keepdims=True))
           - jnp.exp(jnp.sum(lq2_ref[...] * lk2_ref[...], axis=-1, keepdims=True)) + lam_init)
    subln = subln_ref[...]
    q_col, k_col, v_col = 3 * MIX_A, 3 * MIX_A + 512, 3 * MIX_A + 1024
    ones = jnp.ones((PAST_LEN, PAIR), BF16)

    memo = _Memo()

    def own_k(b, h, m):
        kp = proj_ref[b * seq:(b + 1) * seq, k_col + h * PAIR:k_col + (h + 1) * PAIR]
        return jnp.where(_keep_mask_bf16(seq, m), kp, jnp.zeros_like(kp))

    def value_ops(b, h):
        v_ops = []
        if own_keys:
            v_own = proj_ref[b * seq:(b + 1) * seq, v_col + h * PAIR:v_col + (h + 1) * PAIR]
            v_ops.append(jnp.concatenate([v_own, jnp.ones((seq, PAIR), BF16)], axis=1))
        v_ops.append(jnp.concatenate([cv_ref[b, 0, h].astype(BF16), ones], axis=1))
        return v_ops

    def scores(b, qrows, h, m):
        qp = proj_ref[qrows, q_col + h * PAIR:q_col + (h + 1) * PAIR]
        parts = []
        if own_keys:
            parts.append(_dot_nt(qp, memo.get_or_build(("k", b, h, m), functools.partial(own_k, b, h, m))))
        ckt = memo.get_or_build(("ck", b, h, m), lambda: _pad_rows(ckt_ref[b, 0, h, m].astype(BF16), m, 0.0))
        parts.append(_dot(qp, ckt))
        return parts

    first_map = {}

    def pv(b, qrows, h, m, e_parts, _):
        v_ops = memo.get_or_build(("v", b, h), functools.partial(value_ops, b, h))
        o = None
        for e, v_op in zip(e_parts, v_ops):
            t = _dot(e, v_op)
            o = t if o is None else o + t
        attn = o[:, 0:PAIR] * (1.0 / o[:, PAIR:2 * PAIR])
        if m == 0:
            first_map[(qrows.start, h)] = attn
        else:
            y = _rms(first_map.pop((qrows.start, h)) - lam * attn, subln) * (1.0 - lam_init)
            yb_ref[qrows, h * PAIR:(h + 1) * PAIR] = y.astype(BF16)

    heads = []
    for b in range(n_seq):
        for i in range(seq // tq):
            qrows = slice(b * seq + i * tq, b * seq + (i + 1) * tq)
            for h in range(H_B):
                for m in range(2):
                    heads.append(_Head(functools.partial(scores, b, qrows, h, m),
                                       functools.partial(pv, b, qrows, h, m)))
    _attend(heads, batch=4)


def _even_mixer(proj, conv_w, lq1, lk1, lq2, lk2, subln, kt, v, *, seq, n_seq, own_keys, lam_init):
    n_tok = proj.shape[0]
    rows = n_seq * seq
    small = lambda a: pl.BlockSpec((1, a.shape[1]), lambda b: (0, 0))
    in_specs = [pl.BlockSpec((rows, proj.shape[1]), lambda b: (b, 0)),
                pl.BlockSpec((1, 3, MIX_A), lambda b: (0, 0, 0)),
                small(lq1), small(lk1), small(lq2), small(lk2), small(subln),
                pl.BlockSpec((n_seq, 1, H_B, 2, DH, PAST_LEN), lambda b: (b, 0, 0, 0, 0, 0)),
                pl.BlockSpec((n_seq, 1, H_B, PAST_LEN, PAIR), lambda b: (b, 0, 0, 0, 0))]
    y_spec = pl.BlockSpec((rows, 512), lambda b: (b, 0))
    y_shape = jax.ShapeDtypeStruct((n_tok, 512), BF16)
    return pl.pallas_call(
        functools.partial(_even_mixer_kernel, seq=seq, n_seq=n_seq, tq=min(seq, 256), own_keys=own_keys,
                          lam_init=lam_init),
        grid=(n_tok // rows,),
        in_specs=in_specs,
        out_specs=[y_spec, y_spec],
        out_shape=[y_shape, y_shape],
        compiler_params=_cparams(1),
        name=f"even_mixer_{'lat' if own_keys else 'ctx'}",
    )(proj, conv_w, lq1, lk1, lq2, lk2, subln, kt, v)


def _odd_ctx_kernel(q_ref, nkt_ref, nvt_ref, skt_ref, svt_ref, sink_ref, y_ref):
    def kv_refs(b, head):
        if head < H_C:
            return nkt_ref.at[b, 0, head], nvt_ref.at[b, 0, head]
        g = (head - H_C) // GQA
        return skt_ref.at[b, 0, g], svt_ref.at[b, 0, g]

    memo = _Memo()

    def kv_key(head):
        return head if head < H_C else H_C + (head - H_C) // GQA

    def scores(b, p, hh):
        head = 2 * p + hh
        kt = memo.get_or_build(("k", b, kv_key(head), hh),
                               lambda: _pad_rows(kv_refs(b, head)[0][...].astype(BF16), hh, 0.0))
        return [_dot(q_ref[b * SEQ:(b + 1) * SEQ, p * PAIR:(p + 1) * PAIR], kt)]

    def finish(out, b, p, hh, e_parts, term):
        head = 2 * p + hh
        vt = memo.get_or_build(("v", b, kv_key(head), hh),
                               lambda: _pad_rows(kv_refs(b, head)[1][...].astype(BF16), hh, 1.0))
        out.put(hh, _dot_nt(e_parts[0], vt), term)

    heads = []
    for b in range(CTX_SEQS_PER_STEP):
        for p in range((H_C + H_D) // 2):
            out = _PairSink(y_ref, slice(b * SEQ, (b + 1) * SEQ), slice(p * PAIR, (p + 1) * PAIR))
            for hh in range(2):
                d_head = 2 * p + hh - H_C
                sink = sink_ref[d_head] * LOG2E if d_head >= 0 else None
                heads.append(_Head(functools.partial(scores, b, p, hh), functools.partial(finish, out, b, p, hh), sink))
    _attend(heads, batch=8)


def _odd_ctx_mixer(q, nkt, nvt, skt, svt, sink):
    n_tok = q.shape[0]
    n_seq = CTX_SEQS_PER_STEP
    c_spec = pl.BlockSpec((n_seq, 1, H_C, DH, SEQ), lambda b: (b, 0, 0, 0, 0))
    d_spec = pl.BlockSpec((n_seq, 1, KV_D, DH, SEQ), lambda b: (b, 0, 0, 0, 0))
    return pl.pallas_call(
        _odd_ctx_kernel,
        grid=(n_tok // (n_seq * SEQ),),
        in_specs=[pl.BlockSpec((n_seq * SEQ, D_MODEL), lambda b: (b, 0)), c_spec, c_spec, d_spec, d_spec,
                  pl.BlockSpec(memory_space=pltpu.SMEM)],
        out_specs=pl.BlockSpec((n_seq * SEQ, D_MODEL), lambda b: (b, 0)),
        out_shape=jax.ShapeDtypeStruct((n_tok, D_MODEL), BF16),
        compiler_params=_cparams(1),
        name="odd_mixer_ctx",
    )(q, nkt, nvt, skt, svt, sink)


def _build_na_bias(rpb_ref, head, nb_ref, slot):
    qc = lax.broadcasted_iota(jnp.int32, (GRID_W, GRID_W), 0)
    kc = lax.broadcasted_iota(jnp.int32, (GRID_W, GRID_W), 1)
    col_start = jnp.clip(qc - NA_WIN_C // 2, 0, GRID_W - NA_WIN_C)
    col_ok = (kc >= col_start) & (kc < col_start + NA_WIN_C)
    dc = kc - qc + NA_WIN_C - 1
    n_dr, n_dc = 2 * NA_WIN_R - 1, 2 * NA_WIN_C - 1
    neg = jnp.full((GRID_W, GRID_W), NEG, F32)
    toeplitz = []
    for dr in range(n_dr):
        t = neg
        for d in range(n_dc):
            t = jnp.where(dc == d, rpb_ref[(head * n_dr + dr) * n_dc + d] * LOG2E, t)
        toeplitz.append(jnp.where(col_ok, t, NEG))
    rows_per_half = NA_HALF // GRID_W
    key_rows = NA_KEYS // GRID_W
    for half in range(2):
        for rl in range(rows_per_half):
            r = half * rows_per_half + rl
            r_start = min(max(r - NA_WIN_R // 2, 0), GRID_H - NA_WIN_R)
            blocks = []
            for j in range(key_rows):
                rk = half * (GRID_H - key_rows) + j
                inside = r_start <= rk < r_start + NA_WIN_R
                blocks.append(toeplitz[rk - r + NA_WIN_R - 1] if inside else neg)
            nb_ref[slot, half, rl * GRID_W:(rl + 1) * GRID_W, :] = jnp.concatenate(blocks, axis=1)


def _na_kernel(rpb_ref, q_ref, k_ref, v_ref, kct_ref, vct_ref, y_ref, nb_ref):
    @pl.when(pl.program_id(1) == 0)
    def _():
        for hh in range(2):
            _build_na_bias(rpb_ref, 2 * pl.program_id(0) + hh, nb_ref, hh)

    def scores(half, hh):
        w0 = half * (DEC_SEQ - NA_KEYS)
        qp = q_ref[half * NA_HALF:(half + 1) * NA_HALF, :]
        kw = k_ref[w0:w0 + NA_KEYS, :]
        s1 = _dot_nt(qp, jnp.where(_keep_mask_bf16(NA_KEYS, hh), kw, jnp.zeros_like(kw))) + nb_ref[hh, half]
        s2 = _dot(qp, _pad_rows(kct_ref[0, 0, hh].astype(BF16), hh, 0.0))
        return [s1, s2]

    def finish(out, half, hh, e_parts, term):
        w0 = half * (DEC_SEQ - NA_KEYS)
        vw = v_ref[w0:w0 + NA_KEYS, :]
        o = (_dot(e_parts[0], jnp.where(_keep_mask_bf16(NA_KEYS, hh), vw, jnp.ones_like(vw)))
             + _dot_nt(e_parts[1], _pad_rows(vct_ref[0, 0, hh].astype(BF16), hh, 1.0)))
        out.put(hh, o, term)

    heads = []
    for half in range(2):
        out = _PairSink(y_ref, slice(half * NA_HALF, (half + 1) * NA_HALF), slice(0, PAIR))
        for hh in range(2):
            heads.append(_Head(functools.partial(scores, half, hh), functools.partial(finish, out, half, hh)))
    _attend(heads, batch=2)


def _na_mixer(proj, cache_kt, cache_vt, rpb_flat):
    n_tok = proj.shape[0]
    pairs = H_C // 2
    col = lambda base: pl.BlockSpec((DEC_SEQ, PAIR), lambda hp, b: (b, base + hp))
    cache = pl.BlockSpec((1, 1, 2, DH, PAST_LEN), lambda hp, b: (b, 0, hp, 0, 0))
    return pl.pallas_call(
        _na_kernel,
        grid=(pairs, DEC_BATCH),
        in_specs=[pl.BlockSpec(memory_space=pltpu.SMEM), col(0), col(pairs), col(2 * pairs), cache, cache],
        out_specs=pl.BlockSpec((DEC_SEQ, PAIR), lambda hp, b: (b, hp)),
        out_shape=jax.ShapeDtypeStruct((n_tok, 512), BF16),
        scratch_shapes=[pltpu.VMEM((2, 2, NA_HALF, NA_KEYS), F32)],
        compiler_params=_cparams(2),
        name="na_mixer",
    )(rpb_flat, proj, proj, proj, cache_kt, cache_vt)


def _swa_kernel(q_ref, kv_ref, kct_ref, vct_ref, sink_ref, y_ref):
    n_groups = DEC_SEQ // SWA_QROWS
    ql = lax.broadcasted_iota(jnp.int32, (SWA_QROWS, SWA_BAND), 0)
    kj = lax.broadcasted_iota(jnp.int32, (SWA_QROWS, SWA_BAND), 1)

    def band_bias(first_key_minus_first_query):
        return jnp.where(jnp.abs(kj + first_key_minus_first_query - ql) <= SWA_WINDOW, 0.0, NEG)

    bias_first, bias_mid, bias_last = band_bias(0), band_bias(-SWA_BLOCK), band_bias(-2 * SWA_BLOCK)

    def band_start(grp):
        return min(max(grp * SWA_QROWS - SWA_BLOCK, 0), DEC_SEQ - SWA_BAND)

    memo = _Memo()

    def band_k(g, grp, hh):
        start = band_start(grp)
        kb = kv_ref[start:start + SWA_BAND, g * PAIR:(g + 1) * PAIR]
        return jnp.where(_keep_mask_bf16(SWA_BAND, hh), kb, jnp.zeros_like(kb))

    def band_v(g, grp, hh):
        start = band_start(grp)
        vb = kv_ref[start:start + SWA_BAND, (KV_D + g) * PAIR:(KV_D + g + 1) * PAIR]
        return jnp.where(_keep_mask_bf16(SWA_BAND, hh), vb, jnp.ones_like(vb))

    def scores(g, grp, p, hh):
        bias = bias_first if grp == 0 else (bias_last if grp == n_groups - 1 else bias_mid)
        qp = q_ref[grp * SWA_QROWS:(grp + 1) * SWA_QROWS, p * PAIR:(p + 1) * PAIR]
        s1 = _dot_nt(qp, memo.get_or_build(("k", g, grp, hh), functools.partial(band_k, g, grp, hh))) + bias
        ckt = memo.get_or_build(("ck", g, hh), lambda: _pad_rows(kct_ref[0, 0, g].astype(BF16), hh, 0.0))
        return [s1, _dot(qp, ckt)]

    def finish(out, g, grp, hh, e_parts, term):
        cvt = memo.get_or_build(("cv", g, hh), lambda: _pad_rows(vct_ref[0, 0, g].astype(BF16), hh, 1.0))
        o = (_dot(e_parts[0], memo.get_or_build(("v", g, grp, hh), functools.partial(band_v, g, grp, hh)))
             + _dot_nt(e_parts[1], cvt))
        out.put(hh, o, term)

    heads = []
    for g in range(KV_D):
        for grp in range(n_groups):
            for j in range(GQA // 2):
                p = g * (GQA // 2) + j
                out = _PairSink(y_ref, slice(grp * SWA_QROWS, (grp + 1) * SWA_QROWS), slice(p * PAIR, (p + 1) * PAIR))
                for hh in range(2):
                    sink = sink_ref[2 * p + hh] * LOG2E
                    heads.append(_Head(functools.partial(scores, g, grp, p, hh),
                                       functools.partial(finish, out, g, grp, hh), sink))
    _attend(heads, batch=4)


def _swa_mixer(proj, cache_kt, cache_vt, sink):
    n_tok = proj.shape[0]
    cache = pl.BlockSpec((1, 1, KV_D, DH, PAST_LEN), lambda b: (b, 0, 0, 0, 0))
    return pl.pallas_call(
        _swa_kernel,
        grid=(DEC_BATCH,),
        in_specs=[pl.BlockSpec((DEC_SEQ, 512), lambda b: (b, 3)),
                  pl.BlockSpec((DEC_SEQ, 512), lambda b: (b, 4)),
                  cache, cache,
                  pl.BlockSpec(memory_space=pltpu.SMEM)],
        out_specs=pl.BlockSpec((DEC_SEQ, 512), lambda b: (b, 0)),
        out_shape=jax.ShapeDtypeStruct((n_tok, 512), BF16),
        compiler_params=_cparams(1),
        name="swa_mixer",
    )(proj, proj, cache_kt, cache_vt, sink)


def _post_mixer_kernel(ma_ref, mb_ref, x_ref, mod_ref, gpost_ref, gpre_ref, gmlp_ref, wo_ref, w1_ref, w2_ref, o_ref, *,
                       li, latent):
    half = D_MODEL // 2
    mod = _mod_row(mod_ref, latent)
    g_post, g_pre, g_mlp = gpost_ref[li:li + 1, :], gpre_ref[li:li + 1, :], gmlp_ref[li:li + 1, :]
    rows = [slice(i * (TM // POST_SPLIT), (i + 1) * (TM // POST_SPLIT)) for i in range(POST_SPLIT)]
    ys = [_dot(ma_ref[r, :], wo_ref[0, 0:half, :]) + _dot(mb_ref[r, :], wo_ref[0, half:D_MODEL, :]) for r in rows]
    x1 = [x_ref[r, :] + mod[:, 2 * D_MODEL:3 * D_MODEL] * _rms(y, g_post) for r, y in zip(rows, ys)]
    h = [_norm_mod(x, g_pre, mod, 3 * D_MODEL).astype(BF16) for x in x1]
    acc = [None] * POST_SPLIT
    pending = []

    def second_matmul(c, i, f):
        t = _dot(f, w2_ref[0, c * FF_CHUNK:(c + 1) * FF_CHUNK, :])
        acc[i] = t if acc[i] is None else acc[i] + t

    for c in range(D_FF // FF_CHUNK):
        for i in range(POST_SPLIT):
            f = _dot(h[i], w1_ref[0, :, c * FF_CHUNK:(c + 1) * FF_CHUNK])
            pending.append((c, i, jnp.square(jnp.maximum(f, 0.0)).astype(BF16)))
            if len(pending) > 1:
                second_matmul(*pending.pop(0))
    second_matmul(*pending.pop(0))
    for i, r in enumerate(rows):
        o_ref[r, :] = x1[i] + mod[:, 5 * D_MODEL:6 * D_MODEL] * _rms(acc[i], g_mlp)


def _post_mixer(mix_a, a_blk, mix_b, b_blk, x, mod, g_post, g_pre, g_mlp, w_out, w1, w2, li, *, latent):
    n_tok = x.shape[0]
    half = D_MODEL // 2
    gain = pl.BlockSpec((DEPTH, D_MODEL), lambda i: (0, 0))
    return pl.pallas_call(
        functools.partial(_post_mixer_kernel, li=li, latent=latent),
        grid=(n_tok // TM,),
        in_specs=[pl.BlockSpec((TM, half), lambda i: (i, a_blk)),
                  pl.BlockSpec((TM, half), lambda i: (i, b_blk)),
                  pl.BlockSpec((TM, D_MODEL), lambda i: (i, 0)),
                  pl.BlockSpec((1, MOD_ROWS, 6 * D_MODEL), lambda i: (li, 0, 0)),
                  gain, gain, gain,
                  _layer_block((D_MODEL, D_MODEL), li), _layer_block((D_MODEL, D_FF), li),
                  _layer_block((D_FF, D_MODEL), li)],
        out_specs=pl.BlockSpec((TM, D_MODEL), lambda i: (i, 0)),
        out_shape=jax.ShapeDtypeStruct((n_tok, D_MODEL), F32),
        compiler_params=_cparams(1),
        name=f"post_mixer_{'lat' if latent else 'ctx'}",
    )(mix_a, mix_b, x, mod, g_post, g_pre, g_mlp, w_out, w1, w2)


def _rope_tables():
    t = jnp.arange(DEC_SEQ)
    rows = (t // GRID_W).astype(F32)
    cols = (t % GRID_W).astype(F32)
    q4 = DH // 4
    inv = 1.0 / (ROPE_BASE ** (jnp.arange(q4, dtype=F32) / q4))
    ar, ac = rows[:, None] * inv, cols[:, None] * inv
    cos64 = jnp.concatenate([jnp.cos(ar), jnp.cos(ar), jnp.cos(ac), jnp.cos(ac)], axis=1)
    sin64 = jnp.concatenate([-jnp.sin(ar), jnp.sin(ar), -jnp.sin(ac), jnp.sin(ac)], axis=1)
    return jnp.tile(cos64, (1, 2)), jnp.tile(sin64, (1, 2))


def kernel(x_prompt, x_sample, cache_diff_k, cache_diff_v, cache_na_k, cache_na_v, cache_swa_k, cache_swa_v, c, c_ctx, mod_w, mod_b, norm_mix_pre, norm_mix_post, norm_mlp_pre, norm_mlp_post, w_in_even, conv_w, lambda_q1, lambda_k1, lambda_q2, lambda_k2, subln, w_in_odd, rpb, sink, w_out, mlp_w1, mlp_w2):
    cond = jnp.concatenate([c, c_ctx[None, :], jnp.zeros((MOD_ROWS - DEC_BATCH - 1, D_MODEL), F32)], axis=0)
    mod = _modulation(cond, mod_w, mod_b)
    rope = _rope_tables()

    t_minor = lambda a: jnp.swapaxes(a, -1, -2)

    w_in = [w_in_even.astype(BF16), w_in_odd.astype(BF16)]
    w_o, w_1, w_2 = w_out.astype(BF16), mlp_w1.astype(BF16), mlp_w2.astype(BF16)

    xp = x_prompt.reshape(BATCH * SEQ, D_MODEL)
    xs = x_sample.reshape(DEC_BATCH * DEC_SEQ, D_MODEL)
    lam_init0 = 0.8 - 0.6 * math.exp(-0.3 * 0)
    lam_args = (conv_w, lambda_q1, lambda_k1, lambda_q2, lambda_k2, subln)
    sink_flat = sink.reshape(-1)

    def post(li, ma, a_blk, mb, b_blk, x, latent):
        return _post_mixer(ma, a_blk, mb, b_blk, x, mod, norm_mix_post, norm_mlp_pre, norm_mlp_post, w_o, w_1, w_2,
                           li, latent=latent)

    proj, new_diff_kt, new_diff_v = _pre_mixer(xp, mod, norm_mix_pre, w_in[0], 0, latent=False, even=True, rope=None)
    ya, yb = _even_mixer(proj, *lam_args, new_diff_kt, new_diff_v, seq=SEQ, n_seq=CTX_SEQS_PER_STEP // 2,
                         own_keys=False, lam_init=lam_init0)
    xp = post(0, ya, 0, yb, 0, xp, False)
    proj = _pre_mixer(xs, mod, norm_mix_pre, w_in[0], 0, latent=True, even=True, rope=rope)
    ya, yb = _even_mixer(proj, *lam_args, t_minor(cache_diff_k), cache_diff_v, seq=DEC_SEQ, n_seq=1, own_keys=True,
                         lam_init=lam_init0)
    xs = post(0, ya, 0, yb, 0, xs, True)

    q, new_na_kt, new_na_vt, new_swa_kt, new_swa_vt = _pre_mixer(
        xp, mod, norm_mix_pre, w_in[1], 1, latent=False, even=False, rope=None)
    y = _odd_ctx_mixer(q, new_na_kt, new_na_vt, new_swa_kt, new_swa_vt, sink_flat)
    xp = post(1, y, 0, y, 1, xp, False)
    proj = _pre_mixer(xs, mod, norm_mix_pre, w_in[1], 1, latent=True, even=False, rope=rope)
    yc = _na_mixer(proj, t_minor(cache_na_k), t_minor(cache_na_v), rpb.reshape(-1))
    yd = _swa_mixer(proj, t_minor(cache_swa_k), t_minor(cache_swa_v), sink_flat)
    xs = post(1, yc, 0, yd, 0, xs, True)

    return (xp.reshape(BATCH, SEQ, D_MODEL), xs.reshape(DEC_BATCH, DEC_SEQ, D_MODEL),
            t_minor(new_diff_kt), new_diff_v, t_minor(new_na_kt), t_minor(new_na_vt),
            t_minor(new_swa_kt), t_minor(new_swa_vt))
```

```python
import functools
import math

import jax
import jax.numpy as jnp
from jax import lax
from jax.experimental import pallas as pl
from jax.experimental.pallas import tpu as pltpu

F32 = jnp.float32
BF16 = jnp.bfloat16

D_MODEL = 1024
BATCH = 32
SEQ = 256
DEPTH = 2
DEC_BATCH = 8
DEC_SEQ = 1024
PAST_LEN = 256
GRID_W = 64
GRID_H = DEC_SEQ // GRID_W
MIX_A = 512
H_B = 4
DH = 64
PAIR = 2 * DH
H_C = 8
H_D = 8
KV_D = 2
GQA = H_D // KV_D
IN_EVEN = 3072
IN_ODD = 2304
D_FF = 4096
NA_WIN_R = 8
NA_WIN_C = 16
SWA_BLOCK = 128
SWA_WINDOW = 128
ROPE_BASE = 10000.0
EPS = 1e-6
NEG = -1e30
LOG2E = math.log2(math.e)
Q_SCALE = DH ** -0.5 * LOG2E

MOD_ROWS = 16
CTX_ROW = DEC_BATCH
V7X_VMEM_BYTES = 64 * 1024 * 1024
VMEM_LIMIT = V7X_VMEM_BYTES - 8 * 1024 * 1024

CTX_TOKENS = BATCH * SEQ
LAT_TOKENS = DEC_BATCH * DEC_SEQ
ALL_TOKENS = CTX_TOKENS + LAT_TOKENS

TM = 512
FF_CHUNK = 1024
POST_SPLIT = 2
WCHUNK = 512
N_PREP = (D_MODEL + 2 * D_FF) // WCHUNK
CTX_TILES = CTX_TOKENS // TM
CTX_SEQS_PER_STEP = 4
MOD_TN = 1536
NA_HALF = 512
NA_KEYS = 768
SWA_QROWS = 2 * SWA_BLOCK
SWA_BAND = 4 * SWA_BLOCK
LAT_ODD_COLS = IN_ODD + 2 * PAIR


def _cparams(n_axes):
    return pltpu.CompilerParams(dimension_semantics=("arbitrary",) * n_axes, vmem_limit_bytes=VMEM_LIMIT)


def _rms(x, g):
    return x * lax.rsqrt(jnp.mean(x * x, axis=-1, keepdims=True) + EPS) * g


def _dot(a, b):
    return jnp.dot(a, b, preferred_element_type=F32)


def _dot_nt(a, b):
    return lax.dot_general(a, b, (((1,), (1,)), ((), ())), preferred_element_type=F32)


def _layer_block(shape, li):
    zeros = (0,) * len(shape)
    return pl.BlockSpec((1,) + tuple(shape), lambda *_: (li,) + zeros, pipeline_mode=pl.Buffered(1))


def _low_half(shape):
    return (lax.broadcasted_iota(jnp.int32, shape, len(shape) - 1) & DH) == 0


def _keep_mask_bf16(rows, hh):
    lane = lax.broadcasted_iota(jnp.int32, (rows, PAIR), 1)
    half = (lane & DH).astype(F32).astype(BF16)
    return (half == 0) if hh == 0 else (half != 0)


def _pad_rows(t, hh, fill):
    other = jnp.full(t.shape, fill, t.dtype)
    return jnp.concatenate([t, other] if hh == 0 else [other, t], axis=0)


class _Memo(dict):
    def get_or_build(self, key, build):
        if key not in self:
            self[key] = build()
        return self[key]


def _merge_pair(o0, o1, extra_den=None):
    low = _low_half(o0.shape)
    num = jnp.where(low, o0, o1)
    den = pltpu.roll(jnp.where(low, o1, o0), DH, axis=1)
    if extra_den is not None:
        den = den + jnp.where(low, extra_den[0], extra_den[1])
    return num * (1.0 / den)


class _Head:
    def __init__(self, scores, finish, sink=None):
        self.scores, self.finish, self.sink = scores, finish, sink


def _attend(heads, batch):
    batches = [heads[i:i + batch] for i in range(0, len(heads), batch)]
    cur = [hd.scores() for hd in batches[0]]
    for bi, group in enumerate(batches):
        nxt = [hd.scores() for hd in batches[bi + 1]] if bi + 1 < len(batches) else None
        exps = []
        for hd, parts in zip(group, cur):
            chunks = [s[:, j * 128:(j + 1) * 128] for s in parts for j in range(s.shape[1] // 128)]
            m_el = functools.reduce(jnp.maximum, chunks)
            if hd.sink is not None:
                m_el = jnp.maximum(m_el, hd.sink)
            mx = jnp.broadcast_to(jnp.max(m_el, axis=-1, keepdims=True), m_el.shape)
            term = None if hd.sink is None else jnp.exp2(hd.sink - mx)
            e_parts = [jnp.concatenate([jnp.exp2(s[:, j * 128:(j + 1) * 128] - mx)
                                        for j in range(s.shape[1] // 128)], axis=1).astype(BF16) for s in parts]
            exps.append((e_parts, term))
        for hd, (e_parts, term) in zip(group, exps):
            hd.finish(e_parts, term)
        cur = nxt


class _PairSink:
    def __init__(self, y_ref, rows, cols):
        self.y_ref, self.rows, self.cols, self.first = y_ref, rows, cols, None

    def put(self, hh, o, term):
        if hh == 0:
            self.first = (o, term)
        else:
            o0, term0 = self.first
            extra = None if term is None else (term0, term)
            self.y_ref[self.rows, self.cols] = _merge_pair(o0, o, extra).astype(BF16)


def _mod_kernel(cond_ref, w_ref, b_ref, o_ref):
    c = cond_ref[...]
    s = c * (1.0 / (1.0 + jnp.exp(-c)))
    o_ref[0] = _dot(s.astype(BF16), w_ref[0].astype(BF16)) + b_ref[pl.ds(pl.program_id(0), 1), :]


def _modulation(cond, mod_w, mod_b):
    return pl.pallas_call(
        _mod_kernel,
        grid=(DEPTH, 6 * D_MODEL // MOD_TN),
        in_specs=[
            pl.BlockSpec((MOD_ROWS, D_MODEL), lambda l, j: (0, 0)),
            pl.BlockSpec((1, D_MODEL, MOD_TN), lambda l, j: (l, 0, j)),
            pl.BlockSpec((DEPTH, MOD_TN), lambda l, j: (0, j)),
        ],
        out_specs=pl.BlockSpec((1, MOD_ROWS, MOD_TN), lambda l, j: (l, 0, j)),
        out_shape=jax.ShapeDtypeStruct((DEPTH, MOD_ROWS, 6 * D_MODEL), F32),
        compiler_params=_cparams(2),
        name="adaln_modulation",
    )(cond, mod_w, mod_b)


def _norm_mod(x, g, mod, shift_col):
    sh = mod[:, shift_col:shift_col + D_MODEL]
    sc = mod[:, shift_col + D_MODEL:shift_col + 2 * D_MODEL]
    return _rms(x, g) * (1.0 + sc) + sh


def _rope(z, cos, sin):
    outs = []
    for j in range(z.shape[1] // 128):
        zj = z[:, j * 128:(j + 1) * 128]
        lane = lax.broadcasted_iota(jnp.int32, zj.shape, 1)
        partner = jnp.where((lane & 16) == 0, pltpu.roll(zj, 128 - 16, axis=1), pltpu.roll(zj, 16, axis=1))
        outs.append(zj * cos + partner * sin)
    return outs[0] if len(outs) == 1 else jnp.concatenate(outs, axis=1)


def _store_heads_transposed(dst, bi, pc_rows, n_pairs):
    for p in range(n_pairs):
        t = pc_rows[:, p * PAIR:(p + 1) * PAIR].T
        dst[bi, 0, 2 * p] = t[0:DH]
        dst[bi, 0, 2 * p + 1] = t[DH:PAIR]


def _mod_row(mod_ref, latent):
    if latent:
        return mod_ref[0, pl.ds(pl.program_id(0) // (DEC_SEQ // TM), 1), :]
    return mod_ref[0, CTX_ROW:CTX_ROW + 1, :]


def _normed_groups(x_ref, mod_ref, g_ref, li, latent):
    rows = [slice(i * SEQ, (i + 1) * SEQ) for i in range(TM // SEQ)]
    mod, g = _mod_row(mod_ref, latent), g_ref[li:li + 1, :]
    return rows, [_norm_mod(x_ref[r, :], g, mod, 0).astype(BF16) for r in rows]


def _k1_ctx_even(x_ref, mod_ref, g_ref, w_ref, proj_ref, dkt_ref, dv_ref, *, li):
    rows, hs = _normed_groups(x_ref, mod_ref, g_ref, li, False)
    for c in range(IN_EVEN // 512):
        for bi, (r, h) in enumerate(zip(rows, hs)):
            pc = _dot(h, w_ref[0, :, c * 512:(c + 1) * 512])
            if c < 3:
                proj_ref[r, c * 512:(c + 1) * 512] = pc.astype(BF16)
            elif c == 3:
                proj_ref[r, c * 512:(c + 1) * 512] = (pc * Q_SCALE).astype(BF16)
            elif c == 4:
                for hh in range(H_B):
                    t = pc[:, hh * PAIR:(hh + 1) * PAIR].T
                    dkt_ref[bi, 0, hh, 0] = t[0:DH]
                    dkt_ref[bi, 0, hh, 1] = t[DH:PAIR]
            else:
                for hh in range(H_B):
                    dv_ref[bi, 0, hh] = pc[:, hh * PAIR:(hh + 1) * PAIR]


def _k1_lat_even(x_ref, mod_ref, g_ref, w_ref, cos_ref, sin_ref, proj_ref, *, li):
    rows, hs = _normed_groups(x_ref, mod_ref, g_ref, li, True)
    for c in range(IN_EVEN // 512):
        for r, h in zip(rows, hs):
            pc = _dot(h, w_ref[0, :, c * 512:(c + 1) * 512])
            if c == 3:
                pc = _rope(pc * Q_SCALE, cos_ref[r, :], sin_ref[r, :])
            elif c == 4:
                pc = _rope(pc, cos_ref[r, :], sin_ref[r, :])
            proj_ref[r, c * 512:(c + 1) * 512] = pc.astype(BF16)


def _k1_ctx_odd(x_ref, mod_ref, g_ref, w_ref, q_ref, nkt_ref, nvt_ref, skt_ref, svt_ref, *, li):
    rows, hs = _normed_groups(x_ref, mod_ref, g_ref, li, False)
    for c in range(4):
        for bi, (r, h) in enumerate(zip(rows, hs)):
            pc = _dot(h, w_ref[0, :, c * 512:(c + 1) * 512])
            if c == 0 or c == 3:
                q_ref[r, (c // 3) * 512:(c // 3 + 1) * 512] = (pc * Q_SCALE).astype(BF16)
            else:
                _store_heads_transposed(nkt_ref if c == 1 else nvt_ref, bi, pc, H_C // 2)
    for bi, h in enumerate(hs):
        pc = _dot(h, w_ref[0, :, 2048:IN_ODD])
        _store_heads_transposed(skt_ref, bi, pc[:, 0:PAIR], 1)
        _store_heads_transposed(svt_ref, bi, pc[:, PAIR:2 * PAIR], 1)


def _k1_lat_odd(x_ref, mod_ref, g_ref, w_ref, cos_ref, sin_ref, proj_ref, *, li):
    rows, hs = _normed_groups(x_ref, mod_ref, g_ref, li, True)
    for c in range(4):
        for r, h in zip(rows, hs):
            pc = _dot(h, w_ref[0, :, c * 512:(c + 1) * 512])
            if c == 0:
                pc = pc * Q_SCALE
            elif c == 3:
                pc = _rope(pc * Q_SCALE, cos_ref[r, :], sin_ref[r, :])
            proj_ref[r, c * 512:(c + 1) * 512] = pc.astype(BF16)
    low = _low_half((SEQ, PAIR))
    for r, h in zip(rows, hs):
        pc = _dot(h, w_ref[0, :, 2048:IN_ODD])
        for j, z in enumerate((_rope(pc[:, 0:PAIR], cos_ref[r, :], sin_ref[r, :]), pc[:, PAIR:2 * PAIR])):
            zr = pltpu.roll(z, DH, axis=1)
            base = 2048 + j * 2 * PAIR
            proj_ref[r, base:base + PAIR] = jnp.where(low, z, zr).astype(BF16)
            proj_ref[r, base + PAIR:base + 2 * PAIR] = jnp.where(low, zr, z).astype(BF16)


def _pre_mixer(x, mod, gains, w, li, *, latent, even, rope, n_tok, tile0=0):
    n_in = w.shape[2]
    tiles_per_seq = DEC_SEQ // TM
    in_specs = [
        pl.BlockSpec((TM, D_MODEL), lambda i: (i + tile0, 0)),
        pl.BlockSpec((1, MOD_ROWS, 6 * D_MODEL), lambda i: (li, 0, 0)),
        pl.BlockSpec((DEPTH, D_MODEL), lambda i: (0, 0)),
        _layer_block((D_MODEL, n_in), 0),
    ]
    args = [x, mod, gains, w]
    nb = TM // SEQ
    if latent:
        in_specs += [pl.BlockSpec((TM, 128), lambda i: (i % tiles_per_seq, 0))] * 2
        args += list(rope)
        body = _k1_lat_even if even else _k1_lat_odd
        n_out = IN_EVEN if even else LAT_ODD_COLS
        out_specs = pl.BlockSpec((TM, n_out), lambda i: (i, 0))
        out_shape = jax.ShapeDtypeStruct((n_tok, n_out), BF16)
    elif even:
        body = _k1_ctx_even
        out_specs = [pl.BlockSpec((TM, 4 * MIX_A), lambda i: (i, 0)),
                     pl.BlockSpec((nb, 1, H_B, 2, DH, SEQ), lambda i: (i, 0, 0, 0, 0, 0)),
                     pl.BlockSpec((nb, 1, H_B, SEQ, PAIR), lambda i: (i, 0, 0, 0, 0))]
        out_shape = [jax.ShapeDtypeStruct((n_tok, 4 * MIX_A), BF16),
                     jax.ShapeDtypeStruct((BATCH, 1, H_B, 2, DH, SEQ), F32),
                     jax.ShapeDtypeStruct((BATCH, 1, H_B, SEQ, PAIR), F32)]
    else:
        body = _k1_ctx_odd
        c_spec = pl.BlockSpec((nb, 1, H_C, DH, SEQ), lambda i: (i, 0, 0, 0, 0))
        d_spec = pl.BlockSpec((nb, 1, KV_D, DH, SEQ), lambda i: (i, 0, 0, 0, 0))
        c_shape = jax.ShapeDtypeStruct((BATCH, 1, H_C, DH, SEQ), F32)
        d_shape = jax.ShapeDtypeStruct((BATCH, 1, KV_D, DH, SEQ), F32)
        out_specs = [pl.BlockSpec((TM, D_MODEL), lambda i: (i, 0)), c_spec, c_spec, d_spec, d_spec]
        out_shape = [jax.ShapeDtypeStruct((n_tok, D_MODEL), BF16), c_shape, c_shape, d_shape, d_shape]
    return pl.pallas_call(
        functools.partial(body, li=li),
        grid=(n_tok // TM,),
        in_specs=in_specs,
        out_specs=out_specs,
        out_shape=out_shape,
        compiler_params=_cparams(1),
        name=f"pre_mixer_{'lat' if latent else 'ctx'}_{'even' if even else 'odd'}",
    )(*args)


def _even_mixer_kernel(*refs, seq, n_seq, tq, own_keys, lam_init):
    proj_ref, cw_ref, lq1_ref, lk1_ref, lq2_ref, lk2_ref, subln_ref, ckt_ref, cv_ref = refs[:9]
    y_ref = refs[-1]
    n_rows = n_seq * seq

    pos = lax.broadcasted_iota(jnp.int32, (n_rows, 128), 0) % seq
    for j in range(MIX_A // 128):
        cols = slice(j * 128, (j + 1) * 128)
        a_b = proj_ref[:, j * 128:(j + 1) * 128].astype(F32)
        u = (proj_ref[:, MIX_A + j * 128:MIX_A + (j + 1) * 128].astype(F32)
             * proj_ref[:, 2 * MIX_A + j * 128:2 * MIX_A + (j + 1) * 128].astype(F32))
        u_prev = jnp.where(pos == 0, 0.0, pltpu.roll(u, 1, axis=0))
        u_next = jnp.where(pos == seq - 1, 0.0, pltpu.roll(u, n_rows - 1, axis=0))
        w = cw_ref[0, :, cols]
        y_ref[:, cols] = (a_b * (w[0:1] * u_prev + w[1:2] * u + w[2:3] * u_next)).astype(BF16)

    lam = (jnp.exp(jnp.sum(lq1_ref[...] * lk1_ref[...], axis=-1, keepdims=True))
           - jnp.exp(jnp.sum(lq2_ref[...] * lk2_ref[...], axis=-1, keepdims=True)) + lam_init)
    subln = subln_ref[...]
    q_col, k_col, v_col = 3 * MIX_A, 3 * MIX_A + 512, 3 * MIX_A + 1024
    ones = jnp.ones((PAST_LEN, PAIR), BF16)

    memo = _Memo()

    def own_k(b, h, m):
        kp = proj_ref[b * seq:(b + 1) * seq, k_col + h * PAIR:k_col + (h + 1) * PAIR]
        return jnp.where(_keep_mask_bf16(seq, m), kp, jnp.zeros_like(kp))

    def value_ops(b, h):
        v_ops = []
        if own_keys:
            v_own = proj_ref[b * seq:(b + 1) * seq, v_col + h * PAIR:v_col + (h + 1) * PAIR]
            v_ops.append(jnp.concatenate([v_own, jnp.ones((seq, PAIR), BF16)], axis=1))
        v_ops.append(jnp.concatenate([cv_ref[b, 0, h].astype(BF16), ones], axis=1))
        return v_ops

    def scores(b, qrows, h, m):
        qp = proj_ref[qrows, q_col + h * PAIR:q_col + (h + 1) * PAIR]
        parts = []
        if own_keys:
            parts.append(_dot_nt(qp, memo.get_or_build(("k", b, h, m), functools.partial(own_k, b, h, m))))
        ckt = memo.get_or_build(("ck", b, h, m), lambda: _pad_rows(ckt_ref[b, 0, h, m].astype(BF16), m, 0.0))
        parts.append(_dot(qp, ckt))
        return parts

    first_map = {}

    def pv(b, qrows, h, m, e_parts, _):
        v_ops = memo.get_or_build(("v", b, h), functools.partial(value_ops, b, h))
        o = None
        for e, v_op in zip(e_parts, v_ops):
            t = _dot(e, v_op)
            o = t if o is None else o + t
        attn = o[:, 0:PAIR] * (1.0 / o[:, PAIR:2 * PAIR])
        if m == 0:
            first_map[(qrows.start, h)] = attn
        else:
            y = _rms(first_map.pop((qrows.start, h)) - lam * attn, subln) * (1.0 - lam_init)
            y_ref[qrows, MIX_A + h * PAIR:MIX_A + (h + 1) * PAIR] = y.astype(BF16)

    heads = []
    for b in range(n_seq):
        for i in range(seq // tq):
            qrows = slice(b * seq + i * tq, b * seq + (i + 1) * tq)
            for h in range(H_B):
                for m in range(2):
                    heads.append(_Head(functools.partial(scores, b, qrows, h, m),
                                       functools.partial(pv, b, qrows, h, m)))
    _attend(heads, batch=4)


def _even_mixer(proj, conv_w, lq1, lk1, lq2, lk2, subln, kt, v, mix, *, seq, n_seq, own_keys, lam_init):
    n_tok = proj.shape[0]
    rows = n_seq * seq
    row0 = 0 if mix is None else CTX_TOKENS // rows
    small = lambda a: pl.BlockSpec((1, a.shape[1]), lambda b: (0, 0))
    in_specs = [pl.BlockSpec((rows, proj.shape[1]), lambda b: (b, 0)),
                pl.BlockSpec((1, 3, MIX_A), lambda b: (0, 0, 0)),
                small(lq1), small(lk1), small(lq2), small(lk2), small(subln),
                pl.BlockSpec((n_seq, 1, H_B, 2, DH, PAST_LEN), lambda b: (b, 0, 0, 0, 0, 0)),
                pl.BlockSpec((n_seq, 1, H_B, PAST_LEN, PAIR), lambda b: (b, 0, 0, 0, 0))]
    args = [proj, conv_w, lq1, lk1, lq2, lk2, subln, kt, v]
    aliases = {}
    if mix is not None:
        in_specs.append(pl.BlockSpec(memory_space=pl.ANY))
        aliases = {len(args): 0}
        args.append(mix)
    return pl.pallas_call(
        functools.partial(_even_mixer_kernel, seq=seq, n_seq=n_seq, tq=min(seq, 256), own_keys=own_keys,
                          lam_init=lam_init),
        grid=(n_tok // rows,),
        in_specs=in_specs,
        out_specs=pl.BlockSpec((rows, D_MODEL), lambda b: (b + row0, 0)),
        out_shape=jax.ShapeDtypeStruct((ALL_TOKENS, D_MODEL), BF16),
        input_output_aliases=aliases,
        compiler_params=_cparams(1),
        name=f"even_mixer_{'lat' if own_keys else 'ctx'}",
    )(*args)


def _odd_ctx_kernel(q_ref, nkt_ref, nvt_ref, skt_ref, svt_ref, sink_ref, y_ref):
    def kv_refs(b, head):
        if head < H_C:
            return nkt_ref.at[b, 0, head], nvt_ref.at[b, 0, head]
        g = (head - H_C) // GQA
        return skt_ref.at[b, 0, g], svt_ref.at[b, 0, g]

    memo = _Memo()

    def kv_key(head):
        return head if head < H_C else H_C + (head - H_C) // GQA

    def scores(b, p, hh):
        head = 2 * p + hh
        kt = memo.get_or_build(("k", b, kv_key(head), hh),
                               lambda: _pad_rows(kv_refs(b, head)[0][...].astype(BF16), hh, 0.0))
        return [_dot(q_ref[b * SEQ:(b + 1) * SEQ, p * PAIR:(p + 1) * PAIR], kt)]

    def finish(out, b, p, hh, e_parts, term):
        head = 2 * p + hh
        vt = memo.get_or_build(("v", b, kv_key(head), hh),
                               lambda: _pad_rows(kv_refs(b, head)[1][...].astype(BF16), hh, 1.0))
        out.put(hh, _dot_nt(e_parts[0], vt), term)

    heads = []
    for b in range(CTX_SEQS_PER_STEP):
        for p in range((H_C + H_D) // 2):
            out = _PairSink(y_ref, slice(b * SEQ, (b + 1) * SEQ), slice(p * PAIR, (p + 1) * PAIR))
            for hh in range(2):
                d_head = 2 * p + hh - H_C
                sink = sink_ref[d_head] * LOG2E if d_head >= 0 else None
                heads.append(_Head(functools.partial(scores, b, p, hh), functools.partial(finish, out, b, p, hh), sink))
    _attend(heads, batch=8)


def _odd_ctx_mixer(q, nkt, nvt, skt, svt, sink):
    n_tok = q.shape[0]
    n_seq = CTX_SEQS_PER_STEP
    c_spec = pl.BlockSpec((n_seq, 1, H_C, DH, SEQ), lambda b: (b, 0, 0, 0, 0))
    d_spec = pl.BlockSpec((n_seq, 1, KV_D, DH, SEQ), lambda b: (b, 0, 0, 0, 0))
    return pl.pallas_call(
        _odd_ctx_kernel,
        grid=(n_tok // (n_seq * SEQ),),
        in_specs=[pl.BlockSpec((n_seq * SEQ, D_MODEL), lambda b: (b, 0)), c_spec, c_spec, d_spec, d_spec,
                  pl.BlockSpec(memory_space=pltpu.SMEM)],
        out_specs=pl.BlockSpec((n_seq * SEQ, D_MODEL), lambda b: (b, 0)),
        out_shape=jax.ShapeDtypeStruct((ALL_TOKENS, D_MODEL), BF16),
        compiler_params=_cparams(1),
        name="odd_mixer_ctx",
    )(q, nkt, nvt, skt, svt, sink)


def _build_na_bias(rpb_ref, head, nb_ref, slot):
    qc = lax.broadcasted_iota(jnp.int32, (GRID_W, GRID_W), 0)
    kc = lax.broadcasted_iota(jnp.int32, (GRID_W, GRID_W), 1)
    col_start = jnp.clip(qc - NA_WIN_C // 2, 0, GRID_W - NA_WIN_C)
    col_ok = (kc >= col_start) & (kc < col_start + NA_WIN_C)
    dc = kc - qc + NA_WIN_C - 1
    n_dr, n_dc = 2 * NA_WIN_R - 1, 2 * NA_WIN_C - 1
    neg = jnp.full((GRID_W, GRID_W), NEG, F32)
    toeplitz = []
    for dr in range(n_dr):
        t = neg
        for d in range(n_dc):
            t = jnp.where(dc == d, rpb_ref[(head * n_dr + dr) * n_dc + d] * LOG2E, t)
        toeplitz.append(jnp.where(col_ok, t, NEG))
    rows_per_half = NA_HALF // GRID_W
    key_rows = NA_KEYS // GRID_W
    for half in range(2):
        for rl in range(rows_per_half):
            r = half * rows_per_half + rl
            r_start = min(max(r - NA_WIN_R // 2, 0), GRID_H - NA_WIN_R)
            blocks = []
            for j in range(key_rows):
                rk = half * (GRID_H - key_rows) + j
                inside = r_start <= rk < r_start + NA_WIN_R
                blocks.append(toeplitz[rk - r + NA_WIN_R - 1] if inside else neg)
            nb_ref[slot, half, rl * GRID_W:(rl + 1) * GRID_W, :] = jnp.concatenate(blocks, axis=1)


def _na_kernel(rpb_ref, q_ref, k_ref, v_ref, kct_ref, vct_ref, mix_ref, y_ref, nb_ref):
    @pl.when(pl.program_id(1) == 0)
    def _():
        for hh in range(2):
            _build_na_bias(rpb_ref, 2 * pl.program_id(0) + hh, nb_ref, hh)

    def scores(half, hh):
        w0 = half * (DEC_SEQ - NA_KEYS)
        qp = q_ref[half * NA_HALF:(half + 1) * NA_HALF, :]
        kw = k_ref[w0:w0 + NA_KEYS, :]
        s1 = _dot_nt(qp, jnp.where(_keep_mask_bf16(NA_KEYS, hh), kw, jnp.zeros_like(kw))) + nb_ref[hh, half]
        s2 = _dot(qp, _pad_rows(kct_ref[0, 0, hh].astype(BF16), hh, 0.0))
        return [s1, s2]

    def finish(out, half, hh, e_parts, term):
        w0 = half * (DEC_SEQ - NA_KEYS)
        vw = v_ref[w0:w0 + NA_KEYS, :]
        o = (_dot(e_parts[0], jnp.where(_keep_mask_bf16(NA_KEYS, hh), vw, jnp.ones_like(vw)))
             + _dot_nt(e_parts[1], _pad_rows(vct_ref[0, 0, hh].astype(BF16), hh, 1.0)))
        out.put(hh, o, term)

    heads = []
    for half in range(2):
        out = _PairSink(y_ref, slice(half * NA_HALF, (half + 1) * NA_HALF), slice(0, PAIR))
        for hh in range(2):
            heads.append(_Head(functools.partial(scores, half, hh), functools.partial(finish, out, half, hh)))
    _attend(heads, batch=2)


def _na_mixer(proj, cache_kt, cache_vt, rpb_flat, mix):
    pairs = H_C // 2
    row0 = CTX_TOKENS // DEC_SEQ
    col = lambda base: pl.BlockSpec((DEC_SEQ, PAIR), lambda hp, b: (b, base + hp))
    cache = pl.BlockSpec((1, 1, 2, DH, PAST_LEN), lambda hp, b: (b, 0, hp, 0, 0))
    return pl.pallas_call(
        _na_kernel,
        grid=(pairs, DEC_BATCH),
        in_specs=[pl.BlockSpec(memory_space=pltpu.SMEM), col(0), col(pairs), col(2 * pairs), cache, cache,
                  pl.BlockSpec(memory_space=pl.ANY)],
        out_specs=pl.BlockSpec((DEC_SEQ, PAIR), lambda hp, b: (b + row0, hp)),
        out_shape=jax.ShapeDtypeStruct((ALL_TOKENS, D_MODEL), BF16),
        input_output_aliases={6: 0},
        scratch_shapes=[pltpu.VMEM((2, 2, NA_HALF, NA_KEYS), F32)],
        compiler_params=_cparams(2),
        name="na_mixer",
    )(rpb_flat, proj, proj, proj, cache_kt, cache_vt, mix)


def _swa_kernel(q_ref, kv_ref, kct_ref, vct_ref, sink_ref, mix_ref, y_ref):
    n_groups = DEC_SEQ // SWA_QROWS
    ql = lax.broadcasted_iota(jnp.int32, (SWA_QROWS, SWA_BAND), 0)
    kj = lax.broadcasted_iota(jnp.int32, (SWA_QROWS, SWA_BAND), 1)

    def band_bias(first_key_minus_first_query):
        return jnp.where(jnp.abs(kj + first_key_minus_first_query - ql) <= SWA_WINDOW, 0.0, NEG)

    bias_first, bias_mid, bias_last = band_bias(0), band_bias(-SWA_BLOCK), band_bias(-2 * SWA_BLOCK)

    def band_start(grp):
        return min(max(grp * SWA_QROWS - SWA_BLOCK, 0), DEC_SEQ - SWA_BAND)

    memo = _Memo()

    def band_k(g, grp, hh):
        start = band_start(grp)
        kb = kv_ref[start:start + SWA_BAND, g * PAIR:(g + 1) * PAIR]
        return jnp.where(_keep_mask_bf16(SWA_BAND, hh), kb, jnp.zeros_like(kb))

    def band_v(g, grp, hh):
        start = band_start(grp)
        vb = kv_ref[start:start + SWA_BAND, (KV_D + g) * PAIR:(KV_D + g + 1) * PAIR]
        return jnp.where(_keep_mask_bf16(SWA_BAND, hh), vb, jnp.ones_like(vb))

    def scores(g, grp, p, hh):
        bias = bias_first if grp == 0 else (bias_last if grp == n_groups - 1 else bias_mid)
        qp = q_ref[grp * SWA_QROWS:(grp + 1) * SWA_QROWS, p * PAIR:(p + 1) * PAIR]
        s1 = _dot_nt(qp, memo.get_or_build(("k", g, grp, hh), functools.partial(band_k, g, grp, hh))) + bias
        ckt = memo.get_or_build(("ck", g, hh), lambda: _pad_rows(kct_ref[0, 0, g].astype(BF16), hh, 0.0))
        return [s1, _dot(qp, ckt)]

    def finish(out, g, grp, hh, e_parts, term):
        cvt = memo.get_or_build(("cv", g, hh), lambda: _pad_rows(vct_ref[0, 0, g].astype(BF16), hh, 1.0))
        o = (_dot(e_parts[0], memo.get_or_build(("v", g, grp, hh), functools.partial(band_v, g, grp, hh)))
             + _dot_nt(e_parts[1], cvt))
        out.put(hh, o, term)

    heads = []
    for g in range(KV_D):
        for grp in range(n_groups):
            for j in range(GQA // 2):
                p = g * (GQA // 2) + j
                out = _PairSink(y_ref, slice(grp * SWA_QROWS, (grp + 1) * SWA_QROWS), slice(p * PAIR, (p + 1) * PAIR))
                for hh in range(2):
                    sink = sink_ref[2 * p + hh] * LOG2E
                    heads.append(_Head(functools.partial(scores, g, grp, p, hh),
                                       functools.partial(finish, out, g, grp, hh), sink))
    _attend(heads, batch=4)


def _swa_mixer(proj, cache_kt, cache_vt, sink, mix):
    row0 = CTX_TOKENS // DEC_SEQ
    cache = pl.BlockSpec((1, 1, KV_D, DH, PAST_LEN), lambda b: (b, 0, 0, 0, 0))
    return pl.pallas_call(
        _swa_kernel,
        grid=(DEC_BATCH,),
        in_specs=[pl.BlockSpec((DEC_SEQ, 512), lambda b: (b, 3)),
                  pl.BlockSpec((DEC_SEQ, 512), lambda b: (b, 4)),
                  cache, cache,
                  pl.BlockSpec(memory_space=pltpu.SMEM), pl.BlockSpec(memory_space=pl.ANY)],
        out_specs=pl.BlockSpec((DEC_SEQ, 512), lambda b: (b + row0, 1)),
        out_shape=jax.ShapeDtypeStruct((ALL_TOKENS, D_MODEL), BF16),
        input_output_aliases={5: 0},
        compiler_params=_cparams(1),
        name="swa_mixer",
    )(proj, proj, cache_kt, cache_vt, sink, mix)


def _post_mixer_kernel(*refs, li, n_x, n_out):
    mix_ref = refs[0]
    x_refs = refs[1:1 + n_x]
    mod_ref, gpost_ref, gpre_ref, gmlp_ref, wo_ref, w1_ref, w2_ref = refs[1 + n_x:8 + n_x]
    out_refs = refs[8 + n_x:8 + n_x + n_out]
    wo_s, w1_s, w2_s = refs[8 + n_x + n_out:]
    step = pl.program_id(0)

    n_wo, n_w1 = D_MODEL // WCHUNK, D_FF // WCHUNK
    for k in range(N_PREP):
        @pl.when(step == k)
        def _(k=k):
            if k < n_wo:
                wo_s[:, k * WCHUNK:(k + 1) * WCHUNK] = wo_ref[0].astype(BF16)
            elif k < n_wo + n_w1:
                w1_s[:, (k - n_wo) * WCHUNK:(k - n_wo + 1) * WCHUNK] = w1_ref[0].astype(BF16)
            else:
                w2_s[(k - n_wo - n_w1) * WCHUNK:(k - n_wo - n_w1 + 1) * WCHUNK, :] = w2_ref[0].astype(BF16)

    @pl.when(step >= N_PREP)
    def _():
        t = step - N_PREP
        is_ctx = t < CTX_TILES
        mod = mod_ref[0, pl.ds(jnp.where(is_ctx, CTX_ROW, (t - CTX_TILES) // (DEC_SEQ // TM)), 1), :]
        g_post, g_pre, g_mlp = gpost_ref[li:li + 1, :], gpre_ref[li:li + 1, :], gmlp_ref[li:li + 1, :]
        rows = [slice(i * (TM // POST_SPLIT), (i + 1) * (TM // POST_SPLIT)) for i in range(POST_SPLIT)]

        def residual(r):
            if n_x == 1:
                return x_refs[0][r, :]
            return jnp.where(is_ctx, x_refs[0][r, :], x_refs[1][r, :])

        ys = [_dot(mix_ref[r, :], wo_s[...]) for r in rows]
        x1 = [residual(r) + mod[:, 2 * D_MODEL:3 * D_MODEL] * _rms(y, g_post) for r, y in zip(rows, ys)]
        h = [_norm_mod(x, g_pre, mod, 3 * D_MODEL).astype(BF16) for x in x1]
        acc = [None] * POST_SPLIT
        pending = []

        def second_matmul(c, i, f):
            t2 = _dot(f, w2_s[c * FF_CHUNK:(c + 1) * FF_CHUNK, :])
            acc[i] = t2 if acc[i] is None else acc[i] + t2

        for c in range(D_FF // FF_CHUNK):
            for i in range(POST_SPLIT):
                f = _dot(h[i], w1_s[:, c * FF_CHUNK:(c + 1) * FF_CHUNK])
                pending.append((c, i, jnp.square(jnp.maximum(f, 0.0)).astype(BF16)))
                if len(pending) > 1:
                    second_matmul(*pending.pop(0))
        second_matmul(*pending.pop(0))
        res = [x1[i] + mod[:, 5 * D_MODEL:6 * D_MODEL] * _rms(acc[i], g_mlp) for i in range(POST_SPLIT)]
        if n_out == 1:
            for i, r in enumerate(rows):
                out_refs[0][r, :] = res[i]
        else:
            @pl.when(is_ctx)
            def _():
                for i, r in enumerate(rows):
                    out_refs[0][r, :] = res[i]

            @pl.when(jnp.logical_not(is_ctx))
            def _():
                for i, r in enumerate(rows):
                    out_refs[1][r, :] = res[i]


def _post_mixer(mix, xs, mod, g_post, g_pre, g_mlp, w_out, w1, w2, li, *, split_out):
    tile = lambda s: jnp.maximum(s - N_PREP, 0)
    ctx_tile = lambda s: jnp.minimum(tile(s), CTX_TILES - 1)
    lat_tile = lambda s: jnp.maximum(tile(s) - CTX_TILES, 0)
    token_spec = lambda index: pl.BlockSpec((TM, D_MODEL), lambda s: (index(s), 0))
    x_specs = [token_spec(tile)] if len(xs) == 1 else [token_spec(ctx_tile), token_spec(lat_tile)]
    if split_out:
        out_specs = [token_spec(ctx_tile), token_spec(lat_tile)]
        out_shape = [jax.ShapeDtypeStruct((CTX_TOKENS, D_MODEL), F32), jax.ShapeDtypeStruct((LAT_TOKENS, D_MODEL), F32)]
    else:
        out_specs = [token_spec(tile)]
        out_shape = [jax.ShapeDtypeStruct((ALL_TOKENS, D_MODEL), F32)]
    n_wo, n_w1, n_w2 = D_MODEL // WCHUNK, D_FF // WCHUNK, D_FF // WCHUNK
    gain = pl.BlockSpec((DEPTH, D_MODEL), lambda s: (0, 0))
    return pl.pallas_call(
        functools.partial(_post_mixer_kernel, li=li, n_x=len(xs), n_out=len(out_specs)),
        grid=(N_PREP + ALL_TOKENS // TM,),
        in_specs=[token_spec(tile)] + x_specs + [
            pl.BlockSpec((1, MOD_ROWS, 6 * D_MODEL), lambda s: (li, 0, 0)),
            gain, gain, gain,
            pl.BlockSpec((1, D_MODEL, WCHUNK), lambda s: (li, 0, jnp.clip(s, 0, n_wo - 1))),
            pl.BlockSpec((1, D_MODEL, WCHUNK), lambda s: (li, 0, jnp.clip(s - n_wo, 0, n_w1 - 1))),
            pl.BlockSpec((1, WCHUNK, D_MODEL), lambda s: (li, jnp.clip(s - n_wo - n_w1, 0, n_w2 - 1), 0))],
        out_specs=out_specs,
        out_shape=out_shape,
        scratch_shapes=[pltpu.VMEM((D_MODEL, D_MODEL), BF16), pltpu.VMEM((D_MODEL, D_FF), BF16),
                        pltpu.VMEM((D_FF, D_MODEL), BF16)],
        compiler_params=_cparams(1),
        name="post_mixer",
    )(mix, *xs, mod, g_post, g_pre, g_mlp, w_out, w1, w2)


def _rope_tables():
    t = jnp.arange(DEC_SEQ)
    rows = (t // GRID_W).astype(F32)
    cols = (t % GRID_W).astype(F32)
    q4 = DH // 4
    inv = 1.0 / (ROPE_BASE ** (jnp.arange(q4, dtype=F32) / q4))
    ar, ac = rows[:, None] * inv, cols[:, None] * inv
    cos64 = jnp.concatenate([jnp.cos(ar), jnp.cos(ar), jnp.cos(ac), jnp.cos(ac)], axis=1)
    sin64 = jnp.concatenate([-jnp.sin(ar), jnp.sin(ar), -jnp.sin(ac), jnp.sin(ac)], axis=1)
    return jnp.tile(cos64, (1, 2)), jnp.tile(sin64, (1, 2))


def kernel(x_prompt, x_sample, cache_diff_k, cache_diff_v, cache_na_k, cache_na_v, cache_swa_k, cache_swa_v, c, c_ctx, mod_w, mod_b, norm_mix_pre, norm_mix_post, norm_mlp_pre, norm_mlp_post, w_in_even, conv_w, lambda_q1, lambda_k1, lambda_q2, lambda_k2, subln, w_in_odd, rpb, sink, w_out, mlp_w1, mlp_w2):
    cond = jnp.concatenate([c, c_ctx[None, :], jnp.zeros((MOD_ROWS - DEC_BATCH - 1, D_MODEL), F32)], axis=0)
    mod = _modulation(cond, mod_w, mod_b)
    rope = _rope_tables()

    t_minor = lambda a: jnp.swapaxes(a, -1, -2)

    w_in = [w_in_even.astype(BF16), w_in_odd.astype(BF16)]

    xp = x_prompt.reshape(CTX_TOKENS, D_MODEL)
    xs = x_sample.reshape(LAT_TOKENS, D_MODEL)
    lam_init0 = 0.8 - 0.6 * math.exp(-0.3 * 0)
    lam_args = (conv_w, lambda_q1, lambda_k1, lambda_q2, lambda_k2, subln)
    sink_flat = sink.reshape(-1)

    def post(li, mix, x_parts, split_out):
        return _post_mixer(mix, x_parts, mod, norm_mix_post, norm_mlp_pre, norm_mlp_post, w_out, mlp_w1, mlp_w2, li,
                           split_out=split_out)

    proj, new_diff_kt, new_diff_v = _pre_mixer(xp, mod, norm_mix_pre, w_in[0], 0, latent=False, even=True, rope=None,
                                               n_tok=CTX_TOKENS)
    mix = _even_mixer(proj, *lam_args, new_diff_kt, new_diff_v, None, seq=SEQ, n_seq=CTX_SEQS_PER_STEP // 2,
                      own_keys=False, lam_init=lam_init0)
    proj = _pre_mixer(xs, mod, norm_mix_pre, w_in[0], 0, latent=True, even=True, rope=rope, n_tok=LAT_TOKENS)
    mix = _even_mixer(proj, *lam_args, t_minor(cache_diff_k), cache_diff_v, mix, seq=DEC_SEQ, n_seq=1, own_keys=True,
                      lam_init=lam_init0)
    (x_all,) = post(0, mix, [xp, xs], False)

    q, new_na_kt, new_na_vt, new_swa_kt, new_swa_vt = _pre_mixer(
        x_all, mod, norm_mix_pre, w_in[1], 1, latent=False, even=False, rope=None, n_tok=CTX_TOKENS)
    mix = _odd_ctx_mixer(q, new_na_kt, new_na_vt, new_swa_kt, new_swa_vt, sink_flat)
    proj = _pre_mixer(x_all, mod, norm_mix_pre, w_in[1], 1, latent=True, even=False, rope=rope, n_tok=LAT_TOKENS,
                      tile0=CTX_TILES)
    mix = _na_mixer(proj, t_minor(cache_na_k), t_minor(cache_na_v), rpb.reshape(-1), mix)
    mix = _swa_mixer(proj, t_minor(cache_swa_k), t_minor(cache_swa_v), sink_flat, mix)
    xp, xs = post(1, mix, [x_all], True)

    return (xp.reshape(BATCH, SEQ, D_MODEL), xs.reshape(DEC_BATCH, DEC_SEQ, D_MODEL),
            t_minor(new_diff_kt), new_diff_v, t_minor(new_na_kt), t_minor(new_na_vt),
            t_minor(new_swa_kt), t_minor(new_swa_vt))
```

```python
import functools
import math

import jax
import jax.numpy as jnp
from jax import lax
from jax.experimental import pallas as pl
from jax.experimental.pallas import tpu as pltpu

F32 = jnp.float32
BF16 = jnp.bfloat16

D_MODEL = 1024
BATCH = 32
SEQ = 256
DEPTH = 2
DEC_BATCH = 8
DEC_SEQ = 1024
PAST_LEN = 256
GRID_W = 64
GRID_H = DEC_SEQ // GRID_W
MIX_A = 512
H_B = 4
DH = 64
PAIR = 2 * DH
H_C = 8
H_D = 8
KV_D = 2
GQA = H_D // KV_D
IN_EVEN = 3072
IN_ODD = 2304
D_FF = 4096
NA_WIN_R = 8
NA_WIN_C = 16
SWA_BLOCK = 128
SWA_WINDOW = 128
ROPE_BASE = 10000.0
EPS = 1e-6
NEG = -1e30
LOG2E = math.log2(math.e)
Q_SCALE = DH ** -0.5 * LOG2E

MOD_ROWS = 16
CTX_ROW = DEC_BATCH
V7X_VMEM_BYTES = 64 * 1024 * 1024
VMEM_LIMIT = V7X_VMEM_BYTES - 8 * 1024 * 1024

CTX_TOKENS = BATCH * SEQ
LAT_TOKENS = DEC_BATCH * DEC_SEQ
ALL_TOKENS = CTX_TOKENS + LAT_TOKENS

TM = 512
FF_CHUNK = 1024
POST_SPLIT = 2
WCHUNK_ELEMS = 512 * 1024
WO_ROWS, W1_ROWS, W2_ROWS = WCHUNK_ELEMS // D_MODEL, WCHUNK_ELEMS // D_FF, WCHUNK_ELEMS // D_MODEL
N_WO, N_W1, N_W2 = D_MODEL // WO_ROWS, D_MODEL // W1_ROWS, D_FF // W2_ROWS
N_PREP = N_WO + N_W1 + N_W2
CTX_TILES = CTX_TOKENS // TM
CTX_SEQS_PER_STEP = 4
MOD_TN = 1536
NA_SEQS_PER_STEP = 2
NA_HALF = 512
NA_KEYS = 768
SWA_QROWS = 2 * SWA_BLOCK
SWA_BAND = 4 * SWA_BLOCK
LAT_ODD_COLS = IN_ODD + 2 * PAIR


def _cparams(n_axes):
    return pltpu.CompilerParams(dimension_semantics=("arbitrary",) * n_axes, vmem_limit_bytes=VMEM_LIMIT)


def _rms(x, g):
    return x * lax.rsqrt(jnp.mean(x * x, axis=-1, keepdims=True) + EPS) * g


def _dot(a, b):
    return jnp.dot(a, b, preferred_element_type=F32)


def _dot_nt(a, b):
    return lax.dot_general(a, b, (((1,), (1,)), ((), ())), preferred_element_type=F32)


def _layer_block(shape, li):
    zeros = (0,) * len(shape)
    return pl.BlockSpec((1,) + tuple(shape), lambda *_: (li,) + zeros, pipeline_mode=pl.Buffered(1))


def _low_half(shape):
    return (lax.broadcasted_iota(jnp.int32, shape, len(shape) - 1) & DH) == 0


def _keep_mask_bf16(rows, hh):
    lane = lax.broadcasted_iota(jnp.int32, (rows, PAIR), 1)
    half = (lane & DH).astype(F32).astype(BF16)
    return (half == 0) if hh == 0 else (half != 0)


def _pad_rows(t, hh, fill):
    other = jnp.full(t.shape, fill, t.dtype)
    return jnp.concatenate([t, other] if hh == 0 else [other, t], axis=0)


class _Memo(dict):
    def get_or_build(self, key, build):
        if key not in self:
            self[key] = build()
        return self[key]


def _merge_pair(o0, o1, extra_den=None):
    low = _low_half(o0.shape)
    num = jnp.where(low, o0, o1)
    den = pltpu.roll(jnp.where(low, o1, o0), DH, axis=1)
    if extra_den is not None:
        den = den + jnp.where(low, extra_den[0], extra_den[1])
    return num * (1.0 / den)


class _Head:
    def __init__(self, scores, finish, sink=None):
        self.scores, self.finish, self.sink = scores, finish, sink


def _attend(heads, batch):
    batches = [heads[i:i + batch] for i in range(0, len(heads), batch)]
    cur = [hd.scores() for hd in batches[0]]
    for bi, group in enumerate(batches):
        nxt = [hd.scores() for hd in batches[bi + 1]] if bi + 1 < len(batches) else None
        exps = []
        for hd, parts in zip(group, cur):
            chunks = [s[:, j * 128:(j + 1) * 128] for s in parts for j in range(s.shape[1] // 128)]
            m_el = functools.reduce(jnp.maximum, chunks)
            if hd.sink is not None:
                m_el = jnp.maximum(m_el, hd.sink)
            mx = jnp.broadcast_to(jnp.max(m_el, axis=-1, keepdims=True), m_el.shape)
            term = None if hd.sink is None else jnp.exp2(hd.sink - mx)
            e_parts = [jnp.concatenate([jnp.exp2(s[:, j * 128:(j + 1) * 128] - mx)
                                        for j in range(s.shape[1] // 128)], axis=1).astype(BF16) for s in parts]
            exps.append((e_parts, term))
        for hd, (e_parts, term) in zip(group, exps):
            hd.finish(e_parts, term)
        cur = nxt


class _PairSink:
    def __init__(self, y_ref, rows, cols):
        self.y_ref, self.rows, self.cols, self.first = y_ref, rows, cols, None

    def put(self, hh, o, term):
        if hh == 0:
            self.first = (o, term)
        else:
            o0, term0 = self.first
            extra = None if term is None else (term0, term)
            self.y_ref[self.rows, self.cols] = _merge_pair(o0, o, extra).astype(BF16)


def _mod_kernel(cond_ref, w_ref, b_ref, o_ref):
    c = cond_ref[...]
    s = c * (1.0 / (1.0 + jnp.exp(-c)))
    o_ref[0] = _dot(s.astype(BF16), w_ref[0].astype(BF16)) + b_ref[pl.ds(pl.program_id(0), 1), :]


def _modulation(cond, mod_w, mod_b):
    return pl.pallas_call(
        _mod_kernel,
        grid=(DEPTH, 6 * D_MODEL // MOD_TN),
        in_specs=[
            pl.BlockSpec((MOD_ROWS, D_MODEL), lambda l, j: (0, 0)),
            pl.BlockSpec((1, D_MODEL, MOD_TN), lambda l, j: (l, 0, j)),
            pl.BlockSpec((DEPTH, MOD_TN), lambda l, j: (0, j)),
        ],
        out_specs=pl.BlockSpec((1, MOD_ROWS, MOD_TN), lambda l, j: (l, 0, j)),
        out_shape=jax.ShapeDtypeStruct((DEPTH, MOD_ROWS, 6 * D_MODEL), F32),
        compiler_params=_cparams(2),
        name="adaln_modulation",
    )(cond, mod_w, mod_b)


def _norm_mod(x, g, mod, shift_col):
    sh = mod[:, shift_col:shift_col + D_MODEL]
    sc = mod[:, shift_col + D_MODEL:shift_col + 2 * D_MODEL]
    return _rms(x, g) * (1.0 + sc) + sh


def _rope(z, cos, sin):
    outs = []
    for j in range(z.shape[1] // 128):
        zj = z[:, j * 128:(j + 1) * 128]
        lane = lax.broadcasted_iota(jnp.int32, zj.shape, 1)
        partner = jnp.where((lane & 16) == 0, pltpu.roll(zj, 128 - 16, axis=1), pltpu.roll(zj, 16, axis=1))
        outs.append(zj * cos + partner * sin)
    return outs[0] if len(outs) == 1 else jnp.concatenate(outs, axis=1)


def _store_heads_transposed(dst, bi, pc_rows, n_pairs):
    for p in range(n_pairs):
        t = pc_rows[:, p * PAIR:(p + 1) * PAIR].T
        dst[bi, 0, 2 * p] = t[0:DH]
        dst[bi, 0, 2 * p + 1] = t[DH:PAIR]


def _mod_row(mod_ref, latent):
    if latent:
        return mod_ref[0, pl.ds(pl.program_id(0) // (DEC_SEQ // TM), 1), :]
    return mod_ref[0, CTX_ROW:CTX_ROW + 1, :]


def _normed_groups(x_ref, mod_ref, g_ref, li, latent):
    rows = [slice(i * SEQ, (i + 1) * SEQ) for i in range(TM // SEQ)]
    mod, g = _mod_row(mod_ref, latent), g_ref[li:li + 1, :]
    return rows, [_norm_mod(x_ref[r, :], g, mod, 0).astype(BF16) for r in rows]


def _k1_ctx_even(x_ref, mod_ref, g_ref, w_ref, proj_ref, dkt_ref, dv_ref, *, li):
    rows, hs = _normed_groups(x_ref, mod_ref, g_ref, li, False)
    for c in range(IN_EVEN // 512):
        for bi, (r, h) in enumerate(zip(rows, hs)):
            pc = _dot(h, w_ref[0, :, c * 512:(c + 1) * 512])
            if c < 3:
                proj_ref[r, c * 512:(c + 1) * 512] = pc.astype(BF16)
            elif c == 3:
                proj_ref[r, c * 512:(c + 1) * 512] = (pc * Q_SCALE).astype(BF16)
            elif c == 4:
                for hh in range(H_B):
                    t = pc[:, hh * PAIR:(hh + 1) * PAIR].T
                    dkt_ref[bi, 0, hh, 0] = t[0:DH]
                    dkt_ref[bi, 0, hh, 1] = t[DH:PAIR]
            else:
                for hh in range(H_B):
                    dv_ref[bi, 0, hh] = pc[:, hh * PAIR:(hh + 1) * PAIR]


def _k1_lat_even(x_ref, mod_ref, g_ref, w_ref, cos_ref, sin_ref, proj_ref, *, li):
    rows, hs = _normed_groups(x_ref, mod_ref, g_ref, li, True)
    for c in range(IN_EVEN // 512):
        for r, h in zip(rows, hs):
            pc = _dot(h, w_ref[0, :, c * 512:(c + 1) * 512])
            if c == 3:
                pc = _rope(pc * Q_SCALE, cos_ref[r, :], sin_ref[r, :])
            elif c == 4:
                pc = _rope(pc, cos_ref[r, :], sin_ref[r, :])
            proj_ref[r, c * 512:(c + 1) * 512] = pc.astype(BF16)


def _k1_ctx_odd(x_ref, mod_ref, g_ref, w_ref, q_ref, nkt_ref, nvt_ref, skt_ref, svt_ref, *, li):
    rows, hs = _normed_groups(x_ref, mod_ref, g_ref, li, False)
    for c in range(4):
        for bi, (r, h) in enumerate(zip(rows, hs)):
            pc = _dot(h, w_ref[0, :, c * 512:(c + 1) * 512])
            if c == 0 or c == 3:
                q_ref[r, (c // 3) * 512:(c // 3 + 1) * 512] = (pc * Q_SCALE).astype(BF16)
            else:
                _store_heads_transposed(nkt_ref if c == 1 else nvt_ref, bi, pc, H_C // 2)
    for bi, h in enumerate(hs):
        pc = _dot(h, w_ref[0, :, 2048:IN_ODD])
        _store_heads_transposed(skt_ref, bi, pc[:, 0:PAIR], 1)
        _store_heads_transposed(svt_ref, bi, pc[:, PAIR:2 * PAIR], 1)


def _k1_lat_odd(x_ref, mod_ref, g_ref, w_ref, cos_ref, sin_ref, proj_ref, *, li):
    rows, hs = _normed_groups(x_ref, mod_ref, g_ref, li, True)
    for c in range(4):
        for r, h in zip(rows, hs):
            pc = _dot(h, w_ref[0, :, c * 512:(c + 1) * 512])
            if c == 0:
                pc = pc * Q_SCALE
            elif c == 3:
                pc = _rope(pc * Q_SCALE, cos_ref[r, :], sin_ref[r, :])
            proj_ref[r, c * 512:(c + 1) * 512] = pc.astype(BF16)
    low = _low_half((SEQ, PAIR))
    for r, h in zip(rows, hs):
        pc = _dot(h, w_ref[0, :, 2048:IN_ODD])
        for j, z in enumerate((_rope(pc[:, 0:PAIR], cos_ref[r, :], sin_ref[r, :]), pc[:, PAIR:2 * PAIR])):
            zr = pltpu.roll(z, DH, axis=1)
            base = 2048 + j * 2 * PAIR
            proj_ref[r, base:base + PAIR] = jnp.where(low, z, zr).astype(BF16)
            proj_ref[r, base + PAIR:base + 2 * PAIR] = jnp.where(low, zr, z).astype(BF16)


def _pre_mixer(x, mod, gains, w, li, *, latent, even, rope, n_tok, tile0=0):
    n_in = w.shape[2]
    tiles_per_seq = DEC_SEQ // TM
    in_specs = [
        pl.BlockSpec((TM, D_MODEL), lambda i: (i + tile0, 0)),
        pl.BlockSpec((1, MOD_ROWS, 6 * D_MODEL), lambda i: (li, 0, 0)),
        pl.BlockSpec((DEPTH, D_MODEL), lambda i: (0, 0)),
        _layer_block((D_MODEL, n_in), 0),
    ]
    args = [x, mod, gains, w]
    nb = TM // SEQ
    if latent:
        in_specs += [pl.BlockSpec((TM, 128), lambda i: (i % tiles_per_seq, 0))] * 2
        args += list(rope)
        body = _k1_lat_even if even else _k1_lat_odd
        n_out = IN_EVEN if even else LAT_ODD_COLS
        out_specs = pl.BlockSpec((TM, n_out), lambda i: (i, 0))
        out_shape = jax.ShapeDtypeStruct((n_tok, n_out), BF16)
    elif even:
        body = _k1_ctx_even
        out_specs = [pl.BlockSpec((TM, 4 * MIX_A), lambda i: (i, 0)),
                     pl.BlockSpec((nb, 1, H_B, 2, DH, SEQ), lambda i: (i, 0, 0, 0, 0, 0)),
                     pl.BlockSpec((nb, 1, H_B, SEQ, PAIR), lambda i: (i, 0, 0, 0, 0))]
        out_shape = [jax.ShapeDtypeStruct((n_tok, 4 * MIX_A), BF16),
                     jax.ShapeDtypeStruct((BATCH, 1, H_B, 2, DH, SEQ), F32),
                     jax.ShapeDtypeStruct((BATCH, 1, H_B, SEQ, PAIR), F32)]
    else:
        body = _k1_ctx_odd
        c_spec = pl.BlockSpec((nb, 1, H_C, DH, SEQ), lambda i: (i, 0, 0, 0, 0))
        d_spec = pl.BlockSpec((nb, 1, KV_D, DH, SEQ), lambda i: (i, 0, 0, 0, 0))
        c_shape = jax.ShapeDtypeStruct((BATCH, 1, H_C, DH, SEQ), F32)
        d_shape = jax.ShapeDtypeStruct((BATCH, 1, KV_D, DH, SEQ), F32)
        out_specs = [pl.BlockSpec((TM, D_MODEL), lambda i: (i, 0)), c_spec, c_spec, d_spec, d_spec]
        out_shape = [jax.ShapeDtypeStruct((n_tok, D_MODEL), BF16), c_shape, c_shape, d_shape, d_shape]
    return pl.pallas_call(
        functools.partial(body, li=li),
        grid=(n_tok // TM,),
        in_specs=in_specs,
        out_specs=out_specs,
        out_shape=out_shape,
        compiler_params=_cparams(1),
        name=f"pre_mixer_{'lat' if latent else 'ctx'}_{'even' if even else 'odd'}",
    )(*args)


def _even_mixer_kernel(*refs, seq, n_seq, tq, own_keys, lam_init):
    proj_ref, cw_ref, lq1_ref, lk1_ref, lq2_ref, lk2_ref, subln_ref, ckt_ref, cv_ref = refs[:9]
    y_ref = refs[-1]
    n_rows = n_seq * seq

    pos = lax.broadcasted_iota(jnp.int32, (n_rows, 128), 0) % seq
    for j in range(MIX_A // 128):
        cols = slice(j * 128, (j + 1) * 128)
        a_b = proj_ref[:, j * 128:(j + 1) * 128].astype(F32)
        u = (proj_ref[:, MIX_A + j * 128:MIX_A + (j + 1) * 128].astype(F32)
             * proj_ref[:, 2 * MIX_A + j * 128:2 * MIX_A + (j + 1) * 128].astype(F32))
        u_prev = jnp.where(pos == 0, 0.0, pltpu.roll(u, 1, axis=0))
        u_next = jnp.where(pos == seq - 1, 0.0, pltpu.roll(u, n_rows - 1, axis=0))
        w = cw_ref[0, :, cols]
        y_ref[:, cols] = (a_b * (w[0:1] * u_prev + w[1:2] * u + w[2:3] * u_next)).astype(BF16)

    lam = (jnp.exp(jnp.sum(lq1_ref[...] * lk1_ref[...], axis=-1, keepdims=True))
           - jnp.exp(jnp.sum(lq2_ref[...] * lk2_ref[...], axis=-1, keepdims=True)) + lam_init)
    subln = subln_ref[...]
    q_col, k_col, v_col = 3 * MIX_A, 3 * MIX_A + 512, 3 * MIX_A + 1024
    ones = jnp.ones((PAST_LEN, PAIR), BF16)

    memo = _Memo()

    def own_k(b, h, m):
        kp = proj_ref[b * seq:(b + 1) * seq, k_col + h * PAIR:k_col + (h + 1) * PAIR]
        return jnp.where(_keep_mask_bf16(seq, m), kp, jnp.zeros_like(kp))

    def value_ops(b, h):
        v_ops = []
        if own_keys:
            v_own = proj_ref[b * seq:(b + 1) * seq, v_col + h * PAIR:v_col + (h + 1) * PAIR]
            v_ops.append(jnp.concatenate([v_own, jnp.ones((seq, PAIR), BF16)], axis=1))
        v_ops.append(jnp.concatenate([cv_ref[b, 0, h].astype(BF16), ones], axis=1))
        return v_ops

    def scores(b, qrows, h, m):
        qp = proj_ref[qrows, q_col + h * PAIR:q_col + (h + 1) * PAIR]
        parts = []
        if own_keys:
            parts.append(_dot_nt(qp, memo.get_or_build(("k", b, h, m), functools.partial(own_k, b, h, m))))
        ckt = memo.get_or_build(("ck", b, h, m), lambda: _pad_rows(ckt_ref[b, 0, h, m].astype(BF16), m, 0.0))
        parts.append(_dot(qp, ckt))
        return parts

    first_map = {}

    def pv(b, qrows, h, m, e_parts, _):
        v_ops = memo.get_or_build(("v", b, h), functools.partial(value_ops, b, h))
        o = None
        for e, v_op in zip(e_parts, v_ops):
            t = _dot(e, v_op)
            o = t if o is None else o + t
        attn = o[:, 0:PAIR] * (1.0 / o[:, PAIR:2 * PAIR])
        if m == 0:
            first_map[(qrows.start, h)] = attn
        else:
            y = _rms(first_map.pop((qrows.start, h)) - lam * attn, subln) * (1.0 - lam_init)
            y_ref[qrows, MIX_A + h * PAIR:MIX_A + (h + 1) * PAIR] = y.astype(BF16)

    heads = []
    for b in range(n_seq):
        for i in range(seq // tq):
            qrows = slice(b * seq + i * tq, b * seq + (i + 1) * tq)
            for h in range(H_B):
                for m in range(2):
                    heads.append(_Head(functools.partial(scores, b, qrows, h, m),
                                       functools.partial(pv, b, qrows, h, m)))
    _attend(heads, batch=8)


def _even_mixer(proj, conv_w, lq1, lk1, lq2, lk2, subln, kt, v, mix, *, seq, n_seq, own_keys, lam_init):
    n_tok = proj.shape[0]
    rows = n_seq * seq
    row0 = 0 if mix is None else CTX_TOKENS // rows
    small = lambda a: pl.BlockSpec((1, a.shape[1]), lambda b: (0, 0))
    in_specs = [pl.BlockSpec((rows, proj.shape[1]), lambda b: (b, 0)),
                pl.BlockSpec((1, 3, MIX_A), lambda b: (0, 0, 0)),
                small(lq1), small(lk1), small(lq2), small(lk2), small(subln),
                pl.BlockSpec((n_seq, 1, H_B, 2, DH, PAST_LEN), lambda b: (b, 0, 0, 0, 0, 0)),
                pl.BlockSpec((n_seq, 1, H_B, PAST_LEN, PAIR), lambda b: (b, 0, 0, 0, 0))]
    args = [proj, conv_w, lq1, lk1, lq2, lk2, subln, kt, v]
    aliases = {}
    if mix is not None:
        in_specs.append(pl.BlockSpec(memory_space=pl.ANY))
        aliases = {len(args): 0}
        args.append(mix)
    return pl.pallas_call(
        functools.partial(_even_mixer_kernel, seq=seq, n_seq=n_seq, tq=min(seq, 256), own_keys=own_keys,
                          lam_init=lam_init),
        grid=(n_tok // rows,),
        in_specs=in_specs,
        out_specs=pl.BlockSpec((rows, D_MODEL), lambda b: (b + row0, 0)),
        out_shape=jax.ShapeDtypeStruct((ALL_TOKENS, D_MODEL), BF16),
        input_output_aliases=aliases,
        compiler_params=_cparams(1),
        name=f"even_mixer_{'lat' if own_keys else 'ctx'}",
    )(*args)


def _odd_ctx_kernel(q_ref, nkt_ref, nvt_ref, skt_ref, svt_ref, sink_ref, y_ref):
    def kv_refs(b, head):
        if head < H_C:
            return nkt_ref.at[b, 0, head], nvt_ref.at[b, 0, head]
        g = (head - H_C) // GQA
        return skt_ref.at[b, 0, g], svt_ref.at[b, 0, g]

    memo = _Memo()

    def kv_key(head):
        return head if head < H_C else H_C + (head - H_C) // GQA

    def scores(b, p, hh):
        head = 2 * p + hh
        kt = memo.get_or_build(("k", b, kv_key(head), hh),
                               lambda: _pad_rows(kv_refs(b, head)[0][...].astype(BF16), hh, 0.0))
        return [_dot(q_ref[b * SEQ:(b + 1) * SEQ, p * PAIR:(p + 1) * PAIR], kt)]

    def finish(out, b, p, hh, e_parts, term):
        head = 2 * p + hh
        vt = memo.get_or_build(("v", b, kv_key(head), hh),
                               lambda: _pad_rows(kv_refs(b, head)[1][...].astype(BF16), hh, 1.0))
        out.put(hh, _dot_nt(e_parts[0], vt), term)

    heads = []
    for b in range(CTX_SEQS_PER_STEP):
        for p in range((H_C + H_D) // 2):
            out = _PairSink(y_ref, slice(b * SEQ, (b + 1) * SEQ), slice(p * PAIR, (p + 1) * PAIR))
            for hh in range(2):
                d_head = 2 * p + hh - H_C
                sink = sink_ref[d_head] * LOG2E if d_head >= 0 else None
                heads.append(_Head(functools.partial(scores, b, p, hh), functools.partial(finish, out, b, p, hh), sink))
    _attend(heads, batch=8)


def _odd_ctx_mixer(q, nkt, nvt, skt, svt, sink):
    n_tok = q.shape[0]
    n_seq = CTX_SEQS_PER_STEP
    c_spec = pl.BlockSpec((n_seq, 1, H_C, DH, SEQ), lambda b: (b, 0, 0, 0, 0))
    d_spec = pl.BlockSpec((n_seq, 1, KV_D, DH, SEQ), lambda b: (b, 0, 0, 0, 0))
    return pl.pallas_call(
        _odd_ctx_kernel,
        grid=(n_tok // (n_seq * SEQ),),
        in_specs=[pl.BlockSpec((n_seq * SEQ, D_MODEL), lambda b: (b, 0)), c_spec, c_spec, d_spec, d_spec,
                  pl.BlockSpec(memory_space=pltpu.SMEM)],
        out_specs=pl.BlockSpec((n_seq * SEQ, D_MODEL), lambda b: (b, 0)),
        out_shape=jax.ShapeDtypeStruct((ALL_TOKENS, D_MODEL), BF16),
        compiler_params=_cparams(1),
        name="odd_mixer_ctx",
    )(q, nkt, nvt, skt, svt, sink)


def _build_na_bias(rpb_ref, head, nb_ref, slot):
    qc = lax.broadcasted_iota(jnp.int32, (GRID_W, GRID_W), 0)
    kc = lax.broadcasted_iota(jnp.int32, (GRID_W, GRID_W), 1)
    col_start = jnp.clip(qc - NA_WIN_C // 2, 0, GRID_W - NA_WIN_C)
    col_ok = (kc >= col_start) & (kc < col_start + NA_WIN_C)
    dc = kc - qc + NA_WIN_C - 1
    n_dr, n_dc = 2 * NA_WIN_R - 1, 2 * NA_WIN_C - 1
    neg = jnp.full((GRID_W, GRID_W), NEG, F32)
    toeplitz = []
    for dr in range(n_dr):
        t = neg
        for d in range(n_dc):
            t = jnp.where(dc == d, rpb_ref[(head * n_dr + dr) * n_dc + d] * LOG2E, t)
        toeplitz.append(jnp.where(col_ok, t, NEG))
    rows_per_half = NA_HALF // GRID_W
    key_rows = NA_KEYS // GRID_W
    for half in range(2):
        for rl in range(rows_per_half):
            r = half * rows_per_half + rl
            r_start = min(max(r - NA_WIN_R // 2, 0), GRID_H - NA_WIN_R)
            blocks = []
            for j in range(key_rows):
                rk = half * (GRID_H - key_rows) + j
                inside = r_start <= rk < r_start + NA_WIN_R
                blocks.append(toeplitz[rk - r + NA_WIN_R - 1] if inside else neg)
            nb_ref[slot, half, rl * GRID_W:(rl + 1) * GRID_W, :] = jnp.concatenate(blocks, axis=1)


def _na_kernel(rpb_ref, q_ref, k_ref, v_ref, kct_ref, vct_ref, mix_ref, y_ref, nb_ref):
    @pl.when(pl.program_id(1) == 0)
    def _():
        for hh in range(2):
            _build_na_bias(rpb_ref, 2 * pl.program_id(0) + hh, nb_ref, hh)

    def scores(b, half, hh):
        w0 = b * DEC_SEQ + half * (DEC_SEQ - NA_KEYS)
        qp = q_ref[b * DEC_SEQ + half * NA_HALF:b * DEC_SEQ + (half + 1) * NA_HALF, :]
        kw = k_ref[w0:w0 + NA_KEYS, :]
        s1 = _dot_nt(qp, jnp.where(_keep_mask_bf16(NA_KEYS, hh), kw, jnp.zeros_like(kw))) + nb_ref[hh, half]
        s2 = _dot(qp, _pad_rows(kct_ref[b, 0, hh].astype(BF16), hh, 0.0))
        return [s1, s2]

    def finish(out, b, half, hh, e_parts, term):
        w0 = b * DEC_SEQ + half * (DEC_SEQ - NA_KEYS)
        vw = v_ref[w0:w0 + NA_KEYS, :]
        o = (_dot(e_parts[0], jnp.where(_keep_mask_bf16(NA_KEYS, hh), vw, jnp.ones_like(vw)))
             + _dot_nt(e_parts[1], _pad_rows(vct_ref[b, 0, hh].astype(BF16), hh, 1.0)))
        out.put(hh, o, term)

    heads = []
    for b in range(NA_SEQS_PER_STEP):
        for half in range(2):
            rows = slice(b * DEC_SEQ + half * NA_HALF, b * DEC_SEQ + (half + 1) * NA_HALF)
            out = _PairSink(y_ref, rows, slice(0, PAIR))
            for hh in range(2):
                heads.append(_Head(functools.partial(scores, b, half, hh), functools.partial(finish, out, b, half, hh)))
    _attend(heads, batch=4)


def _na_mixer(proj, cache_kt, cache_vt, rpb_flat, mix):
    pairs = H_C // 2
    n_seq = NA_SEQS_PER_STEP
    row0 = CTX_TOKENS // (n_seq * DEC_SEQ)
    col = lambda base: pl.BlockSpec((n_seq * DEC_SEQ, PAIR), lambda hp, b: (b, base + hp))
    cache = pl.BlockSpec((n_seq, 1, 2, DH, PAST_LEN), lambda hp, b: (b, 0, hp, 0, 0))
    return pl.pallas_call(
        _na_kernel,
        grid=(pairs, DEC_BATCH // n_seq),
        in_specs=[pl.BlockSpec(memory_space=pltpu.SMEM), col(0), col(pairs), col(2 * pairs), cache, cache,
                  pl.BlockSpec(memory_space=pl.ANY)],
        out_specs=pl.BlockSpec((n_seq * DEC_SEQ, PAIR), lambda hp, b: (b + row0, hp)),
        out_shape=jax.ShapeDtypeStruct((ALL_TOKENS, D_MODEL), BF16),
        input_output_aliases={6: 0},
        scratch_shapes=[pltpu.VMEM((2, 2, NA_HALF, NA_KEYS), F32)],
        compiler_params=_cparams(2),
        name="na_mixer",
    )(rpb_flat, proj, proj, proj, cache_kt, cache_vt, mix)


def _swa_kernel(q_ref, kv_ref, kct_ref, vct_ref, sink_ref, mix_ref, y_ref):
    n_groups = DEC_SEQ // SWA_QROWS
    ql = lax.broadcasted_iota(jnp.int32, (SWA_QROWS, SWA_BAND), 0)
    kj = lax.broadcasted_iota(jnp.int32, (SWA_QROWS, SWA_BAND), 1)

    def band_bias(first_key_minus_first_query):
        return jnp.where(jnp.abs(kj + first_key_minus_first_query - ql) <= SWA_WINDOW, 0.0, NEG)

    bias_first, bias_mid, bias_last = band_bias(0), band_bias(-SWA_BLOCK), band_bias(-2 * SWA_BLOCK)

    def band_start(grp):
        return min(max(grp * SWA_QROWS - SWA_BLOCK, 0), DEC_SEQ - SWA_BAND)

    memo = _Memo()

    def band_k(g, grp, hh):
        start = band_start(grp)
        kb = kv_ref[start:start + SWA_BAND, g * PAIR:(g + 1) * PAIR]
        return jnp.where(_keep_mask_bf16(SWA_BAND, hh), kb, jnp.zeros_like(kb))

    def band_v(g, grp, hh):
        start = band_start(grp)
        vb = kv_ref[start:start + SWA_BAND, (KV_D + g) * PAIR:(KV_D + g + 1) * PAIR]
        return jnp.where(_keep_mask_bf16(SWA_BAND, hh), vb, jnp.ones_like(vb))

    def scores(g, grp, p, hh):
        bias = bias_first if grp == 0 else (bias_last if grp == n_groups - 1 else bias_mid)
        qp = q_ref[grp * SWA_QROWS:(grp + 1) * SWA_QROWS, p * PAIR:(p + 1) * PAIR]
        s1 = _dot_nt(qp, memo.get_or_build(("k", g, grp, hh), functools.partial(band_k, g, grp, hh))) + bias
        ckt = memo.get_or_build(("ck", g, hh), lambda: _pad_rows(kct_ref[0, 0, g].astype(BF16), hh, 0.0))
        return [s1, _dot(qp, ckt)]

    def finish(out, g, grp, hh, e_parts, term):
        cvt = memo.get_or_build(("cv", g, hh), lambda: _pad_rows(vct_ref[0, 0, g].astype(BF16), hh, 1.0))
        o = (_dot(e_parts[0], memo.get_or_build(("v", g, grp, hh), functools.partial(band_v, g, grp, hh)))
             + _dot_nt(e_parts[1], cvt))
        out.put(hh, o, term)

    heads = []
    for g in range(KV_D):
        for grp in range(n_groups):
            for j in range(GQA // 2):
                p = g * (GQA // 2) + j
                out = _PairSink(y_ref, slice(grp * SWA_QROWS, (grp + 1) * SWA_QROWS), slice(p * PAIR, (p + 1) * PAIR))
                for hh in range(2):
                    sink = sink_ref[2 * p + hh] * LOG2E
                    heads.append(_Head(functools.partial(scores, g, grp, p, hh),
                                       functools.partial(finish, out, g, grp, hh), sink))
    _attend(heads, batch=8)


def _swa_mixer(proj, cache_kt, cache_vt, sink, mix):
    row0 = CTX_TOKENS // DEC_SEQ
    cache = pl.BlockSpec((1, 1, KV_D, DH, PAST_LEN), lambda b: (b, 0, 0, 0, 0))
    return pl.pallas_call(
        _swa_kernel,
        grid=(DEC_BATCH,),
        in_specs=[pl.BlockSpec((DEC_SEQ, 512), lambda b: (b, 3)),
                  pl.BlockSpec((DEC_SEQ, 512), lambda b: (b, 4)),
                  cache, cache,
                  pl.BlockSpec(memory_space=pltpu.SMEM), pl.BlockSpec(memory_space=pl.ANY)],
        out_specs=pl.BlockSpec((DEC_SEQ, 512), lambda b: (b + row0, 1)),
        out_shape=jax.ShapeDtypeStruct((ALL_TOKENS, D_MODEL), BF16),
        input_output_aliases={5: 0},
        compiler_params=_cparams(1),
        name="swa_mixer",
    )(proj, proj, cache_kt, cache_vt, sink, mix)


def _post_mixer_kernel(*refs, li, n_x, n_out):
    mix_ref = refs[0]
    x_refs = refs[1:1 + n_x]
    mod_ref, gpost_ref, gpre_ref, gmlp_ref, wo_ref, w1_ref, w2_ref = refs[1 + n_x:8 + n_x]
    out_refs = refs[8 + n_x:8 + n_x + n_out]
    wo_s, w1_s, w2_s = refs[8 + n_x + n_out:]
    step = pl.program_id(0)

    for k in range(N_PREP):
        @pl.when(step == k)
        def _(k=k):
            if k < N_WO:
                wo_s[k * WO_ROWS:(k + 1) * WO_ROWS, :] = wo_ref[0].astype(BF16)
            elif k < N_WO + N_W1:
                w1_s[(k - N_WO) * W1_ROWS:(k - N_WO + 1) * W1_ROWS, :] = w1_ref[0].astype(BF16)
            else:
                j = k - N_WO - N_W1
                w2_s[j * W2_ROWS:(j + 1) * W2_ROWS, :] = w2_ref[0].astype(BF16)

    @pl.when(step >= N_PREP)
    def _():
        t = step - N_PREP
        is_ctx = t < CTX_TILES
        mod = mod_ref[0, pl.ds(jnp.where(is_ctx, CTX_ROW, (t - CTX_TILES) // (DEC_SEQ // TM)), 1), :]
        g_post, g_pre, g_mlp = gpost_ref[li:li + 1, :], gpre_ref[li:li + 1, :], gmlp_ref[li:li + 1, :]
        rows = [slice(i * (TM // POST_SPLIT), (i + 1) * (TM // POST_SPLIT)) for i in range(POST_SPLIT)]

        def residual(r):
            if n_x == 1:
                return x_refs[0][r, :]
            return jnp.where(is_ctx, x_refs[0][r, :], x_refs[1][r, :])

        ys = [_dot(mix_ref[r, :], wo_s[...]) for r in rows]
        x1 = [residual(r) + mod[:, 2 * D_MODEL:3 * D_MODEL] * _rms(y, g_post) for r, y in zip(rows, ys)]
        h = [_norm_mod(x, g_pre, mod, 3 * D_MODEL).astype(BF16) for x in x1]
        acc = [None] * POST_SPLIT
        pending = []

        def second_matmul(c, i, f):
            t2 = _dot(f, w2_s[c * FF_CHUNK:(c + 1) * FF_CHUNK, :])
            acc[i] = t2 if acc[i] is None else acc[i] + t2

        for c in range(D_FF // FF_CHUNK):
            for i in range(POST_SPLIT):
                f = _dot(h[i], w1_s[:, c * FF_CHUNK:(c + 1) * FF_CHUNK])
                pending.append((c, i, jnp.square(jnp.maximum(f, 0.0)).astype(BF16)))
                if len(pending) > 1:
                    second_matmul(*pending.pop(0))
        second_matmul(*pending.pop(0))
        res = [x1[i] + mod[:, 5 * D_MODEL:6 * D_MODEL] * _rms(acc[i], g_mlp) for i in range(POST_SPLIT)]
        if n_out == 1:
            for i, r in enumerate(rows):
                out_refs[0][r, :] = res[i]
        else:
            @pl.when(is_ctx)
            def _():
                for i, r in enumerate(rows):
                    out_refs[0][r, :] = res[i]

            @pl.when(jnp.logical_not(is_ctx))
            def _():
                for i, r in enumerate(rows):
                    out_refs[1][r, :] = res[i]


def _post_mixer(mix, xs, mod, g_post, g_pre, g_mlp, w_out, w1, w2, li, *, split_out):
    tile = lambda s: jnp.maximum(s - N_PREP, 0)
    ctx_tile = lambda s: jnp.minimum(tile(s), CTX_TILES - 1)
    lat_tile = lambda s: jnp.maximum(tile(s) - CTX_TILES, 0)
    token_spec = lambda index: pl.BlockSpec((TM, D_MODEL), lambda s: (index(s), 0))
    x_specs = [token_spec(tile)] if len(xs) == 1 else [token_spec(ctx_tile), token_spec(lat_tile)]
    if split_out:
        out_specs = [token_spec(ctx_tile), token_spec(lat_tile)]
        out_shape = [jax.ShapeDtypeStruct((CTX_TOKENS, D_MODEL), F32), jax.ShapeDtypeStruct((LAT_TOKENS, D_MODEL), F32)]
    else:
        out_specs = [token_spec(tile)]
        out_shape = [jax.ShapeDtypeStruct((ALL_TOKENS, D_MODEL), F32)]
    gain = pl.BlockSpec((DEPTH, D_MODEL), lambda s: (0, 0))
    return pl.pallas_call(
        functools.partial(_post_mixer_kernel, li=li, n_x=len(xs), n_out=len(out_specs)),
        grid=(N_PREP + ALL_TOKENS // TM,),
        in_specs=[token_spec(tile)] + x_specs + [
            pl.BlockSpec((1, MOD_ROWS, 6 * D_MODEL), lambda s: (li, 0, 0)),
            gain, gain, gain,
            pl.BlockSpec((1, WO_ROWS, D_MODEL), lambda s: (li, jnp.clip(s, 0, N_WO - 1), 0)),
            pl.BlockSpec((1, W1_ROWS, D_FF), lambda s: (li, jnp.clip(s - N_WO, 0, N_W1 - 1), 0)),
            pl.BlockSpec((1, W2_ROWS, D_MODEL), lambda s: (li, jnp.clip(s - N_WO - N_W1, 0, N_W2 - 1), 0))],
        out_specs=out_specs,
        out_shape=out_shape,
        scratch_shapes=[pltpu.VMEM((D_MODEL, D_MODEL), BF16), pltpu.VMEM((D_MODEL, D_FF), BF16),
                        pltpu.VMEM((D_FF, D_MODEL), BF16)],
        compiler_params=_cparams(1),
        name="post_mixer",
    )(mix, *xs, mod, g_post, g_pre, g_mlp, w_out, w1, w2)


def _rope_tables():
    t = jnp.arange(DEC_SEQ)
    rows = (t // GRID_W).astype(F32)
    cols = (t % GRID_W).astype(F32)
    q4 = DH // 4
    inv = 1.0 / (ROPE_BASE ** (jnp.arange(q4, dtype=F32) / q4))
    ar, ac = rows[:, None] * inv, cols[:, None] * inv
    cos64 = jnp.concatenate([jnp.cos(ar), jnp.cos(ar), jnp.cos(ac), jnp.cos(ac)], axis=1)
    sin64 = jnp.concatenate([-jnp.sin(ar), jnp.sin(ar), -jnp.sin(ac), jnp.sin(ac)], axis=1)
    return jnp.tile(cos64, (1, 2)), jnp.tile(sin64, (1, 2))


def kernel(x_prompt, x_sample, cache_diff_k, cache_diff_v, cache_na_k, cache_na_v, cache_swa_k, cache_swa_v, c, c_ctx, mod_w, mod_b, norm_mix_pre, norm_mix_post, norm_mlp_pre, norm_mlp_post, w_in_even, conv_w, lambda_q1, lambda_k1, lambda_q2, lambda_k2, subln, w_in_odd, rpb, sink, w_out, mlp_w1, mlp_w2):
    cond = jnp.concatenate([c, c_ctx[None, :], jnp.zeros((MOD_ROWS - DEC_BATCH - 1, D_MODEL), F32)], axis=0)
    mod = _modulation(cond, mod_w, mod_b)
    rope = _rope_tables()

    t_minor = lambda a: jnp.swapaxes(a, -1, -2)

    w_in = [w_in_even.astype(BF16), w_in_odd.astype(BF16)]

    xp = x_prompt.reshape(CTX_TOKENS, D_MODEL)
    xs = x_sample.reshape(LAT_TOKENS, D_MODEL)
    lam_init0 = 0.8 - 0.6 * math.exp(-0.3 * 0)
    lam_args = (conv_w, lambda_q1, lambda_k1, lambda_q2, lambda_k2, subln)
    sink_flat = sink.reshape(-1)

    def post(li, mix, x_parts, split_out):
        return _post_mixer(mix, x_parts, mod, norm_mix_post, norm_mlp_pre, norm_mlp_post, w_out, mlp_w1, mlp_w2, li,
                           split_out=split_out)

    proj, new_diff_kt, new_diff_v = _pre_mixer(xp, mod, norm_mix_pre, w_in[0], 0, latent=False, even=True, rope=None,
                                               n_tok=CTX_TOKENS)
    mix = _even_mixer(proj, *lam_args, new_diff_kt, new_diff_v, None, seq=SEQ, n_seq=CTX_SEQS_PER_STEP // 2,
                      own_keys=False, lam_init=lam_init0)
    proj = _pre_mixer(xs, mod, norm_mix_pre, w_in[0], 0, latent=True, even=True, rope=rope, n_tok=LAT_TOKENS)
    mix = _even_mixer(proj, *lam_args, t_minor(cache_diff_k), cache_diff_v, mix, seq=DEC_SEQ, n_seq=1, own_keys=True,
                      lam_init=lam_init0)
    (x_all,) = post(0, mix, [xp, xs], False)

    q, new_na_kt, new_na_vt, new_swa_kt, new_swa_vt = _pre_mixer(
        x_all, mod, norm_mix_pre, w_in[1], 1, latent=False, even=False, rope=None, n_tok=CTX_TOKENS)
    mix = _odd_ctx_mixer(q, new_na_kt, new_na_vt, new_swa_kt, new_swa_vt, sink_flat)
    proj = _pre_mixer(x_all, mod, norm_mix_pre, w_in[1], 1, latent=True, even=False, rope=rope, n_tok=LAT_TOKENS,
                      tile0=CTX_TILES)
    mix = _na_mixer(proj, t_minor(cache_na_k), t_minor(cache_na_v), rpb.reshape(-1), mix)
    mix = _swa_mixer(proj, t_minor(cache_swa_k), t_minor(cache_swa_v), sink_flat, mix)
    xp, xs = post(1, mix, [x_all], True)

    return (xp.reshape(BATCH, SEQ, D_MODEL), xs.reshape(DEC_BATCH, DEC_SEQ, D_MODEL),
            t_minor(new_diff_kt), new_diff_v, t_minor(new_na_kt), t_minor(new_na_vt),
            t_minor(new_swa_kt), t_minor(new_swa_vt))
```

```python
import functools
import math

import jax
import jax.numpy as jnp
from jax import lax
from jax.experimental import pallas as pl
from jax.experimental.pallas import tpu as pltpu

F32 = jnp.float32
BF16 = jnp.bfloat16

D_MODEL = 1024
BATCH = 32
SEQ = 256
DEPTH = 2
DEC_BATCH = 8
DEC_SEQ = 1024
PAST_LEN = 256
GRID_W = 64
GRID_H = DEC_SEQ // GRID_W
MIX_A = 512
H_B = 4
DH = 64
PAIR = 2 * DH
H_C = 8
H_D = 8
KV_D = 2
GQA = H_D // KV_D
IN_EVEN = 3072
IN_ODD = 2304
D_FF = 4096
NA_WIN_R = 8
NA_WIN_C = 16
SWA_BLOCK = 128
SWA_WINDOW = 128
ROPE_BASE = 10000.0
EPS = 1e-6
NEG = -1e30
LOG2E = math.log2(math.e)
Q_SCALE = DH ** -0.5 * LOG2E

MOD_ROWS = 16
CTX_ROW = DEC_BATCH
V7X_VMEM_BYTES = 64 * 1024 * 1024
VMEM_LIMIT = V7X_VMEM_BYTES - 12 * 1024 * 1024

CTX_TOKENS = BATCH * SEQ
LAT_TOKENS = DEC_BATCH * DEC_SEQ
ALL_TOKENS = CTX_TOKENS + LAT_TOKENS

TM = 512
FF_CHUNK = 1024
POST_SPLIT = 2
WCHUNK_ELEMS = 256 * 1024
WO_ROWS, W1_ROWS, W2_ROWS = WCHUNK_ELEMS // D_MODEL, WCHUNK_ELEMS // D_FF, WCHUNK_ELEMS // D_MODEL
N_WO, N_W1, N_W2 = D_MODEL // WO_ROWS, D_MODEL // W1_ROWS, D_FF // W2_ROWS
N_PREP = N_WO + N_W1 + N_W2
CTX_TILES = CTX_TOKENS // TM
CTX_SEQS_PER_STEP = 4
MOD_TN = 1536
NA_SEQS_PER_STEP = 2
NA_HALF = 512
NA_KEYS = 768
SWA_QROWS = 2 * SWA_BLOCK
SWA_BAND = 4 * SWA_BLOCK
LAT_ODD_COLS = IN_ODD + 2 * PAIR


def _cparams(n_axes):
    return pltpu.CompilerParams(dimension_semantics=("arbitrary",) * n_axes, vmem_limit_bytes=VMEM_LIMIT)


def _rms(x, g):
    return x * lax.rsqrt(jnp.mean(x * x, axis=-1, keepdims=True) + EPS) * g


def _dot(a, b):
    return jnp.dot(a, b, preferred_element_type=F32)


def _dot_nt(a, b):
    return lax.dot_general(a, b, (((1,), (1,)), ((), ())), preferred_element_type=F32)


def _layer_block(shape, li):
    zeros = (0,) * len(shape)
    return pl.BlockSpec((1,) + tuple(shape), lambda *_: (li,) + zeros, pipeline_mode=pl.Buffered(1))


def _low_half(shape):
    return (lax.broadcasted_iota(jnp.int32, shape, len(shape) - 1) & DH) == 0


def _keep_mask_bf16(rows, hh):
    lane = lax.broadcasted_iota(jnp.int32, (rows, PAIR), 1)
    half = (lane & DH).astype(F32).astype(BF16)
    return (half == 0) if hh == 0 else (half != 0)


def _pad_rows(t, hh, fill):
    other = jnp.full(t.shape, fill, t.dtype)
    return jnp.concatenate([t, other] if hh == 0 else [other, t], axis=0)


class _Memo(dict):
    def get_or_build(self, key, build):
        if key not in self:
            self[key] = build()
        return self[key]


def _merge_pair(o0, o1, extra_den=None):
    low = _low_half(o0.shape)
    num = jnp.where(low, o0, o1)
    den = pltpu.roll(jnp.where(low, o1, o0), DH, axis=1)
    if extra_den is not None:
        den = den + jnp.where(low, extra_den[0], extra_den[1])
    return num * (1.0 / den)


class _Head:
    def __init__(self, scores, finish, sink=None):
        self.scores, self.finish, self.sink = scores, finish, sink


def _attend(heads, batch):
    batches = [heads[i:i + batch] for i in range(0, len(heads), batch)]
    cur = [hd.scores() for hd in batches[0]]
    for bi, group in enumerate(batches):
        nxt = [hd.scores() for hd in batches[bi + 1]] if bi + 1 < len(batches) else None
        exps = []
        for hd, parts in zip(group, cur):
            chunks = [s[:, j * 128:(j + 1) * 128] for s in parts for j in range(s.shape[1] // 128)]
            m_el = functools.reduce(jnp.maximum, chunks)
            if hd.sink is not None:
                m_el = jnp.maximum(m_el, hd.sink)
            mx = jnp.broadcast_to(jnp.max(m_el, axis=-1, keepdims=True), m_el.shape)
            term = None if hd.sink is None else jnp.exp2(hd.sink - mx)
            e_parts = [jnp.concatenate([jnp.exp2(s[:, j * 128:(j + 1) * 128] - mx)
                                        for j in range(s.shape[1] // 128)], axis=1).astype(BF16) for s in parts]
            exps.append((e_parts, term))
        for hd, (e_parts, term) in zip(group, exps):
            hd.finish(e_parts, term)
        cur = nxt


class _PairSink:
    def __init__(self, y_ref, rows, cols):
        self.y_ref, self.rows, self.cols, self.first = y_ref, rows, cols, None

    def put(self, hh, o, term):
        if hh == 0:
            self.first = (o, term)
        else:
            o0, term0 = self.first
            extra = None if term is None else (term0, term)
            self.y_ref[self.rows, self.cols] = _merge_pair(o0, o, extra).astype(BF16)


def _mod_kernel(cond_ref, w_ref, b_ref, o_ref):
    c = cond_ref[...]
    s = c * (1.0 / (1.0 + jnp.exp(-c)))
    o_ref[0] = _dot(s.astype(BF16), w_ref[0].astype(BF16)) + b_ref[pl.ds(pl.program_id(0), 1), :]


def _modulation(cond, mod_w, mod_b):
    return pl.pallas_call(
        _mod_kernel,
        grid=(DEPTH, 6 * D_MODEL // MOD_TN),
        in_specs=[
            pl.BlockSpec((MOD_ROWS, D_MODEL), lambda l, j: (0, 0)),
            pl.BlockSpec((1, D_MODEL, MOD_TN), lambda l, j: (l, 0, j)),
            pl.BlockSpec((DEPTH, MOD_TN), lambda l, j: (0, j)),
        ],
        out_specs=pl.BlockSpec((1, MOD_ROWS, MOD_TN), lambda l, j: (l, 0, j)),
        out_shape=jax.ShapeDtypeStruct((DEPTH, MOD_ROWS, 6 * D_MODEL), F32),
        compiler_params=_cparams(2),
        name="adaln_modulation",
    )(cond, mod_w, mod_b)


def _norm_mod(x, g, mod, shift_col):
    sh = mod[:, shift_col:shift_col + D_MODEL]
    sc = mod[:, shift_col + D_MODEL:shift_col + 2 * D_MODEL]
    return _rms(x, g) * (1.0 + sc) + sh


def _rope(z, cos, sin):
    outs = []
    for j in range(z.shape[1] // 128):
        zj = z[:, j * 128:(j + 1) * 128]
        lane = lax.broadcasted_iota(jnp.int32, zj.shape, 1)
        partner = jnp.where((lane & 16) == 0, pltpu.roll(zj, 128 - 16, axis=1), pltpu.roll(zj, 16, axis=1))
        outs.append(zj * cos + partner * sin)
    return outs[0] if len(outs) == 1 else jnp.concatenate(outs, axis=1)


def _store_heads_transposed(dst, bi, pc_rows, n_pairs):
    for p in range(n_pairs):
        t = pc_rows[:, p * PAIR:(p + 1) * PAIR].T
        dst[bi, 0, 2 * p] = t[0:DH]
        dst[bi, 0, 2 * p + 1] = t[DH:PAIR]


def _mod_row(mod_ref, latent):
    if latent:
        return mod_ref[0, pl.ds(pl.program_id(0) // (DEC_SEQ // TM), 1), :]
    return mod_ref[0, CTX_ROW:CTX_ROW + 1, :]


def _normed_groups(x_ref, mod_ref, g_ref, li, latent):
    rows = [slice(i * SEQ, (i + 1) * SEQ) for i in range(TM // SEQ)]
    mod, g = _mod_row(mod_ref, latent), g_ref[li:li + 1, :]
    return rows, [_norm_mod(x_ref[r, :], g, mod, 0).astype(BF16) for r in rows]


def _k1_ctx_even(x_ref, mod_ref, g_ref, w_ref, proj_ref, dkt_ref, dv_ref, *, li):
    rows, hs = _normed_groups(x_ref, mod_ref, g_ref, li, False)
    for c in range(IN_EVEN // 512):
        for bi, (r, h) in enumerate(zip(rows, hs)):
            pc = _dot(h, w_ref[0, :, c * 512:(c + 1) * 512])
            if c < 3:
                proj_ref[r, c * 512:(c + 1) * 512] = pc.astype(BF16)
            elif c == 3:
                proj_ref[r, c * 512:(c + 1) * 512] = (pc * Q_SCALE).astype(BF16)
            elif c == 4:
                for hh in range(H_B):
                    t = pc[:, hh * PAIR:(hh + 1) * PAIR].T
                    dkt_ref[bi, 0, hh, 0] = t[0:DH]
                    dkt_ref[bi, 0, hh, 1] = t[DH:PAIR]
            else:
                for hh in range(H_B):
                    dv_ref[bi, 0, hh] = pc[:, hh * PAIR:(hh + 1) * PAIR]


def _k1_lat_even(x_ref, mod_ref, g_ref, w_ref, cos_ref, sin_ref, proj_ref, *, li):
    rows, hs = _normed_groups(x_ref, mod_ref, g_ref, li, True)
    for c in range(IN_EVEN // 512):
        for r, h in zip(rows, hs):
            pc = _dot(h, w_ref[0, :, c * 512:(c + 1) * 512])
            if c == 3:
                pc = _rope(pc * Q_SCALE, cos_ref[r, :], sin_ref[r, :])
            elif c == 4:
                pc = _rope(pc, cos_ref[r, :], sin_ref[r, :])
            proj_ref[r, c * 512:(c + 1) * 512] = pc.astype(BF16)


def _k1_ctx_odd(x_ref, mod_ref, g_ref, w_ref, q_ref, nkt_ref, nvt_ref, skt_ref, svt_ref, *, li):
    rows, hs = _normed_groups(x_ref, mod_ref, g_ref, li, False)
    for c in range(4):
        for bi, (r, h) in enumerate(zip(rows, hs)):
            pc = _dot(h, w_ref[0, :, c * 512:(c + 1) * 512])
            if c == 0 or c == 3:
                q_ref[r, (c // 3) * 512:(c // 3 + 1) * 512] = (pc * Q_SCALE).astype(BF16)
            else:
                _store_heads_transposed(nkt_ref if c == 1 else nvt_ref, bi, pc, H_C // 2)
    for bi, h in enumerate(hs):
        pc = _dot(h, w_ref[0, :, 2048:IN_ODD])
        _store_heads_transposed(skt_ref, bi, pc[:, 0:PAIR], 1)
        _store_heads_transposed(svt_ref, bi, pc[:, PAIR:2 * PAIR], 1)


def _k1_lat_odd(x_ref, mod_ref, g_ref, w_ref, cos_ref, sin_ref, proj_ref, *, li):
    rows, hs = _normed_groups(x_ref, mod_ref, g_ref, li, True)
    for c in range(4):
        for r, h in zip(rows, hs):
            pc = _dot(h, w_ref[0, :, c * 512:(c + 1) * 512])
            if c == 0:
                pc = pc * Q_SCALE
            elif c == 3:
                pc = _rope(pc * Q_SCALE, cos_ref[r, :], sin_ref[r, :])
            proj_ref[r, c * 512:(c + 1) * 512] = pc.astype(BF16)
    low = _low_half((SEQ, PAIR))
    for r, h in zip(rows, hs):
        pc = _dot(h, w_ref[0, :, 2048:IN_ODD])
        for j, z in enumerate((_rope(pc[:, 0:PAIR], cos_ref[r, :], sin_ref[r, :]), pc[:, PAIR:2 * PAIR])):
            zr = pltpu.roll(z, DH, axis=1)
            base = 2048 + j * 2 * PAIR
            proj_ref[r, base:base + PAIR] = jnp.where(low, z, zr).astype(BF16)
            proj_ref[r, base + PAIR:base + 2 * PAIR] = jnp.where(low, zr, z).astype(BF16)


def _pre_mixer(x, mod, gains, w, li, *, latent, even, rope, n_tok, tile0=0):
    n_in = w.shape[2]
    tiles_per_seq = DEC_SEQ // TM
    in_specs = [
        pl.BlockSpec((TM, D_MODEL), lambda i: (i + tile0, 0)),
        pl.BlockSpec((1, MOD_ROWS, 6 * D_MODEL), lambda i: (li, 0, 0)),
        pl.BlockSpec((DEPTH, D_MODEL), lambda i: (0, 0)),
        _layer_block((D_MODEL, n_in), 0),
    ]
    args = [x, mod, gains, w]
    nb = TM // SEQ
    if latent:
        in_specs += [pl.BlockSpec((TM, 128), lambda i: (i % tiles_per_seq, 0))] * 2
        args += list(rope)
        body = _k1_lat_even if even else _k1_lat_odd
        n_out = IN_EVEN if even else LAT_ODD_COLS
        out_specs = pl.BlockSpec((TM, n_out), lambda i: (i, 0))
        out_shape = jax.ShapeDtypeStruct((n_tok, n_out), BF16)
    elif even:
        body = _k1_ctx_even
        out_specs = [pl.BlockSpec((TM, 4 * MIX_A), lambda i: (i, 0)),
                     pl.BlockSpec((nb, 1, H_B, 2, DH, SEQ), lambda i: (i, 0, 0, 0, 0, 0)),
                     pl.BlockSpec((nb, 1, H_B, SEQ, PAIR), lambda i: (i, 0, 0, 0, 0))]
        out_shape = [jax.ShapeDtypeStruct((n_tok, 4 * MIX_A), BF16),
                     jax.ShapeDtypeStruct((BATCH, 1, H_B, 2, DH, SEQ), F32),
                     jax.ShapeDtypeStruct((BATCH, 1, H_B, SEQ, PAIR), F32)]
    else:
        body = _k1_ctx_odd
        c_spec = pl.BlockSpec((nb, 1, H_C, DH, SEQ), lambda i: (i, 0, 0, 0, 0))
        d_spec = pl.BlockSpec((nb, 1, KV_D, DH, SEQ), lambda i: (i, 0, 0, 0, 0))
        c_shape = jax.ShapeDtypeStruct((BATCH, 1, H_C, DH, SEQ), F32)
        d_shape = jax.ShapeDtypeStruct((BATCH, 1, KV_D, DH, SEQ), F32)
        out_specs = [pl.BlockSpec((TM, D_MODEL), lambda i: (i, 0)), c_spec, c_spec, d_spec, d_spec]
        out_shape = [jax.ShapeDtypeStruct((n_tok, D_MODEL), BF16), c_shape, c_shape, d_shape, d_shape]
    return pl.pallas_call(
        functools.partial(body, li=li),
        grid=(n_tok // TM,),
        in_specs=in_specs,
        out_specs=out_specs,
        out_shape=out_shape,
        compiler_params=_cparams(1),
        name=f"pre_mixer_{'lat' if latent else 'ctx'}_{'even' if even else 'odd'}",
    )(*args)


def _even_mixer_kernel(*refs, seq, n_seq, tq, own_keys, lam_init):
    proj_ref, cw_ref, lq1_ref, lk1_ref, lq2_ref, lk2_ref, subln_ref, ckt_ref, cv_ref = refs[:9]
    y_ref = refs[-1]
    n_rows = n_seq * seq

    pos = lax.broadcasted_iota(jnp.int32, (n_rows, 128), 0) % seq
    for j in range(MIX_A // 128):
        cols = slice(j * 128, (j + 1) * 128)
        a_b = proj_ref[:, j * 128:(j + 1) * 128].astype(F32)
        u = (proj_ref[:, MIX_A + j * 128:MIX_A + (j + 1) * 128].astype(F32)
             * proj_ref[:, 2 * MIX_A + j * 128:2 * MIX_A + (j + 1) * 128].astype(F32))
        u_prev = jnp.where(pos == 0, 0.0, pltpu.roll(u, 1, axis=0))
        u_next = jnp.where(pos == seq - 1, 0.0, pltpu.roll(u, n_rows - 1, axis=0))
        w = cw_ref[0, :, cols]
        y_ref[:, cols] = (a_b * (w[0:1] * u_prev + w[1:2] * u + w[2:3] * u_next)).astype(BF16)

    lam = (jnp.exp(jnp.sum(lq1_ref[...] * lk1_ref[...], axis=-1, keepdims=True))
           - jnp.exp(jnp.sum(lq2_ref[...] * lk2_ref[...], axis=-1, keepdims=True)) + lam_init)
    subln = subln_ref[...]
    q_col, k_col, v_col = 3 * MIX_A, 3 * MIX_A + 512, 3 * MIX_A + 1024
    ones = jnp.ones((PAST_LEN, PAIR), BF16)

    memo = _Memo()

    def own_k(b, h, m):
        kp = proj_ref[b * seq:(b + 1) * seq, k_col + h * PAIR:k_col + (h + 1) * PAIR]
        return jnp.where(_keep_mask_bf16(seq, m), kp, jnp.zeros_like(kp))

    def value_ops(b, h):
        v_ops = []
        if own_keys:
            v_own = proj_ref[b * seq:(b + 1) * seq, v_col + h * PAIR:v_col + (h + 1) * PAIR]
            v_ops.append(jnp.concatenate([v_own, jnp.ones((seq, PAIR), BF16)], axis=1))
        v_ops.append(jnp.concatenate([cv_ref[b, 0, h].astype(BF16), ones], axis=1))
        return v_ops

    def scores(b, qrows, h, m):
        qp = proj_ref[qrows, q_col + h * PAIR:q_col + (h + 1) * PAIR]
        parts = []
        if own_keys:
            parts.append(_dot_nt(qp, memo.get_or_build(("k", b, h, m), functools.partial(own_k, b, h, m))))
        ckt = memo.get_or_build(("ck", b, h, m), lambda: _pad_rows(ckt_ref[b, 0, h, m].astype(BF16), m, 0.0))
        parts.append(_dot(qp, ckt))
        return parts

    first_map = {}

    def pv(b, qrows, h, m, e_parts, _):
        v_ops = memo.get_or_build(("v", b, h), functools.partial(value_ops, b, h))
        o = None
        for e, v_op in zip(e_parts, v_ops):
            t = _dot(e, v_op)
            o = t if o is None else o + t
        attn = o[:, 0:PAIR] * (1.0 / o[:, PAIR:2 * PAIR])
        if m == 0:
            first_map[(qrows.start, h)] = attn
        else:
            y = _rms(first_map.pop((qrows.start, h)) - lam * attn, subln) * (1.0 - lam_init)
            y_ref[qrows, MIX_A + h * PAIR:MIX_A + (h + 1) * PAIR] = y.astype(BF16)

    heads = []
    for b in range(n_seq):
        for i in range(seq // tq):
            qrows = slice(b * seq + i * tq, b * seq + (i + 1) * tq)
            for h in range(H_B):
                for m in range(2):
                    heads.append(_Head(functools.partial(scores, b, qrows, h, m),
                                       functools.partial(pv, b, qrows, h, m)))
    _attend(heads, batch=8)


def _even_mixer(proj, conv_w, lq1, lk1, lq2, lk2, subln, kt, v, *, seq, n_seq, own_keys, lam_init):
    n_tok = proj.shape[0]
    rows = n_seq * seq
    small = lambda a: pl.BlockSpec((1, a.shape[1]), lambda b: (0, 0))
    in_specs = [pl.BlockSpec((rows, proj.shape[1]), lambda b: (b, 0)),
                pl.BlockSpec((1, 3, MIX_A), lambda b: (0, 0, 0)),
                small(lq1), small(lk1), small(lq2), small(lk2), small(subln),
                pl.BlockSpec((n_seq, 1, H_B, 2, DH, PAST_LEN), lambda b: (b, 0, 0, 0, 0, 0)),
                pl.BlockSpec((n_seq, 1, H_B, PAST_LEN, PAIR), lambda b: (b, 0, 0, 0, 0))]
    return pl.pallas_call(
        functools.partial(_even_mixer_kernel, seq=seq, n_seq=n_seq, tq=min(seq, 256), own_keys=own_keys,
                          lam_init=lam_init),
        grid=(n_tok // rows,),
        in_specs=in_specs,
        out_specs=pl.BlockSpec((rows, D_MODEL), lambda b: (b, 0)),
        out_shape=jax.ShapeDtypeStruct((n_tok, D_MODEL), BF16),
        compiler_params=_cparams(1),
        name=f"even_mixer_{'lat' if own_keys else 'ctx'}",
    )(proj, conv_w, lq1, lk1, lq2, lk2, subln, kt, v)


def _odd_ctx_kernel(q_ref, nkt_ref, nvt_ref, skt_ref, svt_ref, sink_ref, y_ref):
    def kv_refs(b, head):
        if head < H_C:
            return nkt_ref.at[b, 0, head], nvt_ref.at[b, 0, head]
        g = (head - H_C) // GQA
        return skt_ref.at[b, 0, g], svt_ref.at[b, 0, g]

    memo = _Memo()

    def kv_key(head):
        return head if head < H_C else H_C + (head - H_C) // GQA

    def scores(b, p, hh):
        head = 2 * p + hh
        kt = memo.get_or_build(("k", b, kv_key(head), hh),
                               lambda: _pad_rows(kv_refs(b, head)[0][...].astype(BF16), hh, 0.0))
        return [_dot(q_ref[b * SEQ:(b + 1) * SEQ, p * PAIR:(p + 1) * PAIR], kt)]

    def finish(out, b, p, hh, e_parts, term):
        head = 2 * p + hh
        vt = memo.get_or_build(("v", b, kv_key(head), hh),
                               lambda: _pad_rows(kv_refs(b, head)[1][...].astype(BF16), hh, 1.0))
        out.put(hh, _dot_nt(e_parts[0], vt), term)

    heads = []
    for b in range(CTX_SEQS_PER_STEP):
        for p in range((H_C + H_D) // 2):
            out = _PairSink(y_ref, slice(b * SEQ, (b + 1) * SEQ), slice(p * PAIR, (p + 1) * PAIR))
            for hh in range(2):
                d_head = 2 * p + hh - H_C
                sink = sink_ref[d_head] * LOG2E if d_head >= 0 else None
                heads.append(_Head(functools.partial(scores, b, p, hh), functools.partial(finish, out, b, p, hh), sink))
    _attend(heads, batch=8)


def _odd_ctx_mixer(q, nkt, nvt, skt, svt, sink):
    n_tok = q.shape[0]
    n_seq = CTX_SEQS_PER_STEP
    c_spec = pl.BlockSpec((n_seq, 1, H_C, DH, SEQ), lambda b: (b, 0, 0, 0, 0))
    d_spec = pl.BlockSpec((n_seq, 1, KV_D, DH, SEQ), lambda b: (b, 0, 0, 0, 0))
    return pl.pallas_call(
        _odd_ctx_kernel,
        grid=(n_tok // (n_seq * SEQ),),
        in_specs=[pl.BlockSpec((n_seq * SEQ, D_MODEL), lambda b: (b, 0)), c_spec, c_spec, d_spec, d_spec,
                  pl.BlockSpec(memory_space=pltpu.SMEM)],
        out_specs=pl.BlockSpec((n_seq * SEQ, D_MODEL), lambda b: (b, 0)),
        out_shape=jax.ShapeDtypeStruct((n_tok, D_MODEL), BF16),
        compiler_params=_cparams(1),
        name="odd_mixer_ctx",
    )(q, nkt, nvt, skt, svt, sink)


def _build_na_bias(rpb_ref, head, nb_ref, slot):
    qc = lax.broadcasted_iota(jnp.int32, (GRID_W, GRID_W), 0)
    kc = lax.broadcasted_iota(jnp.int32, (GRID_W, GRID_W), 1)
    col_start = jnp.clip(qc - NA_WIN_C // 2, 0, GRID_W - NA_WIN_C)
    col_ok = (kc >= col_start) & (kc < col_start + NA_WIN_C)
    dc = kc - qc + NA_WIN_C - 1
    n_dr, n_dc = 2 * NA_WIN_R - 1, 2 * NA_WIN_C - 1
    neg = jnp.full((GRID_W, GRID_W), NEG, F32)
    toeplitz = []
    for dr in range(n_dr):
        t = neg
        for d in range(n_dc):
            t = jnp.where(dc == d, rpb_ref[(head * n_dr + dr) * n_dc + d] * LOG2E, t)
        toeplitz.append(jnp.where(col_ok, t, NEG))
    rows_per_half = NA_HALF // GRID_W
    key_rows = NA_KEYS // GRID_W
    for half in range(2):
        for rl in range(rows_per_half):
            r = half * rows_per_half + rl
            r_start = min(max(r - NA_WIN_R // 2, 0), GRID_H - NA_WIN_R)
            blocks = []
            for j in range(key_rows):
                rk = half * (GRID_H - key_rows) + j
                inside = r_start <= rk < r_start + NA_WIN_R
                blocks.append(toeplitz[rk - r + NA_WIN_R - 1] if inside else neg)
            nb_ref[slot, half, rl * GRID_W:(rl + 1) * GRID_W, :] = jnp.concatenate(blocks, axis=1)


def _na_kernel(rpb_ref, q_ref, k_ref, v_ref, kct_ref, vct_ref, y_ref, nb_ref):
    @pl.when(pl.program_id(1) == 0)
    def _():
        for hh in range(2):
            _build_na_bias(rpb_ref, 2 * pl.program_id(0) + hh, nb_ref, hh)

    def scores(b, half, hh):
        w0 = b * DEC_SEQ + half * (DEC_SEQ - NA_KEYS)
        qp = q_ref[b * DEC_SEQ + half * NA_HALF:b * DEC_SEQ + (half + 1) * NA_HALF, :]
        kw = k_ref[w0:w0 + NA_KEYS, :]
        s1 = _dot_nt(qp, jnp.where(_keep_mask_bf16(NA_KEYS, hh), kw, jnp.zeros_like(kw))) + nb_ref[hh, half]
        s2 = _dot(qp, _pad_rows(kct_ref[b, 0, hh].astype(BF16), hh, 0.0))
        return [s1, s2]

    def finish(out, b, half, hh, e_parts, term):
        w0 = b * DEC_SEQ + half * (DEC_SEQ - NA_KEYS)
        vw = v_ref[w0:w0 + NA_KEYS, :]
        o = (_dot(e_parts[0], jnp.where(_keep_mask_bf16(NA_KEYS, hh), vw, jnp.ones_like(vw)))
             + _dot_nt(e_parts[1], _pad_rows(vct_ref[b, 0, hh].astype(BF16), hh, 1.0)))
        out.put(hh, o, term)

    heads = []
    for b in range(NA_SEQS_PER_STEP):
        for half in range(2):
            rows = slice(b * DEC_SEQ + half * NA_HALF, b * DEC_SEQ + (half + 1) * NA_HALF)
            out = _PairSink(y_ref, rows, slice(0, PAIR))
            for hh in range(2):
                heads.append(_Head(functools.partial(scores, b, half, hh), functools.partial(finish, out, b, half, hh)))
    _attend(heads, batch=4)


def _na_mixer(proj, cache_kt, cache_vt, rpb_flat):
    pairs = H_C // 2
    n_seq = NA_SEQS_PER_STEP
    col = lambda base: pl.BlockSpec((n_seq * DEC_SEQ, PAIR), lambda hp, b: (b, base + hp))
    cache = pl.BlockSpec((n_seq, 1, 2, DH, PAST_LEN), lambda hp, b: (b, 0, hp, 0, 0))
    return pl.pallas_call(
        _na_kernel,
        grid=(pairs, DEC_BATCH // n_seq),
        in_specs=[pl.BlockSpec(memory_space=pltpu.SMEM), col(0), col(pairs), col(2 * pairs), cache, cache],
        out_specs=pl.BlockSpec((n_seq * DEC_SEQ, PAIR), lambda hp, b: (b, hp)),
        out_shape=jax.ShapeDtypeStruct((LAT_TOKENS, 512), BF16),
        scratch_shapes=[pltpu.VMEM((2, 2, NA_HALF, NA_KEYS), F32)],
        compiler_params=_cparams(2),
        name="na_mixer",
    )(rpb_flat, proj, proj, proj, cache_kt, cache_vt)


def _swa_kernel(q_ref, kv_ref, kct_ref, vct_ref, sink_ref, y_ref):
    n_groups = DEC_SEQ // SWA_QROWS
    ql = lax.broadcasted_iota(jnp.int32, (SWA_QROWS, SWA_BAND), 0)
    kj = lax.broadcasted_iota(jnp.int32, (SWA_QROWS, SWA_BAND), 1)

    def band_bias(first_key_minus_first_query):
        return jnp.where(jnp.abs(kj + first_key_minus_first_query - ql) <= SWA_WINDOW, 0.0, NEG)

    bias_first, bias_mid, bias_last = band_bias(0), band_bias(-SWA_BLOCK), band_bias(-2 * SWA_BLOCK)

    def band_start(grp):
        return min(max(grp * SWA_QROWS - SWA_BLOCK, 0), DEC_SEQ - SWA_BAND)

    memo = _Memo()

    def band_k(g, grp, hh):
        start = band_start(grp)
        kb = kv_ref[start:start + SWA_BAND, g * PAIR:(g + 1) * PAIR]
        return jnp.where(_keep_mask_bf16(SWA_BAND, hh), kb, jnp.zeros_like(kb))

    def band_v(g, grp, hh):
        start = band_start(grp)
        vb = kv_ref[start:start + SWA_BAND, (KV_D + g) * PAIR:(KV_D + g + 1) * PAIR]
        return jnp.where(_keep_mask_bf16(SWA_BAND, hh), vb, jnp.ones_like(vb))

    def scores(g, grp, p, hh):
        bias = bias_first if grp == 0 else (bias_last if grp == n_groups - 1 else bias_mid)
        qp = q_ref[grp * SWA_QROWS:(grp + 1) * SWA_QROWS, p * PAIR:(p + 1) * PAIR]
        s1 = _dot_nt(qp, memo.get_or_build(("k", g, grp, hh), functools.partial(band_k, g, grp, hh))) + bias
        ckt = memo.get_or_build(("ck", g, hh), lambda: _pad_rows(kct_ref[0, 0, g].astype(BF16), hh, 0.0))
        return [s1, _dot(qp, ckt)]

    def finish(out, g, grp, hh, e_parts, term):
        cvt = memo.get_or_build(("cv", g, hh), lambda: _pad_rows(vct_ref[0, 0, g].astype(BF16), hh, 1.0))
        o = (_dot(e_parts[0], memo.get_or_build(("v", g, grp, hh), functools.partial(band_v, g, grp, hh)))
             + _dot_nt(e_parts[1], cvt))
        out.put(hh, o, term)

    heads = []
    for g in range(KV_D):
        for grp in range(n_groups):
            for j in range(GQA // 2):
                p = g * (GQA // 2) + j
                out = _PairSink(y_ref, slice(grp * SWA_QROWS, (grp + 1) * SWA_QROWS), slice(p * PAIR, (p + 1) * PAIR))
                for hh in range(2):
                    sink = sink_ref[2 * p + hh] * LOG2E
                    heads.append(_Head(functools.partial(scores, g, grp, p, hh),
                                       functools.partial(finish, out, g, grp, hh), sink))
    _attend(heads, batch=8)


def _swa_mixer(proj, cache_kt, cache_vt, sink):
    cache = pl.BlockSpec((1, 1, KV_D, DH, PAST_LEN), lambda b: (b, 0, 0, 0, 0))
    return pl.pallas_call(
        _swa_kernel,
        grid=(DEC_BATCH,),
        in_specs=[pl.BlockSpec((DEC_SEQ, 512), lambda b: (b, 3)),
                  pl.BlockSpec((DEC_SEQ, 512), lambda b: (b, 4)),
                  cache, cache,
                  pl.BlockSpec(memory_space=pltpu.SMEM)],
        out_specs=pl.BlockSpec((DEC_SEQ, 512), lambda b: (b, 0)),
        out_shape=jax.ShapeDtypeStruct((LAT_TOKENS, 512), BF16),
        compiler_params=_cparams(1),
        name="swa_mixer",
    )(proj, proj, cache_kt, cache_vt, sink)


def _post_mixer_kernel(*refs, li, n_lat_mix, n_x, n_out):
    mix_ctx_ref, mix_lat_refs = refs[0], refs[1:1 + n_lat_mix]
    refs = refs[1 + n_lat_mix:]
    x_refs = refs[:n_x]
    mod_ref, gpost_ref, gpre_ref, gmlp_ref, wo_ref, w1_ref, w2_ref = refs[n_x:7 + n_x]
    out_refs = refs[7 + n_x:7 + n_x + n_out]
    wo_s, w1_s, w2_s = refs[7 + n_x + n_out:]
    step = pl.program_id(0)

    for k in range(N_PREP):
        @pl.when(step == k)
        def _(k=k):
            if k < N_WO:
                wo_s[k * WO_ROWS:(k + 1) * WO_ROWS, :] = wo_ref[0].astype(BF16)
            elif k < N_WO + N_W1:
                w1_s[(k - N_WO) * W1_ROWS:(k - N_WO + 1) * W1_ROWS, :] = w1_ref[0].astype(BF16)
            else:
                j = k - N_WO - N_W1
                w2_s[j * W2_ROWS:(j + 1) * W2_ROWS, :] = w2_ref[0].astype(BF16)

    @pl.when(step >= N_PREP)
    def _():
        t = step - N_PREP
        is_ctx = t < CTX_TILES
        mod = mod_ref[0, pl.ds(jnp.where(is_ctx, CTX_ROW, (t - CTX_TILES) // (DEC_SEQ // TM)), 1), :]
        g_post, g_pre, g_mlp = gpost_ref[li:li + 1, :], gpre_ref[li:li + 1, :], gmlp_ref[li:li + 1, :]
        rows = [slice(i * (TM // POST_SPLIT), (i + 1) * (TM // POST_SPLIT)) for i in range(POST_SPLIT)]

        def residual(r):
            if n_x == 1:
                return x_refs[0][r, :]
            return jnp.where(is_ctx, x_refs[0][r, :], x_refs[1][r, :])

        def mixed(r):
            lat = [m[r, :] for m in mix_lat_refs]
            lat = lat[0] if n_lat_mix == 1 else jnp.concatenate(lat, axis=1)
            return jnp.where(is_ctx, mix_ctx_ref[r, :], lat)

        ys = [_dot(mixed(r), wo_s[...]) for r in rows]
        x1 = [residual(r) + mod[:, 2 * D_MODEL:3 * D_MODEL] * _rms(y, g_post) for r, y in zip(rows, ys)]
        h = [_norm_mod(x, g_pre, mod, 3 * D_MODEL).astype(BF16) for x in x1]
        acc = [None] * POST_SPLIT
        pending = []

        def second_matmul(c, i, f):
            t2 = _dot(f, w2_s[c * FF_CHUNK:(c + 1) * FF_CHUNK, :])
            acc[i] = t2 if acc[i] is None else acc[i] + t2

        for c in range(D_FF // FF_CHUNK):
            for i in range(POST_SPLIT):
                f = _dot(h[i], w1_s[:, c * FF_CHUNK:(c + 1) * FF_CHUNK])
                pending.append((c, i, jnp.square(jnp.maximum(f, 0.0)).astype(BF16)))
                if len(pending) > 1:
                    second_matmul(*pending.pop(0))
        second_matmul(*pending.pop(0))
        res = [x1[i] + mod[:, 5 * D_MODEL:6 * D_MODEL] * _rms(acc[i], g_mlp) for i in range(POST_SPLIT)]
        if n_out == 1:
            for i, r in enumerate(rows):
                out_refs[0][r, :] = res[i]
        else:
            @pl.when(is_ctx)
            def _():
                for i, r in enumerate(rows):
                    out_refs[0][r, :] = res[i]

            @pl.when(jnp.logical_not(is_ctx))
            def _():
                for i, r in enumerate(rows):
                    out_refs[1][r, :] = res[i]


def _post_mixer(mix_ctx, mix_lat, xs, mod, g_post, g_pre, g_mlp, w_out, w1, w2, li, *, split_out):
    tile = lambda s: jnp.maximum(s - N_PREP, 0)
    ctx_tile = lambda s: jnp.minimum(tile(s), CTX_TILES - 1)
    lat_tile = lambda s: jnp.maximum(tile(s) - CTX_TILES, 0)
    token_spec = lambda index, width=D_MODEL: pl.BlockSpec((TM, width), lambda s: (index(s), 0))
    mix_specs = [token_spec(ctx_tile)] + [token_spec(lat_tile, m.shape[1]) for m in mix_lat]
    x_specs = [token_spec(tile)] if len(xs) == 1 else [token_spec(ctx_tile), token_spec(lat_tile)]
    if split_out:
        out_specs = [token_spec(ctx_tile), token_spec(lat_tile)]
        out_shape = [jax.ShapeDtypeStruct((CTX_TOKENS, D_MODEL), F32), jax.ShapeDtypeStruct((LAT_TOKENS, D_MODEL), F32)]
    else:
        out_specs = [token_spec(tile)]
        out_shape = [jax.ShapeDtypeStruct((ALL_TOKENS, D_MODEL), F32)]
    gain = pl.BlockSpec((DEPTH, D_MODEL), lambda s: (0, 0))
    return pl.pallas_call(
        functools.partial(_post_mixer_kernel, li=li, n_lat_mix=len(mix_lat), n_x=len(xs), n_out=len(out_specs)),
        grid=(N_PREP + ALL_TOKENS // TM,),
        in_specs=mix_specs + x_specs + [
            pl.BlockSpec((1, MOD_ROWS, 6 * D_MODEL), lambda s: (li, 0, 0)),
            gain, gain, gain,
            pl.BlockSpec((1, WO_ROWS, D_MODEL), lambda s: (li, jnp.clip(s, 0, N_WO - 1), 0)),
            pl.BlockSpec((1, W1_ROWS, D_FF), lambda s: (li, jnp.clip(s - N_WO, 0, N_W1 - 1), 0)),
            pl.BlockSpec((1, W2_ROWS, D_MODEL), lambda s: (li, jnp.clip(s - N_WO - N_W1, 0, N_W2 - 1), 0))],
        out_specs=out_specs,
        out_shape=out_shape,
        scratch_shapes=[pltpu.VMEM((D_MODEL, D_MODEL), BF16), pltpu.VMEM((D_MODEL, D_FF), BF16),
                        pltpu.VMEM((D_FF, D_MODEL), BF16)],
        compiler_params=_cparams(1),
        name="post_mixer",
    )(mix_ctx, *mix_lat, *xs, mod, g_post, g_pre, g_mlp, w_out, w1, w2)


def _rope_tables():
    t = jnp.arange(DEC_SEQ)
    rows = (t // GRID_W).astype(F32)
    cols = (t % GRID_W).astype(F32)
    q4 = DH // 4
    inv = 1.0 / (ROPE_BASE ** (jnp.arange(q4, dtype=F32) / q4))
    ar, ac = rows[:, None] * inv, cols[:, None] * inv
    cos64 = jnp.concatenate([jnp.cos(ar), jnp.cos(ar), jnp.cos(ac), jnp.cos(ac)], axis=1)
    sin64 = jnp.concatenate([-jnp.sin(ar), jnp.sin(ar), -jnp.sin(ac), jnp.sin(ac)], axis=1)
    return jnp.tile(cos64, (1, 2)), jnp.tile(sin64, (1, 2))


def kernel(x_prompt, x_sample, cache_diff_k, cache_diff_v, cache_na_k, cache_na_v, cache_swa_k, cache_swa_v, c, c_ctx, mod_w, mod_b, norm_mix_pre, norm_mix_post, norm_mlp_pre, norm_mlp_post, w_in_even, conv_w, lambda_q1, lambda_k1, lambda_q2, lambda_k2, subln, w_in_odd, rpb, sink, w_out, mlp_w1, mlp_w2):
    cond = jnp.concatenate([c, c_ctx[None, :], jnp.zeros((MOD_ROWS - DEC_BATCH - 1, D_MODEL), F32)], axis=0)
    mod = _modulation(cond, mod_w, mod_b)
    rope = _rope_tables()

    t_minor = lambda a: jnp.swapaxes(a, -1, -2)

    w_in = [w_in_even.astype(BF16), w_in_odd.astype(BF16)]

    xp = x_prompt.reshape(CTX_TOKENS, D_MODEL)
    xs = x_sample.reshape(LAT_TOKENS, D_MODEL)
    lam_init0 = 0.8 - 0.6 * math.exp(-0.3 * 0)
    lam_args = (conv_w, lambda_q1, lambda_k1, lambda_q2, lambda_k2, subln)
    sink_flat = sink.reshape(-1)

    def post(li, mix_ctx, mix_lat, x_parts, split_out):
        return _post_mixer(mix_ctx, mix_lat, x_parts, mod, norm_mix_post, norm_mlp_pre, norm_mlp_post, w_out, mlp_w1,
                           mlp_w2, li, split_out=split_out)

    proj, new_diff_kt, new_diff_v = _pre_mixer(xp, mod, norm_mix_pre, w_in[0], 0, latent=False, even=True, rope=None,
                                               n_tok=CTX_TOKENS)
    mix_ctx = _even_mixer(proj, *lam_args, new_diff_kt, new_diff_v, seq=SEQ, n_seq=CTX_SEQS_PER_STEP // 2,
                          own_keys=False, lam_init=lam_init0)
    proj = _pre_mixer(xs, mod, norm_mix_pre, w_in[0], 0, latent=True, even=True, rope=rope, n_tok=LAT_TOKENS)
    mix_lat = _even_mixer(proj, *lam_args, t_minor(cache_diff_k), cache_diff_v, seq=DEC_SEQ, n_seq=1, own_keys=True,
                          lam_init=lam_init0)
    (x_all,) = post(0, mix_ctx, [mix_lat], [xp, xs], False)

    q, new_na_kt, new_na_vt, new_swa_kt, new_swa_vt = _pre_mixer(
        x_all, mod, norm_mix_pre, w_in[1], 1, latent=False, even=False, rope=None, n_tok=CTX_TOKENS)
    mix_ctx = _odd_ctx_mixer(q, new_na_kt, new_na_vt, new_swa_kt, new_swa_vt, sink_flat)
    proj = _pre_mixer(x_all, mod, norm_mix_pre, w_in[1], 1, latent=True, even=False, rope=rope, n_tok=LAT_TOKENS,
                      tile0=CTX_TILES)
    mix_c = _na_mixer(proj, t_minor(cache_na_k), t_minor(cache_na_v), rpb.reshape(-1))
    mix_d = _swa_mixer(proj, t_minor(cache_swa_k), t_minor(cache_swa_v), sink_flat)
    xp, xs = post(1, mix_ctx, [mix_c, mix_d], [x_all], True)

    return (xp.reshape(BATCH, SEQ, D_MODEL), xs.reshape(DEC_BATCH, DEC_SEQ, D_MODEL),
            t_minor(new_diff_kt), new_diff_v, t_minor(new_na_kt), t_minor(new_na_vt),
            t_minor(new_swa_kt), t_minor(new_swa_vt))
```

```python
import functools
import math

import jax
import jax.numpy as jnp
from jax import lax
from jax.experimental import pallas as pl
from jax.experimental.pallas import tpu as pltpu

F32 = jnp.float32
BF16 = jnp.bfloat16

D_MODEL = 1024
BATCH = 32
SEQ = 256
DEPTH = 2
DEC_BATCH = 8
DEC_SEQ = 1024
PAST_LEN = 256
GRID_W = 64
GRID_H = DEC_SEQ // GRID_W
MIX_A = 512
H_B = 4
DH = 64
PAIR = 2 * DH
H_C = 8
H_D = 8
KV_D = 2
GQA = H_D // KV_D
IN_EVEN = 3072
IN_ODD = 2304
D_FF = 4096
NA_WIN_R = 8
NA_WIN_C = 16
SWA_BLOCK = 128
SWA_WINDOW = 128
ROPE_BASE = 10000.0
EPS = 1e-6
NEG = -1e30
LOG2E = math.log2(math.e)
Q_SCALE = DH ** -0.5 * LOG2E

MOD_ROWS = 16
CTX_ROW = DEC_BATCH
V7X_VMEM_BYTES = 64 * 1024 * 1024
VMEM_LIMIT = V7X_VMEM_BYTES - 12 * 1024 * 1024

CTX_TOKENS = BATCH * SEQ
LAT_TOKENS = DEC_BATCH * DEC_SEQ
ALL_TOKENS = CTX_TOKENS + LAT_TOKENS

TM = 512
FF_CHUNK = 1024
POST_SPLIT = 2
WCHUNK_ELEMS = 256 * 1024
WO_ROWS, W1_ROWS, W2_ROWS = WCHUNK_ELEMS // D_MODEL, WCHUNK_ELEMS // D_FF, WCHUNK_ELEMS // D_MODEL
N_WO, N_W1, N_W2 = D_MODEL // WO_ROWS, D_MODEL // W1_ROWS, D_FF // W2_ROWS
STAGE_SLOTS = 2
CTX_TILES = CTX_TOKENS // TM
CTX_SEQS_PER_STEP = 4
MOD_TN = 1536
NA_SEQS_PER_STEP = 2
NA_HALF = 512
NA_KEYS = 768
SWA_QROWS = 2 * SWA_BLOCK
SWA_BAND = 4 * SWA_BLOCK
LAT_ODD_COLS = IN_ODD + 2 * PAIR


def _cparams(n_axes):
    return pltpu.CompilerParams(dimension_semantics=("arbitrary",) * n_axes, vmem_limit_bytes=VMEM_LIMIT)


def _rms(x, g):
    return x * lax.rsqrt(jnp.mean(x * x, axis=-1, keepdims=True) + EPS) * g


def _dot(a, b):
    return jnp.dot(a, b, preferred_element_type=F32)


def _dot_nt(a, b):
    return lax.dot_general(a, b, (((1,), (1,)), ((), ())), preferred_element_type=F32)


def _layer_block(shape, li):
    zeros = (0,) * len(shape)
    return pl.BlockSpec((1,) + tuple(shape), lambda *_: (li,) + zeros, pipeline_mode=pl.Buffered(1))


def _low_half(shape):
    return (lax.broadcasted_iota(jnp.int32, shape, len(shape) - 1) & DH) == 0


def _keep_mask_bf16(rows, hh):
    lane = lax.broadcasted_iota(jnp.int32, (rows, PAIR), 1)
    half = (lane & DH).astype(F32).astype(BF16)
    return (half == 0) if hh == 0 else (half != 0)


def _pad_rows(t, hh, fill):
    other = jnp.full(t.shape, fill, t.dtype)
    return jnp.concatenate([t, other] if hh == 0 else [other, t], axis=0)


class _Memo(dict):
    def get_or_build(self, key, build):
        if key not in self:
            self[key] = build()
        return self[key]


def _merge_pair(o0, o1, extra_den=None):
    low = _low_half(o0.shape)
    num = jnp.where(low, o0, o1)
    den = pltpu.roll(jnp.where(low, o1, o0), DH, axis=1)
    if extra_den is not None:
        den = den + jnp.where(low, extra_den[0], extra_den[1])
    return num * (1.0 / den)


class _Head:
    def __init__(self, scores, finish, sink=None):
        self.scores, self.finish, self.sink = scores, finish, sink


def _attend(heads, batch):
    batches = [heads[i:i + batch] for i in range(0, len(heads), batch)]
    cur = [hd.scores() for hd in batches[0]]
    for bi, group in enumerate(batches):
        nxt = [hd.scores() for hd in batches[bi + 1]] if bi + 1 < len(batches) else None
        exps = []
        for hd, parts in zip(group, cur):
            chunks = [s[:, j * 128:(j + 1) * 128] for s in parts for j in range(s.shape[1] // 128)]
            m_el = functools.reduce(jnp.maximum, chunks)
            if hd.sink is not None:
                m_el = jnp.maximum(m_el, hd.sink)
            mx = jnp.broadcast_to(jnp.max(m_el, axis=-1, keepdims=True), m_el.shape)
            term = None if hd.sink is None else jnp.exp2(hd.sink - mx)
            e_parts = [jnp.concatenate([jnp.exp2(s[:, j * 128:(j + 1) * 128] - mx)
                                        for j in range(s.shape[1] // 128)], axis=1).astype(BF16) for s in parts]
            exps.append((e_parts, term))
        for hd, (e_parts, term) in zip(group, exps):
            hd.finish(e_parts, term)
        cur = nxt


class _PairSink:
    def __init__(self, y_ref, rows, cols):
        self.y_ref, self.rows, self.cols, self.first = y_ref, rows, cols, None

    def put(self, hh, o, term):
        if hh == 0:
            self.first = (o, term)
        else:
            o0, term0 = self.first
            extra = None if term is None else (term0, term)
            self.y_ref[self.rows, self.cols] = _merge_pair(o0, o, extra).astype(BF16)


def _mod_kernel(cond_ref, w_ref, b_ref, o_ref):
    c = cond_ref[...]
    s = c * (1.0 / (1.0 + jnp.exp(-c)))
    o_ref[0] = _dot(s.astype(BF16), w_ref[0].astype(BF16)) + b_ref[pl.ds(pl.program_id(0), 1), :]


def _modulation(cond, mod_w, mod_b):
    return pl.pallas_call(
        _mod_kernel,
        grid=(DEPTH, 6 * D_MODEL // MOD_TN),
        in_specs=[
            pl.BlockSpec((MOD_ROWS, D_MODEL), lambda l, j: (0, 0)),
            pl.BlockSpec((1, D_MODEL, MOD_TN), lambda l, j: (l, 0, j)),
            pl.BlockSpec((DEPTH, MOD_TN), lambda l, j: (0, j)),
        ],
        out_specs=pl.BlockSpec((1, MOD_ROWS, MOD_TN), lambda l, j: (l, 0, j)),
        out_shape=jax.ShapeDtypeStruct((DEPTH, MOD_ROWS, 6 * D_MODEL), F32),
        compiler_params=_cparams(2),
        name="adaln_modulation",
    )(cond, mod_w, mod_b)


def _norm_mod(x, g, mod, shift_col):
    sh = mod[:, shift_col:shift_col + D_MODEL]
    sc = mod[:, shift_col + D_MODEL:shift_col + 2 * D_MODEL]
    return _rms(x, g) * (1.0 + sc) + sh


def _rope(z, cos, sin):
    outs = []
    for j in range(z.shape[1] // 128):
        zj = z[:, j * 128:(j + 1) * 128]
        lane = lax.broadcasted_iota(jnp.int32, zj.shape, 1)
        partner = jnp.where((lane & 16) == 0, pltpu.roll(zj, 128 - 16, axis=1), pltpu.roll(zj, 16, axis=1))
        outs.append(zj * cos + partner * sin)
    return outs[0] if len(outs) == 1 else jnp.concatenate(outs, axis=1)


def _store_heads_transposed(dst, bi, pc_rows, n_pairs):
    for p in range(n_pairs):
        t = pc_rows[:, p * PAIR:(p + 1) * PAIR].T
        dst[bi, 0, 2 * p] = t[0:DH]
        dst[bi, 0, 2 * p + 1] = t[DH:PAIR]


def _mod_row(mod_ref, latent):
    if latent:
        return mod_ref[0, pl.ds(pl.program_id(0) // (DEC_SEQ // TM), 1), :]
    return mod_ref[0, CTX_ROW:CTX_ROW + 1, :]


def _normed_groups(x_ref, mod_ref, g_ref, li, latent):
    rows = [slice(i * SEQ, (i + 1) * SEQ) for i in range(TM // SEQ)]
    mod, g = _mod_row(mod_ref, latent), g_ref[li:li + 1, :]
    return rows, [_norm_mod(x_ref[r, :], g, mod, 0).astype(BF16) for r in rows]


def _k1_ctx_even(x_ref, mod_ref, g_ref, w_ref, proj_ref, dkt_ref, dv_ref, *, li):
    rows, hs = _normed_groups(x_ref, mod_ref, g_ref, li, False)
    for c in range(IN_EVEN // 512):
        for bi, (r, h) in enumerate(zip(rows, hs)):
            pc = _dot(h, w_ref[0, :, c * 512:(c + 1) * 512])
            if c < 3:
                proj_ref[r, c * 512:(c + 1) * 512] = pc.astype(BF16)
            elif c == 3:
                proj_ref[r, c * 512:(c + 1) * 512] = (pc * Q_SCALE).astype(BF16)
            elif c == 4:
                for hh in range(H_B):
                    t = pc[:, hh * PAIR:(hh + 1) * PAIR].T
                    dkt_ref[bi, 0, hh, 0] = t[0:DH]
                    dkt_ref[bi, 0, hh, 1] = t[DH:PAIR]
            else:
                for hh in range(H_B):
                    dv_ref[bi, 0, hh] = pc[:, hh * PAIR:(hh + 1) * PAIR]


def _k1_lat_even(x_ref, mod_ref, g_ref, w_ref, cos_ref, sin_ref, proj_ref, *, li):
    rows, hs = _normed_groups(x_ref, mod_ref, g_ref, li, True)
    for c in range(IN_EVEN // 512):
        for r, h in zip(rows, hs):
            pc = _dot(h, w_ref[0, :, c * 512:(c + 1) * 512])
            if c == 3:
                pc = _rope(pc * Q_SCALE, cos_ref[r, :], sin_ref[r, :])
            elif c == 4:
                pc = _rope(pc, cos_ref[r, :], sin_ref[r, :])
            proj_ref[r, c * 512:(c + 1) * 512] = pc.astype(BF16)


def _k1_ctx_odd(x_ref, mod_ref, g_ref, w_ref, q_ref, nkt_ref, nvt_ref, skt_ref, svt_ref, *, li):
    rows, hs = _normed_groups(x_ref, mod_ref, g_ref, li, False)
    for c in range(4):
        for bi, (r, h) in enumerate(zip(rows, hs)):
            pc = _dot(h, w_ref[0, :, c * 512:(c + 1) * 512])
            if c == 0 or c == 3:
                q_ref[r, (c // 3) * 512:(c // 3 + 1) * 512] = (pc * Q_SCALE).astype(BF16)
            else:
                _store_heads_transposed(nkt_ref if c == 1 else nvt_ref, bi, pc, H_C // 2)
    for bi, h in enumerate(hs):
        pc = _dot(h, w_ref[0, :, 2048:IN_ODD])
        _store_heads_transposed(skt_ref, bi, pc[:, 0:PAIR], 1)
        _store_heads_transposed(svt_ref, bi, pc[:, PAIR:2 * PAIR], 1)


def _k1_lat_odd(x_ref, mod_ref, g_ref, w_ref, cos_ref, sin_ref, proj_ref, *, li):
    rows, hs = _normed_groups(x_ref, mod_ref, g_ref, li, True)
    for c in range(4):
        for r, h in zip(rows, hs):
            pc = _dot(h, w_ref[0, :, c * 512:(c + 1) * 512])
            if c == 0:
                pc = pc * Q_SCALE
            elif c == 3:
                pc = _rope(pc * Q_SCALE, cos_ref[r, :], sin_ref[r, :])
            proj_ref[r, c * 512:(c + 1) * 512] = pc.astype(BF16)
    low = _low_half((SEQ, PAIR))
    for r, h in zip(rows, hs):
        pc = _dot(h, w_ref[0, :, 2048:IN_ODD])
        for j, z in enumerate((_rope(pc[:, 0:PAIR], cos_ref[r, :], sin_ref[r, :]), pc[:, PAIR:2 * PAIR])):
            zr = pltpu.roll(z, DH, axis=1)
            base = 2048 + j * 2 * PAIR
            proj_ref[r, base:base + PAIR] = jnp.where(low, z, zr).astype(BF16)
            proj_ref[r, base + PAIR:base + 2 * PAIR] = jnp.where(low, zr, z).astype(BF16)


def _pre_mixer(x, mod, gains, w, li, *, latent, even, rope, n_tok, tile0=0):
    n_in = w.shape[2]
    tiles_per_seq = DEC_SEQ // TM
    in_specs = [
        pl.BlockSpec((TM, D_MODEL), lambda i: (i + tile0, 0)),
        pl.BlockSpec((1, MOD_ROWS, 6 * D_MODEL), lambda i: (li, 0, 0)),
        pl.BlockSpec((DEPTH, D_MODEL), lambda i: (0, 0)),
        _layer_block((D_MODEL, n_in), 0),
    ]
    args = [x, mod, gains, w]
    nb = TM // SEQ
    if latent:
        in_specs += [pl.BlockSpec((TM, 128), lambda i: (i % tiles_per_seq, 0))] * 2
        args += list(rope)
        body = _k1_lat_even if even else _k1_lat_odd
        n_out = IN_EVEN if even else LAT_ODD_COLS
        out_specs = pl.BlockSpec((TM, n_out), lambda i: (i, 0))
        out_shape = jax.ShapeDtypeStruct((n_tok, n_out), BF16)
    elif even:
        body = _k1_ctx_even
        out_specs = [pl.BlockSpec((TM, 4 * MIX_A), lambda i: (i, 0)),
                     pl.BlockSpec((nb, 1, H_B, 2, DH, SEQ), lambda i: (i, 0, 0, 0, 0, 0)),
                     pl.BlockSpec((nb, 1, H_B, SEQ, PAIR), lambda i: (i, 0, 0, 0, 0))]
        out_shape = [jax.ShapeDtypeStruct((n_tok, 4 * MIX_A), BF16),
                     jax.ShapeDtypeStruct((BATCH, 1, H_B, 2, DH, SEQ), F32),
                     jax.ShapeDtypeStruct((BATCH, 1, H_B, SEQ, PAIR), F32)]
    else:
        body = _k1_ctx_odd
        c_spec = pl.BlockSpec((nb, 1, H_C, DH, SEQ), lambda i: (i, 0, 0, 0, 0))
        d_spec = pl.BlockSpec((nb, 1, KV_D, DH, SEQ), lambda i: (i, 0, 0, 0, 0))
        c_shape = jax.ShapeDtypeStruct((BATCH, 1, H_C, DH, SEQ), F32)
        d_shape = jax.ShapeDtypeStruct((BATCH, 1, KV_D, DH, SEQ), F32)
        out_specs = [pl.BlockSpec((TM, D_MODEL), lambda i: (i, 0)), c_spec, c_spec, d_spec, d_spec]
        out_shape = [jax.ShapeDtypeStruct((n_tok, D_MODEL), BF16), c_shape, c_shape, d_shape, d_shape]
    return pl.pallas_call(
        functools.partial(body, li=li),
        grid=(n_tok // TM,),
        in_specs=in_specs,
        out_specs=out_specs,
        out_shape=out_shape,
        compiler_params=_cparams(1),
        name=f"pre_mixer_{'lat' if latent else 'ctx'}_{'even' if even else 'odd'}",
    )(*args)


def _even_mixer_kernel(*refs, seq, n_seq, tq, own_keys, lam_init):
    proj_ref, cw_ref, lq1_ref, lk1_ref, lq2_ref, lk2_ref, subln_ref, ckt_ref, cv_ref = refs[:9]
    y_ref = refs[-1]
    n_rows = n_seq * seq

    pos = lax.broadcasted_iota(jnp.int32, (n_rows, 128), 0) % seq
    for j in range(MIX_A // 128):
        cols = slice(j * 128, (j + 1) * 128)
        a_b = proj_ref[:, j * 128:(j + 1) * 128].astype(F32)
        u = (proj_ref[:, MIX_A + j * 128:MIX_A + (j + 1) * 128].astype(F32)
             * proj_ref[:, 2 * MIX_A + j * 128:2 * MIX_A + (j + 1) * 128].astype(F32))
        u_prev = jnp.where(pos == 0, 0.0, pltpu.roll(u, 1, axis=0))
        u_next = jnp.where(pos == seq - 1, 0.0, pltpu.roll(u, n_rows - 1, axis=0))
        w = cw_ref[0, :, cols]
        y_ref[:, cols] = (a_b * (w[0:1] * u_prev + w[1:2] * u + w[2:3] * u_next)).astype(BF16)

    lam = (jnp.exp(jnp.sum(lq1_ref[...] * lk1_ref[...], axis=-1, keepdims=True))
           - jnp.exp(jnp.sum(lq2_ref[...] * lk2_ref[...], axis=-1, keepdims=True)) + lam_init)
    subln = subln_ref[...]
    q_col, k_col, v_col = 3 * MIX_A, 3 * MIX_A + 512, 3 * MIX_A + 1024
    ones = jnp.ones((PAST_LEN, PAIR), BF16)

    memo = _Memo()

    def own_k(b, h, m):
        kp = proj_ref[b * seq:(b + 1) * seq, k_col + h * PAIR:k_col + (h + 1) * PAIR]
        return jnp.where(_keep_mask_bf16(seq, m), kp, jnp.zeros_like(kp))

    def value_ops(b, h):
        v_ops = []
        if own_keys:
            v_own = proj_ref[b * seq:(b + 1) * seq, v_col + h * PAIR:v_col + (h + 1) * PAIR]
            v_ops.append(jnp.concatenate([v_own, jnp.ones((seq, PAIR), BF16)], axis=1))
        v_ops.append(jnp.concatenate([cv_ref[b, 0, h].astype(BF16), ones], axis=1))
        return v_ops

    def scores(b, qrows, h, m):
        qp = proj_ref[qrows, q_col + h * PAIR:q_col + (h + 1) * PAIR]
        parts = []
        if own_keys:
            parts.append(_dot_nt(qp, memo.get_or_build(("k", b, h, m), functools.partial(own_k, b, h, m))))
        ckt = memo.get_or_build(("ck", b, h, m), lambda: _pad_rows(ckt_ref[b, 0, h, m].astype(BF16), m, 0.0))
        parts.append(_dot(qp, ckt))
        return parts

    first_map = {}

    def pv(b, qrows, h, m, e_parts, _):
        v_ops = memo.get_or_build(("v", b, h), functools.partial(value_ops, b, h))
        o = None
        for e, v_op in zip(e_parts, v_ops):
            t = _dot(e, v_op)
            o = t if o is None else o + t
        attn = o[:, 0:PAIR] * (1.0 / o[:, PAIR:2 * PAIR])
        if m == 0:
            first_map[(qrows.start, h)] = attn
        else:
            y = _rms(first_map.pop((qrows.start, h)) - lam * attn, subln) * (1.0 - lam_init)
            y_ref[qrows, MIX_A + h * PAIR:MIX_A + (h + 1) * PAIR] = y.astype(BF16)

    heads = []
    for b in range(n_seq):
        for i in range(seq // tq):
            qrows = slice(b * seq + i * tq, b * seq + (i + 1) * tq)
            for h in range(H_B):
                for m in range(2):
                    heads.append(_Head(functools.partial(scores, b, qrows, h, m),
                                       functools.partial(pv, b, qrows, h, m)))
    _attend(heads, batch=8)


def _even_mixer(proj, conv_w, lq1, lk1, lq2, lk2, subln, kt, v, *, seq, n_seq, own_keys, lam_init):
    n_tok = proj.shape[0]
    rows = n_seq * seq
    small = lambda a: pl.BlockSpec((1, a.shape[1]), lambda b: (0, 0))
    in_specs = [pl.BlockSpec((rows, proj.shape[1]), lambda b: (b, 0)),
                pl.BlockSpec((1, 3, MIX_A), lambda b: (0, 0, 0)),
                small(lq1), small(lk1), small(lq2), small(lk2), small(subln),
                pl.BlockSpec((n_seq, 1, H_B, 2, DH, PAST_LEN), lambda b: (b, 0, 0, 0, 0, 0)),
                pl.BlockSpec((n_seq, 1, H_B, PAST_LEN, PAIR), lambda b: (b, 0, 0, 0, 0))]
    return pl.pallas_call(
        functools.partial(_even_mixer_kernel, seq=seq, n_seq=n_seq, tq=min(seq, 256), own_keys=own_keys,
                          lam_init=lam_init),
        grid=(n_tok // rows,),
        in_specs=in_specs,
        out_specs=pl.BlockSpec((rows, D_MODEL), lambda b: (b, 0)),
        out_shape=jax.ShapeDtypeStruct((n_tok, D_MODEL), BF16),
        compiler_params=_cparams(1),
        name=f"even_mixer_{'lat' if own_keys else 'ctx'}",
    )(proj, conv_w, lq1, lk1, lq2, lk2, subln, kt, v)


def _odd_ctx_kernel(q_ref, nkt_ref, nvt_ref, skt_ref, svt_ref, sink_ref, y_ref):
    def kv_refs(b, head):
        if head < H_C:
            return nkt_ref.at[b, 0, head], nvt_ref.at[b, 0, head]
        g = (head - H_C) // GQA
        return skt_ref.at[b, 0, g], svt_ref.at[b, 0, g]

    memo = _Memo()

    def kv_key(head):
        return head if head < H_C else H_C + (head - H_C) // GQA

    def scores(b, p, hh):
        head = 2 * p + hh
        kt = memo.get_or_build(("k", b, kv_key(head), hh),
                               lambda: _pad_rows(kv_refs(b, head)[0][...].astype(BF16), hh, 0.0))
        return [_dot(q_ref[b * SEQ:(b + 1) * SEQ, p * PAIR:(p + 1) * PAIR], kt)]

    def finish(out, b, p, hh, e_parts, term):
        head = 2 * p + hh
        vt = memo.get_or_build(("v", b, kv_key(head), hh),
                               lambda: _pad_rows(kv_refs(b, head)[1][...].astype(BF16), hh, 1.0))
        out.put(hh, _dot_nt(e_parts[0], vt), term)

    heads = []
    for b in range(CTX_SEQS_PER_STEP):
        for p in range((H_C + H_D) // 2):
            out = _PairSink(y_ref, slice(b * SEQ, (b + 1) * SEQ), slice(p * PAIR, (p + 1) * PAIR))
            for hh in range(2):
                d_head = 2 * p + hh - H_C
                sink = sink_ref[d_head] * LOG2E if d_head >= 0 else None
                heads.append(_Head(functools.partial(scores, b, p, hh), functools.partial(finish, out, b, p, hh), sink))
    _attend(heads, batch=8)


def _odd_ctx_mixer(q, nkt, nvt, skt, svt, sink):
    n_tok = q.shape[0]
    n_seq = CTX_SEQS_PER_STEP
    c_spec = pl.BlockSpec((n_seq, 1, H_C, DH, SEQ), lambda b: (b, 0, 0, 0, 0))
    d_spec = pl.BlockSpec((n_seq, 1, KV_D, DH, SEQ), lambda b: (b, 0, 0, 0, 0))
    return pl.pallas_call(
        _odd_ctx_kernel,
        grid=(n_tok // (n_seq * SEQ),),
        in_specs=[pl.BlockSpec((n_seq * SEQ, D_MODEL), lambda b: (b, 0)), c_spec, c_spec, d_spec, d_spec,
                  pl.BlockSpec(memory_space=pltpu.SMEM)],
        out_specs=pl.BlockSpec((n_seq * SEQ, D_MODEL), lambda b: (b, 0)),
        out_shape=jax.ShapeDtypeStruct((n_tok, D_MODEL), BF16),
        compiler_params=_cparams(1),
        name="odd_mixer_ctx",
    )(q, nkt, nvt, skt, svt, sink)


def _build_na_bias(rpb_ref, head, nb_ref, slot):
    qc = lax.broadcasted_iota(jnp.int32, (GRID_W, GRID_W), 0)
    kc = lax.broadcasted_iota(jnp.int32, (GRID_W, GRID_W), 1)
    col_start = jnp.clip(qc - NA_WIN_C // 2, 0, GRID_W - NA_WIN_C)
    col_ok = (kc >= col_start) & (kc < col_start + NA_WIN_C)
    dc = kc - qc + NA_WIN_C - 1
    n_dr, n_dc = 2 * NA_WIN_R - 1, 2 * NA_WIN_C - 1
    neg = jnp.full((GRID_W, GRID_W), NEG, F32)
    toeplitz = []
    for dr in range(n_dr):
        t = neg
        for d in range(n_dc):
            t = jnp.where(dc == d, rpb_ref[(head * n_dr + dr) * n_dc + d] * LOG2E, t)
        toeplitz.append(jnp.where(col_ok, t, NEG))
    rows_per_half = NA_HALF // GRID_W
    key_rows = NA_KEYS // GRID_W
    for half in range(2):
        for rl in range(rows_per_half):
            r = half * rows_per_half + rl
            r_start = min(max(r - NA_WIN_R // 2, 0), GRID_H - NA_WIN_R)
            blocks = []
            for j in range(key_rows):
                rk = half * (GRID_H - key_rows) + j
                inside = r_start <= rk < r_start + NA_WIN_R
                blocks.append(toeplitz[rk - r + NA_WIN_R - 1] if inside else neg)
            nb_ref[slot, half, rl * GRID_W:(rl + 1) * GRID_W, :] = jnp.concatenate(blocks, axis=1)


def _na_kernel(rpb_ref, q_ref, k_ref, v_ref, kct_ref, vct_ref, y_ref, nb_ref):
    @pl.when(pl.program_id(1) == 0)
    def _():
        for hh in range(2):
            _build_na_bias(rpb_ref, 2 * pl.program_id(0) + hh, nb_ref, hh)

    def scores(b, half, hh):
        w0 = b * DEC_SEQ + half * (DEC_SEQ - NA_KEYS)
        qp = q_ref[b * DEC_SEQ + half * NA_HALF:b * DEC_SEQ + (half + 1) * NA_HALF, :]
        kw = k_ref[w0:w0 + NA_KEYS, :]
        s1 = _dot_nt(qp, jnp.where(_keep_mask_bf16(NA_KEYS, hh), kw, jnp.zeros_like(kw))) + nb_ref[hh, half]
        s2 = _dot(qp, _pad_rows(kct_ref[b, 0, hh].astype(BF16), hh, 0.0))
        return [s1, s2]

    def finish(out, b, half, hh, e_parts, term):
        w0 = b * DEC_SEQ + half * (DEC_SEQ - NA_KEYS)
        vw = v_ref[w0:w0 + NA_KEYS, :]
        o = (_dot(e_parts[0], jnp.where(_keep_mask_bf16(NA_KEYS, hh), vw, jnp.ones_like(vw)))
             + _dot_nt(e_parts[1], _pad_rows(vct_ref[b, 0, hh].astype(BF16), hh, 1.0)))
        out.put(hh, o, term)

    heads = []
    for b in range(NA_SEQS_PER_STEP):
        for half in range(2):
            rows = slice(b * DEC_SEQ + half * NA_HALF, b * DEC_SEQ + (half + 1) * NA_HALF)
            out = _PairSink(y_ref, rows, slice(0, PAIR))
            for hh in range(2):
                heads.append(_Head(functools.partial(scores, b, half, hh), functools.partial(finish, out, b, half, hh)))
    _attend(heads, batch=4)


def _na_mixer(proj, cache_kt, cache_vt, rpb_flat):
    pairs = H_C // 2
    n_seq = NA_SEQS_PER_STEP
    col = lambda base: pl.BlockSpec((n_seq * DEC_SEQ, PAIR), lambda hp, b: (b, base + hp))
    cache = pl.BlockSpec((n_seq, 1, 2, DH, PAST_LEN), lambda hp, b: (b, 0, hp, 0, 0))
    return pl.pallas_call(
        _na_kernel,
        grid=(pairs, DEC_BATCH // n_seq),
        in_specs=[pl.BlockSpec(memory_space=pltpu.SMEM), col(0), col(pairs), col(2 * pairs), cache, cache],
        out_specs=pl.BlockSpec((n_seq * DEC_SEQ, PAIR), lambda hp, b: (b, hp)),
        out_shape=jax.ShapeDtypeStruct((LAT_TOKENS, 512), BF16),
        scratch_shapes=[pltpu.VMEM((2, 2, NA_HALF, NA_KEYS), F32)],
        compiler_params=_cparams(2),
        name="na_mixer",
    )(rpb_flat, proj, proj, proj, cache_kt, cache_vt)


def _swa_kernel(q_ref, kv_ref, kct_ref, vct_ref, sink_ref, y_ref):
    n_groups = DEC_SEQ // SWA_QROWS
    ql = lax.broadcasted_iota(jnp.int32, (SWA_QROWS, SWA_BAND), 0)
    kj = lax.broadcasted_iota(jnp.int32, (SWA_QROWS, SWA_BAND), 1)

    def band_bias(first_key_minus_first_query):
        return jnp.where(jnp.abs(kj + first_key_minus_first_query - ql) <= SWA_WINDOW, 0.0, NEG)

    bias_first, bias_mid, bias_last = band_bias(0), band_bias(-SWA_BLOCK), band_bias(-2 * SWA_BLOCK)

    def band_start(grp):
        return min(max(grp * SWA_QROWS - SWA_BLOCK, 0), DEC_SEQ - SWA_BAND)

    memo = _Memo()

    def band_k(g, grp, hh):
        start = band_start(grp)
        kb = kv_ref[start:start + SWA_BAND, g * PAIR:(g + 1) * PAIR]
        return jnp.where(_keep_mask_bf16(SWA_BAND, hh), kb, jnp.zeros_like(kb))

    def band_v(g, grp, hh):
        start = band_start(grp)
        vb = kv_ref[start:start + SWA_BAND, (KV_D + g) * PAIR:(KV_D + g + 1) * PAIR]
        return jnp.where(_keep_mask_bf16(SWA_BAND, hh), vb, jnp.ones_like(vb))

    def scores(g, grp, p, hh):
        bias = bias_first if grp == 0 else (bias_last if grp == n_groups - 1 else bias_mid)
        qp = q_ref[grp * SWA_QROWS:(grp + 1) * SWA_QROWS, p * PAIR:(p + 1) * PAIR]
        s1 = _dot_nt(qp, memo.get_or_build(("k", g, grp, hh), functools.partial(band_k, g, grp, hh))) + bias
        ckt = memo.get_or_build(("ck", g, hh), lambda: _pad_rows(kct_ref[0, 0, g].astype(BF16), hh, 0.0))
        return [s1, _dot(qp, ckt)]

    def finish(out, g, grp, hh, e_parts, term):
        cvt = memo.get_or_build(("cv", g, hh), lambda: _pad_rows(vct_ref[0, 0, g].astype(BF16), hh, 1.0))
        o = (_dot(e_parts[0], memo.get_or_build(("v", g, grp, hh), functools.partial(band_v, g, grp, hh)))
             + _dot_nt(e_parts[1], cvt))
        out.put(hh, o, term)

    heads = []
    for g in range(KV_D):
        for grp in range(n_groups):
            for j in range(GQA // 2):
                p = g * (GQA // 2) + j
                out = _PairSink(y_ref, slice(grp * SWA_QROWS, (grp + 1) * SWA_QROWS), slice(p * PAIR, (p + 1) * PAIR))
                for hh in range(2):
                    sink = sink_ref[2 * p + hh] * LOG2E
                    heads.append(_Head(functools.partial(scores, g, grp, p, hh),
                                       functools.partial(finish, out, g, grp, hh), sink))
    _attend(heads, batch=8)


def _swa_mixer(proj, cache_kt, cache_vt, sink):
    cache = pl.BlockSpec((1, 1, KV_D, DH, PAST_LEN), lambda b: (b, 0, 0, 0, 0))
    return pl.pallas_call(
        _swa_kernel,
        grid=(DEC_BATCH,),
        in_specs=[pl.BlockSpec((DEC_SEQ, 512), lambda b: (b, 3)),
                  pl.BlockSpec((DEC_SEQ, 512), lambda b: (b, 4)),
                  cache, cache,
                  pl.BlockSpec(memory_space=pltpu.SMEM)],
        out_specs=pl.BlockSpec((DEC_SEQ, 512), lambda b: (b, 0)),
        out_shape=jax.ShapeDtypeStruct((LAT_TOKENS, 512), BF16),
        compiler_params=_cparams(1),
        name="swa_mixer",
    )(proj, proj, cache_kt, cache_vt, sink)


def _load_weights_bf16(li, wo_hbm, w1_hbm, w2_hbm, wo_s, w1_s, w2_s, stage_sq, stage_wide, sems):
    square = ([(wo_hbm, wo_s, j, WO_ROWS) for j in range(N_WO)] + [(w2_hbm, w2_s, j, W2_ROWS) for j in range(N_W2)])
    wide = [(w1_hbm, w1_s, j, W1_ROWS) for j in range(N_W1)]
    queues = {"sq": (square, stage_sq, 0), "wide": (wide, stage_wide, STAGE_SLOTS)}

    def copy(kind, idx):
        chunks, stage, sem0 = queues[kind]
        src, _, j, n_rows = chunks[idx]
        slot = idx % STAGE_SLOTS
        return pltpu.make_async_copy(src.at[li, j * n_rows:(j + 1) * n_rows, :], stage.at[slot], sems.at[sem0 + slot])

    for kind in queues:
        for idx in range(min(STAGE_SLOTS, len(queues[kind][0]))):
            copy(kind, idx).start()
    order = []
    for i in range(max(len(square), len(wide))):
        order += [("sq", i)] * (i < len(square)) + [("wide", i)] * (i < len(wide))
    for kind, idx in order:
        chunks, stage, _ = queues[kind]
        _, dst, j, n_rows = chunks[idx]
        copy(kind, idx).wait()
        dst[j * n_rows:(j + 1) * n_rows, :] = stage[idx % STAGE_SLOTS].astype(BF16)
        if idx + STAGE_SLOTS < len(chunks):
            copy(kind, idx + STAGE_SLOTS).start()


def _post_mixer_kernel(*refs, li, n_lat_mix, n_x, n_out):
    mix_ctx_ref, mix_lat_refs = refs[0], refs[1:1 + n_lat_mix]
    refs = refs[1 + n_lat_mix:]
    x_refs = refs[:n_x]
    mod_ref, gpost_ref, gpre_ref, gmlp_ref, wo_hbm, w1_hbm, w2_hbm = refs[n_x:7 + n_x]
    out_refs = refs[7 + n_x:7 + n_x + n_out]
    wo_s, w1_s, w2_s, stage_sq, stage_wide, sems = refs[7 + n_x + n_out:]
    step = pl.program_id(0)

    @pl.when(step == 0)
    def _():
        _load_weights_bf16(li, wo_hbm, w1_hbm, w2_hbm, wo_s, w1_s, w2_s, stage_sq, stage_wide, sems)

    is_ctx = step < CTX_TILES
    mod = mod_ref[0, pl.ds(jnp.where(is_ctx, CTX_ROW, (step - CTX_TILES) // (DEC_SEQ // TM)), 1), :]
    g_post, g_pre, g_mlp = gpost_ref[li:li + 1, :], gpre_ref[li:li + 1, :], gmlp_ref[li:li + 1, :]
    rows = [slice(i * (TM // POST_SPLIT), (i + 1) * (TM // POST_SPLIT)) for i in range(POST_SPLIT)]

    def residual(r):
        if n_x == 1:
            return x_refs[0][r, :]
        return jnp.where(is_ctx, x_refs[0][r, :], x_refs[1][r, :])

    def mixed(r):
        lat = [m[r, :] for m in mix_lat_refs]
        lat = lat[0] if n_lat_mix == 1 else jnp.concatenate(lat, axis=1)
        return jnp.where(is_ctx, mix_ctx_ref[r, :], lat)

    ys = [_dot(mixed(r), wo_s[...]) for r in rows]
    x1 = [residual(r) + mod[:, 2 * D_MODEL:3 * D_MODEL] * _rms(y, g_post) for r, y in zip(rows, ys)]
    h = [_norm_mod(x, g_pre, mod, 3 * D_MODEL).astype(BF16) for x in x1]
    acc = [None] * POST_SPLIT
    pending = []

    def second_matmul(c, i, f):
        t2 = _dot(f, w2_s[c * FF_CHUNK:(c + 1) * FF_CHUNK, :])
        acc[i] = t2 if acc[i] is None else acc[i] + t2

    for c in range(D_FF // FF_CHUNK):
        for i in range(POST_SPLIT):
            f = _dot(h[i], w1_s[:, c * FF_CHUNK:(c + 1) * FF_CHUNK])
            pending.append((c, i, jnp.square(jnp.maximum(f, 0.0)).astype(BF16)))
            if len(pending) > 1:
                second_matmul(*pending.pop(0))
    second_matmul(*pending.pop(0))
    res = [x1[i] + mod[:, 5 * D_MODEL:6 * D_MODEL] * _rms(acc[i], g_mlp) for i in range(POST_SPLIT)]
    if n_out == 1:
        for i, r in enumerate(rows):
            out_refs[0][r, :] = res[i]
    else:
        @pl.when(is_ctx)
        def _():
            for i, r in enumerate(rows):
                out_refs[0][r, :] = res[i]

        @pl.when(jnp.logical_not(is_ctx))
        def _():
            for i, r in enumerate(rows):
                out_refs[1][r, :] = res[i]


def _post_mixer(mix_ctx, mix_lat, xs, mod, g_post, g_pre, g_mlp, w_out, w1, w2, li, *, split_out):
    tile = lambda s: s
    ctx_tile = lambda s: jnp.minimum(tile(s), CTX_TILES - 1)
    lat_tile = lambda s: jnp.maximum(tile(s) - CTX_TILES, 0)
    token_spec = lambda index, width=D_MODEL: pl.BlockSpec((TM, width), lambda s: (index(s), 0))
    mix_specs = [token_spec(ctx_tile)] + [token_spec(lat_tile, m.shape[1]) for m in mix_lat]
    x_specs = [token_spec(tile)] if len(xs) == 1 else [token_spec(ctx_tile), token_spec(lat_tile)]
    if split_out:
        out_specs = [token_spec(ctx_tile), token_spec(lat_tile)]
        out_shape = [jax.ShapeDtypeStruct((CTX_TOKENS, D_MODEL), F32), jax.ShapeDtypeStruct((LAT_TOKENS, D_MODEL), F32)]
    else:
        out_specs = [token_spec(tile)]
        out_shape = [jax.ShapeDtypeStruct((ALL_TOKENS, D_MODEL), F32)]
    gain = pl.BlockSpec((DEPTH, D_MODEL), lambda s: (0, 0))
    return pl.pallas_call(
        functools.partial(_post_mixer_kernel, li=li, n_lat_mix=len(mix_lat), n_x=len(xs), n_out=len(out_specs)),
        grid=(ALL_TOKENS // TM,),
        in_specs=mix_specs + x_specs + [
            pl.BlockSpec((1, MOD_ROWS, 6 * D_MODEL), lambda s: (li, 0, 0)),
            gain, gain, gain] + [pl.BlockSpec(memory_space=pl.ANY)] * 3,
        out_specs=out_specs,
        out_shape=out_shape,
        scratch_shapes=[pltpu.VMEM((D_MODEL, D_MODEL), BF16), pltpu.VMEM((D_MODEL, D_FF), BF16),
                        pltpu.VMEM((D_FF, D_MODEL), BF16),
                        pltpu.VMEM((STAGE_SLOTS, WO_ROWS, D_MODEL), F32), pltpu.VMEM((STAGE_SLOTS, W1_ROWS, D_FF), F32),
                        pltpu.SemaphoreType.DMA((2 * STAGE_SLOTS,))],
        compiler_params=_cparams(1),
        name="post_mixer",
    )(mix_ctx, *mix_lat, *xs, mod, g_post, g_pre, g_mlp, w_out, w1, w2)


def _rope_tables():
    t = jnp.arange(DEC_SEQ)
    rows = (t // GRID_W).astype(F32)
    cols = (t % GRID_W).astype(F32)
    q4 = DH // 4
    inv = 1.0 / (ROPE_BASE ** (jnp.arange(q4, dtype=F32) / q4))
    ar, ac = rows[:, None] * inv, cols[:, None] * inv
    cos64 = jnp.concatenate([jnp.cos(ar), jnp.cos(ar), jnp.cos(ac), jnp.cos(ac)], axis=1)
    sin64 = jnp.concatenate([-jnp.sin(ar), jnp.sin(ar), -jnp.sin(ac), jnp.sin(ac)], axis=1)
    return jnp.tile(cos64, (1, 2)), jnp.tile(sin64, (1, 2))


def kernel(x_prompt, x_sample, cache_diff_k, cache_diff_v, cache_na_k, cache_na_v, cache_swa_k, cache_swa_v, c, c_ctx, mod_w, mod_b, norm_mix_pre, norm_mix_post, norm_mlp_pre, norm_mlp_post, w_in_even, conv_w, lambda_q1, lambda_k1, lambda_q2, lambda_k2, subln, w_in_odd, rpb, sink, w_out, mlp_w1, mlp_w2):
    cond = jnp.concatenate([c, c_ctx[None, :], jnp.zeros((MOD_ROWS - DEC_BATCH - 1, D_MODEL), F32)], axis=0)
    mod = _modulation(cond, mod_w, mod_b)
    rope = _rope_tables()

    t_minor = lambda a: jnp.swapaxes(a, -1, -2)

    w_in = [w_in_even.astype(BF16), w_in_odd.astype(BF16)]

    xp = x_prompt.reshape(CTX_TOKENS, D_MODEL)
    xs = x_sample.reshape(LAT_TOKENS, D_MODEL)
    lam_init0 = 0.8 - 0.6 * math.exp(-0.3 * 0)
    lam_args = (conv_w, lambda_q1, lambda_k1, lambda_q2, lambda_k2, subln)
    sink_flat = sink.reshape(-1)

    def post(li, mix_ctx, mix_lat, x_parts, split_out):
        return _post_mixer(mix_ctx, mix_lat, x_parts, mod, norm_mix_post, norm_mlp_pre, norm_mlp_post, w_out, mlp_w1,
                           mlp_w2, li, split_out=split_out)

    proj, new_diff_kt, new_diff_v = _pre_mixer(xp, mod, norm_mix_pre, w_in[0], 0, latent=False, even=True, rope=None,
                                               n_tok=CTX_TOKENS)
    mix_ctx = _even_mixer(proj, *lam_args, new_diff_kt, new_diff_v, seq=SEQ, n_seq=CTX_SEQS_PER_STEP // 2,
                          own_keys=False, lam_init=lam_init0)
    proj = _pre_mixer(xs, mod, norm_mix_pre, w_in[0], 0, latent=True, even=True, rope=rope, n_tok=LAT_TOKENS)
    mix_lat = _even_mixer(proj, *lam_args, t_minor(cache_diff_k), cache_diff_v, seq=DEC_SEQ, n_seq=1, own_keys=True,
                          lam_init=lam_init0)
    (x_all,) = post(0, mix_ctx, [mix_lat], [xp, xs], False)

    q, new_na_kt, new_na_vt, new_swa_kt, new_swa_vt = _pre_mixer(
        x_all, mod, norm_mix_pre, w_in[1], 1, latent=False, even=False, rope=None, n_tok=CTX_TOKENS)
    mix_ctx = _odd_ctx_mixer(q, new_na_kt, new_na_vt, new_swa_kt, new_swa_vt, sink_flat)
    proj = _pre_mixer(x_all, mod, norm_mix_pre, w_in[1], 1, latent=True, even=False, rope=rope, n_tok=LAT_TOKENS,
                      tile0=CTX_TILES)
    mix_c = _na_mixer(proj, t_minor(cache_na_k), t_minor(cache_na_v), rpb.reshape(-1))
    mix_d = _swa_mixer(proj, t_minor(cache_swa_k), t_minor(cache_swa_v), sink_flat)
    xp, xs = post(1, mix_ctx, [mix_c, mix_d], [x_all], True)

    return (xp.reshape(BATCH, SEQ, D_MODEL), xs.reshape(DEC_BATCH, DEC_SEQ, D_MODEL),
            t_minor(new_diff_kt), new_diff_v, t_minor(new_na_kt), t_minor(new_na_vt),
            t_minor(new_swa_kt), t_minor(new_swa_vt))
```

```python
import functools
import math

import jax
import jax.numpy as jnp
from jax import lax
from jax.experimental import pallas as pl
from jax.experimental.pallas import tpu as pltpu

F32 = jnp.float32
BF16 = jnp.bfloat16

D_MODEL = 1024
BATCH = 32
SEQ = 256
DEPTH = 2
DEC_BATCH = 8
DEC_SEQ = 1024
PAST_LEN = 256
GRID_W = 64
GRID_H = DEC_SEQ // GRID_W
MIX_A = 512
H_B = 4
DH = 64
PAIR = 2 * DH
H_C = 8
H_D = 8
KV_D = 2
GQA = H_D // KV_D
IN_EVEN = 3072
IN_ODD = 2304
D_FF = 4096
NA_WIN_R = 8
NA_WIN_C = 16
SWA_BLOCK = 128
SWA_WINDOW = 128
ROPE_BASE = 10000.0
EPS = 1e-6
NEG = -1e30
LOG2E = math.log2(math.e)
Q_SCALE = DH ** -0.5 * LOG2E

MOD_ROWS = 16
CTX_ROW = DEC_BATCH
V7X_VMEM_BYTES = 64 * 1024 * 1024
VMEM_LIMIT = V7X_VMEM_BYTES - 12 * 1024 * 1024

CTX_TOKENS = BATCH * SEQ
LAT_TOKENS = DEC_BATCH * DEC_SEQ
ALL_TOKENS = CTX_TOKENS + LAT_TOKENS

TM = 512
FF_CHUNK = 1024
POST_SPLIT = 2
W_ROWS = 256
W_COLS = 256
STAGE_SLOTS = 2
CTX_TILES = CTX_TOKENS // TM
CTX_SEQS_PER_STEP = 4
MOD_TN = 1536
NA_SEQS_PER_STEP = 2
NA_HALF = 512
NA_KEYS = 768
SWA_QROWS = 2 * SWA_BLOCK
SWA_BAND = 4 * SWA_BLOCK
LAT_ODD_COLS = IN_ODD + 2 * PAIR


def _cparams(n_axes):
    return pltpu.CompilerParams(dimension_semantics=("arbitrary",) * n_axes, vmem_limit_bytes=VMEM_LIMIT)


def _rms(x, g):
    return x * lax.rsqrt(jnp.mean(x * x, axis=-1, keepdims=True) + EPS) * g


def _dot(a, b):
    return jnp.dot(a, b, preferred_element_type=F32)


def _dot_nt(a, b):
    return lax.dot_general(a, b, (((1,), (1,)), ((), ())), preferred_element_type=F32)


def _layer_block(shape, li):
    zeros = (0,) * len(shape)
    return pl.BlockSpec((1,) + tuple(shape), lambda *_: (li,) + zeros, pipeline_mode=pl.Buffered(1))


def _low_half(shape):
    return (lax.broadcasted_iota(jnp.int32, shape, len(shape) - 1) & DH) == 0


def _keep_mask_bf16(rows, hh):
    lane = lax.broadcasted_iota(jnp.int32, (rows, PAIR), 1)
    half = (lane & DH).astype(F32).astype(BF16)
    return (half == 0) if hh == 0 else (half != 0)


def _pad_rows(t, hh, fill):
    other = jnp.full(t.shape, fill, t.dtype)
    return jnp.concatenate([t, other] if hh == 0 else [other, t], axis=0)


class _Memo(dict):
    def get_or_build(self, key, build):
        if key not in self:
            self[key] = build()
        return self[key]


def _merge_pair(o0, o1, extra_den=None):
    low = _low_half(o0.shape)
    num = jnp.where(low, o0, o1)
    den = pltpu.roll(jnp.where(low, o1, o0), DH, axis=1)
    if extra_den is not None:
        den = den + jnp.where(low, extra_den[0], extra_den[1])
    return num * (1.0 / den)


class _Head:
    def __init__(self, scores, finish, sink=None):
        self.scores, self.finish, self.sink = scores, finish, sink


def _attend(heads, batch):
    batches = [heads[i:i + batch] for i in range(0, len(heads), batch)]
    cur = [hd.scores() for hd in batches[0]]
    for bi, group in enumerate(batches):
        nxt = [hd.scores() for hd in batches[bi + 1]] if bi + 1 < len(batches) else None
        exps = []
        for hd, parts in zip(group, cur):
            chunks = [s[:, j * 128:(j + 1) * 128] for s in parts for j in range(s.shape[1] // 128)]
            m_el = functools.reduce(jnp.maximum, chunks)
            if hd.sink is not None:
                m_el = jnp.maximum(m_el, hd.sink)
            mx = jnp.broadcast_to(jnp.max(m_el, axis=-1, keepdims=True), m_el.shape)
            term = None if hd.sink is None else jnp.exp2(hd.sink - mx)
            e_parts = [jnp.concatenate([jnp.exp2(s[:, j * 128:(j + 1) * 128] - mx)
                                        for j in range(s.shape[1] // 128)], axis=1).astype(BF16) for s in parts]
            exps.append((e_parts, term))
        for hd, (e_parts, term) in zip(group, exps):
            hd.finish(e_parts, term)
        cur = nxt


class _PairSink:
    def __init__(self, y_ref, rows, cols):
        self.y_ref, self.rows, self.cols, self.first = y_ref, rows, cols, None

    def put(self, hh, o, term):
        if hh == 0:
            self.first = (o, term)
        else:
            o0, term0 = self.first
            extra = None if term is None else (term0, term)
            self.y_ref[self.rows, self.cols] = _merge_pair(o0, o, extra).astype(BF16)


def _mod_kernel(cond_ref, w_ref, b_ref, o_ref):
    c = cond_ref[...]
    s = c * (1.0 / (1.0 + jnp.exp(-c)))
    o_ref[0] = _dot(s.astype(BF16), w_ref[0].astype(BF16)) + b_ref[pl.ds(pl.program_id(0), 1), :]


def _modulation(cond, mod_w, mod_b):
    return pl.pallas_call(
        _mod_kernel,
        grid=(DEPTH, 6 * D_MODEL // MOD_TN),
        in_specs=[
            pl.BlockSpec((MOD_ROWS, D_MODEL), lambda l, j: (0, 0)),
            pl.BlockSpec((1, D_MODEL, MOD_TN), lambda l, j: (l, 0, j)),
            pl.BlockSpec((DEPTH, MOD_TN), lambda l, j: (0, j)),
        ],
        out_specs=pl.BlockSpec((1, MOD_ROWS, MOD_TN), lambda l, j: (l, 0, j)),
        out_shape=jax.ShapeDtypeStruct((DEPTH, MOD_ROWS, 6 * D_MODEL), F32),
        compiler_params=_cparams(2),
        name="adaln_modulation",
    )(cond, mod_w, mod_b)


def _norm_mod(x, g, mod, shift_col):
    sh = mod[:, shift_col:shift_col + D_MODEL]
    sc = mod[:, shift_col + D_MODEL:shift_col + 2 * D_MODEL]
    return _rms(x, g) * (1.0 + sc) + sh


def _rope(z, cos, sin):
    outs = []
    for j in range(z.shape[1] // 128):
        zj = z[:, j * 128:(j + 1) * 128]
        lane = lax.broadcasted_iota(jnp.int32, zj.shape, 1)
        partner = jnp.where((lane & 16) == 0, pltpu.roll(zj, 128 - 16, axis=1), pltpu.roll(zj, 16, axis=1))
        outs.append(zj * cos + partner * sin)
    return outs[0] if len(outs) == 1 else jnp.concatenate(outs, axis=1)


def _store_heads_transposed(dst, bi, pc_rows, n_pairs):
    for p in range(n_pairs):
        t = pc_rows[:, p * PAIR:(p + 1) * PAIR].T
        dst[bi, 0, 2 * p] = t[0:DH]
        dst[bi, 0, 2 * p + 1] = t[DH:PAIR]


def _mod_row(mod_ref, latent):
    if latent:
        return mod_ref[0, pl.ds(pl.program_id(0) // (DEC_SEQ // TM), 1), :]
    return mod_ref[0, CTX_ROW:CTX_ROW + 1, :]


def _normed_groups(x_ref, mod_ref, g_ref, li, latent):
    rows = [slice(i * SEQ, (i + 1) * SEQ) for i in range(TM // SEQ)]
    mod, g = _mod_row(mod_ref, latent), g_ref[li:li + 1, :]
    return rows, [_norm_mod(x_ref[r, :], g, mod, 0).astype(BF16) for r in rows]


def _k1_ctx_even(x_ref, mod_ref, g_ref, w_ref, proj_ref, dkt_ref, dv_ref, *, li):
    rows, hs = _normed_groups(x_ref, mod_ref, g_ref, li, False)
    for c in range(IN_EVEN // 512):
        for bi, (r, h) in enumerate(zip(rows, hs)):
            pc = _dot(h, w_ref[0, :, c * 512:(c + 1) * 512])
            if c < 3:
                proj_ref[r, c * 512:(c + 1) * 512] = pc.astype(BF16)
            elif c == 3:
                proj_ref[r, c * 512:(c + 1) * 512] = (pc * Q_SCALE).astype(BF16)
            elif c == 4:
                for hh in range(H_B):
                    t = pc[:, hh * PAIR:(hh + 1) * PAIR].T
                    dkt_ref[bi, 0, hh, 0] = t[0:DH]
                    dkt_ref[bi, 0, hh, 1] = t[DH:PAIR]
            else:
                for hh in range(H_B):
                    dv_ref[bi, 0, hh] = pc[:, hh * PAIR:(hh + 1) * PAIR]


def _k1_lat_even(x_ref, mod_ref, g_ref, w_ref, cos_ref, sin_ref, proj_ref, *, li):
    rows, hs = _normed_groups(x_ref, mod_ref, g_ref, li, True)
    for c in range(IN_EVEN // 512):
        for r, h in zip(rows, hs):
            pc = _dot(h, w_ref[0, :, c * 512:(c + 1) * 512])
            if c == 3:
                pc = _rope(pc * Q_SCALE, cos_ref[r, :], sin_ref[r, :])
            elif c == 4:
                pc = _rope(pc, cos_ref[r, :], sin_ref[r, :])
            proj_ref[r, c * 512:(c + 1) * 512] = pc.astype(BF16)


def _k1_ctx_odd(x_ref, mod_ref, g_ref, w_ref, q_ref, nkt_ref, nvt_ref, skt_ref, svt_ref, *, li):
    rows, hs = _normed_groups(x_ref, mod_ref, g_ref, li, False)
    for c in range(4):
        for bi, (r, h) in enumerate(zip(rows, hs)):
            pc = _dot(h, w_ref[0, :, c * 512:(c + 1) * 512])
            if c == 0 or c == 3:
                q_ref[r, (c // 3) * 512:(c // 3 + 1) * 512] = (pc * Q_SCALE).astype(BF16)
            else:
                _store_heads_transposed(nkt_ref if c == 1 else nvt_ref, bi, pc, H_C // 2)
    for bi, h in enumerate(hs):
        pc = _dot(h, w_ref[0, :, 2048:IN_ODD])
        _store_heads_transposed(skt_ref, bi, pc[:, 0:PAIR], 1)
        _store_heads_transposed(svt_ref, bi, pc[:, PAIR:2 * PAIR], 1)


def _k1_lat_odd(x_ref, mod_ref, g_ref, w_ref, cos_ref, sin_ref, proj_ref, *, li):
    rows, hs = _normed_groups(x_ref, mod_ref, g_ref, li, True)
    for c in range(4):
        for r, h in zip(rows, hs):
            pc = _dot(h, w_ref[0, :, c * 512:(c + 1) * 512])
            if c == 0:
                pc = pc * Q_SCALE
            elif c == 3:
                pc = _rope(pc * Q_SCALE, cos_ref[r, :], sin_ref[r, :])
            proj_ref[r, c * 512:(c + 1) * 512] = pc.astype(BF16)
    low = _low_half((SEQ, PAIR))
    for r, h in zip(rows, hs):
        pc = _dot(h, w_ref[0, :, 2048:IN_ODD])
        for j, z in enumerate((_rope(pc[:, 0:PAIR], cos_ref[r, :], sin_ref[r, :]), pc[:, PAIR:2 * PAIR])):
            zr = pltpu.roll(z, DH, axis=1)
            base = 2048 + j * 2 * PAIR
            proj_ref[r, base:base + PAIR] = jnp.where(low, z, zr).astype(BF16)
            proj_ref[r, base + PAIR:base + 2 * PAIR] = jnp.where(low, zr, z).astype(BF16)


def _pre_mixer(x, mod, gains, w, li, *, latent, even, rope, n_tok, tile0=0):
    n_in = w.shape[2]
    tiles_per_seq = DEC_SEQ // TM
    in_specs = [
        pl.BlockSpec((TM, D_MODEL), lambda i: (i + tile0, 0)),
        pl.BlockSpec((1, MOD_ROWS, 6 * D_MODEL), lambda i: (li, 0, 0)),
        pl.BlockSpec((DEPTH, D_MODEL), lambda i: (0, 0)),
        _layer_block((D_MODEL, n_in), 0),
    ]
    args = [x, mod, gains, w]
    nb = TM // SEQ
    if latent:
        in_specs += [pl.BlockSpec((TM, 128), lambda i: (i % tiles_per_seq, 0))] * 2
        args += list(rope)
        body = _k1_lat_even if even else _k1_lat_odd
        n_out = IN_EVEN if even else LAT_ODD_COLS
        out_specs = pl.BlockSpec((TM, n_out), lambda i: (i, 0))
        out_shape = jax.ShapeDtypeStruct((n_tok, n_out), BF16)
    elif even:
        body = _k1_ctx_even
        out_specs = [pl.BlockSpec((TM, 4 * MIX_A), lambda i: (i, 0)),
                     pl.BlockSpec((nb, 1, H_B, 2, DH, SEQ), lambda i: (i, 0, 0, 0, 0, 0)),
                     pl.BlockSpec((nb, 1, H_B, SEQ, PAIR), lambda i: (i, 0, 0, 0, 0))]
        out_shape = [jax.ShapeDtypeStruct((n_tok, 4 * MIX_A), BF16),
                     jax.ShapeDtypeStruct((BATCH, 1, H_B, 2, DH, SEQ), F32),
                     jax.ShapeDtypeStruct((BATCH, 1, H_B, SEQ, PAIR), F32)]
    else:
        body = _k1_ctx_odd
        c_spec = pl.BlockSpec((nb, 1, H_C, DH, SEQ), lambda i: (i, 0, 0, 0, 0))
        d_spec = pl.BlockSpec((nb, 1, KV_D, DH, SEQ), lambda i: (i, 0, 0, 0, 0))
        c_shape = jax.ShapeDtypeStruct((BATCH, 1, H_C, DH, SEQ), F32)
        d_shape = jax.ShapeDtypeStruct((BATCH, 1, KV_D, DH, SEQ), F32)
        out_specs = [pl.BlockSpec((TM, D_MODEL), lambda i: (i, 0)), c_spec, c_spec, d_spec, d_spec]
        out_shape = [jax.ShapeDtypeStruct((n_tok, D_MODEL), BF16), c_shape, c_shape, d_shape, d_shape]
    return pl.pallas_call(
        functools.partial(body, li=li),
        grid=(n_tok // TM,),
        in_specs=in_specs,
        out_specs=out_specs,
        out_shape=out_shape,
        compiler_params=_cparams(1),
        name=f"pre_mixer_{'lat' if latent else 'ctx'}_{'even' if even else 'odd'}",
    )(*args)


def _even_mixer_kernel(*refs, seq, n_seq, tq, own_keys, lam_init):
    proj_ref, cw_ref, lq1_ref, lk1_ref, lq2_ref, lk2_ref, subln_ref, ckt_ref, cv_ref = refs[:9]
    y_ref = refs[-1]
    n_rows = n_seq * seq

    pos = lax.broadcasted_iota(jnp.int32, (n_rows, 128), 0) % seq
    for j in range(MIX_A // 128):
        cols = slice(j * 128, (j + 1) * 128)
        a_b = proj_ref[:, j * 128:(j + 1) * 128].astype(F32)
        u = (proj_ref[:, MIX_A + j * 128:MIX_A + (j + 1) * 128].astype(F32)
             * proj_ref[:, 2 * MIX_A + j * 128:2 * MIX_A + (j + 1) * 128].astype(F32))
        u_prev = jnp.where(pos == 0, 0.0, pltpu.roll(u, 1, axis=0))
        u_next = jnp.where(pos == seq - 1, 0.0, pltpu.roll(u, n_rows - 1, axis=0))
        w = cw_ref[0, :, cols]
        y_ref[:, cols] = (a_b * (w[0:1] * u_prev + w[1:2] * u + w[2:3] * u_next)).astype(BF16)

    lam = (jnp.exp(jnp.sum(lq1_ref[...] * lk1_ref[...], axis=-1, keepdims=True))
           - jnp.exp(jnp.sum(lq2_ref[...] * lk2_ref[...], axis=-1, keepdims=True)) + lam_init)
    subln = subln_ref[...]
    q_col, k_col, v_col = 3 * MIX_A, 3 * MIX_A + 512, 3 * MIX_A + 1024
    ones = jnp.ones((PAST_LEN, PAIR), BF16)

    memo = _Memo()

    def own_k(b, h, m):
        kp = proj_ref[b * seq:(b + 1) * seq, k_col + h * PAIR:k_col + (h + 1) * PAIR]
        return jnp.where(_keep_mask_bf16(seq, m), kp, jnp.zeros_like(kp))

    def value_ops(b, h):
        v_ops = []
        if own_keys:
            v_own = proj_ref[b * seq:(b + 1) * seq, v_col + h * PAIR:v_col + (h + 1) * PAIR]
            v_ops.append(jnp.concatenate([v_own, jnp.ones((seq, PAIR), BF16)], axis=1))
        v_ops.append(jnp.concatenate([cv_ref[b, 0, h].astype(BF16), ones], axis=1))
        return v_ops

    def scores(b, qrows, h, m):
        qp = proj_ref[qrows, q_col + h * PAIR:q_col + (h + 1) * PAIR]
        parts = []
        if own_keys:
            parts.append(_dot_nt(qp, memo.get_or_build(("k", b, h, m), functools.partial(own_k, b, h, m))))
        ckt = memo.get_or_build(("ck", b, h, m), lambda: _pad_rows(ckt_ref[b, 0, h, m].astype(BF16), m, 0.0))
        parts.append(_dot(qp, ckt))
        return parts

    first_map = {}

    def pv(b, qrows, h, m, e_parts, _):
        v_ops = memo.get_or_build(("v", b, h), functools.partial(value_ops, b, h))
        o = None
        for e, v_op in zip(e_parts, v_ops):
            t = _dot(e, v_op)
            o = t if o is None else o + t
        attn = o[:, 0:PAIR] * (1.0 / o[:, PAIR:2 * PAIR])
        if m == 0:
            first_map[(qrows.start, h)] = attn
        else:
            y = _rms(first_map.pop((qrows.start, h)) - lam * attn, subln) * (1.0 - lam_init)
            y_ref[qrows, MIX_A + h * PAIR:MIX_A + (h + 1) * PAIR] = y.astype(BF16)

    heads = []
    for b in range(n_seq):
        for i in range(seq // tq):
            qrows = slice(b * seq + i * tq, b * seq + (i + 1) * tq)
            for h in range(H_B):
                for m in range(2):
                    heads.append(_Head(functools.partial(scores, b, qrows, h, m),
                                       functools.partial(pv, b, qrows, h, m)))
    _attend(heads, batch=8)


def _even_mixer(proj, conv_w, lq1, lk1, lq2, lk2, subln, kt, v, *, seq, n_seq, own_keys, lam_init):
    n_tok = proj.shape[0]
    rows = n_seq * seq
    small = lambda a: pl.BlockSpec((1, a.shape[1]), lambda b: (0, 0))
    in_specs = [pl.BlockSpec((rows, proj.shape[1]), lambda b: (b, 0)),
                pl.BlockSpec((1, 3, MIX_A), lambda b: (0, 0, 0)),
                small(lq1), small(lk1), small(lq2), small(lk2), small(subln),
                pl.BlockSpec((n_seq, 1, H_B, 2, DH, PAST_LEN), lambda b: (b, 0, 0, 0, 0, 0)),
                pl.BlockSpec((n_seq, 1, H_B, PAST_LEN, PAIR), lambda b: (b, 0, 0, 0, 0))]
    return pl.pallas_call(
        functools.partial(_even_mixer_kernel, seq=seq, n_seq=n_seq, tq=min(seq, 256), own_keys=own_keys,
                          lam_init=lam_init),
        grid=(n_tok // rows,),
        in_specs=in_specs,
        out_specs=pl.BlockSpec((rows, D_MODEL), lambda b: (b, 0)),
        out_shape=jax.ShapeDtypeStruct((n_tok, D_MODEL), BF16),
        compiler_params=_cparams(1),
        name=f"even_mixer_{'lat' if own_keys else 'ctx'}",
    )(proj, conv_w, lq1, lk1, lq2, lk2, subln, kt, v)


def _odd_ctx_kernel(q_ref, nkt_ref, nvt_ref, skt_ref, svt_ref, sink_ref, y_ref):
    def kv_refs(b, head):
        if head < H_C:
            return nkt_ref.at[b, 0, head], nvt_ref.at[b, 0, head]
        g = (head - H_C) // GQA
        return skt_ref.at[b, 0, g], svt_ref.at[b, 0, g]

    memo = _Memo()

    def kv_key(head):
        return head if head < H_C else H_C + (head - H_C) // GQA

    def scores(b, p, hh):
        head = 2 * p + hh
        kt = memo.get_or_build(("k", b, kv_key(head), hh),
                               lambda: _pad_rows(kv_refs(b, head)[0][...].astype(BF16), hh, 0.0))
        return [_dot(q_ref[b * SEQ:(b + 1) * SEQ, p * PAIR:(p + 1) * PAIR], kt)]

    def finish(out, b, p, hh, e_parts, term):
        head = 2 * p + hh
        vt = memo.get_or_build(("v", b, kv_key(head), hh),
                               lambda: _pad_rows(kv_refs(b, head)[1][...].astype(BF16), hh, 1.0))
        out.put(hh, _dot_nt(e_parts[0], vt), term)

    heads = []
    for b in range(CTX_SEQS_PER_STEP):
        for p in range((H_C + H_D) // 2):
            out = _PairSink(y_ref, slice(b * SEQ, (b + 1) * SEQ), slice(p * PAIR, (p + 1) * PAIR))
            for hh in range(2):
                d_head = 2 * p + hh - H_C
                sink = sink_ref[d_head] * LOG2E if d_head >= 0 else None
                heads.append(_Head(functools.partial(scores, b, p, hh), functools.partial(finish, out, b, p, hh), sink))
    _attend(heads, batch=8)


def _odd_ctx_mixer(q, nkt, nvt, skt, svt, sink):
    n_tok = q.shape[0]
    n_seq = CTX_SEQS_PER_STEP
    c_spec = pl.BlockSpec((n_seq, 1, H_C, DH, SEQ), lambda b: (b, 0, 0, 0, 0))
    d_spec = pl.BlockSpec((n_seq, 1, KV_D, DH, SEQ), lambda b: (b, 0, 0, 0, 0))
    return pl.pallas_call(
        _odd_ctx_kernel,
        grid=(n_tok // (n_seq * SEQ),),
        in_specs=[pl.BlockSpec((n_seq * SEQ, D_MODEL), lambda b: (b, 0)), c_spec, c_spec, d_spec, d_spec,
                  pl.BlockSpec(memory_space=pltpu.SMEM)],
        out_specs=pl.BlockSpec((n_seq * SEQ, D_MODEL), lambda b: (b, 0)),
        out_shape=jax.ShapeDtypeStruct((n_tok, D_MODEL), BF16),
        compiler_params=_cparams(1),
        name="odd_mixer_ctx",
    )(q, nkt, nvt, skt, svt, sink)


def _build_na_bias(rpb_ref, head, nb_ref, slot):
    qc = lax.broadcasted_iota(jnp.int32, (GRID_W, GRID_W), 0)
    kc = lax.broadcasted_iota(jnp.int32, (GRID_W, GRID_W), 1)
    col_start = jnp.clip(qc - NA_WIN_C // 2, 0, GRID_W - NA_WIN_C)
    col_ok = (kc >= col_start) & (kc < col_start + NA_WIN_C)
    dc = kc - qc + NA_WIN_C - 1
    n_dr, n_dc = 2 * NA_WIN_R - 1, 2 * NA_WIN_C - 1
    neg = jnp.full((GRID_W, GRID_W), NEG, F32)
    toeplitz = []
    for dr in range(n_dr):
        t = neg
        for d in range(n_dc):
            t = jnp.where(dc == d, rpb_ref[(head * n_dr + dr) * n_dc + d] * LOG2E, t)
        toeplitz.append(jnp.where(col_ok, t, NEG))
    rows_per_half = NA_HALF // GRID_W
    key_rows = NA_KEYS // GRID_W
    for half in range(2):
        for rl in range(rows_per_half):
            r = half * rows_per_half + rl
            r_start = min(max(r - NA_WIN_R // 2, 0), GRID_H - NA_WIN_R)
            blocks = []
            for j in range(key_rows):
                rk = half * (GRID_H - key_rows) + j
                inside = r_start <= rk < r_start + NA_WIN_R
                blocks.append(toeplitz[rk - r + NA_WIN_R - 1] if inside else neg)
            nb_ref[slot, half, rl * GRID_W:(rl + 1) * GRID_W, :] = jnp.concatenate(blocks, axis=1)


def _na_kernel(rpb_ref, q_ref, k_ref, v_ref, kct_ref, vct_ref, y_ref, nb_ref):
    @pl.when(pl.program_id(1) == 0)
    def _():
        for hh in range(2):
            _build_na_bias(rpb_ref, 2 * pl.program_id(0) + hh, nb_ref, hh)

    def scores(b, half, hh):
        w0 = b * DEC_SEQ + half * (DEC_SEQ - NA_KEYS)
        qp = q_ref[b * DEC_SEQ + half * NA_HALF:b * DEC_SEQ + (half + 1) * NA_HALF, :]
        kw = k_ref[w0:w0 + NA_KEYS, :]
        s1 = _dot_nt(qp, jnp.where(_keep_mask_bf16(NA_KEYS, hh), kw, jnp.zeros_like(kw))) + nb_ref[hh, half]
        s2 = _dot(qp, _pad_rows(kct_ref[b, 0, hh].astype(BF16), hh, 0.0))
        return [s1, s2]

    def finish(out, b, half, hh, e_parts, term):
        w0 = b * DEC_SEQ + half * (DEC_SEQ - NA_KEYS)
        vw = v_ref[w0:w0 + NA_KEYS, :]
        o = (_dot(e_parts[0], jnp.where(_keep_mask_bf16(NA_KEYS, hh), vw, jnp.ones_like(vw)))
             + _dot_nt(e_parts[1], _pad_rows(vct_ref[b, 0, hh].astype(BF16), hh, 1.0)))
        out.put(hh, o, term)

    heads = []
    for b in range(NA_SEQS_PER_STEP):
        for half in range(2):
            rows = slice(b * DEC_SEQ + half * NA_HALF, b * DEC_SEQ + (half + 1) * NA_HALF)
            out = _PairSink(y_ref, rows, slice(0, PAIR))
            for hh in range(2):
                heads.append(_Head(functools.partial(scores, b, half, hh), functools.partial(finish, out, b, half, hh)))
    _attend(heads, batch=4)


def _na_mixer(proj, cache_kt, cache_vt, rpb_flat):
    pairs = H_C // 2
    n_seq = NA_SEQS_PER_STEP
    col = lambda base: pl.BlockSpec((n_seq * DEC_SEQ, PAIR), lambda hp, b: (b, base + hp))
    cache = pl.BlockSpec((n_seq, 1, 2, DH, PAST_LEN), lambda hp, b: (b, 0, hp, 0, 0))
    return pl.pallas_call(
        _na_kernel,
        grid=(pairs, DEC_BATCH // n_seq),
        in_specs=[pl.BlockSpec(memory_space=pltpu.SMEM), col(0), col(pairs), col(2 * pairs), cache, cache],
        out_specs=pl.BlockSpec((n_seq * DEC_SEQ, PAIR), lambda hp, b: (b, hp)),
        out_shape=jax.ShapeDtypeStruct((LAT_TOKENS, 512), BF16),
        scratch_shapes=[pltpu.VMEM((2, 2, NA_HALF, NA_KEYS), F32)],
        compiler_params=_cparams(2),
        name="na_mixer",
    )(rpb_flat, proj, proj, proj, cache_kt, cache_vt)


def _swa_kernel(q_ref, kv_ref, kct_ref, vct_ref, sink_ref, y_ref):
    n_groups = DEC_SEQ // SWA_QROWS
    ql = lax.broadcasted_iota(jnp.int32, (SWA_QROWS, SWA_BAND), 0)
    kj = lax.broadcasted_iota(jnp.int32, (SWA_QROWS, SWA_BAND), 1)

    def band_bias(first_key_minus_first_query):
        return jnp.where(jnp.abs(kj + first_key_minus_first_query - ql) <= SWA_WINDOW, 0.0, NEG)

    bias_first, bias_mid, bias_last = band_bias(0), band_bias(-SWA_BLOCK), band_bias(-2 * SWA_BLOCK)

    def band_start(grp):
        return min(max(grp * SWA_QROWS - SWA_BLOCK, 0), DEC_SEQ - SWA_BAND)

    memo = _Memo()

    def band_k(g, grp, hh):
        start = band_start(grp)
        kb = kv_ref[start:start + SWA_BAND, g * PAIR:(g + 1) * PAIR]
        return jnp.where(_keep_mask_bf16(SWA_BAND, hh), kb, jnp.zeros_like(kb))

    def band_v(g, grp, hh):
        start = band_start(grp)
        vb = kv_ref[start:start + SWA_BAND, (KV_D + g) * PAIR:(KV_D + g + 1) * PAIR]
        return jnp.where(_keep_mask_bf16(SWA_BAND, hh), vb, jnp.ones_like(vb))

    def scores(g, grp, p, hh):
        bias = bias_first if grp == 0 else (bias_last if grp == n_groups - 1 else bias_mid)
        qp = q_ref[grp * SWA_QROWS:(grp + 1) * SWA_QROWS, p * PAIR:(p + 1) * PAIR]
        s1 = _dot_nt(qp, memo.get_or_build(("k", g, grp, hh), functools.partial(band_k, g, grp, hh))) + bias
        ckt = memo.get_or_build(("ck", g, hh), lambda: _pad_rows(kct_ref[0, 0, g].astype(BF16), hh, 0.0))
        return [s1, _dot(qp, ckt)]

    def finish(out, g, grp, hh, e_parts, term):
        cvt = memo.get_or_build(("cv", g, hh), lambda: _pad_rows(vct_ref[0, 0, g].astype(BF16), hh, 1.0))
        o = (_dot(e_parts[0], memo.get_or_build(("v", g, grp, hh), functools.partial(band_v, g, grp, hh)))
             + _dot_nt(e_parts[1], cvt))
        out.put(hh, o, term)

    heads = []
    for g in range(KV_D):
        for grp in range(n_groups):
            for j in range(GQA // 2):
                p = g * (GQA // 2) + j
                out = _PairSink(y_ref, slice(grp * SWA_QROWS, (grp + 1) * SWA_QROWS), slice(p * PAIR, (p + 1) * PAIR))
                for hh in range(2):
                    sink = sink_ref[2 * p + hh] * LOG2E
                    heads.append(_Head(functools.partial(scores, g, grp, p, hh),
                                       functools.partial(finish, out, g, grp, hh), sink))
    _attend(heads, batch=8)


def _swa_mixer(proj, cache_kt, cache_vt, sink):
    cache = pl.BlockSpec((1, 1, KV_D, DH, PAST_LEN), lambda b: (b, 0, 0, 0, 0))
    return pl.pallas_call(
        _swa_kernel,
        grid=(DEC_BATCH,),
        in_specs=[pl.BlockSpec((DEC_SEQ, 512), lambda b: (b, 3)),
                  pl.BlockSpec((DEC_SEQ, 512), lambda b: (b, 4)),
                  cache, cache,
                  pl.BlockSpec(memory_space=pltpu.SMEM)],
        out_specs=pl.BlockSpec((DEC_SEQ, 512), lambda b: (b, 0)),
        out_shape=jax.ShapeDtypeStruct((LAT_TOKENS, 512), BF16),
        compiler_params=_cparams(1),
        name="swa_mixer",
    )(proj, proj, cache_kt, cache_vt, sink)


class _WeightStream:
    def __init__(self, li, wo_hbm, w1_hbm, w2_hbm, wo_s, w1_s, w2_s, stage_row, stage_col, sems):
        rows, cols = slice(None), slice(None)
        self.chunks, self.marks = [], {}
        for j in range(D_MODEL // W_ROWS):
            r = slice(j * W_ROWS, (j + 1) * W_ROWS)
            self.chunks.append(("row", wo_hbm.at[li, r, :], wo_s, (r, cols)))
        self.marks["w_out"] = len(self.chunks)
        for c in range(D_FF // FF_CHUNK):
            for j in range(FF_CHUNK // W_COLS):
                k = slice(c * FF_CHUNK + j * W_COLS, c * FF_CHUNK + (j + 1) * W_COLS)
                self.chunks.append(("col", w1_hbm.at[li, :, k], w1_s, (rows, k)))
            self.marks[("w1", c)] = len(self.chunks)
            for j in range(FF_CHUNK // W_ROWS):
                r = slice(c * FF_CHUNK + j * W_ROWS, c * FF_CHUNK + (j + 1) * W_ROWS)
                self.chunks.append(("row", w2_hbm.at[li, r, :], w2_s, (r, cols)))
            self.marks[("w2", c)] = len(self.chunks)
        self.stage = {"row": (stage_row, 0), "col": (stage_col, STAGE_SLOTS)}
        self.sems = sems
        self.by_kind = {kind: [i for i, ch in enumerate(self.chunks) if ch[0] == kind] for kind in self.stage}
        self.done = 0
        for kind in self.stage:
            for i in self.by_kind[kind][:STAGE_SLOTS]:
                self._copy(i).start()

    def _slot(self, i):
        kind = self.chunks[i][0]
        return self.by_kind[kind].index(i) % STAGE_SLOTS

    def _copy(self, i):
        kind, src, _, _ = self.chunks[i]
        stage, sem0 = self.stage[kind]
        slot = self._slot(i)
        return pltpu.make_async_copy(src, stage.at[slot], self.sems.at[sem0 + slot])

    def need(self, tag):
        while self.done < self.marks[tag]:
            i = self.done
            kind, _, dst, index = self.chunks[i]
            self._copy(i).wait()
            dst[index] = self.stage[kind][0][self._slot(i)].astype(BF16)
            later = self.by_kind[kind]
            nxt = later.index(i) + STAGE_SLOTS
            if nxt < len(later):
                self._copy(later[nxt]).start()
            self.done += 1


def _post_mixer_kernel(*refs, li, n_lat_mix, n_x, n_out):
    mix_ctx_ref, mix_lat_refs = refs[0], refs[1:1 + n_lat_mix]
    refs = refs[1 + n_lat_mix:]
    x_refs = refs[:n_x]
    mod_ref, gpost_ref, gpre_ref, gmlp_ref, wo_hbm, w1_hbm, w2_hbm = refs[n_x:7 + n_x]
    out_refs = refs[7 + n_x:7 + n_x + n_out]
    wo_s, w1_s, w2_s, stage_row, stage_col, sems = refs[7 + n_x + n_out:]
    step = pl.program_id(0)
    is_ctx = step < CTX_TILES

    def tile(stream, rows):
        need = (lambda tag: None) if stream is None else stream.need
        n_groups = len(rows)
        mod = mod_ref[0, pl.ds(jnp.where(is_ctx, CTX_ROW, (step - CTX_TILES) // (DEC_SEQ // TM)), 1), :]
        g_post, g_pre, g_mlp = gpost_ref[li:li + 1, :], gpre_ref[li:li + 1, :], gmlp_ref[li:li + 1, :]

        def residual(r):
            if n_x == 1:
                return x_refs[0][r, :]
            return jnp.where(is_ctx, x_refs[0][r, :], x_refs[1][r, :])

        def mixed(r):
            lat = [m[r, :] for m in mix_lat_refs]
            lat = lat[0] if n_lat_mix == 1 else jnp.concatenate(lat, axis=1)
            return jnp.where(is_ctx, mix_ctx_ref[r, :], lat)

        need("w_out")
        ys = [_dot(mixed(r), wo_s[...]) for r in rows]
        x1 = [residual(r) + mod[:, 2 * D_MODEL:3 * D_MODEL] * _rms(y, g_post) for r, y in zip(rows, ys)]
        h = [_norm_mod(x, g_pre, mod, 3 * D_MODEL).astype(BF16) for x in x1]
        acc = [None] * n_groups
        pending = []

        def second_matmul(c, i, f):
            need(("w2", c))
            t2 = _dot(f, w2_s[c * FF_CHUNK:(c + 1) * FF_CHUNK, :])
            acc[i] = t2 if acc[i] is None else acc[i] + t2

        for c in range(D_FF // FF_CHUNK):
            need(("w1", c))
            for i in range(n_groups):
                f = _dot(h[i], w1_s[:, c * FF_CHUNK:(c + 1) * FF_CHUNK])
                pending.append((c, i, jnp.square(jnp.maximum(f, 0.0)).astype(BF16)))
                if len(pending) > 1:
                    second_matmul(*pending.pop(0))
        second_matmul(*pending.pop(0))
        res = [x1[i] + mod[:, 5 * D_MODEL:6 * D_MODEL] * _rms(acc[i], g_mlp) for i in range(n_groups)]
        if n_out == 1:
            for i, r in enumerate(rows):
                out_refs[0][r, :] = res[i]
        else:
            @pl.when(is_ctx)
            def _():
                for i, r in enumerate(rows):
                    out_refs[0][r, :] = res[i]

            @pl.when(jnp.logical_not(is_ctx))
            def _():
                for i, r in enumerate(rows):
                    out_refs[1][r, :] = res[i]

    groups = [slice(i * (TM // POST_SPLIT), (i + 1) * (TM // POST_SPLIT)) for i in range(POST_SPLIT)]

    @pl.when(step == 0)
    def _():
        stream = _WeightStream(li, wo_hbm, w1_hbm, w2_hbm, wo_s, w1_s, w2_s, stage_row, stage_col, sems)
        tile(stream, groups[:1])
        for g in groups[1:]:
            tile(None, [g])

    @pl.when(step > 0)
    def _():
        tile(None, groups)


def _post_mixer(mix_ctx, mix_lat, xs, mod, g_post, g_pre, g_mlp, w_out, w1, w2, li, *, split_out):
    tile = lambda s: s
    ctx_tile = lambda s: jnp.minimum(tile(s), CTX_TILES - 1)
    lat_tile = lambda s: jnp.maximum(tile(s) - CTX_TILES, 0)
    token_spec = lambda index, width=D_MODEL: pl.BlockSpec((TM, width), lambda s: (index(s), 0))
    mix_specs = [token_spec(ctx_tile)] + [token_spec(lat_tile, m.shape[1]) for m in mix_lat]
    x_specs = [token_spec(tile)] if len(xs) == 1 else [token_spec(ctx_tile), token_spec(lat_tile)]
    if split_out:
        out_specs = [token_spec(ctx_tile), token_spec(lat_tile)]
        out_shape = [jax.ShapeDtypeStruct((CTX_TOKENS, D_MODEL), F32), jax.ShapeDtypeStruct((LAT_TOKENS, D_MODEL), F32)]
    else:
        out_specs = [token_spec(tile)]
        out_shape = [jax.ShapeDtypeStruct((ALL_TOKENS, D_MODEL), F32)]
    gain = pl.BlockSpec((DEPTH, D_MODEL), lambda s: (0, 0))
    return pl.pallas_call(
        functools.partial(_post_mixer_kernel, li=li, n_lat_mix=len(mix_lat), n_x=len(xs), n_out=len(out_specs)),
        grid=(ALL_TOKENS // TM,),
        in_specs=mix_specs + x_specs + [
            pl.BlockSpec((1, MOD_ROWS, 6 * D_MODEL), lambda s: (li, 0, 0)),
            gain, gain, gain] + [pl.BlockSpec(memory_space=pl.ANY)] * 3,
        out_specs=out_specs,
        out_shape=out_shape,
        scratch_shapes=[pltpu.VMEM((D_MODEL, D_MODEL), BF16), pltpu.VMEM((D_MODEL, D_FF), BF16),
                        pltpu.VMEM((D_FF, D_MODEL), BF16),
                        pltpu.VMEM((STAGE_SLOTS, W_ROWS, D_MODEL), F32), pltpu.VMEM((STAGE_SLOTS, D_MODEL, W_COLS), F32),
                        pltpu.SemaphoreType.DMA((2 * STAGE_SLOTS,))],
        compiler_params=_cparams(1),
        name="post_mixer",
    )(mix_ctx, *mix_lat, *xs, mod, g_post, g_pre, g_mlp, w_out, w1, w2)


def _rope_tables():
    t = jnp.arange(DEC_SEQ)
    rows = (t // GRID_W).astype(F32)
    cols = (t % GRID_W).astype(F32)
    q4 = DH // 4
    inv = 1.0 / (ROPE_BASE ** (jnp.arange(q4, dtype=F32) / q4))
    ar, ac = rows[:, None] * inv, cols[:, None] * inv
    cos64 = jnp.concatenate([jnp.cos(ar), jnp.cos(ar), jnp.cos(ac), jnp.cos(ac)], axis=1)
    sin64 = jnp.concatenate([-jnp.sin(ar), jnp.sin(ar), -jnp.sin(ac), jnp.sin(ac)], axis=1)
    return jnp.tile(cos64, (1, 2)), jnp.tile(sin64, (1, 2))


def kernel(x_prompt, x_sample, cache_diff_k, cache_diff_v, cache_na_k, cache_na_v, cache_swa_k, cache_swa_v, c, c_ctx, mod_w, mod_b, norm_mix_pre, norm_mix_post, norm_mlp_pre, norm_mlp_post, w_in_even, conv_w, lambda_q1, lambda_k1, lambda_q2, lambda_k2, subln, w_in_odd, rpb, sink, w_out, mlp_w1, mlp_w2):
    cond = jnp.concatenate([c, c_ctx[None, :], jnp.zeros((MOD_ROWS - DEC_BATCH - 1, D_MODEL), F32)], axis=0)
    mod = _modulation(cond, mod_w, mod_b)
    rope = _rope_tables()

    t_minor = lambda a: jnp.swapaxes(a, -1, -2)

    w_in = [w_in_even.astype(BF16), w_in_odd.astype(BF16)]

    xp = x_prompt.reshape(CTX_TOKENS, D_MODEL)
    xs = x_sample.reshape(LAT_TOKENS, D_MODEL)
    lam_init0 = 0.8 - 0.6 * math.exp(-0.3 * 0)
    lam_args = (conv_w, lambda_q1, lambda_k1, lambda_q2, lambda_k2, subln)
    sink_flat = sink.reshape(-1)

    def post(li, mix_ctx, mix_lat, x_parts, split_out):
        return _post_mixer(mix_ctx, mix_lat, x_parts, mod, norm_mix_post, norm_mlp_pre, norm_mlp_post, w_out, mlp_w1,
                           mlp_w2, li, split_out=split_out)

    proj, new_diff_kt, new_diff_v = _pre_mixer(xp, mod, norm_mix_pre, w_in[0], 0, latent=False, even=True, rope=None,
                                               n_tok=CTX_TOKENS)
    mix_ctx = _even_mixer(proj, *lam_args, new_diff_kt, new_diff_v, seq=SEQ, n_seq=CTX_SEQS_PER_STEP // 2,
                          own_keys=False, lam_init=lam_init0)
    proj = _pre_mixer(xs, mod, norm_mix_pre, w_in[0], 0, latent=True, even=True, rope=rope, n_tok=LAT_TOKENS)
    mix_lat = _even_mixer(proj, *lam_args, t_minor(cache_diff_k), cache_diff_v, seq=DEC_SEQ, n_seq=1, own_keys=True,
                          lam_init=lam_init0)
    (x_all,) = post(0, mix_ctx, [mix_lat], [xp, xs], False)

    q, new_na_kt, new_na_vt, new_swa_kt, new_swa_vt = _pre_mixer(
        x_all, mod, norm_mix_pre, w_in[1], 1, latent=False, even=False, rope=None, n_tok=CTX_TOKENS)
    mix_ctx = _odd_ctx_mixer(q, new_na_kt, new_na_vt, new_swa_kt, new_swa_vt, sink_flat)
    proj = _pre_mixer(x_all, mod, norm_mix_pre, w_in[1], 1, latent=True, even=False, rope=rope, n_tok=LAT_TOKENS,
                      tile0=CTX_TILES)
    mix_c = _na_mixer(proj, t_minor(cache_na_k), t_minor(cache_na_v), rpb.reshape(-1))
    mix_d = _swa_mixer(proj, t_minor(cache_swa_k), t_minor(cache_swa_v), sink_flat)
    xp, xs = post(1, mix_ctx, [mix_c, mix_d], [x_all], True)

    return (xp.reshape(BATCH, SEQ, D_MODEL), xs.reshape(DEC_BATCH, DEC_SEQ, D_MODEL),
            t_minor(new_diff_kt), new_diff_v, t_minor(new_na_kt), t_minor(new_na_vt),
            t_minor(new_swa_kt), t_minor(new_swa_vt))
```

```python
import functools
import math

import jax
import jax.numpy as jnp
from jax import lax
from jax.experimental import pallas as pl
from jax.experimental.pallas import tpu as pltpu

F32 = jnp.float32
BF16 = jnp.bfloat16

D_MODEL = 1024
BATCH = 32
SEQ = 256
DEPTH = 2
DEC_BATCH = 8
DEC_SEQ = 1024
PAST_LEN = 256
GRID_W = 64
GRID_H = DEC_SEQ // GRID_W
MIX_A = 512
H_B = 4
DH = 64
PAIR = 2 * DH
H_C = 8
H_D = 8
KV_D = 2
GQA = H_D // KV_D
IN_EVEN = 3072
IN_ODD = 2304
D_FF = 4096
NA_WIN_R = 8
NA_WIN_C = 16
SWA_BLOCK = 128
SWA_WINDOW = 128
ROPE_BASE = 10000.0
EPS = 1e-6
NEG = -1e30
LOG2E = math.log2(math.e)
Q_SCALE = DH ** -0.5 * LOG2E

MOD_ROWS = 16
CTX_ROW = DEC_BATCH
V7X_VMEM_BYTES = 64 * 1024 * 1024
VMEM_LIMIT = V7X_VMEM_BYTES - 12 * 1024 * 1024

CTX_TOKENS = BATCH * SEQ
LAT_TOKENS = DEC_BATCH * DEC_SEQ
ALL_TOKENS = CTX_TOKENS + LAT_TOKENS

TM = 512
TM_PRE = 1024
FF_CHUNK = 1024
POST_SPLIT = 2
WCHUNK_ELEMS = 256 * 1024
WO_ROWS, W1_ROWS, W2_ROWS = WCHUNK_ELEMS // D_MODEL, WCHUNK_ELEMS // D_FF, WCHUNK_ELEMS // D_MODEL
N_WO, N_W1, N_W2 = D_MODEL // WO_ROWS, D_MODEL // W1_ROWS, D_FF // W2_ROWS
STAGE_SLOTS = 2
CTX_TILES = CTX_TOKENS // TM
CTX_SEQS_PER_STEP = 4
MOD_TN = 1536
NA_SEQS_PER_STEP = 2
NA_HALF = 512
NA_KEYS = 768
SWA_QROWS = 2 * SWA_BLOCK
SWA_BAND = 4 * SWA_BLOCK
LAT_ODD_COLS = IN_ODD + 2 * PAIR


def _cparams(n_axes):
    return pltpu.CompilerParams(dimension_semantics=("arbitrary",) * n_axes, vmem_limit_bytes=VMEM_LIMIT)


def _rms(x, g):
    return x * lax.rsqrt(jnp.mean(x * x, axis=-1, keepdims=True) + EPS) * g


def _dot(a, b):
    return jnp.dot(a, b, preferred_element_type=F32)


def _dot_nt(a, b):
    return lax.dot_general(a, b, (((1,), (1,)), ((), ())), preferred_element_type=F32)


def _layer_block(shape, li):
    zeros = (0,) * len(shape)
    return pl.BlockSpec((1,) + tuple(shape), lambda *_: (li,) + zeros, pipeline_mode=pl.Buffered(1))


def _low_half(shape):
    return (lax.broadcasted_iota(jnp.int32, shape, len(shape) - 1) & DH) == 0


def _keep_mask_bf16(rows, hh):
    lane = lax.broadcasted_iota(jnp.int32, (rows, PAIR), 1)
    half = (lane & DH).astype(F32).astype(BF16)
    return (half == 0) if hh == 0 else (half != 0)


def _pad_rows(t, hh, fill):
    other = jnp.full(t.shape, fill, t.dtype)
    return jnp.concatenate([t, other] if hh == 0 else [other, t], axis=0)


class _Memo(dict):
    def get_or_build(self, key, build):
        if key not in self:
            self[key] = build()
        return self[key]


def _merge_pair(o0, o1, extra_den=None):
    low = _low_half(o0.shape)
    num = jnp.where(low, o0, o1)
    den = pltpu.roll(jnp.where(low, o1, o0), DH, axis=1)
    if extra_den is not None:
        den = den + jnp.where(low, extra_den[0], extra_den[1])
    return num * (1.0 / den)


class _Head:
    def __init__(self, scores, finish, sink=None):
        self.scores, self.finish, self.sink = scores, finish, sink


def _attend(heads, batch):
    batches = [heads[i:i + batch] for i in range(0, len(heads), batch)]
    cur = [hd.scores() for hd in batches[0]]
    for bi, group in enumerate(batches):
        nxt = [hd.scores() for hd in batches[bi + 1]] if bi + 1 < len(batches) else None
        exps = []
        for hd, parts in zip(group, cur):
            chunks = [s[:, j * 128:(j + 1) * 128] for s in parts for j in range(s.shape[1] // 128)]
            m_el = functools.reduce(jnp.maximum, chunks)
            if hd.sink is not None:
                m_el = jnp.maximum(m_el, hd.sink)
            mx = jnp.broadcast_to(jnp.max(m_el, axis=-1, keepdims=True), m_el.shape)
            term = None if hd.sink is None else jnp.exp2(hd.sink - mx)
            e_parts = [jnp.concatenate([jnp.exp2(s[:, j * 128:(j + 1) * 128] - mx)
                                        for j in range(s.shape[1] // 128)], axis=1).astype(BF16) for s in parts]
            exps.append((e_parts, term))
        for hd, (e_parts, term) in zip(group, exps):
            hd.finish(e_parts, term)
        cur = nxt


class _PairSink:
    def __init__(self, y_ref, rows, cols):
        self.y_ref, self.rows, self.cols, self.first = y_ref, rows, cols, None

    def put(self, hh, o, term):
        if hh == 0:
            self.first = (o, term)
        else:
            o0, term0 = self.first
            extra = None if term is None else (term0, term)
            self.y_ref[self.rows, self.cols] = _merge_pair(o0, o, extra).astype(BF16)


def _mod_kernel(cond_ref, w_ref, b_ref, o_ref):
    c = cond_ref[...]
    s = c * (1.0 / (1.0 + jnp.exp(-c)))
    o_ref[0] = _dot(s.astype(BF16), w_ref[0].astype(BF16)) + b_ref[pl.ds(pl.program_id(0), 1), :]


def _modulation(cond, mod_w, mod_b):
    return pl.pallas_call(
        _mod_kernel,
        grid=(DEPTH, 6 * D_MODEL // MOD_TN),
        in_specs=[
            pl.BlockSpec((MOD_ROWS, D_MODEL), lambda l, j: (0, 0)),
            pl.BlockSpec((1, D_MODEL, MOD_TN), lambda l, j: (l, 0, j)),
            pl.BlockSpec((DEPTH, MOD_TN), lambda l, j: (0, j)),
        ],
        out_specs=pl.BlockSpec((1, MOD_ROWS, MOD_TN), lambda l, j: (l, 0, j)),
        out_shape=jax.ShapeDtypeStruct((DEPTH, MOD_ROWS, 6 * D_MODEL), F32),
        compiler_params=_cparams(2),
        name="adaln_modulation",
    )(cond, mod_w, mod_b)


def _norm_mod(x, g, mod, shift_col):
    sh = mod[:, shift_col:shift_col + D_MODEL]
    sc = mod[:, shift_col + D_MODEL:shift_col + 2 * D_MODEL]
    return _rms(x, g) * (1.0 + sc) + sh


def _rope(z, cos, sin):
    outs = []
    for j in range(z.shape[1] // 128):
        zj = z[:, j * 128:(j + 1) * 128]
        lane = lax.broadcasted_iota(jnp.int32, zj.shape, 1)
        partner = jnp.where((lane & 16) == 0, pltpu.roll(zj, 128 - 16, axis=1), pltpu.roll(zj, 16, axis=1))
        outs.append(zj * cos + partner * sin)
    return outs[0] if len(outs) == 1 else jnp.concatenate(outs, axis=1)


def _store_heads_transposed(dst, bi, pc_rows, n_pairs):
    for p in range(n_pairs):
        t = pc_rows[:, p * PAIR:(p + 1) * PAIR].T
        dst[bi, 0, 2 * p] = t[0:DH]
        dst[bi, 0, 2 * p + 1] = t[DH:PAIR]


def _mod_row(mod_ref, latent):
    if latent:
        return mod_ref[0, pl.ds(pl.program_id(0) // (DEC_SEQ // TM_PRE), 1), :]
    return mod_ref[0, CTX_ROW:CTX_ROW + 1, :]


def _normed_groups(x_ref, mod_ref, g_ref, li, latent):
    rows = [slice(i * SEQ, (i + 1) * SEQ) for i in range(TM_PRE // SEQ)]
    mod, g = _mod_row(mod_ref, latent), g_ref[li:li + 1, :]
    return rows, [_norm_mod(x_ref[r, :], g, mod, 0).astype(BF16) for r in rows]


def _k1_ctx_even(x_ref, mod_ref, g_ref, w_ref, proj_ref, dkt_ref, dv_ref, *, li):
    rows, hs = _normed_groups(x_ref, mod_ref, g_ref, li, False)
    for c in range(IN_EVEN // 512):
        for bi, (r, h) in enumerate(zip(rows, hs)):
            pc = _dot(h, w_ref[0, :, c * 512:(c + 1) * 512])
            if c < 3:
                proj_ref[r, c * 512:(c + 1) * 512] = pc.astype(BF16)
            elif c == 3:
                proj_ref[r, c * 512:(c + 1) * 512] = (pc * Q_SCALE).astype(BF16)
            elif c == 4:
                for hh in range(H_B):
                    t = pc[:, hh * PAIR:(hh + 1) * PAIR].T
                    dkt_ref[bi, 0, hh, 0] = t[0:DH]
                    dkt_ref[bi, 0, hh, 1] = t[DH:PAIR]
            else:
                for hh in range(H_B):
                    dv_ref[bi, 0, hh] = pc[:, hh * PAIR:(hh + 1) * PAIR]


def _k1_lat_even(x_ref, mod_ref, g_ref, w_ref, cos_ref, sin_ref, proj_ref, *, li):
    rows, hs = _normed_groups(x_ref, mod_ref, g_ref, li, True)
    for c in range(IN_EVEN // 512):
        for r, h in zip(rows, hs):
            pc = _dot(h, w_ref[0, :, c * 512:(c + 1) * 512])
            if c == 3:
                pc = _rope(pc * Q_SCALE, cos_ref[r, :], sin_ref[r, :])
            elif c == 4:
                pc = _rope(pc, cos_ref[r, :], sin_ref[r, :])
            proj_ref[r, c * 512:(c + 1) * 512] = pc.astype(BF16)


def _k1_ctx_odd(x_ref, mod_ref, g_ref, w_ref, q_ref, nkt_ref, nvt_ref, skt_ref, svt_ref, *, li):
    rows, hs = _normed_groups(x_ref, mod_ref, g_ref, li, False)
    for c in range(4):
        for bi, (r, h) in enumerate(zip(rows, hs)):
            pc = _dot(h, w_ref[0, :, c * 512:(c + 1) * 512])
            if c == 0 or c == 3:
                q_ref[r, (c // 3) * 512:(c // 3 + 1) * 512] = (pc * Q_SCALE).astype(BF16)
            else:
                _store_heads_transposed(nkt_ref if c == 1 else nvt_ref, bi, pc, H_C // 2)
    for bi, h in enumerate(hs):
        pc = _dot(h, w_ref[0, :, 2048:IN_ODD])
        _store_heads_transposed(skt_ref, bi, pc[:, 0:PAIR], 1)
        _store_heads_transposed(svt_ref, bi, pc[:, PAIR:2 * PAIR], 1)


def _k1_lat_odd(x_ref, mod_ref, g_ref, w_ref, cos_ref, sin_ref, proj_ref, *, li):
    rows, hs = _normed_groups(x_ref, mod_ref, g_ref, li, True)
    for c in range(4):
        for r, h in zip(rows, hs):
            pc = _dot(h, w_ref[0, :, c * 512:(c + 1) * 512])
            if c == 0:
                pc = pc * Q_SCALE
            elif c == 3:
                pc = _rope(pc * Q_SCALE, cos_ref[r, :], sin_ref[r, :])
            proj_ref[r, c * 512:(c + 1) * 512] = pc.astype(BF16)
    low = _low_half((SEQ, PAIR))
    for r, h in zip(rows, hs):
        pc = _dot(h, w_ref[0, :, 2048:IN_ODD])
        for j, z in enumerate((_rope(pc[:, 0:PAIR], cos_ref[r, :], sin_ref[r, :]), pc[:, PAIR:2 * PAIR])):
            zr = pltpu.roll(z, DH, axis=1)
            base = 2048 + j * 2 * PAIR
            proj_ref[r, base:base + PAIR] = jnp.where(low, z, zr).astype(BF16)
            proj_ref[r, base + PAIR:base + 2 * PAIR] = jnp.where(low, zr, z).astype(BF16)


def _pre_mixer(x, mod, gains, w, li, *, latent, even, rope, n_tok, tile0=0):
    n_in = w.shape[2]
    TM = TM_PRE
    tiles_per_seq = DEC_SEQ // TM
    in_specs = [
        pl.BlockSpec((TM, D_MODEL), lambda i: (i + tile0, 0)),
        pl.BlockSpec((1, MOD_ROWS, 6 * D_MODEL), lambda i: (li, 0, 0)),
        pl.BlockSpec((DEPTH, D_MODEL), lambda i: (0, 0)),
        _layer_block((D_MODEL, n_in), 0),
    ]
    args = [x, mod, gains, w]
    nb = TM // SEQ
    if latent:
        in_specs += [pl.BlockSpec((TM, 128), lambda i: (i % tiles_per_seq, 0))] * 2
        args += list(rope)
        body = _k1_lat_even if even else _k1_lat_odd
        n_out = IN_EVEN if even else LAT_ODD_COLS
        out_specs = pl.BlockSpec((TM, n_out), lambda i: (i, 0))
        out_shape = jax.ShapeDtypeStruct((n_tok, n_out), BF16)
    elif even:
        body = _k1_ctx_even
        out_specs = [pl.BlockSpec((TM, 4 * MIX_A), lambda i: (i, 0)),
                     pl.BlockSpec((nb, 1, H_B, 2, DH, SEQ), lambda i: (i, 0, 0, 0, 0, 0)),
                     pl.BlockSpec((nb, 1, H_B, SEQ, PAIR), lambda i: (i, 0, 0, 0, 0))]
        out_shape = [jax.ShapeDtypeStruct((n_tok, 4 * MIX_A), BF16),
                     jax.ShapeDtypeStruct((BATCH, 1, H_B, 2, DH, SEQ), F32),
                     jax.ShapeDtypeStruct((BATCH, 1, H_B, SEQ, PAIR), F32)]
    else:
        body = _k1_ctx_odd
        c_spec = pl.BlockSpec((nb, 1, H_C, DH, SEQ), lambda i: (i, 0, 0, 0, 0))
        d_spec = pl.BlockSpec((nb, 1, KV_D, DH, SEQ), lambda i: (i, 0, 0, 0, 0))
        c_shape = jax.ShapeDtypeStruct((BATCH, 1, H_C, DH, SEQ), F32)
        d_shape = jax.ShapeDtypeStruct((BATCH, 1, KV_D, DH, SEQ), F32)
        out_specs = [pl.BlockSpec((TM, D_MODEL), lambda i: (i, 0)), c_spec, c_spec, d_spec, d_spec]
        out_shape = [jax.ShapeDtypeStruct((n_tok, D_MODEL), BF16), c_shape, c_shape, d_shape, d_shape]
    return pl.pallas_call(
        functools.partial(body, li=li),
        grid=(n_tok // TM,),
        in_specs=in_specs,
        out_specs=out_specs,
        out_shape=out_shape,
        compiler_params=_cparams(1),
        name=f"pre_mixer_{'lat' if latent else 'ctx'}_{'even' if even else 'odd'}",
    )(*args)


def _even_mixer_kernel(*refs, seq, n_seq, tq, own_keys, lam_init):
    proj_ref, cw_ref, lq1_ref, lk1_ref, lq2_ref, lk2_ref, subln_ref, ckt_ref, cv_ref = refs[:9]
    y_ref = refs[-1]
    n_rows = n_seq * seq

    pos = lax.broadcasted_iota(jnp.int32, (n_rows, 128), 0) % seq
    for j in range(MIX_A // 128):
        cols = slice(j * 128, (j + 1) * 128)
        a_b = proj_ref[:, j * 128:(j + 1) * 128].astype(F32)
        u = (proj_ref[:, MIX_A + j * 128:MIX_A + (j + 1) * 128].astype(F32)
             * proj_ref[:, 2 * MIX_A + j * 128:2 * MIX_A + (j + 1) * 128].astype(F32))
        u_prev = jnp.where(pos == 0, 0.0, pltpu.roll(u, 1, axis=0))
        u_next = jnp.where(pos == seq - 1, 0.0, pltpu.roll(u, n_rows - 1, axis=0))
        w = cw_ref[0, :, cols]
        y_ref[:, cols] = (a_b * (w[0:1] * u_prev + w[1:2] * u + w[2:3] * u_next)).astype(BF16)

    lam = (jnp.exp(jnp.sum(lq1_ref[...] * lk1_ref[...], axis=-1, keepdims=True))
           - jnp.exp(jnp.sum(lq2_ref[...] * lk2_ref[...], axis=-1, keepdims=True)) + lam_init)
    subln = subln_ref[...]
    q_col, k_col, v_col = 3 * MIX_A, 3 * MIX_A + 512, 3 * MIX_A + 1024
    ones = jnp.ones((PAST_LEN, PAIR), BF16)

    memo = _Memo()

    def own_k(b, h, m):
        kp = proj_ref[b * seq:(b + 1) * seq, k_col + h * PAIR:k_col + (h + 1) * PAIR]
        return jnp.where(_keep_mask_bf16(seq, m), kp, jnp.zeros_like(kp))

    def value_ops(b, h):
        v_ops = []
        if own_keys:
            v_own = proj_ref[b * seq:(b + 1) * seq, v_col + h * PAIR:v_col + (h + 1) * PAIR]
            v_ops.append(jnp.concatenate([v_own, jnp.ones((seq, PAIR), BF16)], axis=1))
        v_ops.append(jnp.concatenate([cv_ref[b, 0, h].astype(BF16), ones], axis=1))
        return v_ops

    def scores(b, qrows, h, m):
        qp = proj_ref[qrows, q_col + h * PAIR:q_col + (h + 1) * PAIR]
        parts = []
        if own_keys:
            parts.append(_dot_nt(qp, memo.get_or_build(("k", b, h, m), functools.partial(own_k, b, h, m))))
        ckt = memo.get_or_build(("ck", b, h, m), lambda: _pad_rows(ckt_ref[b, 0, h, m].astype(BF16), m, 0.0))
        parts.append(_dot(qp, ckt))
        return parts

    first_map = {}

    def pv(b, qrows, h, m, e_parts, _):
        v_ops = memo.get_or_build(("v", b, h), functools.partial(value_ops, b, h))
        o = None
        for e, v_op in zip(e_parts, v_ops):
            t = _dot(e, v_op)
            o = t if o is None else o + t
        attn = o[:, 0:PAIR] * (1.0 / o[:, PAIR:2 * PAIR])
        if m == 0:
            first_map[(qrows.start, h)] = attn
        else:
            y = _rms(first_map.pop((qrows.start, h)) - lam * attn, subln) * (1.0 - lam_init)
            y_ref[qrows, MIX_A + h * PAIR:MIX_A + (h + 1) * PAIR] = y.astype(BF16)

    heads = []
    for b in range(n_seq):
        for i in range(seq // tq):
            qrows = slice(b * seq + i * tq, b * seq + (i + 1) * tq)
            for h in range(H_B):
                for m in range(2):
                    heads.append(_Head(functools.partial(scores, b, qrows, h, m),
                                       functools.partial(pv, b, qrows, h, m)))
    _attend(heads, batch=8)


def _even_mixer(proj, conv_w, lq1, lk1, lq2, lk2, subln, kt, v, *, seq, n_seq, own_keys, lam_init):
    n_tok = proj.shape[0]
    rows = n_seq * seq
    small = lambda a: pl.BlockSpec((1, a.shape[1]), lambda b: (0, 0))
    in_specs = [pl.BlockSpec((rows, proj.shape[1]), lambda b: (b, 0)),
                pl.BlockSpec((1, 3, MIX_A), lambda b: (0, 0, 0)),
                small(lq1), small(lk1), small(lq2), small(lk2), small(subln),
                pl.BlockSpec((n_seq, 1, H_B, 2, DH, PAST_LEN), lambda b: (b, 0, 0, 0, 0, 0)),
                pl.BlockSpec((n_seq, 1, H_B, PAST_LEN, PAIR), lambda b: (b, 0, 0, 0, 0))]
    return pl.pallas_call(
        functools.partial(_even_mixer_kernel, seq=seq, n_seq=n_seq, tq=min(seq, 256), own_keys=own_keys,
                          lam_init=lam_init),
        grid=(n_tok // rows,),
        in_specs=in_specs,
        out_specs=pl.BlockSpec((rows, D_MODEL), lambda b: (b, 0)),
        out_shape=jax.ShapeDtypeStruct((n_tok, D_MODEL), BF16),
        compiler_params=_cparams(1),
        name=f"even_mixer_{'lat' if own_keys else 'ctx'}",
    )(proj, conv_w, lq1, lk1, lq2, lk2, subln, kt, v)


def _odd_ctx_kernel(q_ref, nkt_ref, nvt_ref, skt_ref, svt_ref, sink_ref, y_ref):
    def kv_refs(b, head):
        if head < H_C:
            return nkt_ref.at[b, 0, head], nvt_ref.at[b, 0, head]
        g = (head - H_C) // GQA
        return skt_ref.at[b, 0, g], svt_ref.at[b, 0, g]

    memo = _Memo()

    def kv_key(head):
        return head if head < H_C else H_C + (head - H_C) // GQA

    def scores(b, p, hh):
        head = 2 * p + hh
        kt = memo.get_or_build(("k", b, kv_key(head), hh),
                               lambda: _pad_rows(kv_refs(b, head)[0][...].astype(BF16), hh, 0.0))
        return [_dot(q_ref[b * SEQ:(b + 1) * SEQ, p * PAIR:(p + 1) * PAIR], kt)]

    def finish(out, b, p, hh, e_parts, term):
        head = 2 * p + hh
        vt = memo.get_or_build(("v", b, kv_key(head), hh),
                               lambda: _pad_rows(kv_refs(b, head)[1][...].astype(BF16), hh, 1.0))
        out.put(hh, _dot_nt(e_parts[0], vt), term)

    heads = []
    for b in range(CTX_SEQS_PER_STEP):
        for p in range((H_C + H_D) // 2):
            out = _PairSink(y_ref, slice(b * SEQ, (b + 1) * SEQ), slice(p * PAIR, (p + 1) * PAIR))
            for hh in range(2):
                d_head = 2 * p + hh - H_C
                sink = sink_ref[d_head] * LOG2E if d_head >= 0 else None
                heads.append(_Head(functools.partial(scores, b, p, hh), functools.partial(finish, out, b, p, hh), sink))
    _attend(heads, batch=8)


def _odd_ctx_mixer(q, nkt, nvt, skt, svt, sink):
    n_tok = q.shape[0]
    n_seq = CTX_SEQS_PER_STEP
    c_spec = pl.BlockSpec((n_seq, 1, H_C, DH, SEQ), lambda b: (b, 0, 0, 0, 0))
    d_spec = pl.BlockSpec((n_seq, 1, KV_D, DH, SEQ), lambda b: (b, 0, 0, 0, 0))
    return pl.pallas_call(
        _odd_ctx_kernel,
        grid=(n_tok // (n_seq * SEQ),),
        in_specs=[pl.BlockSpec((n_seq * SEQ, D_MODEL), lambda b: (b, 0)), c_spec, c_spec, d_spec, d_spec,
                  pl.BlockSpec(memory_space=pltpu.SMEM)],
        out_specs=pl.BlockSpec((n_seq * SEQ, D_MODEL), lambda b: (b, 0)),
        out_shape=jax.ShapeDtypeStruct((n_tok, D_MODEL), BF16),
        compiler_params=_cparams(1),
        name="odd_mixer_ctx",
    )(q, nkt, nvt, skt, svt, sink)


def _build_na_bias(rpb_ref, head, nb_ref, slot):
    qc = lax.broadcasted_iota(jnp.int32, (GRID_W, GRID_W), 0)
    kc = lax.broadcasted_iota(jnp.int32, (GRID_W, GRID_W), 1)
    col_start = jnp.clip(qc - NA_WIN_C // 2, 0, GRID_W - NA_WIN_C)
    col_ok = (kc >= col_start) & (kc < col_start + NA_WIN_C)
    dc = kc - qc + NA_WIN_C - 1
    n_dr, n_dc = 2 * NA_WIN_R - 1, 2 * NA_WIN_C - 1
    neg = jnp.full((GRID_W, GRID_W), NEG, F32)
    toeplitz = []
    for dr in range(n_dr):
        t = neg
        for d in range(n_dc):
            t = jnp.where(dc == d, rpb_ref[(head * n_dr + dr) * n_dc + d] * LOG2E, t)
        toeplitz.append(jnp.where(col_ok, t, NEG))
    rows_per_half = NA_HALF // GRID_W
    key_rows = NA_KEYS // GRID_W
    for half in range(2):
        for rl in range(rows_per_half):
            r = half * rows_per_half + rl
            r_start = min(max(r - NA_WIN_R // 2, 0), GRID_H - NA_WIN_R)
            blocks = []
            for j in range(key_rows):
                rk = half * (GRID_H - key_rows) + j
                inside = r_start <= rk < r_start + NA_WIN_R
                blocks.append(toeplitz[rk - r + NA_WIN_R - 1] if inside else neg)
            nb_ref[slot, half, rl * GRID_W:(rl + 1) * GRID_W, :] = jnp.concatenate(blocks, axis=1)


def _na_kernel(rpb_ref, q_ref, k_ref, v_ref, kct_ref, vct_ref, y_ref, nb_ref):
    @pl.when(pl.program_id(1) == 0)
    def _():
        for hh in range(2):
            _build_na_bias(rpb_ref, 2 * pl.program_id(0) + hh, nb_ref, hh)

    def scores(b, half, hh):
        w0 = b * DEC_SEQ + half * (DEC_SEQ - NA_KEYS)
        qp = q_ref[b * DEC_SEQ + half * NA_HALF:b * DEC_SEQ + (half + 1) * NA_HALF, :]
        kw = k_ref[w0:w0 + NA_KEYS, :]
        s1 = _dot_nt(qp, jnp.where(_keep_mask_bf16(NA_KEYS, hh), kw, jnp.zeros_like(kw))) + nb_ref[hh, half]
        s2 = _dot(qp, _pad_rows(kct_ref[b, 0, hh].astype(BF16), hh, 0.0))
        return [s1, s2]

    def finish(out, b, half, hh, e_parts, term):
        w0 = b * DEC_SEQ + half * (DEC_SEQ - NA_KEYS)
        vw = v_ref[w0:w0 + NA_KEYS, :]
        o = (_dot(e_parts[0], jnp.where(_keep_mask_bf16(NA_KEYS, hh), vw, jnp.ones_like(vw)))
             + _dot_nt(e_parts[1], _pad_rows(vct_ref[b, 0, hh].astype(BF16), hh, 1.0)))
        out.put(hh, o, term)

    heads = []
    for b in range(NA_SEQS_PER_STEP):
        for half in range(2):
            rows = slice(b * DEC_SEQ + half * NA_HALF, b * DEC_SEQ + (half + 1) * NA_HALF)
            out = _PairSink(y_ref, rows, slice(0, PAIR))
            for hh in range(2):
                heads.append(_Head(functools.partial(scores, b, half, hh), functools.partial(finish, out, b, half, hh)))
    _attend(heads, batch=4)


def _na_mixer(proj, cache_kt, cache_vt, rpb_flat):
    pairs = H_C // 2
    n_seq = NA_SEQS_PER_STEP
    col = lambda base: pl.BlockSpec((n_seq * DEC_SEQ, PAIR), lambda hp, b: (b, base + hp))
    cache = pl.BlockSpec((n_seq, 1, 2, DH, PAST_LEN), lambda hp, b: (b, 0, hp, 0, 0))
    return pl.pallas_call(
        _na_kernel,
        grid=(pairs, DEC_BATCH // n_seq),
        in_specs=[pl.BlockSpec(memory_space=pltpu.SMEM), col(0), col(pairs), col(2 * pairs), cache, cache],
        out_specs=pl.BlockSpec((n_seq * DEC_SEQ, PAIR), lambda hp, b: (b, hp)),
        out_shape=jax.ShapeDtypeStruct((LAT_TOKENS, 512), BF16),
        scratch_shapes=[pltpu.VMEM((2, 2, NA_HALF, NA_KEYS), F32)],
        compiler_params=_cparams(2),
        name="na_mixer",
    )(rpb_flat, proj, proj, proj, cache_kt, cache_vt)


def _swa_kernel(q_ref, kv_ref, kct_ref, vct_ref, sink_ref, y_ref):
    n_groups = DEC_SEQ // SWA_QROWS
    ql = lax.broadcasted_iota(jnp.int32, (SWA_QROWS, SWA_BAND), 0)
    kj = lax.broadcasted_iota(jnp.int32, (SWA_QROWS, SWA_BAND), 1)

    def band_bias(first_key_minus_first_query):
        return jnp.where(jnp.abs(kj + first_key_minus_first_query - ql) <= SWA_WINDOW, 0.0, NEG)

    bias_first, bias_mid, bias_last = band_bias(0), band_bias(-SWA_BLOCK), band_bias(-2 * SWA_BLOCK)

    def band_start(grp):
        return min(max(grp * SWA_QROWS - SWA_BLOCK, 0), DEC_SEQ - SWA_BAND)

    memo = _Memo()

    def band_k(g, grp, hh):
        start = band_start(grp)
        kb = kv_ref[start:start + SWA_BAND, g * PAIR:(g + 1) * PAIR]
        return jnp.where(_keep_mask_bf16(SWA_BAND, hh), kb, jnp.zeros_like(kb))

    def band_v(g, grp, hh):
        start = band_start(grp)
        vb = kv_ref[start:start + SWA_BAND, (KV_D + g) * PAIR:(KV_D + g + 1) * PAIR]
        return jnp.where(_keep_mask_bf16(SWA_BAND, hh), vb, jnp.ones_like(vb))

    def scores(g, grp, p, hh):
        bias = bias_first if grp == 0 else (bias_last if grp == n_groups - 1 else bias_mid)
        qp = q_ref[grp * SWA_QROWS:(grp + 1) * SWA_QROWS, p * PAIR:(p + 1) * PAIR]
        s1 = _dot_nt(qp, memo.get_or_build(("k", g, grp, hh), functools.partial(band_k, g, grp, hh))) + bias
        ckt = memo.get_or_build(("ck", g, hh), lambda: _pad_rows(kct_ref[0, 0, g].astype(BF16), hh, 0.0))
        return [s1, _dot(qp, ckt)]

    def finish(out, g, grp, hh, e_parts, term):
        cvt = memo.get_or_build(("cv", g, hh), lambda: _pad_rows(vct_ref[0, 0, g].astype(BF16), hh, 1.0))
        o = (_dot(e_parts[0], memo.get_or_build(("v", g, grp, hh), functools.partial(band_v, g, grp, hh)))
             + _dot_nt(e_parts[1], cvt))
        out.put(hh, o, term)

    heads = []
    for g in range(KV_D):
        for grp in range(n_groups):
            for j in range(GQA // 2):
                p = g * (GQA // 2) + j
                out = _PairSink(y_ref, slice(grp * SWA_QROWS, (grp + 1) * SWA_QROWS), slice(p * PAIR, (p + 1) * PAIR))
                for hh in range(2):
                    sink = sink_ref[2 * p + hh] * LOG2E
                    heads.append(_Head(functools.partial(scores, g, grp, p, hh),
                                       functools.partial(finish, out, g, grp, hh), sink))
    _attend(heads, batch=8)


def _swa_mixer(proj, cache_kt, cache_vt, sink):
    cache = pl.BlockSpec((1, 1, KV_D, DH, PAST_LEN), lambda b: (b, 0, 0, 0, 0))
    return pl.pallas_call(
        _swa_kernel,
        grid=(DEC_BATCH,),
        in_specs=[pl.BlockSpec((DEC_SEQ, 512), lambda b: (b, 3)),
                  pl.BlockSpec((DEC_SEQ, 512), lambda b: (b, 4)),
                  cache, cache,
                  pl.BlockSpec(memory_space=pltpu.SMEM)],
        out_specs=pl.BlockSpec((DEC_SEQ, 512), lambda b: (b, 0)),
        out_shape=jax.ShapeDtypeStruct((LAT_TOKENS, 512), BF16),
        compiler_params=_cparams(1),
        name="swa_mixer",
    )(proj, proj, cache_kt, cache_vt, sink)


def _load_weights_bf16(li, wo_hbm, w1_hbm, w2_hbm, wo_s, w1_s, w2_s, stage_sq, stage_wide, sems):
    square = ([(wo_hbm, wo_s, j, WO_ROWS) for j in range(N_WO)] + [(w2_hbm, w2_s, j, W2_ROWS) for j in range(N_W2)])
    wide = [(w1_hbm, w1_s, j, W1_ROWS) for j in range(N_W1)]
    queues = {"sq": (square, stage_sq, 0), "wide": (wide, stage_wide, STAGE_SLOTS)}

    def copy(kind, idx):
        chunks, stage, sem0 = queues[kind]
        src, _, j, n_rows = chunks[idx]
        slot = idx % STAGE_SLOTS
        return pltpu.make_async_copy(src.at[li, j * n_rows:(j + 1) * n_rows, :], stage.at[slot], sems.at[sem0 + slot])

    for kind in queues:
        for idx in range(min(STAGE_SLOTS, len(queues[kind][0]))):
            copy(kind, idx).start()
    order = []
    for i in range(max(len(square), len(wide))):
        order += [("sq", i)] * (i < len(square)) + [("wide", i)] * (i < len(wide))
    for kind, idx in order:
        chunks, stage, _ = queues[kind]
        _, dst, j, n_rows = chunks[idx]
        copy(kind, idx).wait()
        dst[j * n_rows:(j + 1) * n_rows, :] = stage[idx % STAGE_SLOTS].astype(BF16)
        if idx + STAGE_SLOTS < len(chunks):
            copy(kind, idx + STAGE_SLOTS).start()


def _post_mixer_kernel(*refs, li, n_lat_mix, n_x, n_out):
    mix_ctx_ref, mix_lat_refs = refs[0], refs[1:1 + n_lat_mix]
    refs = refs[1 + n_lat_mix:]
    x_refs = refs[:n_x]
    mod_ref, gpost_ref, gpre_ref, gmlp_ref, wo_hbm, w1_hbm, w2_hbm = refs[n_x:7 + n_x]
    out_refs = refs[7 + n_x:7 + n_x + n_out]
    wo_s, w1_s, w2_s, stage_sq, stage_wide, sems = refs[7 + n_x + n_out:]
    step = pl.program_id(0)

    @pl.when(step == 0)
    def _():
        _load_weights_bf16(li, wo_hbm, w1_hbm, w2_hbm, wo_s, w1_s, w2_s, stage_sq, stage_wide, sems)

    is_ctx = step < CTX_TILES
    mod = mod_ref[0, pl.ds(jnp.where(is_ctx, CTX_ROW, (step - CTX_TILES) // (DEC_SEQ // TM)), 1), :]
    g_post, g_pre, g_mlp = gpost_ref[li:li + 1, :], gpre_ref[li:li + 1, :], gmlp_ref[li:li + 1, :]
    rows = [slice(i * (TM // POST_SPLIT), (i + 1) * (TM // POST_SPLIT)) for i in range(POST_SPLIT)]

    def residual(r):
        if n_x == 1:
            return x_refs[0][r, :]
        return jnp.where(is_ctx, x_refs[0][r, :], x_refs[1][r, :])

    def mixed(r):
        lat = [m[r, :] for m in mix_lat_refs]
        lat = lat[0] if n_lat_mix == 1 else jnp.concatenate(lat, axis=1)
        return jnp.where(is_ctx, mix_ctx_ref[r, :], lat)

    ys = [_dot(mixed(r), wo_s[...]) for r in rows]
    x1 = [residual(r) + mod[:, 2 * D_MODEL:3 * D_MODEL] * _rms(y, g_post) for r, y in zip(rows, ys)]
    h = [_norm_mod(x, g_pre, mod, 3 * D_MODEL).astype(BF16) for x in x1]
    acc = [None] * POST_SPLIT
    pending = []

    def second_matmul(c, i, f):
        t2 = _dot(f, w2_s[c * FF_CHUNK:(c + 1) * FF_CHUNK, :])
        acc[i] = t2 if acc[i] is None else acc[i] + t2

    for c in range(D_FF // FF_CHUNK):
        for i in range(POST_SPLIT):
            f = _dot(h[i], w1_s[:, c * FF_CHUNK:(c + 1) * FF_CHUNK])
            pending.append((c, i, jnp.square(jnp.maximum(f, 0.0)).astype(BF16)))
            if len(pending) > 1:
                second_matmul(*pending.pop(0))
    second_matmul(*pending.pop(0))
    res = [x1[i] + mod[:, 5 * D_MODEL:6 * D_MODEL] * _rms(acc[i], g_mlp) for i in range(POST_SPLIT)]
    if n_out == 1:
        for i, r in enumerate(rows):
            out_refs[0][r, :] = res[i]
    else:
        @pl.when(is_ctx)
        def _():
            for i, r in enumerate(rows):
                out_refs[0][r, :] = res[i]

        @pl.when(jnp.logical_not(is_ctx))
        def _():
            for i, r in enumerate(rows):
                out_refs[1][r, :] = res[i]


def _post_mixer(mix_ctx, mix_lat, xs, mod, g_post, g_pre, g_mlp, w_out, w1, w2, li, *, split_out):
    tile = lambda s: s
    ctx_tile = lambda s: jnp.minimum(tile(s), CTX_TILES - 1)
    lat_tile = lambda s: jnp.maximum(tile(s) - CTX_TILES, 0)
    token_spec = lambda index, width=D_MODEL: pl.BlockSpec((TM, width), lambda s: (index(s), 0))
    mix_specs = [token_spec(ctx_tile)] + [token_spec(lat_tile, m.shape[1]) for m in mix_lat]
    x_specs = [token_spec(tile)] if len(xs) == 1 else [token_spec(ctx_tile), token_spec(lat_tile)]
    if split_out:
        out_specs = [token_spec(ctx_tile), token_spec(lat_tile)]
        out_shape = [jax.ShapeDtypeStruct((CTX_TOKENS, D_MODEL), F32), jax.ShapeDtypeStruct((LAT_TOKENS, D_MODEL), F32)]
    else:
        out_specs = [token_spec(tile)]
        out_shape = [jax.ShapeDtypeStruct((ALL_TOKENS, D_MODEL), F32)]
    gain = pl.BlockSpec((DEPTH, D_MODEL), lambda s: (0, 0))
    return pl.pallas_call(
        functools.partial(_post_mixer_kernel, li=li, n_lat_mix=len(mix_lat), n_x=len(xs), n_out=len(out_specs)),
        grid=(ALL_TOKENS // TM,),
        in_specs=mix_specs + x_specs + [
            pl.BlockSpec((1, MOD_ROWS, 6 * D_MODEL), lambda s: (li, 0, 0)),
            gain, gain, gain] + [pl.BlockSpec(memory_space=pl.ANY)] * 3,
        out_specs=out_specs,
        out_shape=out_shape,
        scratch_shapes=[pltpu.VMEM((D_MODEL, D_MODEL), BF16), pltpu.VMEM((D_MODEL, D_FF), BF16),
                        pltpu.VMEM((D_FF, D_MODEL), BF16),
                        pltpu.VMEM((STAGE_SLOTS, WO_ROWS, D_MODEL), F32), pltpu.VMEM((STAGE_SLOTS, W1_ROWS, D_FF), F32),
                        pltpu.SemaphoreType.DMA((2 * STAGE_SLOTS,))],
        compiler_params=_cparams(1),
        name="post_mixer",
    )(mix_ctx, *mix_lat, *xs, mod, g_post, g_pre, g_mlp, w_out, w1, w2)


def _rope_tables():
    t = jnp.arange(DEC_SEQ)
    rows = (t // GRID_W).astype(F32)
    cols = (t % GRID_W).astype(F32)
    q4 = DH // 4
    inv = 1.0 / (ROPE_BASE ** (jnp.arange(q4, dtype=F32) / q4))
    ar, ac = rows[:, None] * inv, cols[:, None] * inv
    cos64 = jnp.concatenate([jnp.cos(ar), jnp.cos(ar), jnp.cos(ac), jnp.cos(ac)], axis=1)
    sin64 = jnp.concatenate([-jnp.sin(ar), jnp.sin(ar), -jnp.sin(ac), jnp.sin(ac)], axis=1)
    return jnp.tile(cos64, (1, 2)), jnp.tile(sin64, (1, 2))


def kernel(x_prompt, x_sample, cache_diff_k, cache_diff_v, cache_na_k, cache_na_v, cache_swa_k, cache_swa_v, c, c_ctx, mod_w, mod_b, norm_mix_pre, norm_mix_post, norm_mlp_pre, norm_mlp_post, w_in_even, conv_w, lambda_q1, lambda_k1, lambda_q2, lambda_k2, subln, w_in_odd, rpb, sink, w_out, mlp_w1, mlp_w2):
    cond = jnp.concatenate([c, c_ctx[None, :], jnp.zeros((MOD_ROWS - DEC_BATCH - 1, D_MODEL), F32)], axis=0)
    mod = _modulation(cond, mod_w, mod_b)
    rope = _rope_tables()

    t_minor = lambda a: jnp.swapaxes(a, -1, -2)

    w_in = [w_in_even.astype(BF16), w_in_odd.astype(BF16)]

    xp = x_prompt.reshape(CTX_TOKENS, D_MODEL)
    xs = x_sample.reshape(LAT_TOKENS, D_MODEL)
    lam_init0 = 0.8 - 0.6 * math.exp(-0.3 * 0)
    lam_args = (conv_w, lambda_q1, lambda_k1, lambda_q2, lambda_k2, subln)
    sink_flat = sink.reshape(-1)

    def post(li, mix_ctx, mix_lat, x_parts, split_out):
        return _post_mixer(mix_ctx, mix_lat, x_parts, mod, norm_mix_post, norm_mlp_pre, norm_mlp_post, w_out, mlp_w1,
                           mlp_w2, li, split_out=split_out)

    proj, new_diff_kt, new_diff_v = _pre_mixer(xp, mod, norm_mix_pre, w_in[0], 0, latent=False, even=True, rope=None,
                                               n_tok=CTX_TOKENS)
    mix_ctx = _even_mixer(proj, *lam_args, new_diff_kt, new_diff_v, seq=SEQ, n_seq=CTX_SEQS_PER_STEP // 2,
                          own_keys=False, lam_init=lam_init0)
    proj = _pre_mixer(xs, mod, norm_mix_pre, w_in[0], 0, latent=True, even=True, rope=rope, n_tok=LAT_TOKENS)
    mix_lat = _even_mixer(proj, *lam_args, t_minor(cache_diff_k), cache_diff_v, seq=DEC_SEQ, n_seq=1, own_keys=True,
                          lam_init=lam_init0)
    (x_all,) = post(0, mix_ctx, [mix_lat], [xp, xs], False)

    q, new_na_kt, new_na_vt, new_swa_kt, new_swa_vt = _pre_mixer(
        x_all, mod, norm_mix_pre, w_in[1], 1, latent=False, even=False, rope=None, n_tok=CTX_TOKENS)
    mix_ctx = _odd_ctx_mixer(q, new_na_kt, new_na_vt, new_swa_kt, new_swa_vt, sink_flat)
    proj = _pre_mixer(x_all, mod, norm_mix_pre, w_in[1], 1, latent=True, even=False, rope=rope, n_tok=LAT_TOKENS,
                      tile0=CTX_TOKENS // TM_PRE)
    mix_c = _na_mixer(proj, t_minor(cache_na_k), t_minor(cache_na_v), rpb.reshape(-1))
    mix_d = _swa_mixer(proj, t_minor(cache_swa_k), t_minor(cache_swa_v), sink_flat)
    xp, xs = post(1, mix_ctx, [mix_c, mix_d], [x_all], True)

    return (xp.reshape(BATCH, SEQ, D_MODEL), xs.reshape(DEC_BATCH, DEC_SEQ, D_MODEL),
            t_minor(new_diff_kt), new_diff_v, t_minor(new_na_kt), t_minor(new_na_vt),
            t_minor(new_swa_kt), t_minor(new_swa_vt))
```

```python
import functools
import math

import jax
import jax.numpy as jnp
from jax import lax
from jax.experimental import pallas as pl
from jax.experimental.pallas import tpu as pltpu

F32 = jnp.float32
BF16 = jnp.bfloat16

D_MODEL = 1024
BATCH = 32
SEQ = 256
DEPTH = 2
DEC_BATCH = 8
DEC_SEQ = 1024
PAST_LEN = 256
GRID_W = 64
GRID_H = DEC_SEQ // GRID_W
MIX_A = 512
H_B = 4
DH = 64
PAIR = 2 * DH
H_C = 8
H_D = 8
KV_D = 2
GQA = H_D // KV_D
IN_EVEN = 3072
IN_ODD = 2304
D_FF = 4096
NA_WIN_R = 8
NA_WIN_C = 16
SWA_BLOCK = 128
SWA_WINDOW = 128
ROPE_BASE = 10000.0
EPS = 1e-6
NEG = -1e30
LOG2E = math.log2(math.e)
Q_SCALE = DH ** -0.5 * LOG2E

MOD_ROWS = 16
CTX_ROW = DEC_BATCH
V7X_VMEM_BYTES = 64 * 1024 * 1024
VMEM_LIMIT = V7X_VMEM_BYTES - 12 * 1024 * 1024

CTX_TOKENS = BATCH * SEQ
LAT_TOKENS = DEC_BATCH * DEC_SEQ
ALL_TOKENS = CTX_TOKENS + LAT_TOKENS

TM = 512
TM_PRE = 1024
FF_CHUNK = 1024
POST_SPLIT = 2
WCHUNK_ELEMS = 128 * 1024
WO_ROWS, W1_ROWS, W2_ROWS = WCHUNK_ELEMS // D_MODEL, WCHUNK_ELEMS // D_FF, WCHUNK_ELEMS // D_MODEL
N_WO, N_W1, N_W2 = D_MODEL // WO_ROWS, D_MODEL // W1_ROWS, D_FF // W2_ROWS
STAGE_SLOTS = 4
CTX_TILES = CTX_TOKENS // TM
CTX_SEQS_PER_STEP = 4
MOD_TN = 1536
EVEN_BATCH_ROWS = 2048
NA_SEQS_PER_STEP = 2
NA_HALF = 512
NA_KEYS = 768
SWA_QROWS = 2 * SWA_BLOCK
SWA_BAND = 4 * SWA_BLOCK
LAT_ODD_COLS = IN_ODD + 2 * PAIR


def _cparams(n_axes):
    return pltpu.CompilerParams(dimension_semantics=("arbitrary",) * n_axes, vmem_limit_bytes=VMEM_LIMIT)


def _rms(x, g):
    return x * lax.rsqrt(jnp.mean(x * x, axis=-1, keepdims=True) + EPS) * g


def _dot(a, b):
    return jnp.dot(a, b, preferred_element_type=F32)


def _dot_nt(a, b):
    return lax.dot_general(a, b, (((1,), (1,)), ((), ())), preferred_element_type=F32)


def _layer_block(shape, li):
    zeros = (0,) * len(shape)
    return pl.BlockSpec((1,) + tuple(shape), lambda *_: (li,) + zeros, pipeline_mode=pl.Buffered(1))


def _low_half(shape):
    return (lax.broadcasted_iota(jnp.int32, shape, len(shape) - 1) & DH) == 0


def _keep_mask_bf16(rows, hh):
    lane = lax.broadcasted_iota(jnp.int32, (rows, PAIR), 1)
    half = (lane & DH).astype(F32).astype(BF16)
    return (half == 0) if hh == 0 else (half != 0)


def _pad_rows(t, hh, fill):
    other = jnp.full(t.shape, fill, t.dtype)
    return jnp.concatenate([t, other] if hh == 0 else [other, t], axis=0)


class _Memo(dict):
    def get_or_build(self, key, build):
        if key not in self:
            self[key] = build()
        return self[key]


def _merge_pair(o0, o1, extra_den=None):
    low = _low_half(o0.shape)
    num = jnp.where(low, o0, o1)
    den = pltpu.roll(jnp.where(low, o1, o0), DH, axis=1)
    if extra_den is not None:
        den = den + jnp.where(low, extra_den[0], extra_den[1])
    return num * (1.0 / den)


class _Head:
    def __init__(self, scores, finish, sink=None):
        self.scores, self.finish, self.sink = scores, finish, sink


def _attend(heads, batch):
    batches = [heads[i:i + batch] for i in range(0, len(heads), batch)]
    cur = [hd.scores() for hd in batches[0]]
    for bi, group in enumerate(batches):
        nxt = [hd.scores() for hd in batches[bi + 1]] if bi + 1 < len(batches) else None
        exps = []
        for hd, parts in zip(group, cur):
            chunks = [s[:, j * 128:(j + 1) * 128] for s in parts for j in range(s.shape[1] // 128)]
            m_el = functools.reduce(jnp.maximum, chunks)
            if hd.sink is not None:
                m_el = jnp.maximum(m_el, hd.sink)
            mx = jnp.broadcast_to(jnp.max(m_el, axis=-1, keepdims=True), m_el.shape)
            term = None if hd.sink is None else jnp.exp2(hd.sink - mx)
            e_parts = [jnp.concatenate([jnp.exp2(s[:, j * 128:(j + 1) * 128] - mx)
                                        for j in range(s.shape[1] // 128)], axis=1).astype(BF16) for s in parts]
            exps.append((e_parts, term))
        for hd, (e_parts, term) in zip(group, exps):
            hd.finish(e_parts, term)
        cur = nxt


class _PairSink:
    def __init__(self, y_ref, rows, cols):
        self.y_ref, self.rows, self.cols, self.first = y_ref, rows, cols, None

    def put(self, hh, o, term):
        if hh == 0:
            self.first = (o, term)
        else:
            o0, term0 = self.first
            extra = None if term is None else (term0, term)
            self.y_ref[self.rows, self.cols] = _merge_pair(o0, o, extra).astype(BF16)


def _mod_kernel(cond_ref, w_ref, b_ref, o_ref):
    c = cond_ref[...]
    s = c * (1.0 / (1.0 + jnp.exp(-c)))
    o_ref[0] = _dot(s.astype(BF16), w_ref[0].astype(BF16)) + b_ref[pl.ds(pl.program_id(0), 1), :]


def _modulation(cond, mod_w, mod_b):
    return pl.pallas_call(
        _mod_kernel,
        grid=(DEPTH, 6 * D_MODEL // MOD_TN),
        in_specs=[
            pl.BlockSpec((MOD_ROWS, D_MODEL), lambda l, j: (0, 0)),
            pl.BlockSpec((1, D_MODEL, MOD_TN), lambda l, j: (l, 0, j)),
            pl.BlockSpec((DEPTH, MOD_TN), lambda l, j: (0, j)),
        ],
        out_specs=pl.BlockSpec((1, MOD_ROWS, MOD_TN), lambda l, j: (l, 0, j)),
        out_shape=jax.ShapeDtypeStruct((DEPTH, MOD_ROWS, 6 * D_MODEL), F32),
        compiler_params=_cparams(2),
        name="adaln_modulation",
    )(cond, mod_w, mod_b)


def _norm_mod(x, g, mod, shift_col):
    sh = mod[:, shift_col:shift_col + D_MODEL]
    sc = mod[:, shift_col + D_MODEL:shift_col + 2 * D_MODEL]
    return _rms(x, g) * (1.0 + sc) + sh


def _rope(z, cos, sin):
    outs = []
    for j in range(z.shape[1] // 128):
        zj = z[:, j * 128:(j + 1) * 128]
        lane = lax.broadcasted_iota(jnp.int32, zj.shape, 1)
        partner = jnp.where((lane & 16) == 0, pltpu.roll(zj, 128 - 16, axis=1), pltpu.roll(zj, 16, axis=1))
        outs.append(zj * cos + partner * sin)
    return outs[0] if len(outs) == 1 else jnp.concatenate(outs, axis=1)


def _store_heads_transposed(dst, bi, pc_rows, n_pairs):
    for p in range(n_pairs):
        t = pc_rows[:, p * PAIR:(p + 1) * PAIR].T
        dst[bi, 0, 2 * p] = t[0:DH]
        dst[bi, 0, 2 * p + 1] = t[DH:PAIR]


def _mod_row(mod_ref, latent):
    if latent:
        return mod_ref[0, pl.ds(pl.program_id(0) // (DEC_SEQ // TM_PRE), 1), :]
    return mod_ref[0, CTX_ROW:CTX_ROW + 1, :]


def _normed_groups(x_ref, mod_ref, g_ref, li, latent):
    rows = [slice(i * SEQ, (i + 1) * SEQ) for i in range(TM_PRE // SEQ)]
    mod, g = _mod_row(mod_ref, latent), g_ref[li:li + 1, :]
    return rows, [_norm_mod(x_ref[r, :], g, mod, 0).astype(BF16) for r in rows]


def _k1_ctx_even(x_ref, mod_ref, g_ref, w_ref, proj_ref, dkt_ref, dv_ref, *, li):
    rows, hs = _normed_groups(x_ref, mod_ref, g_ref, li, False)
    for c in range(IN_EVEN // 512):
        for bi, (r, h) in enumerate(zip(rows, hs)):
            pc = _dot(h, w_ref[0, :, c * 512:(c + 1) * 512])
            if c < 3:
                proj_ref[r, c * 512:(c + 1) * 512] = pc.astype(BF16)
            elif c == 3:
                proj_ref[r, c * 512:(c + 1) * 512] = (pc * Q_SCALE).astype(BF16)
            elif c == 4:
                for hh in range(H_B):
                    t = pc[:, hh * PAIR:(hh + 1) * PAIR].T
                    dkt_ref[bi, 0, hh, 0] = t[0:DH]
                    dkt_ref[bi, 0, hh, 1] = t[DH:PAIR]
            else:
                for hh in range(H_B):
                    dv_ref[bi, 0, hh] = pc[:, hh * PAIR:(hh + 1) * PAIR]


def _k1_lat_even(x_ref, mod_ref, g_ref, w_ref, cos_ref, sin_ref, proj_ref, *, li):
    rows, hs = _normed_groups(x_ref, mod_ref, g_ref, li, True)
    for c in range(IN_EVEN // 512):
        for r, h in zip(rows, hs):
            pc = _dot(h, w_ref[0, :, c * 512:(c + 1) * 512])
            if c == 3:
                pc = _rope(pc * Q_SCALE, cos_ref[r, :], sin_ref[r, :])
            elif c == 4:
                pc = _rope(pc, cos_ref[r, :], sin_ref[r, :])
            proj_ref[r, c * 512:(c + 1) * 512] = pc.astype(BF16)


def _k1_ctx_odd(x_ref, mod_ref, g_ref, w_ref, q_ref, nkt_ref, nvt_ref, skt_ref, svt_ref, *, li):
    rows, hs = _normed_groups(x_ref, mod_ref, g_ref, li, False)
    for c in range(4):
        for bi, (r, h) in enumerate(zip(rows, hs)):
            pc = _dot(h, w_ref[0, :, c * 512:(c + 1) * 512])
            if c == 0 or c == 3:
                q_ref[r, (c // 3) * 512:(c // 3 + 1) * 512] = (pc * Q_SCALE).astype(BF16)
            else:
                _store_heads_transposed(nkt_ref if c == 1 else nvt_ref, bi, pc, H_C // 2)
    for bi, h in enumerate(hs):
        pc = _dot(h, w_ref[0, :, 2048:IN_ODD])
        _store_heads_transposed(skt_ref, bi, pc[:, 0:PAIR], 1)
        _store_heads_transposed(svt_ref, bi, pc[:, PAIR:2 * PAIR], 1)


def _k1_lat_odd(x_ref, mod_ref, g_ref, w_ref, cos_ref, sin_ref, proj_ref, *, li):
    rows, hs = _normed_groups(x_ref, mod_ref, g_ref, li, True)
    for c in range(4):
        for r, h in zip(rows, hs):
            pc = _dot(h, w_ref[0, :, c * 512:(c + 1) * 512])
            if c == 0:
                pc = pc * Q_SCALE
            elif c == 3:
                pc = _rope(pc * Q_SCALE, cos_ref[r, :], sin_ref[r, :])
            proj_ref[r, c * 512:(c + 1) * 512] = pc.astype(BF16)
    low = _low_half((SEQ, PAIR))
    for r, h in zip(rows, hs):
        pc = _dot(h, w_ref[0, :, 2048:IN_ODD])
        for j, z in enumerate((_rope(pc[:, 0:PAIR], cos_ref[r, :], sin_ref[r, :]), pc[:, PAIR:2 * PAIR])):
            zr = pltpu.roll(z, DH, axis=1)
            base = 2048 + j * 2 * PAIR
            proj_ref[r, base:base + PAIR] = jnp.where(low, z, zr).astype(BF16)
            proj_ref[r, base + PAIR:base + 2 * PAIR] = jnp.where(low, zr, z).astype(BF16)


def _pre_mixer(x, mod, gains, w, li, *, latent, even, rope, n_tok, tile0=0):
    n_in = w.shape[2]
    tm = TM_PRE
    tiles_per_seq = DEC_SEQ // tm
    in_specs = [
        pl.BlockSpec((tm, D_MODEL), lambda i: (i + tile0, 0)),
        pl.BlockSpec((1, MOD_ROWS, 6 * D_MODEL), lambda i: (li, 0, 0)),
        pl.BlockSpec((DEPTH, D_MODEL), lambda i: (0, 0)),
        _layer_block((D_MODEL, n_in), 0),
    ]
    args = [x, mod, gains, w]
    nb = tm // SEQ
    if latent:
        in_specs += [pl.BlockSpec((tm, 128), lambda i: (i % tiles_per_seq, 0))] * 2
        args += list(rope)
        body = _k1_lat_even if even else _k1_lat_odd
        n_out = IN_EVEN if even else LAT_ODD_COLS
        out_specs = pl.BlockSpec((tm, n_out), lambda i: (i, 0))
        out_shape = jax.ShapeDtypeStruct((n_tok, n_out), BF16)
    elif even:
        body = _k1_ctx_even
        out_specs = [pl.BlockSpec((tm, 4 * MIX_A), lambda i: (i, 0)),
                     pl.BlockSpec((nb, 1, H_B, 2, DH, SEQ), lambda i: (i, 0, 0, 0, 0, 0)),
                     pl.BlockSpec((nb, 1, H_B, SEQ, PAIR), lambda i: (i, 0, 0, 0, 0))]
        out_shape = [jax.ShapeDtypeStruct((n_tok, 4 * MIX_A), BF16),
                     jax.ShapeDtypeStruct((BATCH, 1, H_B, 2, DH, SEQ), F32),
                     jax.ShapeDtypeStruct((BATCH, 1, H_B, SEQ, PAIR), F32)]
    else:
        body = _k1_ctx_odd
        c_spec = pl.BlockSpec((nb, 1, H_C, DH, SEQ), lambda i: (i, 0, 0, 0, 0))
        d_spec = pl.BlockSpec((nb, 1, KV_D, DH, SEQ), lambda i: (i, 0, 0, 0, 0))
        c_shape = jax.ShapeDtypeStruct((BATCH, 1, H_C, DH, SEQ), F32)
        d_shape = jax.ShapeDtypeStruct((BATCH, 1, KV_D, DH, SEQ), F32)
        out_specs = [pl.BlockSpec((tm, D_MODEL), lambda i: (i, 0)), c_spec, c_spec, d_spec, d_spec]
        out_shape = [jax.ShapeDtypeStruct((n_tok, D_MODEL), BF16), c_shape, c_shape, d_shape, d_shape]
    return pl.pallas_call(
        functools.partial(body, li=li),
        grid=(n_tok // tm,),
        in_specs=in_specs,
        out_specs=out_specs,
        out_shape=out_shape,
        compiler_params=_cparams(1),
        name=f"pre_mixer_{'lat' if latent else 'ctx'}_{'even' if even else 'odd'}",
    )(*args)


def _even_mixer_kernel(*refs, seq, n_seq, tq, own_keys, lam_init):
    proj_ref, cw_ref, lq1_ref, lk1_ref, lq2_ref, lk2_ref, subln_ref, ckt_ref, cv_ref = refs[:9]
    y_ref = refs[-1]
    n_rows = n_seq * seq

    pos = lax.broadcasted_iota(jnp.int32, (n_rows, 128), 0) % seq
    for j in range(MIX_A // 128):
        cols = slice(j * 128, (j + 1) * 128)
        a_b = proj_ref[:, j * 128:(j + 1) * 128].astype(F32)
        u = (proj_ref[:, MIX_A + j * 128:MIX_A + (j + 1) * 128].astype(F32)
             * proj_ref[:, 2 * MIX_A + j * 128:2 * MIX_A + (j + 1) * 128].astype(F32))
        u_prev = jnp.where(pos == 0, 0.0, pltpu.roll(u, 1, axis=0))
        u_next = jnp.where(pos == seq - 1, 0.0, pltpu.roll(u, n_rows - 1, axis=0))
        w = cw_ref[0, :, cols]
        y_ref[:, cols] = (a_b * (w[0:1] * u_prev + w[1:2] * u + w[2:3] * u_next)).astype(BF16)

    lam = (jnp.exp(jnp.sum(lq1_ref[...] * lk1_ref[...], axis=-1, keepdims=True))
           - jnp.exp(jnp.sum(lq2_ref[...] * lk2_ref[...], axis=-1, keepdims=True)) + lam_init)
    subln = subln_ref[...]
    q_col, k_col, v_col = 3 * MIX_A, 3 * MIX_A + 512, 3 * MIX_A + 1024
    ones = jnp.ones((PAST_LEN, PAIR), BF16)

    memo = _Memo()

    def own_k(b, h, m):
        kp = proj_ref[b * seq:(b + 1) * seq, k_col + h * PAIR:k_col + (h + 1) * PAIR]
        return jnp.where(_keep_mask_bf16(seq, m), kp, jnp.zeros_like(kp))

    def value_ops(b, h):
        v_ops = []
        if own_keys:
            v_own = proj_ref[b * seq:(b + 1) * seq, v_col + h * PAIR:v_col + (h + 1) * PAIR]
            v_ops.append(jnp.concatenate([v_own, jnp.ones((seq, PAIR), BF16)], axis=1))
        v_ops.append(jnp.concatenate([cv_ref[b, 0, h].astype(BF16), ones], axis=1))
        return v_ops

    def scores(b, qrows, h, m):
        qp = proj_ref[qrows, q_col + h * PAIR:q_col + (h + 1) * PAIR]
        parts = []
        if own_keys:
            parts.append(_dot_nt(qp, memo.get_or_build(("k", b, h, m), functools.partial(own_k, b, h, m))))
        ckt = memo.get_or_build(("ck", b, h, m), lambda: _pad_rows(ckt_ref[b, 0, h, m].astype(BF16), m, 0.0))
        parts.append(_dot(qp, ckt))
        return parts

    first_map = {}

    def pv(b, qrows, h, m, e_parts, _):
        v_ops = memo.get_or_build(("v", b, h), functools.partial(value_ops, b, h))
        o = None
        for e, v_op in zip(e_parts, v_ops):
            t = _dot(e, v_op)
            o = t if o is None else o + t
        attn = o[:, 0:PAIR] * (1.0 / o[:, PAIR:2 * PAIR])
        if m == 0:
            first_map[(qrows.start, h)] = attn
        else:
            y = _rms(first_map.pop((qrows.start, h)) - lam * attn, subln) * (1.0 - lam_init)
            y_ref[qrows, MIX_A + h * PAIR:MIX_A + (h + 1) * PAIR] = y.astype(BF16)

    heads = []
    for b in range(n_seq):
        for i in range(seq // tq):
            qrows = slice(b * seq + i * tq, b * seq + (i + 1) * tq)
            for h in range(H_B):
                for m in range(2):
                    heads.append(_Head(functools.partial(scores, b, qrows, h, m),
                                       functools.partial(pv, b, qrows, h, m)))
    _attend(heads, batch=EVEN_BATCH_ROWS // tq)


def _even_mixer(proj, conv_w, lq1, lk1, lq2, lk2, subln, kt, v, *, seq, n_seq, own_keys, lam_init):
    n_tok = proj.shape[0]
    rows = n_seq * seq
    small = lambda a: pl.BlockSpec((1, a.shape[1]), lambda b: (0, 0))
    in_specs = [pl.BlockSpec((rows, proj.shape[1]), lambda b: (b, 0)),
                pl.BlockSpec((1, 3, MIX_A), lambda b: (0, 0, 0)),
                small(lq1), small(lk1), small(lq2), small(lk2), small(subln),
                pl.BlockSpec((n_seq, 1, H_B, 2, DH, PAST_LEN), lambda b: (b, 0, 0, 0, 0, 0)),
                pl.BlockSpec((n_seq, 1, H_B, PAST_LEN, PAIR), lambda b: (b, 0, 0, 0, 0))]
    return pl.pallas_call(
        functools.partial(_even_mixer_kernel, seq=seq, n_seq=n_seq, tq=min(seq, 256), own_keys=own_keys,
                          lam_init=lam_init),
        grid=(n_tok // rows,),
        in_specs=in_specs,
        out_specs=pl.BlockSpec((rows, D_MODEL), lambda b: (b, 0)),
        out_shape=jax.ShapeDtypeStruct((n_tok, D_MODEL), BF16),
        compiler_params=_cparams(1),
        name=f"even_mixer_{'lat' if own_keys else 'ctx'}",
    )(proj, conv_w, lq1, lk1, lq2, lk2, subln, kt, v)


def _odd_ctx_kernel(q_ref, nkt_ref, nvt_ref, skt_ref, svt_ref, sink_ref, y_ref):
    def kv_refs(b, head):
        if head < H_C:
            return nkt_ref.at[b, 0, head], nvt_ref.at[b, 0, head]
        g = (head - H_C) // GQA
        return skt_ref.at[b, 0, g], svt_ref.at[b, 0, g]

    memo = _Memo()

    def kv_key(head):
        return head if head < H_C else H_C + (head - H_C) // GQA

    def scores(b, p, hh):
        head = 2 * p + hh
        kt = memo.get_or_build(("k", b, kv_key(head), hh),
                               lambda: _pad_rows(kv_refs(b, head)[0][...].astype(BF16), hh, 0.0))
        return [_dot(q_ref[b * SEQ:(b + 1) * SEQ, p * PAIR:(p + 1) * PAIR], kt)]

    def finish(out, b, p, hh, e_parts, term):
        head = 2 * p + hh
        vt = memo.get_or_build(("v", b, kv_key(head), hh),
                               lambda: _pad_rows(kv_refs(b, head)[1][...].astype(BF16), hh, 1.0))
        out.put(hh, _dot_nt(e_parts[0], vt), term)

    heads = []
    for b in range(CTX_SEQS_PER_STEP):
        for p in range((H_C + H_D) // 2):
            out = _PairSink(y_ref, slice(b * SEQ, (b + 1) * SEQ), slice(p * PAIR, (p + 1) * PAIR))
            for hh in range(2):
                d_head = 2 * p + hh - H_C
                sink = sink_ref[d_head] * LOG2E if d_head >= 0 else None
                heads.append(_Head(functools.partial(scores, b, p, hh), functools.partial(finish, out, b, p, hh), sink))
    _attend(heads, batch=8)


def _odd_ctx_mixer(q, nkt, nvt, skt, svt, sink):
    n_tok = q.shape[0]
    n_seq = CTX_SEQS_PER_STEP
    c_spec = pl.BlockSpec((n_seq, 1, H_C, DH, SEQ), lambda b: (b, 0, 0, 0, 0))
    d_spec = pl.BlockSpec((n_seq, 1, KV_D, DH, SEQ), lambda b: (b, 0, 0, 0, 0))
    return pl.pallas_call(
        _odd_ctx_kernel,
        grid=(n_tok // (n_seq * SEQ),),
        in_specs=[pl.BlockSpec((n_seq * SEQ, D_MODEL), lambda b: (b, 0)), c_spec, c_spec, d_spec, d_spec,
                  pl.BlockSpec(memory_space=pltpu.SMEM)],
        out_specs=pl.BlockSpec((n_seq * SEQ, D_MODEL), lambda b: (b, 0)),
        out_shape=jax.ShapeDtypeStruct((n_tok, D_MODEL), BF16),
        compiler_params=_cparams(1),
        name="odd_mixer_ctx",
    )(q, nkt, nvt, skt, svt, sink)


def _build_na_bias(rpb_ref, head, nb_ref, slot):
    qc = lax.broadcasted_iota(jnp.int32, (GRID_W, GRID_W), 0)
    kc = lax.broadcasted_iota(jnp.int32, (GRID_W, GRID_W), 1)
    col_start = jnp.clip(qc - NA_WIN_C // 2, 0, GRID_W - NA_WIN_C)
    col_ok = (kc >= col_start) & (kc < col_start + NA_WIN_C)
    dc = kc - qc + NA_WIN_C - 1
    n_dr, n_dc = 2 * NA_WIN_R - 1, 2 * NA_WIN_C - 1
    neg = jnp.full((GRID_W, GRID_W), NEG, F32)
    toeplitz = []
    for dr in range(n_dr):
        t = neg
        for d in range(n_dc):
            t = jnp.where(dc == d, rpb_ref[(head * n_dr + dr) * n_dc + d] * LOG2E, t)
        toeplitz.append(jnp.where(col_ok, t, NEG))
    rows_per_half = NA_HALF // GRID_W
    key_rows = NA_KEYS // GRID_W
    for half in range(2):
        for rl in range(rows_per_half):
            r = half * rows_per_half + rl
            r_start = min(max(r - NA_WIN_R // 2, 0), GRID_H - NA_WIN_R)
            blocks = []
            for j in range(key_rows):
                rk = half * (GRID_H - key_rows) + j
                inside = r_start <= rk < r_start + NA_WIN_R
                blocks.append(toeplitz[rk - r + NA_WIN_R - 1] if inside else neg)
            nb_ref[slot, half, rl * GRID_W:(rl + 1) * GRID_W, :] = jnp.concatenate(blocks, axis=1)


def _na_kernel(rpb_ref, q_ref, k_ref, v_ref, kct_ref, vct_ref, y_ref, nb_ref):
    @pl.when(pl.program_id(1) == 0)
    def _():
        for hh in range(2):
            _build_na_bias(rpb_ref, 2 * pl.program_id(0) + hh, nb_ref, hh)

    def scores(b, half, hh):
        w0 = b * DEC_SEQ + half * (DEC_SEQ - NA_KEYS)
        qp = q_ref[b * DEC_SEQ + half * NA_HALF:b * DEC_SEQ + (half + 1) * NA_HALF, :]
        kw = k_ref[w0:w0 + NA_KEYS, :]
        s1 = _dot_nt(qp, jnp.where(_keep_mask_bf16(NA_KEYS, hh), kw, jnp.zeros_like(kw))) + nb_ref[hh, half]
        s2 = _dot(qp, _pad_rows(kct_ref[b, 0, hh].astype(BF16), hh, 0.0))
        return [s1, s2]

    def finish(out, b, half, hh, e_parts, term):
        w0 = b * DEC_SEQ + half * (DEC_SEQ - NA_KEYS)
        vw = v_ref[w0:w0 + NA_KEYS, :]
        o = (_dot(e_parts[0], jnp.where(_keep_mask_bf16(NA_KEYS, hh), vw, jnp.ones_like(vw)))
             + _dot_nt(e_parts[1], _pad_rows(vct_ref[b, 0, hh].astype(BF16), hh, 1.0)))
        out.put(hh, o, term)

    heads = []
    for b in range(NA_SEQS_PER_STEP):
        for half in range(2):
            rows = slice(b * DEC_SEQ + half * NA_HALF, b * DEC_SEQ + (half + 1) * NA_HALF)
            out = _PairSink(y_ref, rows, slice(0, PAIR))
            for hh in range(2):
                heads.append(_Head(functools.partial(scores, b, half, hh), functools.partial(finish, out, b, half, hh)))
    _attend(heads, batch=4)


def _na_mixer(proj, cache_kt, cache_vt, rpb_flat):
    pairs = H_C // 2
    n_seq = NA_SEQS_PER_STEP
    col = lambda base: pl.BlockSpec((n_seq * DEC_SEQ, PAIR), lambda hp, b: (b, base + hp))
    cache = pl.BlockSpec((n_seq, 1, 2, DH, PAST_LEN), lambda hp, b: (b, 0, hp, 0, 0))
    return pl.pallas_call(
        _na_kernel,
        grid=(pairs, DEC_BATCH // n_seq),
        in_specs=[pl.BlockSpec(memory_space=pltpu.SMEM), col(0), col(pairs), col(2 * pairs), cache, cache],
        out_specs=pl.BlockSpec((n_seq * DEC_SEQ, PAIR), lambda hp, b: (b, hp)),
        out_shape=jax.ShapeDtypeStruct((LAT_TOKENS, 512), BF16),
        scratch_shapes=[pltpu.VMEM((2, 2, NA_HALF, NA_KEYS), F32)],
        compiler_params=_cparams(2),
        name="na_mixer",
    )(rpb_flat, proj, proj, proj, cache_kt, cache_vt)


def _swa_kernel(q_ref, kv_ref, kct_ref, vct_ref, sink_ref, y_ref):
    n_groups = DEC_SEQ // SWA_QROWS
    ql = lax.broadcasted_iota(jnp.int32, (SWA_QROWS, SWA_BAND), 0)
    kj = lax.broadcasted_iota(jnp.int32, (SWA_QROWS, SWA_BAND), 1)

    def band_bias(first_key_minus_first_query):
        return jnp.where(jnp.abs(kj + first_key_minus_first_query - ql) <= SWA_WINDOW, 0.0, NEG)

    bias_first, bias_mid, bias_last = band_bias(0), band_bias(-SWA_BLOCK), band_bias(-2 * SWA_BLOCK)

    def band_start(grp):
        return min(max(grp * SWA_QROWS - SWA_BLOCK, 0), DEC_SEQ - SWA_BAND)

    memo = _Memo()

    def band_k(g, grp, hh):
        start = band_start(grp)
        kb = kv_ref[start:start + SWA_BAND, g * PAIR:(g + 1) * PAIR]
        return jnp.where(_keep_mask_bf16(SWA_BAND, hh), kb, jnp.zeros_like(kb))

    def band_v(g, grp, hh):
        start = band_start(grp)
        vb = kv_ref[start:start + SWA_BAND, (KV_D + g) * PAIR:(KV_D + g + 1) * PAIR]
        return jnp.where(_keep_mask_bf16(SWA_BAND, hh), vb, jnp.ones_like(vb))

    def scores(g, grp, p, hh):
        bias = bias_first if grp == 0 else (bias_last if grp == n_groups - 1 else bias_mid)
        qp = q_ref[grp * SWA_QROWS:(grp + 1) * SWA_QROWS, p * PAIR:(p + 1) * PAIR]
        s1 = _dot_nt(qp, memo.get_or_build(("k", g, grp, hh), functools.partial(band_k, g, grp, hh))) + bias
        ckt = memo.get_or_build(("ck", g, hh), lambda: _pad_rows(kct_ref[0, 0, g].astype(BF16), hh, 0.0))
        return [s1, _dot(qp, ckt)]

    def finish(out, g, grp, hh, e_parts, term):
        cvt = memo.get_or_build(("cv", g, hh), lambda: _pad_rows(vct_ref[0, 0, g].astype(BF16), hh, 1.0))
        o = (_dot(e_parts[0], memo.get_or_build(("v", g, grp, hh), functools.partial(band_v, g, grp, hh)))
             + _dot_nt(e_parts[1], cvt))
        out.put(hh, o, term)

    heads = []
    for g in range(KV_D):
        for grp in range(n_groups):
            for j in range(GQA // 2):
                p = g * (GQA // 2) + j
                out = _PairSink(y_ref, slice(grp * SWA_QROWS, (grp + 1) * SWA_QROWS), slice(p * PAIR, (p + 1) * PAIR))
                for hh in range(2):
                    sink = sink_ref[2 * p + hh] * LOG2E
                    heads.append(_Head(functools.partial(scores, g, grp, p, hh),
                                       functools.partial(finish, out, g, grp, hh), sink))
    _attend(heads, batch=8)


def _swa_mixer(proj, cache_kt, cache_vt, sink):
    cache = pl.BlockSpec((1, 1, KV_D, DH, PAST_LEN), lambda b: (b, 0, 0, 0, 0))
    return pl.pallas_call(
        _swa_kernel,
        grid=(DEC_BATCH,),
        in_specs=[pl.BlockSpec((DEC_SEQ, 512), lambda b: (b, 3)),
                  pl.BlockSpec((DEC_SEQ, 512), lambda b: (b, 4)),
                  cache, cache,
                  pl.BlockSpec(memory_space=pltpu.SMEM)],
        out_specs=pl.BlockSpec((DEC_SEQ, 512), lambda b: (b, 0)),
        out_shape=jax.ShapeDtypeStruct((LAT_TOKENS, 512), BF16),
        compiler_params=_cparams(1),
        name="swa_mixer",
    )(proj, proj, cache_kt, cache_vt, sink)


def _load_weights_bf16(li, wo_hbm, w1_hbm, w2_hbm, wo_s, w1_s, w2_s, stage_sq, stage_wide, sems):
    square = ([(wo_hbm, wo_s, j, WO_ROWS) for j in range(N_WO)] + [(w2_hbm, w2_s, j, W2_ROWS) for j in range(N_W2)])
    wide = [(w1_hbm, w1_s, j, W1_ROWS) for j in range(N_W1)]
    queues = {"sq": (square, stage_sq, 0), "wide": (wide, stage_wide, STAGE_SLOTS)}

    def copy(kind, idx):
        chunks, stage, sem0 = queues[kind]
        src, _, j, n_rows = chunks[idx]
        slot = idx % STAGE_SLOTS
        return pltpu.make_async_copy(src.at[li, j * n_rows:(j + 1) * n_rows, :], stage.at[slot], sems.at[sem0 + slot])

    for kind in queues:
        for idx in range(min(STAGE_SLOTS, len(queues[kind][0]))):
            copy(kind, idx).start()
    order = []
    for i in range(max(len(square), len(wide))):
        order += [("sq", i)] * (i < len(square)) + [("wide", i)] * (i < len(wide))
    for kind, idx in order:
        chunks, stage, _ = queues[kind]
        _, dst, j, n_rows = chunks[idx]
        copy(kind, idx).wait()
        dst[j * n_rows:(j + 1) * n_rows, :] = stage[idx % STAGE_SLOTS].astype(BF16)
        if idx + STAGE_SLOTS < len(chunks):
            copy(kind, idx + STAGE_SLOTS).start()


def _post_mixer_kernel(*refs, li, n_lat_mix, n_x, n_out):
    mix_ctx_ref, mix_lat_refs = refs[0], refs[1:1 + n_lat_mix]
    refs = refs[1 + n_lat_mix:]
    x_refs = refs[:n_x]
    mod_ref, gpost_ref, gpre_ref, gmlp_ref, wo_hbm, w1_hbm, w2_hbm = refs[n_x:7 + n_x]
    out_refs = refs[7 + n_x:7 + n_x + n_out]
    wo_s, w1_s, w2_s, stage_sq, stage_wide, sems = refs[7 + n_x + n_out:]
    step = pl.program_id(0)

    @pl.when(step == 0)
    def _():
        _load_weights_bf16(li, wo_hbm, w1_hbm, w2_hbm, wo_s, w1_s, w2_s, stage_sq, stage_wide, sems)

    is_ctx = step < CTX_TILES
    mod = mod_ref[0, pl.ds(jnp.where(is_ctx, CTX_ROW, (step - CTX_TILES) // (DEC_SEQ // TM)), 1), :]
    g_post, g_pre, g_mlp = gpost_ref[li:li + 1, :], gpre_ref[li:li + 1, :], gmlp_ref[li:li + 1, :]
    rows = [slice(i * (TM // POST_SPLIT), (i + 1) * (TM // POST_SPLIT)) for i in range(POST_SPLIT)]

    def residual(r):
        if n_x == 1:
            return x_refs[0][r, :]
        return jnp.where(is_ctx, x_refs[0][r, :], x_refs[1][r, :])

    def mixed(r):
        lat = [m[r, :] for m in mix_lat_refs]
        lat = lat[0] if n_lat_mix == 1 else jnp.concatenate(lat, axis=1)
        return jnp.where(is_ctx, mix_ctx_ref[r, :], lat)

    ys = [_dot(mixed(r), wo_s[...]) for r in rows]
    x1 = [residual(r) + mod[:, 2 * D_MODEL:3 * D_MODEL] * _rms(y, g_post) for r, y in zip(rows, ys)]
    h = [_norm_mod(x, g_pre, mod, 3 * D_MODEL).astype(BF16) for x in x1]
    acc = [None] * POST_SPLIT
    pending = []

    def second_matmul(c, i, f):
        t2 = _dot(f, w2_s[c * FF_CHUNK:(c + 1) * FF_CHUNK, :])
        acc[i] = t2 if acc[i] is None else acc[i] + t2

    for c in range(D_FF // FF_CHUNK):
        for i in range(POST_SPLIT):
            f = _dot(h[i], w1_s[:, c * FF_CHUNK:(c + 1) * FF_CHUNK])
            pending.append((c, i, jnp.square(jnp.maximum(f, 0.0)).astype(BF16)))
            if len(pending) > 1:
                second_matmul(*pending.pop(0))
    second_matmul(*pending.pop(0))
    res = [x1[i] + mod[:, 5 * D_MODEL:6 * D_MODEL] * _rms(acc[i], g_mlp) for i in range(POST_SPLIT)]
    if n_out == 1:
        for i, r in enumerate(rows):
            out_refs[0][r, :] = res[i]
    else:
        @pl.when(is_ctx)
        def _():
            for i, r in enumerate(rows):
                out_refs[0][r, :] = res[i]

        @pl.when(jnp.logical_not(is_ctx))
        def _():
            for i, r in enumerate(rows):
                out_refs[1][r, :] = res[i]


def _post_mixer(mix_ctx, mix_lat, xs, mod, g_post, g_pre, g_mlp, w_out, w1, w2, li, *, split_out):
    tile = lambda s: s
    ctx_tile = lambda s: jnp.minimum(tile(s), CTX_TILES - 1)
    lat_tile = lambda s: jnp.maximum(tile(s) - CTX_TILES, 0)
    token_spec = lambda index, width=D_MODEL: pl.BlockSpec((TM, width), lambda s: (index(s), 0))
    mix_specs = [token_spec(ctx_tile)] + [token_spec(lat_tile, m.shape[1]) for m in mix_lat]
    x_specs = [token_spec(tile)] if len(xs) == 1 else [token_spec(ctx_tile), token_spec(lat_tile)]
    if split_out:
        out_specs = [token_spec(ctx_tile), token_spec(lat_tile)]
        out_shape = [jax.ShapeDtypeStruct((CTX_TOKENS, D_MODEL), F32), jax.ShapeDtypeStruct((LAT_TOKENS, D_MODEL), F32)]
    else:
        out_specs = [token_spec(tile)]
        out_shape = [jax.ShapeDtypeStruct((ALL_TOKENS, D_MODEL), F32)]
    gain = pl.BlockSpec((DEPTH, D_MODEL), lambda s: (0, 0))
    return pl.pallas_call(
        functools.partial(_post_mixer_kernel, li=li, n_lat_mix=len(mix_lat), n_x=len(xs), n_out=len(out_specs)),
        grid=(ALL_TOKENS // TM,),
        in_specs=mix_specs + x_specs + [
            pl.BlockSpec((1, MOD_ROWS, 6 * D_MODEL), lambda s: (li, 0, 0)),
            gain, gain, gain] + [pl.BlockSpec(memory_space=pl.ANY)] * 3,
        out_specs=out_specs,
        out_shape=out_shape,
        scratch_shapes=[pltpu.VMEM((D_MODEL, D_MODEL), BF16), pltpu.VMEM((D_MODEL, D_FF), BF16),
                        pltpu.VMEM((D_FF, D_MODEL), BF16),
                        pltpu.VMEM((STAGE_SLOTS, WO_ROWS, D_MODEL), F32), pltpu.VMEM((STAGE_SLOTS, W1_ROWS, D_FF), F32),
                        pltpu.SemaphoreType.DMA((2 * STAGE_SLOTS,))],
        compiler_params=_cparams(1),
        name="post_mixer",
    )(mix_ctx, *mix_lat, *xs, mod, g_post, g_pre, g_mlp, w_out, w1, w2)


def _rope_tables():
    t = jnp.arange(DEC_SEQ)
    rows = (t // GRID_W).astype(F32)
    cols = (t % GRID_W).astype(F32)
    q4 = DH // 4
    inv = 1.0 / (ROPE_BASE ** (jnp.arange(q4, dtype=F32) / q4))
    ar, ac = rows[:, None] * inv, cols[:, None] * inv
    cos64 = jnp.concatenate([jnp.cos(ar), jnp.cos(ar), jnp.cos(ac), jnp.cos(ac)], axis=1)
    sin64 = jnp.concatenate([-jnp.sin(ar), jnp.sin(ar), -jnp.sin(ac), jnp.sin(ac)], axis=1)
    return jnp.tile(cos64, (1, 2)), jnp.tile(sin64, (1, 2))


def kernel(x_prompt, x_sample, cache_diff_k, cache_diff_v, cache_na_k, cache_na_v, cache_swa_k, cache_swa_v, c, c_ctx, mod_w, mod_b, norm_mix_pre, norm_mix_post, norm_mlp_pre, norm_mlp_post, w_in_even, conv_w, lambda_q1, lambda_k1, lambda_q2, lambda_k2, subln, w_in_odd, rpb, sink, w_out, mlp_w1, mlp_w2):
    cond = jnp.concatenate([c, c_ctx[None, :], jnp.zeros((MOD_ROWS - DEC_BATCH - 1, D_MODEL), F32)], axis=0)
    mod = _modulation(cond, mod_w, mod_b)
    rope = _rope_tables()

    t_minor = lambda a: jnp.swapaxes(a, -1, -2)

    w_in = [w_in_even.astype(BF16), w_in_odd.astype(BF16)]

    xp = x_prompt.reshape(CTX_TOKENS, D_MODEL)
    xs = x_sample.reshape(LAT_TOKENS, D_MODEL)
    lam_init0 = 0.8 - 0.6 * math.exp(-0.3 * 0)
    lam_args = (conv_w, lambda_q1, lambda_k1, lambda_q2, lambda_k2, subln)
    sink_flat = sink.reshape(-1)

    def post(li, mix_ctx, mix_lat, x_parts, split_out):
        return _post_mixer(mix_ctx, mix_lat, x_parts, mod, norm_mix_post, norm_mlp_pre, norm_mlp_post, w_out, mlp_w1,
                           mlp_w2, li, split_out=split_out)

    proj, new_diff_kt, new_diff_v = _pre_mixer(xp, mod, norm_mix_pre, w_in[0], 0, latent=False, even=True, rope=None,
                                               n_tok=CTX_TOKENS)
    mix_ctx = _even_mixer(proj, *lam_args, new_diff_kt, new_diff_v, seq=SEQ, n_seq=CTX_SEQS_PER_STEP // 2,
                          own_keys=False, lam_init=lam_init0)
    proj = _pre_mixer(xs, mod, norm_mix_pre, w_in[0], 0, latent=True, even=True, rope=rope, n_tok=LAT_TOKENS)
    mix_lat = _even_mixer(proj, *lam_args, t_minor(cache_diff_k), cache_diff_v, seq=DEC_SEQ, n_seq=1, own_keys=True,
                          lam_init=lam_init0)
    (x_all,) = post(0, mix_ctx, [mix_lat], [xp, xs], False)

    q, new_na_kt, new_na_vt, new_swa_kt, new_swa_vt = _pre_mixer(
        x_all, mod, norm_mix_pre, w_in[1], 1, latent=False, even=False, rope=None, n_tok=CTX_TOKENS)
    mix_ctx = _odd_ctx_mixer(q, new_na_kt, new_na_vt, new_swa_kt, new_swa_vt, sink_flat)
    proj = _pre_mixer(x_all, mod, norm_mix_pre, w_in[1], 1, latent=True, even=False, rope=rope, n_tok=LAT_TOKENS,
                      tile0=CTX_TOKENS // TM_PRE)
    mix_c = _na_mixer(proj, t_minor(cache_na_k), t_minor(cache_na_v), rpb.reshape(-1))
    mix_d = _swa_mixer(proj, t_minor(cache_swa_k), t_minor(cache_swa_v), sink_flat)
    xp, xs = post(1, mix_ctx, [mix_c, mix_d], [x_all], True)

    return (xp.reshape(BATCH, SEQ, D_MODEL), xs.reshape(DEC_BATCH, DEC_SEQ, D_MODEL),
            t_minor(new_diff_kt), new_diff_v, t_minor(new_na_kt), t_minor(new_na_vt),
            t_minor(new_swa_kt), t_minor(new_swa_vt))
```

```python
import functools
import math

import jax
import jax.numpy as jnp
from jax import lax
from jax.experimental import pallas as pl
from jax.experimental.pallas import tpu as pltpu

F32 = jnp.float32
BF16 = jnp.bfloat16

D_MODEL = 1024
BATCH = 32
SEQ = 256
DEPTH = 2
DEC_BATCH = 8
DEC_SEQ = 1024
PAST_LEN = 256
GRID_W = 64
GRID_H = DEC_SEQ // GRID_W
MIX_A = 512
H_B = 4
DH = 64
PAIR = 2 * DH
H_C = 8
H_D = 8
KV_D = 2
GQA = H_D // KV_D
IN_EVEN = 3072
IN_ODD = 2304
D_FF = 4096
NA_WIN_R = 8
NA_WIN_C = 16
SWA_BLOCK = 128
SWA_WINDOW = 128
ROPE_BASE = 10000.0
EPS = 1e-6
NEG = -1e30
LOG2E = math.log2(math.e)
Q_SCALE = DH ** -0.5 * LOG2E

MOD_ROWS = 16
CTX_ROW = DEC_BATCH
V7X_VMEM_BYTES = 64 * 1024 * 1024
VMEM_LIMIT = V7X_VMEM_BYTES - 12 * 1024 * 1024

CTX_TOKENS = BATCH * SEQ
LAT_TOKENS = DEC_BATCH * DEC_SEQ
ALL_TOKENS = CTX_TOKENS + LAT_TOKENS

TM = 512
TM_PRE = 1024
FF_CHUNK = 1024
POST_SPLIT = 2
WCHUNK_ELEMS = 128 * 1024
WO_ROWS, W1_ROWS, W2_ROWS = WCHUNK_ELEMS // D_MODEL, WCHUNK_ELEMS // D_FF, WCHUNK_ELEMS // D_MODEL
N_WO, N_W1, N_W2 = D_MODEL // WO_ROWS, D_MODEL // W1_ROWS, D_FF // W2_ROWS
STAGE_SLOTS = 4
CTX_TILES = CTX_TOKENS // TM
CTX_SEQS_PER_STEP = 4
MOD_TN = 3072
EVEN_BATCH_ROWS = 2048
NA_SEQS_PER_STEP = 2
NA_HALF = 512
NA_KEYS = 768
SWA_QROWS = 2 * SWA_BLOCK
SWA_BAND = 4 * SWA_BLOCK
LAT_ODD_COLS = IN_ODD + 2 * PAIR


def _cparams(n_axes):
    return pltpu.CompilerParams(dimension_semantics=("arbitrary",) * n_axes, vmem_limit_bytes=VMEM_LIMIT)


def _rms(x, g):
    return x * lax.rsqrt(jnp.mean(x * x, axis=-1, keepdims=True) + EPS) * g


def _dot(a, b):
    return jnp.dot(a, b, preferred_element_type=F32)


def _dot_nt(a, b):
    return lax.dot_general(a, b, (((1,), (1,)), ((), ())), preferred_element_type=F32)


def _layer_block(shape, li):
    zeros = (0,) * len(shape)
    return pl.BlockSpec((1,) + tuple(shape), lambda *_: (li,) + zeros, pipeline_mode=pl.Buffered(1))


def _low_half(shape):
    return (lax.broadcasted_iota(jnp.int32, shape, len(shape) - 1) & DH) == 0


def _keep_mask_bf16(rows, hh):
    lane = lax.broadcasted_iota(jnp.int32, (rows, PAIR), 1)
    half = (lane & DH).astype(F32).astype(BF16)
    return (half == 0) if hh == 0 else (half != 0)


def _pad_rows(t, hh, fill):
    other = jnp.full(t.shape, fill, t.dtype)
    return jnp.concatenate([t, other] if hh == 0 else [other, t], axis=0)


class _Memo(dict):
    def get_or_build(self, key, build):
        if key not in self:
            self[key] = build()
        return self[key]


def _merge_pair(o0, o1, extra_den=None):
    low = _low_half(o0.shape)
    num = jnp.where(low, o0, o1)
    den = pltpu.roll(jnp.where(low, o1, o0), DH, axis=1)
    if extra_den is not None:
        den = den + jnp.where(low, extra_den[0], extra_den[1])
    return num * (1.0 / den)


class _Head:
    def __init__(self, scores, finish, sink=None):
        self.scores, self.finish, self.sink = scores, finish, sink


def _attend(heads, batch):
    batches = [heads[i:i + batch] for i in range(0, len(heads), batch)]
    cur = [hd.scores() for hd in batches[0]]
    for bi, group in enumerate(batches):
        nxt = [hd.scores() for hd in batches[bi + 1]] if bi + 1 < len(batches) else None
        exps = []
        for hd, parts in zip(group, cur):
            chunks = [s[:, j * 128:(j + 1) * 128] for s in parts for j in range(s.shape[1] // 128)]
            m_el = functools.reduce(jnp.maximum, chunks)
            if hd.sink is not None:
                m_el = jnp.maximum(m_el, hd.sink)
            mx = jnp.broadcast_to(jnp.max(m_el, axis=-1, keepdims=True), m_el.shape)
            term = None if hd.sink is None else jnp.exp2(hd.sink - mx)
            e_parts = [jnp.concatenate([jnp.exp2(s[:, j * 128:(j + 1) * 128] - mx)
                                        for j in range(s.shape[1] // 128)], axis=1).astype(BF16) for s in parts]
            exps.append((e_parts, term))
        for hd, (e_parts, term) in zip(group, exps):
            hd.finish(e_parts, term)
        cur = nxt


class _PairSink:
    def __init__(self, y_ref, rows, cols):
        self.y_ref, self.rows, self.cols, self.first = y_ref, rows, cols, None

    def put(self, hh, o, term):
        if hh == 0:
            self.first = (o, term)
        else:
            o0, term0 = self.first
            extra = None if term is None else (term0, term)
            self.y_ref[self.rows, self.cols] = _merge_pair(o0, o, extra).astype(BF16)


def _mod_kernel(cond_ref, w_ref, b_ref, o_ref):
    c = cond_ref[...]
    s = c * (1.0 / (1.0 + jnp.exp(-c)))
    o_ref[0] = _dot(s.astype(BF16), w_ref[0].astype(BF16)) + b_ref[pl.ds(pl.program_id(0), 1), :]


def _modulation(cond, mod_w, mod_b):
    return pl.pallas_call(
        _mod_kernel,
        grid=(DEPTH, 6 * D_MODEL // MOD_TN),
        in_specs=[
            pl.BlockSpec((MOD_ROWS, D_MODEL), lambda l, j: (0, 0)),
            pl.BlockSpec((1, D_MODEL, MOD_TN), lambda l, j: (l, 0, j)),
            pl.BlockSpec((DEPTH, MOD_TN), lambda l, j: (0, j)),
        ],
        out_specs=pl.BlockSpec((1, MOD_ROWS, MOD_TN), lambda l, j: (l, 0, j)),
        out_shape=jax.ShapeDtypeStruct((DEPTH, MOD_ROWS, 6 * D_MODEL), F32),
        compiler_params=_cparams(2),
        name="adaln_modulation",
    )(cond, mod_w, mod_b)


def _norm_mod(x, g, mod, shift_col):
    sh = mod[:, shift_col:shift_col + D_MODEL]
    sc = mod[:, shift_col + D_MODEL:shift_col + 2 * D_MODEL]
    return _rms(x, g) * (1.0 + sc) + sh


def _rope(z, cos, sin):
    outs = []
    for j in range(z.shape[1] // 128):
        zj = z[:, j * 128:(j + 1) * 128]
        lane = lax.broadcasted_iota(jnp.int32, zj.shape, 1)
        partner = jnp.where((lane & 16) == 0, pltpu.roll(zj, 128 - 16, axis=1), pltpu.roll(zj, 16, axis=1))
        outs.append(zj * cos + partner * sin)
    return outs[0] if len(outs) == 1 else jnp.concatenate(outs, axis=1)


def _store_heads_transposed(dst, bi, pc_rows, n_pairs):
    for p in range(n_pairs):
        t = pc_rows[:, p * PAIR:(p + 1) * PAIR].T
        dst[bi, 0, 2 * p] = t[0:DH]
        dst[bi, 0, 2 * p + 1] = t[DH:PAIR]


def _mod_row(mod_ref, latent):
    if latent:
        return mod_ref[0, pl.ds(pl.program_id(0) // (DEC_SEQ // TM_PRE), 1), :]
    return mod_ref[0, CTX_ROW:CTX_ROW + 1, :]


def _normed_groups(x_ref, mod_ref, g_ref, li, latent):
    rows = [slice(i * SEQ, (i + 1) * SEQ) for i in range(TM_PRE // SEQ)]
    mod, g = _mod_row(mod_ref, latent), g_ref[li:li + 1, :]
    return rows, [_norm_mod(x_ref[r, :], g, mod, 0).astype(BF16) for r in rows]


def _k1_ctx_even(x_ref, mod_ref, g_ref, w_ref, proj_ref, dkt_ref, dv_ref, *, li):
    rows, hs = _normed_groups(x_ref, mod_ref, g_ref, li, False)
    for c in range(IN_EVEN // 512):
        for bi, (r, h) in enumerate(zip(rows, hs)):
            pc = _dot(h, w_ref[0, :, c * 512:(c + 1) * 512])
            if c < 3:
                proj_ref[r, c * 512:(c + 1) * 512] = pc.astype(BF16)
            elif c == 3:
                proj_ref[r, c * 512:(c + 1) * 512] = (pc * Q_SCALE).astype(BF16)
            elif c == 4:
                for hh in range(H_B):
                    t = pc[:, hh * PAIR:(hh + 1) * PAIR].T
                    dkt_ref[bi, 0, hh, 0] = t[0:DH]
                    dkt_ref[bi, 0, hh, 1] = t[DH:PAIR]
            else:
                for hh in range(H_B):
                    dv_ref[bi, 0, hh] = pc[:, hh * PAIR:(hh + 1) * PAIR]


def _k1_lat_even(x_ref, mod_ref, g_ref, w_ref, cos_ref, sin_ref, proj_ref, *, li):
    rows, hs = _normed_groups(x_ref, mod_ref, g_ref, li, True)
    for c in range(IN_EVEN // 512):
        for r, h in zip(rows, hs):
            pc = _dot(h, w_ref[0, :, c * 512:(c + 1) * 512])
            if c == 3:
                pc = _rope(pc * Q_SCALE, cos_ref[r, :], sin_ref[r, :])
            elif c == 4:
                pc = _rope(pc, cos_ref[r, :], sin_ref[r, :])
            proj_ref[r, c * 512:(c + 1) * 512] = pc.astype(BF16)


def _k1_ctx_odd(x_ref, mod_ref, g_ref, w_ref, q_ref, nkt_ref, nvt_ref, skt_ref, svt_ref, *, li):
    rows, hs = _normed_groups(x_ref, mod_ref, g_ref, li, False)
    for c in range(4):
        for bi, (r, h) in enumerate(zip(rows, hs)):
            pc = _dot(h, w_ref[0, :, c * 512:(c + 1) * 512])
            if c == 0 or c == 3:
                q_ref[r, (c // 3) * 512:(c // 3 + 1) * 512] = (pc * Q_SCALE).astype(BF16)
            else:
                _store_heads_transposed(nkt_ref if c == 1 else nvt_ref, bi, pc, H_C // 2)
    for bi, h in enumerate(hs):
        pc = _dot(h, w_ref[0, :, 2048:IN_ODD])
        _store_heads_transposed(skt_ref, bi, pc[:, 0:PAIR], 1)
        _store_heads_transposed(svt_ref, bi, pc[:, PAIR:2 * PAIR], 1)


def _k1_lat_odd(x_ref, mod_ref, g_ref, w_ref, cos_ref, sin_ref, proj_ref, *, li):
    rows, hs = _normed_groups(x_ref, mod_ref, g_ref, li, True)
    for c in range(4):
        for r, h in zip(rows, hs):
            pc = _dot(h, w_ref[0, :, c * 512:(c + 1) * 512])
            if c == 0:
                pc = pc * Q_SCALE
            elif c == 3:
                pc = _rope(pc * Q_SCALE, cos_ref[r, :], sin_ref[r, :])
            proj_ref[r, c * 512:(c + 1) * 512] = pc.astype(BF16)
    low = _low_half((SEQ, PAIR))
    for r, h in zip(rows, hs):
        pc = _dot(h, w_ref[0, :, 2048:IN_ODD])
        for j, z in enumerate((_rope(pc[:, 0:PAIR], cos_ref[r, :], sin_ref[r, :]), pc[:, PAIR:2 * PAIR])):
            zr = pltpu.roll(z, DH, axis=1)
            base = 2048 + j * 2 * PAIR
            proj_ref[r, base:base + PAIR] = jnp.where(low, z, zr).astype(BF16)
            proj_ref[r, base + PAIR:base + 2 * PAIR] = jnp.where(low, zr, z).astype(BF16)


def _pre_mixer(x, mod, gains, w, li, *, latent, even, rope, n_tok, tile0=0):
    n_in = w.shape[2]
    tm = TM_PRE
    tiles_per_seq = DEC_SEQ // tm
    in_specs = [
        pl.BlockSpec((tm, D_MODEL), lambda i: (i + tile0, 0)),
        pl.BlockSpec((1, MOD_ROWS, 6 * D_MODEL), lambda i: (li, 0, 0)),
        pl.BlockSpec((DEPTH, D_MODEL), lambda i: (0, 0)),
        _layer_block((D_MODEL, n_in), 0),
    ]
    args = [x, mod, gains, w]
    nb = tm // SEQ
    if latent:
        in_specs += [pl.BlockSpec((tm, 128), lambda i: (i % tiles_per_seq, 0))] * 2
        args += list(rope)
        body = _k1_lat_even if even else _k1_lat_odd
        n_out = IN_EVEN if even else LAT_ODD_COLS
        out_specs = pl.BlockSpec((tm, n_out), lambda i: (i, 0))
        out_shape = jax.ShapeDtypeStruct((n_tok, n_out), BF16)
    elif even:
        body = _k1_ctx_even
        out_specs = [pl.BlockSpec((tm, 4 * MIX_A), lambda i: (i, 0)),
                     pl.BlockSpec((nb, 1, H_B, 2, DH, SEQ), lambda i: (i, 0, 0, 0, 0, 0)),
                     pl.BlockSpec((nb, 1, H_B, SEQ, PAIR), lambda i: (i, 0, 0, 0, 0))]
        out_shape = [jax.ShapeDtypeStruct((n_tok, 4 * MIX_A), BF16),
                     jax.ShapeDtypeStruct((BATCH, 1, H_B, 2, DH, SEQ), F32),
                     jax.ShapeDtypeStruct((BATCH, 1, H_B, SEQ, PAIR), F32)]
    else:
        body = _k1_ctx_odd
        c_spec = pl.BlockSpec((nb, 1, H_C, DH, SEQ), lambda i: (i, 0, 0, 0, 0))
        d_spec = pl.BlockSpec((nb, 1, KV_D, DH, SEQ), lambda i: (i, 0, 0, 0, 0))
        c_shape = jax.ShapeDtypeStruct((BATCH, 1, H_C, DH, SEQ), F32)
        d_shape = jax.ShapeDtypeStruct((BATCH, 1, KV_D, DH, SEQ), F32)
        out_specs = [pl.BlockSpec((tm, D_MODEL), lambda i: (i, 0)), c_spec, c_spec, d_spec, d_spec]
        out_shape = [jax.ShapeDtypeStruct((n_tok, D_MODEL), BF16), c_shape, c_shape, d_shape, d_shape]
    return pl.pallas_call(
        functools.partial(body, li=li),
        grid=(n_tok // tm,),
        in_specs=in_specs,
        out_specs=out_specs,
        out_shape=out_shape,
        compiler_params=_cparams(1),
        name=f"pre_mixer_{'lat' if latent else 'ctx'}_{'even' if even else 'odd'}",
    )(*args)


def _even_mixer_kernel(*refs, seq, n_seq, tq, own_keys, lam_init):
    proj_ref, cw_ref, lq1_ref, lk1_ref, lq2_ref, lk2_ref, subln_ref, ckt_ref, cv_ref = refs[:9]
    y_ref = refs[-1]
    n_rows = n_seq * seq

    pos = lax.broadcasted_iota(jnp.int32, (n_rows, 128), 0) % seq
    for j in range(MIX_A // 128):
        cols = slice(j * 128, (j + 1) * 128)
        a_b = proj_ref[:, j * 128:(j + 1) * 128].astype(F32)
        u = (proj_ref[:, MIX_A + j * 128:MIX_A + (j + 1) * 128].astype(F32)
             * proj_ref[:, 2 * MIX_A + j * 128:2 * MIX_A + (j + 1) * 128].astype(F32))
        u_prev = jnp.where(pos == 0, 0.0, pltpu.roll(u, 1, axis=0))
        u_next = jnp.where(pos == seq - 1, 0.0, pltpu.roll(u, n_rows - 1, axis=0))
        w = cw_ref[0, :, cols]
        y_ref[:, cols] = (a_b * (w[0:1] * u_prev + w[1:2] * u + w[2:3] * u_next)).astype(BF16)

    lam = (jnp.exp(jnp.sum(lq1_ref[...] * lk1_ref[...], axis=-1, keepdims=True))
           - jnp.exp(jnp.sum(lq2_ref[...] * lk2_ref[...], axis=-1, keepdims=True)) + lam_init)
    subln = subln_ref[...]
    q_col, k_col, v_col = 3 * MIX_A, 3 * MIX_A + 512, 3 * MIX_A + 1024
    ones = jnp.ones((PAST_LEN, PAIR), BF16)

    memo = _Memo()

    def own_k(b, h, m):
        kp = proj_ref[b * seq:(b + 1) * seq, k_col + h * PAIR:k_col + (h + 1) * PAIR]
        return jnp.where(_keep_mask_bf16(seq, m), kp, jnp.zeros_like(kp))

    def value_ops(b, h):
        v_ops = []
        if own_keys:
            v_own = proj_ref[b * seq:(b + 1) * seq, v_col + h * PAIR:v_col + (h + 1) * PAIR]
            v_ops.append(jnp.concatenate([v_own, jnp.ones((seq, PAIR), BF16)], axis=1))
        v_ops.append(jnp.concatenate([cv_ref[b, 0, h].astype(BF16), ones], axis=1))
        return v_ops

    def scores(b, qrows, h, m):
        qp = proj_ref[qrows, q_col + h * PAIR:q_col + (h + 1) * PAIR]
        parts = []
        if own_keys:
            parts.append(_dot_nt(qp, memo.get_or_build(("k", b, h, m), functools.partial(own_k, b, h, m))))
        ckt = memo.get_or_build(("ck", b, h, m), lambda: _pad_rows(ckt_ref[b, 0, h, m].astype(BF16), m, 0.0))
        parts.append(_dot(qp, ckt))
        return parts

    first_map = {}

    def pv(b, qrows, h, m, e_parts, _):
        v_ops = memo.get_or_build(("v", b, h), functools.partial(value_ops, b, h))
        o = None
        for e, v_op in zip(e_parts, v_ops):
            t = _dot(e, v_op)
            o = t if o is None else o + t
        attn = o[:, 0:PAIR] * (1.0 / o[:, PAIR:2 * PAIR])
        if m == 0:
            first_map[(qrows.start, h)] = attn
        else:
            y = _rms(first_map.pop((qrows.start, h)) - lam * attn, subln) * (1.0 - lam_init)
            y_ref[qrows, MIX_A + h * PAIR:MIX_A + (h + 1) * PAIR] = y.astype(BF16)

    heads = []
    for b in range(n_seq):
        for i in range(seq // tq):
            qrows = slice(b * seq + i * tq, b * seq + (i + 1) * tq)
            for h in range(H_B):
                for m in range(2):
                    heads.append(_Head(functools.partial(scores, b, qrows, h, m),
                                       functools.partial(pv, b, qrows, h, m)))
    _attend(heads, batch=EVEN_BATCH_ROWS // tq)


def _even_mixer(proj, conv_w, lq1, lk1, lq2, lk2, subln, kt, v, *, seq, n_seq, own_keys, lam_init):
    n_tok = proj.shape[0]
    rows = n_seq * seq
    small = lambda a: pl.BlockSpec((1, a.shape[1]), lambda b: (0, 0))
    in_specs = [pl.BlockSpec((rows, proj.shape[1]), lambda b: (b, 0)),
                pl.BlockSpec((1, 3, MIX_A), lambda b: (0, 0, 0)),
                small(lq1), small(lk1), small(lq2), small(lk2), small(subln),
                pl.BlockSpec((n_seq, 1, H_B, 2, DH, PAST_LEN), lambda b: (b, 0, 0, 0, 0, 0)),
                pl.BlockSpec((n_seq, 1, H_B, PAST_LEN, PAIR), lambda b: (b, 0, 0, 0, 0))]
    return pl.pallas_call(
        functools.partial(_even_mixer_kernel, seq=seq, n_seq=n_seq, tq=min(seq, 256), own_keys=own_keys,
                          lam_init=lam_init),
        grid=(n_tok // rows,),
        in_specs=in_specs,
        out_specs=pl.BlockSpec((rows, D_MODEL), lambda b: (b, 0)),
        out_shape=jax.ShapeDtypeStruct((n_tok, D_MODEL), BF16),
        compiler_params=_cparams(1),
        name=f"even_mixer_{'lat' if own_keys else 'ctx'}",
    )(proj, conv_w, lq1, lk1, lq2, lk2, subln, kt, v)


def _odd_ctx_kernel(q_ref, nkt_ref, nvt_ref, skt_ref, svt_ref, sink_ref, y_ref):
    def kv_refs(b, head):
        if head < H_C:
            return nkt_ref.at[b, 0, head], nvt_ref.at[b, 0, head]
        g = (head - H_C) // GQA
        return skt_ref.at[b, 0, g], svt_ref.at[b, 0, g]

    memo = _Memo()

    def kv_key(head):
        return head if head < H_C else H_C + (head - H_C) // GQA

    def scores(b, p, hh):
        head = 2 * p + hh
        kt = memo.get_or_build(("k", b, kv_key(head), hh),
                               lambda: _pad_rows(kv_refs(b, head)[0][...].astype(BF16), hh, 0.0))
        return [_dot(q_ref[b * SEQ:(b + 1) * SEQ, p * PAIR:(p + 1) * PAIR], kt)]

    def finish(out, b, p, hh, e_parts, term):
        head = 2 * p + hh
        vt = memo.get_or_build(("v", b, kv_key(head), hh),
                               lambda: _pad_rows(kv_refs(b, head)[1][...].astype(BF16), hh, 1.0))
        out.put(hh, _dot_nt(e_parts[0], vt), term)

    heads = []
    for b in range(CTX_SEQS_PER_STEP):
        for p in range((H_C + H_D) // 2):
            out = _PairSink(y_ref, slice(b * SEQ, (b + 1) * SEQ), slice(p * PAIR, (p + 1) * PAIR))
            for hh in range(2):
                d_head = 2 * p + hh - H_C
                sink = sink_ref[d_head] * LOG2E if d_head >= 0 else None
                heads.append(_Head(functools.partial(scores, b, p, hh), functools.partial(finish, out, b, p, hh), sink))
    _attend(heads, batch=8)


def _odd_ctx_mixer(q, nkt, nvt, skt, svt, sink):
    n_tok = q.shape[0]
    n_seq = CTX_SEQS_PER_STEP
    c_spec = pl.BlockSpec((n_seq, 1, H_C, DH, SEQ), lambda b: (b, 0, 0, 0, 0))
    d_spec = pl.BlockSpec((n_seq, 1, KV_D, DH, SEQ), lambda b: (b, 0, 0, 0, 0))
    return pl.pallas_call(
        _odd_ctx_kernel,
        grid=(n_tok // (n_seq * SEQ),),
        in_specs=[pl.BlockSpec((n_seq * SEQ, D_MODEL), lambda b: (b, 0)), c_spec, c_spec, d_spec, d_spec,
                  pl.BlockSpec(memory_space=pltpu.SMEM)],
        out_specs=pl.BlockSpec((n_seq * SEQ, D_MODEL), lambda b: (b, 0)),
        out_shape=jax.ShapeDtypeStruct((n_tok, D_MODEL), BF16),
        compiler_params=_cparams(1),
        name="odd_mixer_ctx",
    )(q, nkt, nvt, skt, svt, sink)


def _build_na_bias(rpb_ref, head, nb_ref, slot):
    qc = lax.broadcasted_iota(jnp.int32, (GRID_W, GRID_W), 0)
    kc = lax.broadcasted_iota(jnp.int32, (GRID_W, GRID_W), 1)
    col_start = jnp.clip(qc - NA_WIN_C // 2, 0, GRID_W - NA_WIN_C)
    col_ok = (kc >= col_start) & (kc < col_start + NA_WIN_C)
    dc = kc - qc + NA_WIN_C - 1
    n_dr, n_dc = 2 * NA_WIN_R - 1, 2 * NA_WIN_C - 1
    neg = jnp.full((GRID_W, GRID_W), NEG, F32)
    toeplitz = []
    for dr in range(n_dr):
        t = neg
        for d in range(n_dc):
            t = jnp.where(dc == d, rpb_ref[(head * n_dr + dr) * n_dc + d] * LOG2E, t)
        toeplitz.append(jnp.where(col_ok, t, NEG))
    rows_per_half = NA_HALF // GRID_W
    key_rows = NA_KEYS // GRID_W
    for half in range(2):
        for rl in range(rows_per_half):
            r = half * rows_per_half + rl
            r_start = min(max(r - NA_WIN_R // 2, 0), GRID_H - NA_WIN_R)
            blocks = []
            for j in range(key_rows):
                rk = half * (GRID_H - key_rows) + j
                inside = r_start <= rk < r_start + NA_WIN_R
                blocks.append(toeplitz[rk - r + NA_WIN_R - 1] if inside else neg)
            nb_ref[slot, half, rl * GRID_W:(rl + 1) * GRID_W, :] = jnp.concatenate(blocks, axis=1)


def _na_kernel(rpb_ref, q_ref, k_ref, v_ref, kct_ref, vct_ref, y_ref, nb_ref):
    @pl.when(pl.program_id(1) == 0)
    def _():
        for hh in range(2):
            _build_na_bias(rpb_ref, 2 * pl.program_id(0) + hh, nb_ref, hh)

    def scores(b, half, hh):
        w0 = b * DEC_SEQ + half * (DEC_SEQ - NA_KEYS)
        qp = q_ref[b * DEC_SEQ + half * NA_HALF:b * DEC_SEQ + (half + 1) * NA_HALF, :]
        kw = k_ref[w0:w0 + NA_KEYS, :]
        s1 = _dot_nt(qp, jnp.where(_keep_mask_bf16(NA_KEYS, hh), kw, jnp.zeros_like(kw))) + nb_ref[hh, half]
        s2 = _dot(qp, _pad_rows(kct_ref[b, 0, hh].astype(BF16), hh, 0.0))
        return [s1, s2]

    def finish(out, b, half, hh, e_parts, term):
        w0 = b * DEC_SEQ + half * (DEC_SEQ - NA_KEYS)
        vw = v_ref[w0:w0 + NA_KEYS, :]
        o = (_dot(e_parts[0], jnp.where(_keep_mask_bf16(NA_KEYS, hh), vw, jnp.ones_like(vw)))
             + _dot_nt(e_parts[1], _pad_rows(vct_ref[b, 0, hh].astype(BF16), hh, 1.0)))
        out.put(hh, o, term)

    heads = []
    for b in range(NA_SEQS_PER_STEP):
        for half in range(2):
            rows = slice(b * DEC_SEQ + half * NA_HALF, b * DEC_SEQ + (half + 1) * NA_HALF)
            out = _PairSink(y_ref, rows, slice(0, PAIR))
            for hh in range(2):
                heads.append(_Head(functools.partial(scores, b, half, hh), functools.partial(finish, out, b, half, hh)))
    _attend(heads, batch=4)


def _na_mixer(proj, cache_kt, cache_vt, rpb_flat):
    pairs = H_C // 2
    n_seq = NA_SEQS_PER_STEP
    col = lambda base: pl.BlockSpec((n_seq * DEC_SEQ, PAIR), lambda hp, b: (b, base + hp))
    cache = pl.BlockSpec((n_seq, 1, 2, DH, PAST_LEN), lambda hp, b: (b, 0, hp, 0, 0))
    return pl.pallas_call(
        _na_kernel,
        grid=(pairs, DEC_BATCH // n_seq),
        in_specs=[pl.BlockSpec(memory_space=pltpu.SMEM), col(0), col(pairs), col(2 * pairs), cache, cache],
        out_specs=pl.BlockSpec((n_seq * DEC_SEQ, PAIR), lambda hp, b: (b, hp)),
        out_shape=jax.ShapeDtypeStruct((LAT_TOKENS, 512), BF16),
        scratch_shapes=[pltpu.VMEM((2, 2, NA_HALF, NA_KEYS), F32)],
        compiler_params=_cparams(2),
        name="na_mixer",
    )(rpb_flat, proj, proj, proj, cache_kt, cache_vt)


def _swa_kernel(q_ref, kv_ref, kct_ref, vct_ref, sink_ref, y_ref):
    n_groups = DEC_SEQ // SWA_QROWS
    ql = lax.broadcasted_iota(jnp.int32, (SWA_QROWS, SWA_BAND), 0)
    kj = lax.broadcasted_iota(jnp.int32, (SWA_QROWS, SWA_BAND), 1)

    def band_bias(first_key_minus_first_query):
        return jnp.where(jnp.abs(kj + first_key_minus_first_query - ql) <= SWA_WINDOW, 0.0, NEG)

    bias_first, bias_mid, bias_last = band_bias(0), band_bias(-SWA_BLOCK), band_bias(-2 * SWA_BLOCK)

    def band_start(grp):
        return min(max(grp * SWA_QROWS - SWA_BLOCK, 0), DEC_SEQ - SWA_BAND)

    memo = _Memo()

    def band_k(g, grp, hh):
        start = band_start(grp)
        kb = kv_ref[start:start + SWA_BAND, g * PAIR:(g + 1) * PAIR]
        return jnp.where(_keep_mask_bf16(SWA_BAND, hh), kb, jnp.zeros_like(kb))

    def band_v(g, grp, hh):
        start = band_start(grp)
        vb = kv_ref[start:start + SWA_BAND, (KV_D + g) * PAIR:(KV_D + g + 1) * PAIR]
        return jnp.where(_keep_mask_bf16(SWA_BAND, hh), vb, jnp.ones_like(vb))

    def scores(g, grp, p, hh):
        bias = bias_first if grp == 0 else (bias_last if grp == n_groups - 1 else bias_mid)
        qp = q_ref[grp * SWA_QROWS:(grp + 1) * SWA_QROWS, p * PAIR:(p + 1) * PAIR]
        s1 = _dot_nt(qp, memo.get_or_build(("k", g, grp, hh), functools.partial(band_k, g, grp, hh))) + bias
        ckt = memo.get_or_build(("ck", g, hh), lambda: _pad_rows(kct_ref[0, 0, g].astype(BF16), hh, 0.0))
        return [s1, _dot(qp, ckt)]

    def finish(out, g, grp, hh, e_parts, term):
        cvt = memo.get_or_build(("cv", g, hh), lambda: _pad_rows(vct_ref[0, 0, g].astype(BF16), hh, 1.0))
        o = (_dot(e_parts[0], memo.get_or_build(("v", g, grp, hh), functools.partial(band_v, g, grp, hh)))
             + _dot_nt(e_parts[1], cvt))
        out.put(hh, o, term)

    heads = []
    for g in range(KV_D):
        for grp in range(n_groups):
            for j in range(GQA // 2):
                p = g * (GQA // 2) + j
                out = _PairSink(y_ref, slice(grp * SWA_QROWS, (grp + 1) * SWA_QROWS), slice(p * PAIR, (p + 1) * PAIR))
                for hh in range(2):
                    sink = sink_ref[2 * p + hh] * LOG2E
                    heads.append(_Head(functools.partial(scores, g, grp, p, hh),
                                       functools.partial(finish, out, g, grp, hh), sink))
    _attend(heads, batch=16)


def _swa_mixer(proj, cache_kt, cache_vt, sink):
    cache = pl.BlockSpec((1, 1, KV_D, DH, PAST_LEN), lambda b: (b, 0, 0, 0, 0))
    return pl.pallas_call(
        _swa_kernel,
        grid=(DEC_BATCH,),
        in_specs=[pl.BlockSpec((DEC_SEQ, 512), lambda b: (b, 3)),
                  pl.BlockSpec((DEC_SEQ, 512), lambda b: (b, 4)),
                  cache, cache,
                  pl.BlockSpec(memory_space=pltpu.SMEM)],
        out_specs=pl.BlockSpec((DEC_SEQ, 512), lambda b: (b, 0)),
        out_shape=jax.ShapeDtypeStruct((LAT_TOKENS, 512), BF16),
        compiler_params=_cparams(1),
        name="swa_mixer",
    )(proj, proj, cache_kt, cache_vt, sink)


def _load_weights_bf16(li, wo_hbm, w1_hbm, w2_hbm, wo_s, w1_s, w2_s, stage_sq, stage_wide, sems):
    square = ([(wo_hbm, wo_s, j, WO_ROWS) for j in range(N_WO)] + [(w2_hbm, w2_s, j, W2_ROWS) for j in range(N_W2)])
    wide = [(w1_hbm, w1_s, j, W1_ROWS) for j in range(N_W1)]
    queues = {"sq": (square, stage_sq, 0), "wide": (wide, stage_wide, STAGE_SLOTS)}

    def copy(kind, idx):
        chunks, stage, sem0 = queues[kind]
        src, _, j, n_rows = chunks[idx]
        slot = idx % STAGE_SLOTS
        return pltpu.make_async_copy(src.at[li, j * n_rows:(j + 1) * n_rows, :], stage.at[slot], sems.at[sem0 + slot])

    for kind in queues:
        for idx in range(min(STAGE_SLOTS, len(queues[kind][0]))):
            copy(kind, idx).start()
    order = []
    for i in range(max(len(square), len(wide))):
        order += [("sq", i)] * (i < len(square)) + [("wide", i)] * (i < len(wide))
    for kind, idx in order:
        chunks, stage, _ = queues[kind]
        _, dst, j, n_rows = chunks[idx]
        copy(kind, idx).wait()
        dst[j * n_rows:(j + 1) * n_rows, :] = stage[idx % STAGE_SLOTS].astype(BF16)
        if idx + STAGE_SLOTS < len(chunks):
            copy(kind, idx + STAGE_SLOTS).start()


def _post_mixer_kernel(*refs, li, n_lat_mix, n_x, n_out):
    mix_ctx_ref, mix_lat_refs = refs[0], refs[1:1 + n_lat_mix]
    refs = refs[1 + n_lat_mix:]
    x_refs = refs[:n_x]
    mod_ref, gpost_ref, gpre_ref, gmlp_ref, wo_hbm, w1_hbm, w2_hbm = refs[n_x:7 + n_x]
    out_refs = refs[7 + n_x:7 + n_x + n_out]
    wo_s, w1_s, w2_s, stage_sq, stage_wide, sems = refs[7 + n_x + n_out:]
    step = pl.program_id(0)

    @pl.when(step == 0)
    def _():
        _load_weights_bf16(li, wo_hbm, w1_hbm, w2_hbm, wo_s, w1_s, w2_s, stage_sq, stage_wide, sems)

    is_ctx = step < CTX_TILES
    mod = mod_ref[0, pl.ds(jnp.where(is_ctx, CTX_ROW, (step - CTX_TILES) // (DEC_SEQ // TM)), 1), :]
    g_post, g_pre, g_mlp = gpost_ref[li:li + 1, :], gpre_ref[li:li + 1, :], gmlp_ref[li:li + 1, :]
    rows = [slice(i * (TM // POST_SPLIT), (i + 1) * (TM // POST_SPLIT)) for i in range(POST_SPLIT)]

    def residual(r):
        if n_x == 1:
            return x_refs[0][r, :]
        return jnp.where(is_ctx, x_refs[0][r, :], x_refs[1][r, :])

    def mixed(r):
        lat = [m[r, :] for m in mix_lat_refs]
        lat = lat[0] if n_lat_mix == 1 else jnp.concatenate(lat, axis=1)
        return jnp.where(is_ctx, mix_ctx_ref[r, :], lat)

    ys = [_dot(mixed(r), wo_s[...]) for r in rows]
    x1 = [residual(r) + mod[:, 2 * D_MODEL:3 * D_MODEL] * _rms(y, g_post) for r, y in zip(rows, ys)]
    h = [_norm_mod(x, g_pre, mod, 3 * D_MODEL).astype(BF16) for x in x1]
    acc = [None] * POST_SPLIT
    pending = []

    def second_matmul(c, i, f):
        t2 = _dot(f, w2_s[c * FF_CHUNK:(c + 1) * FF_CHUNK, :])
        acc[i] = t2 if acc[i] is None else acc[i] + t2

    for c in range(D_FF // FF_CHUNK):
        for i in range(POST_SPLIT):
            f = _dot(h[i], w1_s[:, c * FF_CHUNK:(c + 1) * FF_CHUNK])
            pending.append((c, i, jnp.square(jnp.maximum(f, 0.0)).astype(BF16)))
            if len(pending) > 1:
                second_matmul(*pending.pop(0))
    second_matmul(*pending.pop(0))
    res = [x1[i] + mod[:, 5 * D_MODEL:6 * D_MODEL] * _rms(acc[i], g_mlp) for i in range(POST_SPLIT)]
    if n_out == 1:
        for i, r in enumerate(rows):
            out_refs[0][r, :] = res[i]
    else:
        @pl.when(is_ctx)
        def _():
            for i, r in enumerate(rows):
                out_refs[0][r, :] = res[i]

        @pl.when(jnp.logical_not(is_ctx))
        def _():
            for i, r in enumerate(rows):
                out_refs[1][r, :] = res[i]


def _post_mixer(mix_ctx, mix_lat, xs, mod, g_post, g_pre, g_mlp, w_out, w1, w2, li, *, split_out):
    tile = lambda s: s
    ctx_tile = lambda s: jnp.minimum(tile(s), CTX_TILES - 1)
    lat_tile = lambda s: jnp.maximum(tile(s) - CTX_TILES, 0)
    token_spec = lambda index, width=D_MODEL: pl.BlockSpec((TM, width), lambda s: (index(s), 0))
    mix_specs = [token_spec(ctx_tile)] + [token_spec(lat_tile, m.shape[1]) for m in mix_lat]
    x_specs = [token_spec(tile)] if len(xs) == 1 else [token_spec(ctx_tile), token_spec(lat_tile)]
    if split_out:
        out_specs = [token_spec(ctx_tile), token_spec(lat_tile)]
        out_shape = [jax.ShapeDtypeStruct((CTX_TOKENS, D_MODEL), F32), jax.ShapeDtypeStruct((LAT_TOKENS, D_MODEL), F32)]
    else:
        out_specs = [token_spec(tile)]
        out_shape = [jax.ShapeDtypeStruct((ALL_TOKENS, D_MODEL), F32)]
    gain = pl.BlockSpec((DEPTH, D_MODEL), lambda s: (0, 0))
    return pl.pallas_call(
        functools.partial(_post_mixer_kernel, li=li, n_lat_mix=len(mix_lat), n_x=len(xs), n_out=len(out_specs)),
        grid=(ALL_TOKENS // TM,),
        in_specs=mix_specs + x_specs + [
            pl.BlockSpec((1, MOD_ROWS, 6 * D_MODEL), lambda s: (li, 0, 0)),
            gain, gain, gain] + [pl.BlockSpec(memory_space=pl.ANY)] * 3,
        out_specs=out_specs,
        out_shape=out_shape,
        scratch_shapes=[pltpu.VMEM((D_MODEL, D_MODEL), BF16), pltpu.VMEM((D_MODEL, D_FF), BF16),
                        pltpu.VMEM((D_FF, D_MODEL), BF16),
                        pltpu.VMEM((STAGE_SLOTS, WO_ROWS, D_MODEL), F32), pltpu.VMEM((STAGE_SLOTS, W1_ROWS, D_FF), F32),
                        pltpu.SemaphoreType.DMA((2 * STAGE_SLOTS,))],
        compiler_params=_cparams(1),
        name="post_mixer",
    )(mix_ctx, *mix_lat, *xs, mod, g_post, g_pre, g_mlp, w_out, w1, w2)


def _rope_tables():
    t = jnp.arange(DEC_SEQ)
    rows = (t // GRID_W).astype(F32)
    cols = (t % GRID_W).astype(F32)
    q4 = DH // 4
    inv = 1.0 / (ROPE_BASE ** (jnp.arange(q4, dtype=F32) / q4))
    ar, ac = rows[:, None] * inv, cols[:, None] * inv
    cos64 = jnp.concatenate([jnp.cos(ar), jnp.cos(ar), jnp.cos(ac), jnp.cos(ac)], axis=1)
    sin64 = jnp.concatenate([-jnp.sin(ar), jnp.sin(ar), -jnp.sin(ac), jnp.sin(ac)], axis=1)
    return jnp.tile(cos64, (1, 2)), jnp.tile(sin64, (1, 2))


def kernel(x_prompt, x_sample, cache_diff_k, cache_diff_v, cache_na_k, cache_na_v, cache_swa_k, cache_swa_v, c, c_ctx, mod_w, mod_b, norm_mix_pre, norm_mix_post, norm_mlp_pre, norm_mlp_post, w_in_even, conv_w, lambda_q1, lambda_k1, lambda_q2, lambda_k2, subln, w_in_odd, rpb, sink, w_out, mlp_w1, mlp_w2):
    cond = jnp.concatenate([c, c_ctx[None, :], jnp.zeros((MOD_ROWS - DEC_BATCH - 1, D_MODEL), F32)], axis=0)
    mod = _modulation(cond, mod_w, mod_b)
    rope = _rope_tables()

    t_minor = lambda a: jnp.swapaxes(a, -1, -2)

    w_in = [w_in_even.astype(BF16), w_in_odd.astype(BF16)]

    xp = x_prompt.reshape(CTX_TOKENS, D_MODEL)
    xs = x_sample.reshape(LAT_TOKENS, D_MODEL)
    lam_init0 = 0.8 - 0.6 * math.exp(-0.3 * 0)
    lam_args = (conv_w, lambda_q1, lambda_k1, lambda_q2, lambda_k2, subln)
    sink_flat = sink.reshape(-1)

    def post(li, mix_ctx, mix_lat, x_parts, split_out):
        return _post_mixer(mix_ctx, mix_lat, x_parts, mod, norm_mix_post, norm_mlp_pre, norm_mlp_post, w_out, mlp_w1,
                           mlp_w2, li, split_out=split_out)

    proj, new_diff_kt, new_diff_v = _pre_mixer(xp, mod, norm_mix_pre, w_in[0], 0, latent=False, even=True, rope=None,
                                               n_tok=CTX_TOKENS)
    mix_ctx = _even_mixer(proj, *lam_args, new_diff_kt, new_diff_v, seq=SEQ, n_seq=CTX_SEQS_PER_STEP // 2,
                          own_keys=False, lam_init=lam_init0)
    proj = _pre_mixer(xs, mod, norm_mix_pre, w_in[0], 0, latent=True, even=True, rope=rope, n_tok=LAT_TOKENS)
    mix_lat = _even_mixer(proj, *lam_args, t_minor(cache_diff_k), cache_diff_v, seq=DEC_SEQ, n_seq=1, own_keys=True,
                          lam_init=lam_init0)
    (x_all,) = post(0, mix_ctx, [mix_lat], [xp, xs], False)

    q, new_na_kt, new_na_vt, new_swa_kt, new_swa_vt = _pre_mixer(
        x_all, mod, norm_mix_pre, w_in[1], 1, latent=False, even=False, rope=None, n_tok=CTX_TOKENS)
    mix_ctx = _odd_ctx_mixer(q, new_na_kt, new_na_vt, new_swa_kt, new_swa_vt, sink_flat)
    proj = _pre_mixer(x_all, mod, norm_mix_pre, w_in[1], 1, latent=True, even=False, rope=rope, n_tok=LAT_TOKENS,
                      tile0=CTX_TOKENS // TM_PRE)
    mix_c = _na_mixer(proj, t_minor(cache_na_k), t_minor(cache_na_v), rpb.reshape(-1))
    mix_d = _swa_mixer(proj, t_minor(cache_swa_k), t_minor(cache_swa_v), sink_flat)
    xp, xs = post(1, mix_ctx, [mix_c, mix_d], [x_all], True)

    return (xp.reshape(BATCH, SEQ, D_MODEL), xs.reshape(DEC_BATCH, DEC_SEQ, D_MODEL),
            t_minor(new_diff_kt), new_diff_v, t_minor(new_na_kt), t_minor(new_na_vt),
            t_minor(new_swa_kt), t_minor(new_swa_vt))
```

```python
import functools
import math

import jax
import jax.numpy as jnp
from jax import lax
from jax.experimental import pallas as pl
from jax.experimental.pallas import tpu as pltpu

F32 = jnp.float32
BF16 = jnp.bfloat16

D_MODEL = 1024
BATCH = 32
SEQ = 256
DEPTH = 2
DEC_BATCH = 8
DEC_SEQ = 1024
PAST_LEN = 256
GRID_W = 64
GRID_H = DEC_SEQ // GRID_W
MIX_A = 512
H_B = 4
DH = 64
PAIR = 2 * DH
H_C = 8
H_D = 8
KV_D = 2
GQA = H_D // KV_D
IN_EVEN = 3072
IN_ODD = 2304
D_FF = 4096
NA_WIN_R = 8
NA_WIN_C = 16
SWA_BLOCK = 128
SWA_WINDOW = 128
ROPE_BASE = 10000.0
EPS = 1e-6
NEG = -1e30
LOG2E = math.log2(math.e)
Q_SCALE = DH ** -0.5 * LOG2E

MOD_ROWS = 16
CTX_ROW = DEC_BATCH
V7X_VMEM_BYTES = 64 * 1024 * 1024
VMEM_LIMIT = V7X_VMEM_BYTES - 12 * 1024 * 1024

CTX_TOKENS = BATCH * SEQ
LAT_TOKENS = DEC_BATCH * DEC_SEQ
ALL_TOKENS = CTX_TOKENS + LAT_TOKENS

TM = 512
TM_PRE = 1024
FF_CHUNK = 1024
POST_SPLIT = 2
WCHUNK_ELEMS = 128 * 1024
WO_ROWS, W1_ROWS, W2_ROWS = WCHUNK_ELEMS // D_MODEL, WCHUNK_ELEMS // D_FF, WCHUNK_ELEMS // D_MODEL
N_WO, N_W1, N_W2 = D_MODEL // WO_ROWS, D_MODEL // W1_ROWS, D_FF // W2_ROWS
STAGE_SLOTS = 4
CTX_TILES = CTX_TOKENS // TM
CTX_SEQS_PER_STEP = 4
CTX_SEQS_PER_STEP_EVEN = 2
MOD_TN = 3072
EVEN_BATCH_ROWS = 2048
NA_SEQS_PER_STEP = 2
NA_HALF = 512
NA_KEYS = 768
SWA_QROWS = 2 * SWA_BLOCK
SWA_BAND = 4 * SWA_BLOCK
LAT_ODD_COLS = IN_ODD + 2 * PAIR


def _cparams(n_axes):
    return pltpu.CompilerParams(dimension_semantics=("arbitrary",) * n_axes, vmem_limit_bytes=VMEM_LIMIT)


def _rms(x, g):
    return x * lax.rsqrt(jnp.mean(x * x, axis=-1, keepdims=True) + EPS) * g


def _dot(a, b):
    return jnp.dot(a, b, preferred_element_type=F32)


def _dot_nt(a, b):
    return lax.dot_general(a, b, (((1,), (1,)), ((), ())), preferred_element_type=F32)


def _layer_block(shape, li):
    zeros = (0,) * len(shape)
    return pl.BlockSpec((1,) + tuple(shape), lambda *_: (li,) + zeros, pipeline_mode=pl.Buffered(1))


def _low_half(shape):
    return (lax.broadcasted_iota(jnp.int32, shape, len(shape) - 1) & DH) == 0


def _keep_mask_bf16(rows, hh):
    lane = lax.broadcasted_iota(jnp.int32, (rows, PAIR), 1)
    half = (lane & DH).astype(F32).astype(BF16)
    return (half == 0) if hh == 0 else (half != 0)


def _pad_rows(t, hh, fill):
    other = jnp.full(t.shape, fill, t.dtype)
    return jnp.concatenate([t, other] if hh == 0 else [other, t], axis=0)


class _Memo(dict):
    def get_or_build(self, key, build):
        if key not in self:
            self[key] = build()
        return self[key]


def _merge_pair(o0, o1, extra_den=None):
    low = _low_half(o0.shape)
    num = jnp.where(low, o0, o1)
    den = pltpu.roll(jnp.where(low, o1, o0), DH, axis=1)
    if extra_den is not None:
        den = den + jnp.where(low, extra_den[0], extra_den[1])
    return num * (1.0 / den)


class _Head:
    def __init__(self, scores, finish, sink=None):
        self.scores, self.finish, self.sink = scores, finish, sink


def _attend(heads, batch):
    batches = [heads[i:i + batch] for i in range(0, len(heads), batch)]
    cur = [hd.scores() for hd in batches[0]]
    for bi, group in enumerate(batches):
        nxt = [hd.scores() for hd in batches[bi + 1]] if bi + 1 < len(batches) else None
        exps = []
        for hd, parts in zip(group, cur):
            chunks = [s[:, j * 128:(j + 1) * 128] for s in parts for j in range(s.shape[1] // 128)]
            m_el = functools.reduce(jnp.maximum, chunks)
            if hd.sink is not None:
                m_el = jnp.maximum(m_el, hd.sink)
            mx = jnp.broadcast_to(jnp.max(m_el, axis=-1, keepdims=True), m_el.shape)
            term = None if hd.sink is None else jnp.exp2(hd.sink - mx)
            e_parts = [jnp.concatenate([jnp.exp2(s[:, j * 128:(j + 1) * 128] - mx)
                                        for j in range(s.shape[1] // 128)], axis=1).astype(BF16) for s in parts]
            exps.append((e_parts, term))
        for hd, (e_parts, term) in zip(group, exps):
            hd.finish(e_parts, term)
        cur = nxt


class _PairSink:
    def __init__(self, y_ref, rows, cols):
        self.y_ref, self.rows, self.cols, self.first = y_ref, rows, cols, None

    def put(self, hh, o, term):
        if hh == 0:
            self.first = (o, term)
        else:
            o0, term0 = self.first
            extra = None if term is None else (term0, term)
            self.y_ref[self.rows, self.cols] = _merge_pair(o0, o, extra).astype(BF16)


def _mod_kernel(cond_ref, w_ref, b_ref, win_e_ref, win_o_ref, o_ref, win_e_bf, win_o_bf):
    c = cond_ref[...]
    s = c * (1.0 / (1.0 + jnp.exp(-c)))
    o_ref[0] = _dot(s.astype(BF16), w_ref[0].astype(BF16)) + b_ref[pl.ds(pl.program_id(0), 1), :]
    win_e_bf[...] = win_e_ref[...].astype(BF16)
    win_o_bf[...] = win_o_ref[...].astype(BF16)


def _modulation(cond, mod_w, mod_b, w_in_even, w_in_odd):
    n_col = 6 * D_MODEL // MOD_TN
    slab = D_MODEL // (DEPTH * n_col)
    w_slab = lambda width: pl.BlockSpec((1, slab, width), lambda l, j: (0, l * n_col + j, 0))
    return pl.pallas_call(
        _mod_kernel,
        grid=(DEPTH, n_col),
        in_specs=[
            pl.BlockSpec((MOD_ROWS, D_MODEL), lambda l, j: (0, 0)),
            pl.BlockSpec((1, D_MODEL, MOD_TN), lambda l, j: (l, 0, j)),
            pl.BlockSpec((DEPTH, MOD_TN), lambda l, j: (0, j)),
            w_slab(IN_EVEN), w_slab(IN_ODD),
        ],
        out_specs=[pl.BlockSpec((1, MOD_ROWS, MOD_TN), lambda l, j: (l, 0, j)), w_slab(IN_EVEN), w_slab(IN_ODD)],
        out_shape=[jax.ShapeDtypeStruct((DEPTH, MOD_ROWS, 6 * D_MODEL), F32),
                   jax.ShapeDtypeStruct((1, D_MODEL, IN_EVEN), BF16), jax.ShapeDtypeStruct((1, D_MODEL, IN_ODD), BF16)],
        compiler_params=_cparams(2),
        name="adaln_modulation",
    )(cond, mod_w, mod_b, w_in_even, w_in_odd)


def _norm_mod(x, g, mod, shift_col):
    sh = mod[:, shift_col:shift_col + D_MODEL]
    sc = mod[:, shift_col + D_MODEL:shift_col + 2 * D_MODEL]
    return _rms(x, g) * (1.0 + sc) + sh


def _rope(z, cos, sin):
    outs = []
    for j in range(z.shape[1] // 128):
        zj = z[:, j * 128:(j + 1) * 128]
        lane = lax.broadcasted_iota(jnp.int32, zj.shape, 1)
        partner = jnp.where((lane & 16) == 0, pltpu.roll(zj, 128 - 16, axis=1), pltpu.roll(zj, 16, axis=1))
        outs.append(zj * cos + partner * sin)
    return outs[0] if len(outs) == 1 else jnp.concatenate(outs, axis=1)


def _store_heads_transposed(dst, bi, pc_rows, n_pairs):
    for p in range(n_pairs):
        t = pc_rows[:, p * PAIR:(p + 1) * PAIR].T
        dst[bi, 0, 2 * p] = t[0:DH]
        dst[bi, 0, 2 * p + 1] = t[DH:PAIR]


def _mod_row(mod_ref, latent):
    if latent:
        return mod_ref[0, pl.ds(pl.program_id(0) // (DEC_SEQ // TM_PRE), 1), :]
    return mod_ref[0, CTX_ROW:CTX_ROW + 1, :]


def _normed_groups(x_ref, mod_ref, g_ref, li, latent):
    rows = [slice(i * SEQ, (i + 1) * SEQ) for i in range(TM_PRE // SEQ)]
    mod, g = _mod_row(mod_ref, latent), g_ref[li:li + 1, :]
    return rows, [_norm_mod(x_ref[r, :], g, mod, 0).astype(BF16) for r in rows]


def _k1_ctx_even(x_ref, mod_ref, g_ref, w_ref, proj_ref, dkt_ref, dv_ref, *, li):
    rows, hs = _normed_groups(x_ref, mod_ref, g_ref, li, False)
    for c in range(IN_EVEN // 512):
        for bi, (r, h) in enumerate(zip(rows, hs)):
            pc = _dot(h, w_ref[0, :, c * 512:(c + 1) * 512])
            if c < 3:
                proj_ref[r, c * 512:(c + 1) * 512] = pc.astype(BF16)
            elif c == 3:
                proj_ref[r, c * 512:(c + 1) * 512] = (pc * Q_SCALE).astype(BF16)
            elif c == 4:
                for hh in range(H_B):
                    t = pc[:, hh * PAIR:(hh + 1) * PAIR].T
                    dkt_ref[bi, 0, hh, 0] = t[0:DH]
                    dkt_ref[bi, 0, hh, 1] = t[DH:PAIR]
            else:
                for hh in range(H_B):
                    dv_ref[bi, 0, hh] = pc[:, hh * PAIR:(hh + 1) * PAIR]


def _k1_lat_even(x_ref, mod_ref, g_ref, w_ref, cos_ref, sin_ref, proj_ref, *, li):
    rows, hs = _normed_groups(x_ref, mod_ref, g_ref, li, True)
    for c in range(IN_EVEN // 512):
        for r, h in zip(rows, hs):
            pc = _dot(h, w_ref[0, :, c * 512:(c + 1) * 512])
            if c == 3:
                pc = _rope(pc * Q_SCALE, cos_ref[r, :], sin_ref[r, :])
            elif c == 4:
                pc = _rope(pc, cos_ref[r, :], sin_ref[r, :])
            proj_ref[r, c * 512:(c + 1) * 512] = pc.astype(BF16)


def _k1_ctx_odd(x_ref, mod_ref, g_ref, w_ref, q_ref, nkt_ref, nvt_ref, skt_ref, svt_ref, *, li):
    rows, hs = _normed_groups(x_ref, mod_ref, g_ref, li, False)
    for c in range(4):
        for bi, (r, h) in enumerate(zip(rows, hs)):
            pc = _dot(h, w_ref[0, :, c * 512:(c + 1) * 512])
            if c == 0 or c == 3:
                q_ref[r, (c // 3) * 512:(c // 3 + 1) * 512] = (pc * Q_SCALE).astype(BF16)
            else:
                _store_heads_transposed(nkt_ref if c == 1 else nvt_ref, bi, pc, H_C // 2)
    for bi, h in enumerate(hs):
        pc = _dot(h, w_ref[0, :, 2048:IN_ODD])
        _store_heads_transposed(skt_ref, bi, pc[:, 0:PAIR], 1)
        _store_heads_transposed(svt_ref, bi, pc[:, PAIR:2 * PAIR], 1)


def _k1_lat_odd(x_ref, mod_ref, g_ref, w_ref, cos_ref, sin_ref, proj_ref, *, li):
    rows, hs = _normed_groups(x_ref, mod_ref, g_ref, li, True)
    for c in range(4):
        for r, h in zip(rows, hs):
            pc = _dot(h, w_ref[0, :, c * 512:(c + 1) * 512])
            if c == 0:
                pc = pc * Q_SCALE
            elif c == 3:
                pc = _rope(pc * Q_SCALE, cos_ref[r, :], sin_ref[r, :])
            proj_ref[r, c * 512:(c + 1) * 512] = pc.astype(BF16)
    low = _low_half((SEQ, PAIR))
    for r, h in zip(rows, hs):
        pc = _dot(h, w_ref[0, :, 2048:IN_ODD])
        for j, z in enumerate((_rope(pc[:, 0:PAIR], cos_ref[r, :], sin_ref[r, :]), pc[:, PAIR:2 * PAIR])):
            zr = pltpu.roll(z, DH, axis=1)
            base = 2048 + j * 2 * PAIR
            proj_ref[r, base:base + PAIR] = jnp.where(low, z, zr).astype(BF16)
            proj_ref[r, base + PAIR:base + 2 * PAIR] = jnp.where(low, zr, z).astype(BF16)


def _pre_mixer(x, mod, gains, w, li, *, latent, even, rope, n_tok, tile0=0):
    n_in = w.shape[2]
    tm = TM_PRE
    tiles_per_seq = DEC_SEQ // tm
    in_specs = [
        pl.BlockSpec((tm, D_MODEL), lambda i: (i + tile0, 0)),
        pl.BlockSpec((1, MOD_ROWS, 6 * D_MODEL), lambda i: (li, 0, 0)),
        pl.BlockSpec((DEPTH, D_MODEL), lambda i: (0, 0)),
        _layer_block((D_MODEL, n_in), 0),
    ]
    args = [x, mod, gains, w]
    nb = tm // SEQ
    if latent:
        in_specs += [pl.BlockSpec((tm, 128), lambda i: (i % tiles_per_seq, 0))] * 2
        args += list(rope)
        body = _k1_lat_even if even else _k1_lat_odd
        n_out = IN_EVEN if even else LAT_ODD_COLS
        out_specs = pl.BlockSpec((tm, n_out), lambda i: (i, 0))
        out_shape = jax.ShapeDtypeStruct((n_tok, n_out), BF16)
    elif even:
        body = _k1_ctx_even
        out_specs = [pl.BlockSpec((tm, 4 * MIX_A), lambda i: (i, 0)),
                     pl.BlockSpec((nb, 1, H_B, 2, DH, SEQ), lambda i: (i, 0, 0, 0, 0, 0)),
                     pl.BlockSpec((nb, 1, H_B, SEQ, PAIR), lambda i: (i, 0, 0, 0, 0))]
        out_shape = [jax.ShapeDtypeStruct((n_tok, 4 * MIX_A), BF16),
                     jax.ShapeDtypeStruct((BATCH, 1, H_B, 2, DH, SEQ), F32),
                     jax.ShapeDtypeStruct((BATCH, 1, H_B, SEQ, PAIR), F32)]
    else:
        body = _k1_ctx_odd
        c_spec = pl.BlockSpec((nb, 1, H_C, DH, SEQ), lambda i: (i, 0, 0, 0, 0))
        d_spec = pl.BlockSpec((nb, 1, KV_D, DH, SEQ), lambda i: (i, 0, 0, 0, 0))
        c_shape = jax.ShapeDtypeStruct((BATCH, 1, H_C, DH, SEQ), F32)
        d_shape = jax.ShapeDtypeStruct((BATCH, 1, KV_D, DH, SEQ), F32)
        out_specs = [pl.BlockSpec((tm, D_MODEL), lambda i: (i, 0)), c_spec, c_spec, d_spec, d_spec]
        out_shape = [jax.ShapeDtypeStruct((n_tok, D_MODEL), BF16), c_shape, c_shape, d_shape, d_shape]
    return pl.pallas_call(
        functools.partial(body, li=li),
        grid=(n_tok // tm,),
        in_specs=in_specs,
        out_specs=out_specs,
        out_shape=out_shape,
        compiler_params=_cparams(1),
        name=f"pre_mixer_{'lat' if latent else 'ctx'}_{'even' if even else 'odd'}",
    )(*args)


def _even_mixer_kernel(*refs, seq, n_seq, tq, own_keys, lam_init):
    proj_ref, cw_ref, lq1_ref, lk1_ref, lq2_ref, lk2_ref, subln_ref, ckt_ref, cv_ref = refs[:9]
    y_ref = refs[-1]
    n_rows = n_seq * seq

    pos = lax.broadcasted_iota(jnp.int32, (n_rows, 128), 0) % seq
    for j in range(MIX_A // 128):
        cols = slice(j * 128, (j + 1) * 128)
        a_b = proj_ref[:, j * 128:(j + 1) * 128].astype(F32)
        u = (proj_ref[:, MIX_A + j * 128:MIX_A + (j + 1) * 128].astype(F32)
             * proj_ref[:, 2 * MIX_A + j * 128:2 * MIX_A + (j + 1) * 128].astype(F32))
        u_prev = jnp.where(pos == 0, 0.0, pltpu.roll(u, 1, axis=0))
        u_next = jnp.where(pos == seq - 1, 0.0, pltpu.roll(u, n_rows - 1, axis=0))
        w = cw_ref[0, :, cols]
        y_ref[:, cols] = (a_b * (w[0:1] * u_prev + w[1:2] * u + w[2:3] * u_next)).astype(BF16)

    lam = (jnp.exp(jnp.sum(lq1_ref[...] * lk1_ref[...], axis=-1, keepdims=True))
           - jnp.exp(jnp.sum(lq2_ref[...] * lk2_ref[...], axis=-1, keepdims=True)) + lam_init)
    subln = subln_ref[...]
    q_col, k_col, v_col = 3 * MIX_A, 3 * MIX_A + 512, 3 * MIX_A + 1024
    ones = jnp.ones((PAST_LEN, PAIR), BF16)

    memo = _Memo()

    def own_k(b, h, m):
        kp = proj_ref[b * seq:(b + 1) * seq, k_col + h * PAIR:k_col + (h + 1) * PAIR]
        return jnp.where(_keep_mask_bf16(seq, m), kp, jnp.zeros_like(kp))

    def value_ops(b, h):
        v_ops = []
        if own_keys:
            v_own = proj_ref[b * seq:(b + 1) * seq, v_col + h * PAIR:v_col + (h + 1) * PAIR]
            v_ops.append(jnp.concatenate([v_own, jnp.ones((seq, PAIR), BF16)], axis=1))
        v_ops.append(jnp.concatenate([cv_ref[b, 0, h].astype(BF16), ones], axis=1))
        return v_ops

    def scores(b, qrows, h, m):
        qp = proj_ref[qrows, q_col + h * PAIR:q_col + (h + 1) * PAIR]
        parts = []
        if own_keys:
            parts.append(_dot_nt(qp, memo.get_or_build(("k", b, h, m), functools.partial(own_k, b, h, m))))
        ckt = memo.get_or_build(("ck", b, h, m), lambda: _pad_rows(ckt_ref[b, 0, h, m].astype(BF16), m, 0.0))
        parts.append(_dot(qp, ckt))
        return parts

    first_map = {}

    def pv(b, qrows, h, m, e_parts, _):
        v_ops = memo.get_or_build(("v", b, h), functools.partial(value_ops, b, h))
        o = None
        for e, v_op in zip(e_parts, v_ops):
            t = _dot(e, v_op)
            o = t if o is None else o + t
        attn = o[:, 0:PAIR] * (1.0 / o[:, PAIR:2 * PAIR])
        if m == 0:
            first_map[(qrows.start, h)] = attn
        else:
            y = _rms(first_map.pop((qrows.start, h)) - lam * attn, subln) * (1.0 - lam_init)
            y_ref[qrows, MIX_A + h * PAIR:MIX_A + (h + 1) * PAIR] = y.astype(BF16)

    heads = []
    for b in range(n_seq):
        for i in range(seq // tq):
            qrows = slice(b * seq + i * tq, b * seq + (i + 1) * tq)
            for h in range(H_B):
                for m in range(2):
                    heads.append(_Head(functools.partial(scores, b, qrows, h, m),
                                       functools.partial(pv, b, qrows, h, m)))
    _attend(heads, batch=EVEN_BATCH_ROWS // tq)


def _even_mixer(proj, conv_w, lq1, lk1, lq2, lk2, subln, kt, v, *, seq, n_seq, own_keys, lam_init):
    n_tok = proj.shape[0]
    rows = n_seq * seq
    small = lambda a: pl.BlockSpec((1, a.shape[1]), lambda b: (0, 0))
    in_specs = [pl.BlockSpec((rows, proj.shape[1]), lambda b: (b, 0)),
                pl.BlockSpec((1, 3, MIX_A), lambda b: (0, 0, 0)),
                small(lq1), small(lk1), small(lq2), small(lk2), small(subln),
                pl.BlockSpec((n_seq, 1, H_B, 2, DH, PAST_LEN), lambda b: (b, 0, 0, 0, 0, 0)),
                pl.BlockSpec((n_seq, 1, H_B, PAST_LEN, PAIR), lambda b: (b, 0, 0, 0, 0))]
    return pl.pallas_call(
        functools.partial(_even_mixer_kernel, seq=seq, n_seq=n_seq, tq=min(seq, 256), own_keys=own_keys,
                          lam_init=lam_init),
        grid=(n_tok // rows,),
        in_specs=in_specs,
        out_specs=pl.BlockSpec((rows, D_MODEL), lambda b: (b, 0)),
        out_shape=jax.ShapeDtypeStruct((n_tok, D_MODEL), BF16),
        compiler_params=_cparams(1),
        name=f"even_mixer_{'lat' if own_keys else 'ctx'}",
    )(proj, conv_w, lq1, lk1, lq2, lk2, subln, kt, v)


def _odd_ctx_kernel(q_ref, nkt_ref, nvt_ref, skt_ref, svt_ref, sink_ref, y_ref):
    def kv_refs(b, head):
        if head < H_C:
            return nkt_ref.at[b, 0, head], nvt_ref.at[b, 0, head]
        g = (head - H_C) // GQA
        return skt_ref.at[b, 0, g], svt_ref.at[b, 0, g]

    memo = _Memo()

    def kv_key(head):
        return head if head < H_C else H_C + (head - H_C) // GQA

    def scores(b, p, hh):
        head = 2 * p + hh
        kt = memo.get_or_build(("k", b, kv_key(head), hh),
                               lambda: _pad_rows(kv_refs(b, head)[0][...].astype(BF16), hh, 0.0))
        return [_dot(q_ref[b * SEQ:(b + 1) * SEQ, p * PAIR:(p + 1) * PAIR], kt)]

    def finish(out, b, p, hh, e_parts, term):
        head = 2 * p + hh
        vt = memo.get_or_build(("v", b, kv_key(head), hh),
                               lambda: _pad_rows(kv_refs(b, head)[1][...].astype(BF16), hh, 1.0))
        out.put(hh, _dot_nt(e_parts[0], vt), term)

    heads = []
    for b in range(CTX_SEQS_PER_STEP):
        for p in range((H_C + H_D) // 2):
            out = _PairSink(y_ref, slice(b * SEQ, (b + 1) * SEQ), slice(p * PAIR, (p + 1) * PAIR))
            for hh in range(2):
                d_head = 2 * p + hh - H_C
                sink = sink_ref[d_head] * LOG2E if d_head >= 0 else None
                heads.append(_Head(functools.partial(scores, b, p, hh), functools.partial(finish, out, b, p, hh), sink))
    _attend(heads, batch=16)


def _odd_ctx_mixer(q, nkt, nvt, skt, svt, sink):
    n_tok = q.shape[0]
    n_seq = CTX_SEQS_PER_STEP
    c_spec = pl.BlockSpec((n_seq, 1, H_C, DH, SEQ), lambda b: (b, 0, 0, 0, 0))
    d_spec = pl.BlockSpec((n_seq, 1, KV_D, DH, SEQ), lambda b: (b, 0, 0, 0, 0))
    return pl.pallas_call(
        _odd_ctx_kernel,
        grid=(n_tok // (n_seq * SEQ),),
        in_specs=[pl.BlockSpec((n_seq * SEQ, D_MODEL), lambda b: (b, 0)), c_spec, c_spec, d_spec, d_spec,
                  pl.BlockSpec(memory_space=pltpu.SMEM)],
        out_specs=pl.BlockSpec((n_seq * SEQ, D_MODEL), lambda b: (b, 0)),
        out_shape=jax.ShapeDtypeStruct((n_tok, D_MODEL), BF16),
        compiler_params=_cparams(1),
        name="odd_mixer_ctx",
    )(q, nkt, nvt, skt, svt, sink)


def _build_na_bias(rpb_ref, head, nb_ref, slot):
    qc = lax.broadcasted_iota(jnp.int32, (GRID_W, GRID_W), 0)
    kc = lax.broadcasted_iota(jnp.int32, (GRID_W, GRID_W), 1)
    col_start = jnp.clip(qc - NA_WIN_C // 2, 0, GRID_W - NA_WIN_C)
    col_ok = (kc >= col_start) & (kc < col_start + NA_WIN_C)
    dc = kc - qc + NA_WIN_C - 1
    n_dr, n_dc = 2 * NA_WIN_R - 1, 2 * NA_WIN_C - 1
    neg = jnp.full((GRID_W, GRID_W), NEG, F32)
    toeplitz = []
    for dr in range(n_dr):
        t = neg
        for d in range(n_dc):
            t = jnp.where(dc == d, rpb_ref[(head * n_dr + dr) * n_dc + d] * LOG2E, t)
        toeplitz.append(jnp.where(col_ok, t, NEG))
    rows_per_half = NA_HALF // GRID_W
    key_rows = NA_KEYS // GRID_W
    for half in range(2):
        for rl in range(rows_per_half):
            r = half * rows_per_half + rl
            r_start = min(max(r - NA_WIN_R // 2, 0), GRID_H - NA_WIN_R)
            blocks = []
            for j in range(key_rows):
                rk = half * (GRID_H - key_rows) + j
                inside = r_start <= rk < r_start + NA_WIN_R
                blocks.append(toeplitz[rk - r + NA_WIN_R - 1] if inside else neg)
            nb_ref[slot, half, rl * GRID_W:(rl + 1) * GRID_W, :] = jnp.concatenate(blocks, axis=1)


def _na_kernel(rpb_ref, q_ref, k_ref, v_ref, kct_ref, vct_ref, y_ref, nb_ref):
    @pl.when(pl.program_id(1) == 0)
    def _():
        for hh in range(2):
            _build_na_bias(rpb_ref, 2 * pl.program_id(0) + hh, nb_ref, hh)

    def scores(b, half, hh):
        w0 = b * DEC_SEQ + half * (DEC_SEQ - NA_KEYS)
        qp = q_ref[b * DEC_SEQ + half * NA_HALF:b * DEC_SEQ + (half + 1) * NA_HALF, :]
        kw = k_ref[w0:w0 + NA_KEYS, :]
        s1 = _dot_nt(qp, jnp.where(_keep_mask_bf16(NA_KEYS, hh), kw, jnp.zeros_like(kw))) + nb_ref[hh, half]
        s2 = _dot(qp, _pad_rows(kct_ref[b, 0, hh].astype(BF16), hh, 0.0))
        return [s1, s2]

    def finish(out, b, half, hh, e_parts, term):
        w0 = b * DEC_SEQ + half * (DEC_SEQ - NA_KEYS)
        vw = v_ref[w0:w0 + NA_KEYS, :]
        o = (_dot(e_parts[0], jnp.where(_keep_mask_bf16(NA_KEYS, hh), vw, jnp.ones_like(vw)))
             + _dot_nt(e_parts[1], _pad_rows(vct_ref[b, 0, hh].astype(BF16), hh, 1.0)))
        out.put(hh, o, term)

    heads = []
    for b in range(NA_SEQS_PER_STEP):
        for half in range(2):
            rows = slice(b * DEC_SEQ + half * NA_HALF, b * DEC_SEQ + (half + 1) * NA_HALF)
            out = _PairSink(y_ref, rows, slice(0, PAIR))
            for hh in range(2):
                heads.append(_Head(functools.partial(scores, b, half, hh), functools.partial(finish, out, b, half, hh)))
    _attend(heads, batch=4)


def _na_mixer(proj, cache_kt, cache_vt, rpb_flat):
    pairs = H_C // 2
    n_seq = NA_SEQS_PER_STEP
    col = lambda base: pl.BlockSpec((n_seq * DEC_SEQ, PAIR), lambda hp, b: (b, base + hp))
    cache = pl.BlockSpec((n_seq, 1, 2, DH, PAST_LEN), lambda hp, b: (b, 0, hp, 0, 0))
    return pl.pallas_call(
        _na_kernel,
        grid=(pairs, DEC_BATCH // n_seq),
        in_specs=[pl.BlockSpec(memory_space=pltpu.SMEM), col(0), col(pairs), col(2 * pairs), cache, cache],
        out_specs=pl.BlockSpec((n_seq * DEC_SEQ, PAIR), lambda hp, b: (b, hp)),
        out_shape=jax.ShapeDtypeStruct((LAT_TOKENS, 512), BF16),
        scratch_shapes=[pltpu.VMEM((2, 2, NA_HALF, NA_KEYS), F32)],
        compiler_params=_cparams(2),
        name="na_mixer",
    )(rpb_flat, proj, proj, proj, cache_kt, cache_vt)


def _swa_kernel(q_ref, kv_ref, kct_ref, vct_ref, sink_ref, y_ref):
    n_groups = DEC_SEQ // SWA_QROWS
    ql = lax.broadcasted_iota(jnp.int32, (SWA_QROWS, SWA_BAND), 0)
    kj = lax.broadcasted_iota(jnp.int32, (SWA_QROWS, SWA_BAND), 1)

    def band_bias(first_key_minus_first_query):
        return jnp.where(jnp.abs(kj + first_key_minus_first_query - ql) <= SWA_WINDOW, 0.0, NEG)

    bias_first, bias_mid, bias_last = band_bias(0), band_bias(-SWA_BLOCK), band_bias(-2 * SWA_BLOCK)

    def band_start(grp):
        return min(max(grp * SWA_QROWS - SWA_BLOCK, 0), DEC_SEQ - SWA_BAND)

    memo = _Memo()

    def band_k(g, grp, hh):
        start = band_start(grp)
        kb = kv_ref[start:start + SWA_BAND, g * PAIR:(g + 1) * PAIR]
        return jnp.where(_keep_mask_bf16(SWA_BAND, hh), kb, jnp.zeros_like(kb))

    def band_v(g, grp, hh):
        start = band_start(grp)
        vb = kv_ref[start:start + SWA_BAND, (KV_D + g) * PAIR:(KV_D + g + 1) * PAIR]
        return jnp.where(_keep_mask_bf16(SWA_BAND, hh), vb, jnp.ones_like(vb))

    def scores(g, grp, p, hh):
        bias = bias_first if grp == 0 else (bias_last if grp == n_groups - 1 else bias_mid)
        qp = q_ref[grp * SWA_QROWS:(grp + 1) * SWA_QROWS, p * PAIR:(p + 1) * PAIR]
        s1 = _dot_nt(qp, memo.get_or_build(("k", g, grp, hh), functools.partial(band_k, g, grp, hh))) + bias
        ckt = memo.get_or_build(("ck", g, hh), lambda: _pad_rows(kct_ref[0, 0, g].astype(BF16), hh, 0.0))
        return [s1, _dot(qp, ckt)]

    def finish(out, g, grp, hh, e_parts, term):
        cvt = memo.get_or_build(("cv", g, hh), lambda: _pad_rows(vct_ref[0, 0, g].astype(BF16), hh, 1.0))
        o = (_dot(e_parts[0], memo.get_or_build(("v", g, grp, hh), functools.partial(band_v, g, grp, hh)))
             + _dot_nt(e_parts[1], cvt))
        out.put(hh, o, term)

    heads = []
    for g in range(KV_D):
        for grp in range(n_groups):
            for j in range(GQA // 2):
                p = g * (GQA // 2) + j
                out = _PairSink(y_ref, slice(grp * SWA_QROWS, (grp + 1) * SWA_QROWS), slice(p * PAIR, (p + 1) * PAIR))
                for hh in range(2):
                    sink = sink_ref[2 * p + hh] * LOG2E
                    heads.append(_Head(functools.partial(scores, g, grp, p, hh),
                                       functools.partial(finish, out, g, grp, hh), sink))
    _attend(heads, batch=16)


def _swa_mixer(proj, cache_kt, cache_vt, sink):
    cache = pl.BlockSpec((1, 1, KV_D, DH, PAST_LEN), lambda b: (b, 0, 0, 0, 0))
    return pl.pallas_call(
        _swa_kernel,
        grid=(DEC_BATCH,),
        in_specs=[pl.BlockSpec((DEC_SEQ, 512), lambda b: (b, 3)),
                  pl.BlockSpec((DEC_SEQ, 512), lambda b: (b, 4)),
                  cache, cache,
                  pl.BlockSpec(memory_space=pltpu.SMEM)],
        out_specs=pl.BlockSpec((DEC_SEQ, 512), lambda b: (b, 0)),
        out_shape=jax.ShapeDtypeStruct((LAT_TOKENS, 512), BF16),
        compiler_params=_cparams(1),
        name="swa_mixer",
    )(proj, proj, cache_kt, cache_vt, sink)


def _load_weights_bf16(li, wo_hbm, w1_hbm, w2_hbm, wo_s, w1_s, w2_s, stage_sq, stage_wide, sems):
    square = ([(wo_hbm, wo_s, j, WO_ROWS) for j in range(N_WO)] + [(w2_hbm, w2_s, j, W2_ROWS) for j in range(N_W2)])
    wide = [(w1_hbm, w1_s, j, W1_ROWS) for j in range(N_W1)]
    queues = {"sq": (square, stage_sq, 0), "wide": (wide, stage_wide, STAGE_SLOTS)}

    def copy(kind, idx):
        chunks, stage, sem0 = queues[kind]
        src, _, j, n_rows = chunks[idx]
        slot = idx % STAGE_SLOTS
        return pltpu.make_async_copy(src.at[li, j * n_rows:(j + 1) * n_rows, :], stage.at[slot], sems.at[sem0 + slot])

    for kind in queues:
        for idx in range(min(STAGE_SLOTS, len(queues[kind][0]))):
            copy(kind, idx).start()
    order = []
    for i in range(max(len(square), len(wide))):
        order += [("sq", i)] * (i < len(square)) + [("wide", i)] * (i < len(wide))
    for kind, idx in order:
        chunks, stage, _ = queues[kind]
        _, dst, j, n_rows = chunks[idx]
        copy(kind, idx).wait()
        dst[j * n_rows:(j + 1) * n_rows, :] = stage[idx % STAGE_SLOTS].astype(BF16)
        if idx + STAGE_SLOTS < len(chunks):
            copy(kind, idx + STAGE_SLOTS).start()


def _post_mixer_kernel(*refs, li, n_lat_mix, n_x, n_out):
    mix_ctx_ref, mix_lat_refs = refs[0], refs[1:1 + n_lat_mix]
    refs = refs[1 + n_lat_mix:]
    x_refs = refs[:n_x]
    mod_ref, gpost_ref, gpre_ref, gmlp_ref, wo_hbm, w1_hbm, w2_hbm = refs[n_x:7 + n_x]
    out_refs = refs[7 + n_x:7 + n_x + n_out]
    wo_s, w1_s, w2_s, stage_sq, stage_wide, sems = refs[7 + n_x + n_out:]
    step = pl.program_id(0)

    @pl.when(step == 0)
    def _():
        _load_weights_bf16(li, wo_hbm, w1_hbm, w2_hbm, wo_s, w1_s, w2_s, stage_sq, stage_wide, sems)

    is_ctx = step < CTX_TILES
    mod = mod_ref[0, pl.ds(jnp.where(is_ctx, CTX_ROW, (step - CTX_TILES) // (DEC_SEQ // TM)), 1), :]
    g_post, g_pre, g_mlp = gpost_ref[li:li + 1, :], gpre_ref[li:li + 1, :], gmlp_ref[li:li + 1, :]
    rows = [slice(i * (TM // POST_SPLIT), (i + 1) * (TM // POST_SPLIT)) for i in range(POST_SPLIT)]

    def residual(r):
        if n_x == 1:
            return x_refs[0][r, :]
        return jnp.where(is_ctx, x_refs[0][r, :], x_refs[1][r, :])

    def mixed(r):
        lat = [m[r, :] for m in mix_lat_refs]
        lat = lat[0] if n_lat_mix == 1 else jnp.concatenate(lat, axis=1)
        return jnp.where(is_ctx, mix_ctx_ref[r, :], lat)

    ys = [_dot(mixed(r), wo_s[...]) for r in rows]
    x1 = [residual(r) + mod[:, 2 * D_MODEL:3 * D_MODEL] * _rms(y, g_post) for r, y in zip(rows, ys)]
    h = [_norm_mod(x, g_pre, mod, 3 * D_MODEL).astype(BF16) for x in x1]
    acc = [None] * POST_SPLIT
    pending = []

    def second_matmul(c, i, f):
        t2 = _dot(f, w2_s[c * FF_CHUNK:(c + 1) * FF_CHUNK, :])
        acc[i] = t2 if acc[i] is None else acc[i] + t2

    for c in range(D_FF // FF_CHUNK):
        for i in range(POST_SPLIT):
            f = _dot(h[i], w1_s[:, c * FF_CHUNK:(c + 1) * FF_CHUNK])
            pending.append((c, i, jnp.square(jnp.maximum(f, 0.0)).astype(BF16)))
            if len(pending) > 1:
                second_matmul(*pending.pop(0))
    second_matmul(*pending.pop(0))
    res = [x1[i] + mod[:, 5 * D_MODEL:6 * D_MODEL] * _rms(acc[i], g_mlp) for i in range(POST_SPLIT)]
    if n_out == 1:
        for i, r in enumerate(rows):
            out_refs[0][r, :] = res[i]
    else:
        @pl.when(is_ctx)
        def _():
            for i, r in enumerate(rows):
                out_refs[0][r, :] = res[i]

        @pl.when(jnp.logical_not(is_ctx))
        def _():
            for i, r in enumerate(rows):
                out_refs[1][r, :] = res[i]


def _post_mixer(mix_ctx, mix_lat, xs, mod, g_post, g_pre, g_mlp, w_out, w1, w2, li, *, split_out):
    tile = lambda s: s
    ctx_tile = lambda s: jnp.minimum(tile(s), CTX_TILES - 1)
    lat_tile = lambda s: jnp.maximum(tile(s) - CTX_TILES, 0)
    token_spec = lambda index, width=D_MODEL: pl.BlockSpec((TM, width), lambda s: (index(s), 0))
    mix_specs = [token_spec(ctx_tile)] + [token_spec(lat_tile, m.shape[1]) for m in mix_lat]
    x_specs = [token_spec(tile)] if len(xs) == 1 else [token_spec(ctx_tile), token_spec(lat_tile)]
    if split_out:
        out_specs = [token_spec(ctx_tile), token_spec(lat_tile)]
        out_shape = [jax.ShapeDtypeStruct((CTX_TOKENS, D_MODEL), F32), jax.ShapeDtypeStruct((LAT_TOKENS, D_MODEL), F32)]
    else:
        out_specs = [token_spec(tile)]
        out_shape = [jax.ShapeDtypeStruct((ALL_TOKENS, D_MODEL), F32)]
    gain = pl.BlockSpec((DEPTH, D_MODEL), lambda s: (0, 0))
    return pl.pallas_call(
        functools.partial(_post_mixer_kernel, li=li, n_lat_mix=len(mix_lat), n_x=len(xs), n_out=len(out_specs)),
        grid=(ALL_TOKENS // TM,),
        in_specs=mix_specs + x_specs + [
            pl.BlockSpec((1, MOD_ROWS, 6 * D_MODEL), lambda s: (li, 0, 0)),
            gain, gain, gain] + [pl.BlockSpec(memory_space=pl.ANY)] * 3,
        out_specs=out_specs,
        out_shape=out_shape,
        scratch_shapes=[pltpu.VMEM((D_MODEL, D_MODEL), BF16), pltpu.VMEM((D_MODEL, D_FF), BF16),
                        pltpu.VMEM((D_FF, D_MODEL), BF16),
                        pltpu.VMEM((STAGE_SLOTS, WO_ROWS, D_MODEL), F32), pltpu.VMEM((STAGE_SLOTS, W1_ROWS, D_FF), F32),
                        pltpu.SemaphoreType.DMA((2 * STAGE_SLOTS,))],
        compiler_params=_cparams(1),
        name="post_mixer",
    )(mix_ctx, *mix_lat, *xs, mod, g_post, g_pre, g_mlp, w_out, w1, w2)


def _rope_tables():
    t = jnp.arange(DEC_SEQ)
    rows = (t // GRID_W).astype(F32)
    cols = (t % GRID_W).astype(F32)
    q4 = DH // 4
    inv = 1.0 / (ROPE_BASE ** (jnp.arange(q4, dtype=F32) / q4))
    ar, ac = rows[:, None] * inv, cols[:, None] * inv
    cos64 = jnp.concatenate([jnp.cos(ar), jnp.cos(ar), jnp.cos(ac), jnp.cos(ac)], axis=1)
    sin64 = jnp.concatenate([-jnp.sin(ar), jnp.sin(ar), -jnp.sin(ac), jnp.sin(ac)], axis=1)
    return jnp.tile(cos64, (1, 2)), jnp.tile(sin64, (1, 2))


def kernel(x_prompt, x_sample, cache_diff_k, cache_diff_v, cache_na_k, cache_na_v, cache_swa_k, cache_swa_v, c, c_ctx, mod_w, mod_b, norm_mix_pre, norm_mix_post, norm_mlp_pre, norm_mlp_post, w_in_even, conv_w, lambda_q1, lambda_k1, lambda_q2, lambda_k2, subln, w_in_odd, rpb, sink, w_out, mlp_w1, mlp_w2):
    cond = jnp.concatenate([c, c_ctx[None, :], jnp.zeros((MOD_ROWS - DEC_BATCH - 1, D_MODEL), F32)], axis=0)
    mod, *w_in = _modulation(cond, mod_w, mod_b, w_in_even, w_in_odd)
    rope = _rope_tables()

    t_minor = lambda a: jnp.swapaxes(a, -1, -2)

    xp = x_prompt.reshape(CTX_TOKENS, D_MODEL)
    xs = x_sample.reshape(LAT_TOKENS, D_MODEL)
    lam_init0 = 0.8 - 0.6 * math.exp(-0.3 * 0)
    lam_args = (conv_w, lambda_q1, lambda_k1, lambda_q2, lambda_k2, subln)
    sink_flat = sink.reshape(-1)

    def post(li, mix_ctx, mix_lat, x_parts, split_out):
        return _post_mixer(mix_ctx, mix_lat, x_parts, mod, norm_mix_post, norm_mlp_pre, norm_mlp_post, w_out, mlp_w1,
                           mlp_w2, li, split_out=split_out)

    proj, new_diff_kt, new_diff_v = _pre_mixer(xp, mod, norm_mix_pre, w_in[0], 0, latent=False, even=True, rope=None,
                                               n_tok=CTX_TOKENS)
    mix_ctx = _even_mixer(proj, *lam_args, new_diff_kt, new_diff_v, seq=SEQ, n_seq=CTX_SEQS_PER_STEP_EVEN,
                          own_keys=False, lam_init=lam_init0)
    proj = _pre_mixer(xs, mod, norm_mix_pre, w_in[0], 0, latent=True, even=True, rope=rope, n_tok=LAT_TOKENS)
    mix_lat = _even_mixer(proj, *lam_args, t_minor(cache_diff_k), cache_diff_v, seq=DEC_SEQ, n_seq=1, own_keys=True,
                          lam_init=lam_init0)
    (x_all,) = post(0, mix_ctx, [mix_lat], [xp, xs], False)

    q, new_na_kt, new_na_vt, new_swa_kt, new_swa_vt = _pre_mixer(
        x_all, mod, norm_mix_pre, w_in[1], 1, latent=False, even=False, rope=None, n_tok=CTX_TOKENS)
    mix_ctx = _odd_ctx_mixer(q, new_na_kt, new_na_vt, new_swa_kt, new_swa_vt, sink_flat)
    proj = _pre_mixer(x_all, mod, norm_mix_pre, w_in[1], 1, latent=True, even=False, rope=rope, n_tok=LAT_TOKENS,
                      tile0=CTX_TOKENS // TM_PRE)
    mix_c = _na_mixer(proj, t_minor(cache_na_k), t_minor(cache_na_v), rpb.reshape(-1))
    mix_d = _swa_mixer(proj, t_minor(cache_swa_k), t_minor(cache_swa_v), sink_flat)
    xp, xs = post(1, mix_ctx, [mix_c, mix_d], [x_all], True)

    return (xp.reshape(BATCH, SEQ, D_MODEL), xs.reshape(DEC_BATCH, DEC_SEQ, D_MODEL),
            t_minor(new_diff_kt), new_diff_v, t_minor(new_na_kt), t_minor(new_na_vt),
            t_minor(new_swa_kt), t_minor(new_swa_vt))
```

```python
import functools
import math

import jax
import jax.numpy as jnp
from jax import lax
from jax.experimental import pallas as pl
from jax.experimental.pallas import tpu as pltpu

F32 = jnp.float32
BF16 = jnp.bfloat16

D_MODEL = 1024
BATCH = 32
SEQ = 256
DEPTH = 2
DEC_BATCH = 8
DEC_SEQ = 1024
PAST_LEN = 256
GRID_W = 64
GRID_H = DEC_SEQ // GRID_W
MIX_A = 512
H_B = 4
DH = 64
PAIR = 2 * DH
H_C = 8
H_D = 8
KV_D = 2
GQA = H_D // KV_D
IN_EVEN = 3072
IN_ODD = 2304
D_FF = 4096
NA_WIN_R = 8
NA_WIN_C = 16
SWA_BLOCK = 128
SWA_WINDOW = 128
ROPE_BASE = 10000.0
EPS = 1e-6
NEG = -1e30
LOG2E = math.log2(math.e)
Q_SCALE = DH ** -0.5 * LOG2E

MOD_ROWS = 16
CTX_ROW = DEC_BATCH
V7X_VMEM_BYTES = 64 * 1024 * 1024
VMEM_LIMIT = V7X_VMEM_BYTES - 12 * 1024 * 1024

CTX_TOKENS = BATCH * SEQ
LAT_TOKENS = DEC_BATCH * DEC_SEQ
ALL_TOKENS = CTX_TOKENS + LAT_TOKENS

TM = 512
TM_PRE = 1024
FF_CHUNK = 1024
POST_SPLIT = 2
WCHUNK_ELEMS = 128 * 1024
WO_ROWS, W1_ROWS, W2_ROWS = WCHUNK_ELEMS // D_MODEL, WCHUNK_ELEMS // D_FF, WCHUNK_ELEMS // D_MODEL
N_WO, N_W1, N_W2 = D_MODEL // WO_ROWS, D_MODEL // W1_ROWS, D_FF // W2_ROWS
STAGE_SLOTS = 4
CTX_TILES = CTX_TOKENS // TM
CTX_SEQS_PER_STEP = 4
CTX_SEQS_PER_STEP_EVEN = 2
MOD_TN = 3072
EVEN_BATCH_ROWS = 2048
NA_SEQS_PER_STEP = 2
NA_BLOCK_ROWS = 4
NA_MAX_KEYS = 768
SWA_QROWS = 2 * SWA_BLOCK
SWA_BAND = 4 * SWA_BLOCK
LAT_ODD_COLS = IN_ODD + 2 * PAIR


def _cparams(n_axes):
    return pltpu.CompilerParams(dimension_semantics=("arbitrary",) * n_axes, vmem_limit_bytes=VMEM_LIMIT)


def _rms(x, g):
    return x * lax.rsqrt(jnp.mean(x * x, axis=-1, keepdims=True) + EPS) * g


def _dot(a, b):
    return jnp.dot(a, b, preferred_element_type=F32)


def _dot_nt(a, b):
    return lax.dot_general(a, b, (((1,), (1,)), ((), ())), preferred_element_type=F32)


def _layer_block(shape, li):
    zeros = (0,) * len(shape)
    return pl.BlockSpec((1,) + tuple(shape), lambda *_: (li,) + zeros, pipeline_mode=pl.Buffered(1))


def _low_half(shape):
    return (lax.broadcasted_iota(jnp.int32, shape, len(shape) - 1) & DH) == 0


def _keep_mask_bf16(rows, hh):
    lane = lax.broadcasted_iota(jnp.int32, (rows, PAIR), 1)
    half = (lane & DH).astype(F32).astype(BF16)
    return (half == 0) if hh == 0 else (half != 0)


def _pad_rows(t, hh, fill):
    other = jnp.full(t.shape, fill, t.dtype)
    return jnp.concatenate([t, other] if hh == 0 else [other, t], axis=0)


class _Memo(dict):
    def get_or_build(self, key, build):
        if key not in self:
            self[key] = build()
        return self[key]


def _merge_pair(o0, o1, extra_den=None):
    low = _low_half(o0.shape)
    num = jnp.where(low, o0, o1)
    den = pltpu.roll(jnp.where(low, o1, o0), DH, axis=1)
    if extra_den is not None:
        den = den + jnp.where(low, extra_den[0], extra_den[1])
    return num * (1.0 / den)


class _Head:
    def __init__(self, scores, finish, sink=None):
        self.scores, self.finish, self.sink = scores, finish, sink


def _attend(heads, batch):
    batches = [heads[i:i + batch] for i in range(0, len(heads), batch)]
    cur = [hd.scores() for hd in batches[0]]
    for bi, group in enumerate(batches):
        nxt = [hd.scores() for hd in batches[bi + 1]] if bi + 1 < len(batches) else None
        exps = []
        for hd, parts in zip(group, cur):
            chunks = [s[:, j * 128:(j + 1) * 128] for s in parts for j in range(s.shape[1] // 128)]
            m_el = functools.reduce(jnp.maximum, chunks)
            if hd.sink is not None:
                m_el = jnp.maximum(m_el, hd.sink)
            mx = jnp.broadcast_to(jnp.max(m_el, axis=-1, keepdims=True), m_el.shape)
            term = None if hd.sink is None else jnp.exp2(hd.sink - mx)
            e_parts = [jnp.concatenate([jnp.exp2(s[:, j * 128:(j + 1) * 128] - mx)
                                        for j in range(s.shape[1] // 128)], axis=1).astype(BF16) for s in parts]
            exps.append((e_parts, term))
        for hd, (e_parts, term) in zip(group, exps):
            hd.finish(e_parts, term)
        cur = nxt


class _PairSink:
    def __init__(self, y_ref, rows, cols):
        self.y_ref, self.rows, self.cols, self.first = y_ref, rows, cols, None

    def put(self, hh, o, term):
        if hh == 0:
            self.first = (o, term)
        else:
            o0, term0 = self.first
            extra = None if term is None else (term0, term)
            self.y_ref[self.rows, self.cols] = _merge_pair(o0, o, extra).astype(BF16)


def _mod_kernel(cond_ref, w_ref, b_ref, win_e_ref, win_o_ref, o_ref, win_e_bf, win_o_bf):
    c = cond_ref[...]
    s = c * (1.0 / (1.0 + jnp.exp(-c)))
    o_ref[0] = _dot(s.astype(BF16), w_ref[0].astype(BF16)) + b_ref[pl.ds(pl.program_id(0), 1), :]
    win_e_bf[...] = win_e_ref[...].astype(BF16)
    win_o_bf[...] = win_o_ref[...].astype(BF16)


def _modulation(cond, mod_w, mod_b, w_in_even, w_in_odd):
    n_col = 6 * D_MODEL // MOD_TN
    slab = D_MODEL // (DEPTH * n_col)
    w_slab = lambda width: pl.BlockSpec((1, slab, width), lambda l, j: (0, l * n_col + j, 0))
    return pl.pallas_call(
        _mod_kernel,
        grid=(DEPTH, n_col),
        in_specs=[
            pl.BlockSpec((MOD_ROWS, D_MODEL), lambda l, j: (0, 0)),
            pl.BlockSpec((1, D_MODEL, MOD_TN), lambda l, j: (l, 0, j)),
            pl.BlockSpec((DEPTH, MOD_TN), lambda l, j: (0, j)),
            w_slab(IN_EVEN), w_slab(IN_ODD),
        ],
        out_specs=[pl.BlockSpec((1, MOD_ROWS, MOD_TN), lambda l, j: (l, 0, j)), w_slab(IN_EVEN), w_slab(IN_ODD)],
        out_shape=[jax.ShapeDtypeStruct((DEPTH, MOD_ROWS, 6 * D_MODEL), F32),
                   jax.ShapeDtypeStruct((1, D_MODEL, IN_EVEN), BF16), jax.ShapeDtypeStruct((1, D_MODEL, IN_ODD), BF16)],
        compiler_params=_cparams(2),
        name="adaln_modulation",
    )(cond, mod_w, mod_b, w_in_even, w_in_odd)


def _norm_mod(x, g, mod, shift_col):
    sh = mod[:, shift_col:shift_col + D_MODEL]
    sc = mod[:, shift_col + D_MODEL:shift_col + 2 * D_MODEL]
    return _rms(x, g) * (1.0 + sc) + sh


def _rope(z, cos, sin):
    outs = []
    for j in range(z.shape[1] // 128):
        zj = z[:, j * 128:(j + 1) * 128]
        lane = lax.broadcasted_iota(jnp.int32, zj.shape, 1)
        partner = jnp.where((lane & 16) == 0, pltpu.roll(zj, 128 - 16, axis=1), pltpu.roll(zj, 16, axis=1))
        outs.append(zj * cos + partner * sin)
    return outs[0] if len(outs) == 1 else jnp.concatenate(outs, axis=1)


def _store_heads_transposed(dst, bi, pc_rows, n_pairs):
    for p in range(n_pairs):
        t = pc_rows[:, p * PAIR:(p + 1) * PAIR].T
        dst[bi, 0, 2 * p] = t[0:DH]
        dst[bi, 0, 2 * p + 1] = t[DH:PAIR]


def _mod_row(mod_ref, latent):
    if latent:
        return mod_ref[0, pl.ds(pl.program_id(0) // (DEC_SEQ // TM_PRE), 1), :]
    return mod_ref[0, CTX_ROW:CTX_ROW + 1, :]


def _normed_groups(x_ref, mod_ref, g_ref, li, latent):
    rows = [slice(i * SEQ, (i + 1) * SEQ) for i in range(TM_PRE // SEQ)]
    mod, g = _mod_row(mod_ref, latent), g_ref[li:li + 1, :]
    return rows, [_norm_mod(x_ref[r, :], g, mod, 0).astype(BF16) for r in rows]


def _k1_ctx_even(x_ref, mod_ref, g_ref, w_ref, proj_ref, dkt_ref, dv_ref, *, li):
    rows, hs = _normed_groups(x_ref, mod_ref, g_ref, li, False)
    for c in range(IN_EVEN // 512):
        for bi, (r, h) in enumerate(zip(rows, hs)):
            pc = _dot(h, w_ref[0, :, c * 512:(c + 1) * 512])
            if c < 3:
                proj_ref[r, c * 512:(c + 1) * 512] = pc.astype(BF16)
            elif c == 3:
                proj_ref[r, c * 512:(c + 1) * 512] = (pc * Q_SCALE).astype(BF16)
            elif c == 4:
                for hh in range(H_B):
                    t = pc[:, hh * PAIR:(hh + 1) * PAIR].T
                    dkt_ref[bi, 0, hh, 0] = t[0:DH]
                    dkt_ref[bi, 0, hh, 1] = t[DH:PAIR]
            else:
                for hh in range(H_B):
                    dv_ref[bi, 0, hh] = pc[:, hh * PAIR:(hh + 1) * PAIR]


def _k1_lat_even(x_ref, mod_ref, g_ref, w_ref, cos_ref, sin_ref, proj_ref, *, li):
    rows, hs = _normed_groups(x_ref, mod_ref, g_ref, li, True)
    for c in range(IN_EVEN // 512):
        for r, h in zip(rows, hs):
            pc = _dot(h, w_ref[0, :, c * 512:(c + 1) * 512])
            if c == 3:
                pc = _rope(pc * Q_SCALE, cos_ref[r, :], sin_ref[r, :])
            elif c == 4:
                pc = _rope(pc, cos_ref[r, :], sin_ref[r, :])
            proj_ref[r, c * 512:(c + 1) * 512] = pc.astype(BF16)


def _k1_ctx_odd(x_ref, mod_ref, g_ref, w_ref, q_ref, nkt_ref, nvt_ref, skt_ref, svt_ref, *, li):
    rows, hs = _normed_groups(x_ref, mod_ref, g_ref, li, False)
    for c in range(4):
        for bi, (r, h) in enumerate(zip(rows, hs)):
            pc = _dot(h, w_ref[0, :, c * 512:(c + 1) * 512])
            if c == 0 or c == 3:
                q_ref[r, (c // 3) * 512:(c // 3 + 1) * 512] = (pc * Q_SCALE).astype(BF16)
            else:
                _store_heads_transposed(nkt_ref if c == 1 else nvt_ref, bi, pc, H_C // 2)
    for bi, h in enumerate(hs):
        pc = _dot(h, w_ref[0, :, 2048:IN_ODD])
        _store_heads_transposed(skt_ref, bi, pc[:, 0:PAIR], 1)
        _store_heads_transposed(svt_ref, bi, pc[:, PAIR:2 * PAIR], 1)


def _k1_lat_odd(x_ref, mod_ref, g_ref, w_ref, cos_ref, sin_ref, proj_ref, *, li):
    rows, hs = _normed_groups(x_ref, mod_ref, g_ref, li, True)
    for c in range(4):
        for r, h in zip(rows, hs):
            pc = _dot(h, w_ref[0, :, c * 512:(c + 1) * 512])
            if c == 0:
                pc = pc * Q_SCALE
            elif c == 3:
                pc = _rope(pc * Q_SCALE, cos_ref[r, :], sin_ref[r, :])
            proj_ref[r, c * 512:(c + 1) * 512] = pc.astype(BF16)
    low = _low_half((SEQ, PAIR))
    for r, h in zip(rows, hs):
        pc = _dot(h, w_ref[0, :, 2048:IN_ODD])
        for j, z in enumerate((_rope(pc[:, 0:PAIR], cos_ref[r, :], sin_ref[r, :]), pc[:, PAIR:2 * PAIR])):
            zr = pltpu.roll(z, DH, axis=1)
            base = 2048 + j * 2 * PAIR
            proj_ref[r, base:base + PAIR] = jnp.where(low, z, zr).astype(BF16)
            proj_ref[r, base + PAIR:base + 2 * PAIR] = jnp.where(low, zr, z).astype(BF16)


def _pre_mixer(x, mod, gains, w, li, *, latent, even, rope, n_tok, tile0=0):
    n_in = w.shape[2]
    tm = TM_PRE
    tiles_per_seq = DEC_SEQ // tm
    in_specs = [
        pl.BlockSpec((tm, D_MODEL), lambda i: (i + tile0, 0)),
        pl.BlockSpec((1, MOD_ROWS, 6 * D_MODEL), lambda i: (li, 0, 0)),
        pl.BlockSpec((DEPTH, D_MODEL), lambda i: (0, 0)),
        _layer_block((D_MODEL, n_in), 0),
    ]
    args = [x, mod, gains, w]
    nb = tm // SEQ
    if latent:
        in_specs += [pl.BlockSpec((tm, 128), lambda i: (i % tiles_per_seq, 0))] * 2
        args += list(rope)
        body = _k1_lat_even if even else _k1_lat_odd
        n_out = IN_EVEN if even else LAT_ODD_COLS
        out_specs = pl.BlockSpec((tm, n_out), lambda i: (i, 0))
        out_shape = jax.ShapeDtypeStruct((n_tok, n_out), BF16)
    elif even:
        body = _k1_ctx_even
        out_specs = [pl.BlockSpec((tm, 4 * MIX_A), lambda i: (i, 0)),
                     pl.BlockSpec((nb, 1, H_B, 2, DH, SEQ), lambda i: (i, 0, 0, 0, 0, 0)),
                     pl.BlockSpec((nb, 1, H_B, SEQ, PAIR), lambda i: (i, 0, 0, 0, 0))]
        out_shape = [jax.ShapeDtypeStruct((n_tok, 4 * MIX_A), BF16),
                     jax.ShapeDtypeStruct((BATCH, 1, H_B, 2, DH, SEQ), F32),
                     jax.ShapeDtypeStruct((BATCH, 1, H_B, SEQ, PAIR), F32)]
    else:
        body = _k1_ctx_odd
        c_spec = pl.BlockSpec((nb, 1, H_C, DH, SEQ), lambda i: (i, 0, 0, 0, 0))
        d_spec = pl.BlockSpec((nb, 1, KV_D, DH, SEQ), lambda i: (i, 0, 0, 0, 0))
        c_shape = jax.ShapeDtypeStruct((BATCH, 1, H_C, DH, SEQ), F32)
        d_shape = jax.ShapeDtypeStruct((BATCH, 1, KV_D, DH, SEQ), F32)
        out_specs = [pl.BlockSpec((tm, D_MODEL), lambda i: (i, 0)), c_spec, c_spec, d_spec, d_spec]
        out_shape = [jax.ShapeDtypeStruct((n_tok, D_MODEL), BF16), c_shape, c_shape, d_shape, d_shape]
    return pl.pallas_call(
        functools.partial(body, li=li),
        grid=(n_tok // tm,),
        in_specs=in_specs,
        out_specs=out_specs,
        out_shape=out_shape,
        compiler_params=_cparams(1),
        name=f"pre_mixer_{'lat' if latent else 'ctx'}_{'even' if even else 'odd'}",
    )(*args)


def _even_mixer_kernel(*refs, seq, n_seq, tq, own_keys, lam_init):
    proj_ref, cw_ref, lq1_ref, lk1_ref, lq2_ref, lk2_ref, subln_ref, ckt_ref, cv_ref = refs[:9]
    y_ref = refs[-1]
    n_rows = n_seq * seq

    pos = lax.broadcasted_iota(jnp.int32, (n_rows, 128), 0) % seq
    for j in range(MIX_A // 128):
        cols = slice(j * 128, (j + 1) * 128)
        a_b = proj_ref[:, j * 128:(j + 1) * 128].astype(F32)
        u = (proj_ref[:, MIX_A + j * 128:MIX_A + (j + 1) * 128].astype(F32)
             * proj_ref[:, 2 * MIX_A + j * 128:2 * MIX_A + (j + 1) * 128].astype(F32))
        u_prev = jnp.where(pos == 0, 0.0, pltpu.roll(u, 1, axis=0))
        u_next = jnp.where(pos == seq - 1, 0.0, pltpu.roll(u, n_rows - 1, axis=0))
        w = cw_ref[0, :, cols]
        y_ref[:, cols] = (a_b * (w[0:1] * u_prev + w[1:2] * u + w[2:3] * u_next)).astype(BF16)

    lam = (jnp.exp(jnp.sum(lq1_ref[...] * lk1_ref[...], axis=-1, keepdims=True))
           - jnp.exp(jnp.sum(lq2_ref[...] * lk2_ref[...], axis=-1, keepdims=True)) + lam_init)
    subln = subln_ref[...]
    q_col, k_col, v_col = 3 * MIX_A, 3 * MIX_A + 512, 3 * MIX_A + 1024
    ones = jnp.ones((PAST_LEN, PAIR), BF16)

    memo = _Memo()

    def own_k(b, h, m):
        kp = proj_ref[b * seq:(b + 1) * seq, k_col + h * PAIR:k_col + (h + 1) * PAIR]
        return jnp.where(_keep_mask_bf16(seq, m), kp, jnp.zeros_like(kp))

    def value_ops(b, h):
        v_ops = []
        if own_keys:
            v_own = proj_ref[b * seq:(b + 1) * seq, v_col + h * PAIR:v_col + (h + 1) * PAIR]
            v_ops.append(jnp.concatenate([v_own, jnp.ones((seq, PAIR), BF16)], axis=1))
        v_ops.append(jnp.concatenate([cv_ref[b, 0, h].astype(BF16), ones], axis=1))
        return v_ops

    def scores(b, qrows, h, m):
        qp = proj_ref[qrows, q_col + h * PAIR:q_col + (h + 1) * PAIR]
        parts = []
        if own_keys:
            parts.append(_dot_nt(qp, memo.get_or_build(("k", b, h, m), functools.partial(own_k, b, h, m))))
        ckt = memo.get_or_build(("ck", b, h, m), lambda: _pad_rows(ckt_ref[b, 0, h, m].astype(BF16), m, 0.0))
        parts.append(_dot(qp, ckt))
        return parts

    first_map = {}

    def pv(b, qrows, h, m, e_parts, _):
        v_ops = memo.get_or_build(("v", b, h), functools.partial(value_ops, b, h))
        o = None
        for e, v_op in zip(e_parts, v_ops):
            t = _dot(e, v_op)
            o = t if o is None else o + t
        attn = o[:, 0:PAIR] * (1.0 / o[:, PAIR:2 * PAIR])
        if m == 0:
            first_map[(qrows.start, h)] = attn
        else:
            y = _rms(first_map.pop((qrows.start, h)) - lam * attn, subln) * (1.0 - lam_init)
            y_ref[qrows, MIX_A + h * PAIR:MIX_A + (h + 1) * PAIR] = y.astype(BF16)

    heads = []
    for b in range(n_seq):
        for i in range(seq // tq):
            qrows = slice(b * seq + i * tq, b * seq + (i + 1) * tq)
            for h in range(H_B):
                for m in range(2):
                    heads.append(_Head(functools.partial(scores, b, qrows, h, m),
                                       functools.partial(pv, b, qrows, h, m)))
    _attend(heads, batch=EVEN_BATCH_ROWS // tq)


def _even_mixer(proj, conv_w, lq1, lk1, lq2, lk2, subln, kt, v, *, seq, n_seq, own_keys, lam_init):
    n_tok = proj.shape[0]
    rows = n_seq * seq
    small = lambda a: pl.BlockSpec((1, a.shape[1]), lambda b: (0, 0))
    in_specs = [pl.BlockSpec((rows, proj.shape[1]), lambda b: (b, 0)),
                pl.BlockSpec((1, 3, MIX_A), lambda b: (0, 0, 0)),
                small(lq1), small(lk1), small(lq2), small(lk2), small(subln),
                pl.BlockSpec((n_seq, 1, H_B, 2, DH, PAST_LEN), lambda b: (b, 0, 0, 0, 0, 0)),
                pl.BlockSpec((n_seq, 1, H_B, PAST_LEN, PAIR), lambda b: (b, 0, 0, 0, 0))]
    return pl.pallas_call(
        functools.partial(_even_mixer_kernel, seq=seq, n_seq=n_seq, tq=min(seq, 256), own_keys=own_keys,
                          lam_init=lam_init),
        grid=(n_tok // rows,),
        in_specs=in_specs,
        out_specs=pl.BlockSpec((rows, D_MODEL), lambda b: (b, 0)),
        out_shape=jax.ShapeDtypeStruct((n_tok, D_MODEL), BF16),
        compiler_params=_cparams(1),
        name=f"even_mixer_{'lat' if own_keys else 'ctx'}",
    )(proj, conv_w, lq1, lk1, lq2, lk2, subln, kt, v)


def _odd_ctx_kernel(q_ref, nkt_ref, nvt_ref, skt_ref, svt_ref, sink_ref, y_ref):
    def kv_refs(b, head):
        if head < H_C:
            return nkt_ref.at[b, 0, head], nvt_ref.at[b, 0, head]
        g = (head - H_C) // GQA
        return skt_ref.at[b, 0, g], svt_ref.at[b, 0, g]

    memo = _Memo()

    def kv_key(head):
        return head if head < H_C else H_C + (head - H_C) // GQA

    def scores(b, p, hh):
        head = 2 * p + hh
        kt = memo.get_or_build(("k", b, kv_key(head), hh),
                               lambda: _pad_rows(kv_refs(b, head)[0][...].astype(BF16), hh, 0.0))
        return [_dot(q_ref[b * SEQ:(b + 1) * SEQ, p * PAIR:(p + 1) * PAIR], kt)]

    def finish(out, b, p, hh, e_parts, term):
        head = 2 * p + hh
        vt = memo.get_or_build(("v", b, kv_key(head), hh),
                               lambda: _pad_rows(kv_refs(b, head)[1][...].astype(BF16), hh, 1.0))
        out.put(hh, _dot_nt(e_parts[0], vt), term)

    heads = []
    for b in range(CTX_SEQS_PER_STEP):
        for p in range((H_C + H_D) // 2):
            out = _PairSink(y_ref, slice(b * SEQ, (b + 1) * SEQ), slice(p * PAIR, (p + 1) * PAIR))
            for hh in range(2):
                d_head = 2 * p + hh - H_C
                sink = sink_ref[d_head] * LOG2E if d_head >= 0 else None
                heads.append(_Head(functools.partial(scores, b, p, hh), functools.partial(finish, out, b, p, hh), sink))
    _attend(heads, batch=16)


def _odd_ctx_mixer(q, nkt, nvt, skt, svt, sink):
    n_tok = q.shape[0]
    n_seq = CTX_SEQS_PER_STEP
    c_spec = pl.BlockSpec((n_seq, 1, H_C, DH, SEQ), lambda b: (b, 0, 0, 0, 0))
    d_spec = pl.BlockSpec((n_seq, 1, KV_D, DH, SEQ), lambda b: (b, 0, 0, 0, 0))
    return pl.pallas_call(
        _odd_ctx_kernel,
        grid=(n_tok // (n_seq * SEQ),),
        in_specs=[pl.BlockSpec((n_seq * SEQ, D_MODEL), lambda b: (b, 0)), c_spec, c_spec, d_spec, d_spec,
                  pl.BlockSpec(memory_space=pltpu.SMEM)],
        out_specs=pl.BlockSpec((n_seq * SEQ, D_MODEL), lambda b: (b, 0)),
        out_shape=jax.ShapeDtypeStruct((n_tok, D_MODEL), BF16),
        compiler_params=_cparams(1),
        name="odd_mixer_ctx",
    )(q, nkt, nvt, skt, svt, sink)


def _na_block(blk):
    r0 = blk * NA_BLOCK_ROWS
    first = min(max(r0 - NA_WIN_R // 2, 0), GRID_H - NA_WIN_R)
    last = min(max(r0 + NA_BLOCK_ROWS - 1 - NA_WIN_R // 2, 0), GRID_H - NA_WIN_R) + NA_WIN_R
    n_rows = -(-(last - first) // 4) * 4
    return min(first, GRID_H - n_rows), n_rows


def _build_na_bias(rpb_ref, head, nb_ref, slot):
    qc = lax.broadcasted_iota(jnp.int32, (GRID_W, GRID_W), 0)
    kc = lax.broadcasted_iota(jnp.int32, (GRID_W, GRID_W), 1)
    col_start = jnp.clip(qc - NA_WIN_C // 2, 0, GRID_W - NA_WIN_C)
    col_ok = (kc >= col_start) & (kc < col_start + NA_WIN_C)
    dc = kc - qc + NA_WIN_C - 1
    n_dr, n_dc = 2 * NA_WIN_R - 1, 2 * NA_WIN_C - 1
    neg = jnp.full((GRID_W, GRID_W), NEG, F32)
    toeplitz = []
    for dr in range(n_dr):
        t = neg
        for d in range(n_dc):
            t = jnp.where(dc == d, rpb_ref[(head * n_dr + dr) * n_dc + d] * LOG2E, t)
        toeplitz.append(jnp.where(col_ok, t, NEG))
    for blk in range(GRID_H // NA_BLOCK_ROWS):
        first, n_rows = _na_block(blk)
        for rl in range(NA_BLOCK_ROWS):
            r = blk * NA_BLOCK_ROWS + rl
            r_start = min(max(r - NA_WIN_R // 2, 0), GRID_H - NA_WIN_R)
            blocks = []
            for rk in range(first, first + n_rows):
                inside = r_start <= rk < r_start + NA_WIN_R
                blocks.append(toeplitz[rk - r + NA_WIN_R - 1] if inside else neg)
            nb_ref[slot, blk, rl * GRID_W:(rl + 1) * GRID_W, 0:n_rows * GRID_W] = jnp.concatenate(blocks, axis=1)


def _na_kernel(rpb_ref, q_ref, k_ref, v_ref, kct_ref, vct_ref, y_ref, nb_ref):
    @pl.when(pl.program_id(1) == 0)
    def _():
        for hh in range(2):
            _build_na_bias(rpb_ref, 2 * pl.program_id(0) + hh, nb_ref, hh)

    n_q = NA_BLOCK_ROWS * GRID_W

    def key_rows(b, blk):
        first, n_rows = _na_block(blk)
        return slice(b * DEC_SEQ + first * GRID_W, b * DEC_SEQ + (first + n_rows) * GRID_W), n_rows * GRID_W

    def scores(b, blk, hh):
        keys, n_keys = key_rows(b, blk)
        qp = q_ref[b * DEC_SEQ + blk * n_q:b * DEC_SEQ + (blk + 1) * n_q, :]
        kw = k_ref[keys, :]
        s1 = (_dot_nt(qp, jnp.where(_keep_mask_bf16(n_keys, hh), kw, jnp.zeros_like(kw)))
              + nb_ref[hh, blk, :, 0:n_keys])
        s2 = _dot(qp, _pad_rows(kct_ref[b, 0, hh].astype(BF16), hh, 0.0))
        return [s1, s2]

    def finish(out, b, blk, hh, e_parts, term):
        keys, n_keys = key_rows(b, blk)
        vw = v_ref[keys, :]
        o = (_dot(e_parts[0], jnp.where(_keep_mask_bf16(n_keys, hh), vw, jnp.ones_like(vw)))
             + _dot_nt(e_parts[1], _pad_rows(vct_ref[b, 0, hh].astype(BF16), hh, 1.0)))
        out.put(hh, o, term)

    heads = []
    for b in range(NA_SEQS_PER_STEP):
        for blk in range(GRID_H // NA_BLOCK_ROWS):
            rows = slice(b * DEC_SEQ + blk * n_q, b * DEC_SEQ + (blk + 1) * n_q)
            out = _PairSink(y_ref, rows, slice(0, PAIR))
            for hh in range(2):
                heads.append(_Head(functools.partial(scores, b, blk, hh), functools.partial(finish, out, b, blk, hh)))
    _attend(heads, batch=8)


def _na_mixer(proj, cache_kt, cache_vt, rpb_flat):
    pairs = H_C // 2
    n_seq = NA_SEQS_PER_STEP
    col = lambda base: pl.BlockSpec((n_seq * DEC_SEQ, PAIR), lambda hp, b: (b, base + hp))
    cache = pl.BlockSpec((n_seq, 1, 2, DH, PAST_LEN), lambda hp, b: (b, 0, hp, 0, 0))
    return pl.pallas_call(
        _na_kernel,
        grid=(pairs, DEC_BATCH // n_seq),
        in_specs=[pl.BlockSpec(memory_space=pltpu.SMEM), col(0), col(pairs), col(2 * pairs), cache, cache],
        out_specs=pl.BlockSpec((n_seq * DEC_SEQ, PAIR), lambda hp, b: (b, hp)),
        out_shape=jax.ShapeDtypeStruct((LAT_TOKENS, 512), BF16),
        scratch_shapes=[pltpu.VMEM((2, GRID_H // NA_BLOCK_ROWS, NA_BLOCK_ROWS * GRID_W, NA_MAX_KEYS), F32)],
        compiler_params=_cparams(2),
        name="na_mixer",
    )(rpb_flat, proj, proj, proj, cache_kt, cache_vt)


def _swa_kernel(q_ref, kv_ref, kct_ref, vct_ref, sink_ref, y_ref):
    n_groups = DEC_SEQ // SWA_QROWS
    ql = lax.broadcasted_iota(jnp.int32, (SWA_QROWS, SWA_BAND), 0)
    kj = lax.broadcasted_iota(jnp.int32, (SWA_QROWS, SWA_BAND), 1)

    def band_bias(first_key_minus_first_query):
        return jnp.where(jnp.abs(kj + first_key_minus_first_query - ql) <= SWA_WINDOW, 0.0, NEG)

    bias_first, bias_mid, bias_last = band_bias(0), band_bias(-SWA_BLOCK), band_bias(-2 * SWA_BLOCK)

    def band_start(grp):
        return min(max(grp * SWA_QROWS - SWA_BLOCK, 0), DEC_SEQ - SWA_BAND)

    memo = _Memo()

    def band_k(g, grp, hh):
        start = band_start(grp)
        kb = kv_ref[start:start + SWA_BAND, g * PAIR:(g + 1) * PAIR]
        return jnp.where(_keep_mask_bf16(SWA_BAND, hh), kb, jnp.zeros_like(kb))

    def band_v(g, grp, hh):
        start = band_start(grp)
        vb = kv_ref[start:start + SWA_BAND, (KV_D + g) * PAIR:(KV_D + g + 1) * PAIR]
        return jnp.where(_keep_mask_bf16(SWA_BAND, hh), vb, jnp.ones_like(vb))

    def scores(g, grp, p, hh):
        bias = bias_first if grp == 0 else (bias_last if grp == n_groups - 1 else bias_mid)
        qp = q_ref[grp * SWA_QROWS:(grp + 1) * SWA_QROWS, p * PAIR:(p + 1) * PAIR]
        s1 = _dot_nt(qp, memo.get_or_build(("k", g, grp, hh), functools.partial(band_k, g, grp, hh))) + bias
        ckt = memo.get_or_build(("ck", g, hh), lambda: _pad_rows(kct_ref[0, 0, g].astype(BF16), hh, 0.0))
        return [s1, _dot(qp, ckt)]

    def finish(out, g, grp, hh, e_parts, term):
        cvt = memo.get_or_build(("cv", g, hh), lambda: _pad_rows(vct_ref[0, 0, g].astype(BF16), hh, 1.0))
        o = (_dot(e_parts[0], memo.get_or_build(("v", g, grp, hh), functools.partial(band_v, g, grp, hh)))
             + _dot_nt(e_parts[1], cvt))
        out.put(hh, o, term)

    heads = []
    for g in range(KV_D):
        for grp in range(n_groups):
            for j in range(GQA // 2):
                p = g * (GQA // 2) + j
                out = _PairSink(y_ref, slice(grp * SWA_QROWS, (grp + 1) * SWA_QROWS), slice(p * PAIR, (p + 1) * PAIR))
                for hh in range(2):
                    sink = sink_ref[2 * p + hh] * LOG2E
                    heads.append(_Head(functools.partial(scores, g, grp, p, hh),
                                       functools.partial(finish, out, g, grp, hh), sink))
    _attend(heads, batch=16)


def _swa_mixer(proj, cache_kt, cache_vt, sink):
    cache = pl.BlockSpec((1, 1, KV_D, DH, PAST_LEN), lambda b: (b, 0, 0, 0, 0))
    return pl.pallas_call(
        _swa_kernel,
        grid=(DEC_BATCH,),
        in_specs=[pl.BlockSpec((DEC_SEQ, 512), lambda b: (b, 3)),
                  pl.BlockSpec((DEC_SEQ, 512), lambda b: (b, 4)),
                  cache, cache,
                  pl.BlockSpec(memory_space=pltpu.SMEM)],
        out_specs=pl.BlockSpec((DEC_SEQ, 512), lambda b: (b, 0)),
        out_shape=jax.ShapeDtypeStruct((LAT_TOKENS, 512), BF16),
        compiler_params=_cparams(1),
        name="swa_mixer",
    )(proj, proj, cache_kt, cache_vt, sink)


def _load_weights_bf16(li, wo_hbm, w1_hbm, w2_hbm, wo_s, w1_s, w2_s, stage_sq, stage_wide, sems):
    square = ([(wo_hbm, wo_s, j, WO_ROWS) for j in range(N_WO)] + [(w2_hbm, w2_s, j, W2_ROWS) for j in range(N_W2)])
    wide = [(w1_hbm, w1_s, j, W1_ROWS) for j in range(N_W1)]
    queues = {"sq": (square, stage_sq, 0), "wide": (wide, stage_wide, STAGE_SLOTS)}

    def copy(kind, idx):
        chunks, stage, sem0 = queues[kind]
        src, _, j, n_rows = chunks[idx]
        slot = idx % STAGE_SLOTS
        return pltpu.make_async_copy(src.at[li, j * n_rows:(j + 1) * n_rows, :], stage.at[slot], sems.at[sem0 + slot])

    for kind in queues:
        for idx in range(min(STAGE_SLOTS, len(queues[kind][0]))):
            copy(kind, idx).start()
    order = []
    for i in range(max(len(square), len(wide))):
        order += [("sq", i)] * (i < len(square)) + [("wide", i)] * (i < len(wide))
    for kind, idx in order:
        chunks, stage, _ = queues[kind]
        _, dst, j, n_rows = chunks[idx]
        copy(kind, idx).wait()
        dst[j * n_rows:(j + 1) * n_rows, :] = stage[idx % STAGE_SLOTS].astype(BF16)
        if idx + STAGE_SLOTS < len(chunks):
            copy(kind, idx + STAGE_SLOTS).start()


def _post_mixer_kernel(*refs, li, n_lat_mix, n_x, n_out):
    mix_ctx_ref, mix_lat_refs = refs[0], refs[1:1 + n_lat_mix]
    refs = refs[1 + n_lat_mix:]
    x_refs = refs[:n_x]
    mod_ref, gpost_ref, gpre_ref, gmlp_ref, wo_hbm, w1_hbm, w2_hbm = refs[n_x:7 + n_x]
    out_refs = refs[7 + n_x:7 + n_x + n_out]
    wo_s, w1_s, w2_s, stage_sq, stage_wide, sems = refs[7 + n_x + n_out:]
    step = pl.program_id(0)

    @pl.when(step == 0)
    def _():
        _load_weights_bf16(li, wo_hbm, w1_hbm, w2_hbm, wo_s, w1_s, w2_s, stage_sq, stage_wide, sems)

    is_ctx = step < CTX_TILES
    mod = mod_ref[0, pl.ds(jnp.where(is_ctx, CTX_ROW, (step - CTX_TILES) // (DEC_SEQ // TM)), 1), :]
    g_post, g_pre, g_mlp = gpost_ref[li:li + 1, :], gpre_ref[li:li + 1, :], gmlp_ref[li:li + 1, :]
    rows = [slice(i * (TM // POST_SPLIT), (i + 1) * (TM // POST_SPLIT)) for i in range(POST_SPLIT)]

    def residual(r):
        if n_x == 1:
            return x_refs[0][r, :]
        return jnp.where(is_ctx, x_refs[0][r, :], x_refs[1][r, :])

    def mixed(r):
        lat = [m[r, :] for m in mix_lat_refs]
        lat = lat[0] if n_lat_mix == 1 else jnp.concatenate(lat, axis=1)
        return jnp.where(is_ctx, mix_ctx_ref[r, :], lat)

    ys = [_dot(mixed(r), wo_s[...]) for r in rows]
    x1 = [residual(r) + mod[:, 2 * D_MODEL:3 * D_MODEL] * _rms(y, g_post) for r, y in zip(rows, ys)]
    h = [_norm_mod(x, g_pre, mod, 3 * D_MODEL).astype(BF16) for x in x1]
    acc = [None] * POST_SPLIT
    pending = []

    def second_matmul(c, i, f):
        t2 = _dot(f, w2_s[c * FF_CHUNK:(c + 1) * FF_CHUNK, :])
        acc[i] = t2 if acc[i] is None else acc[i] + t2

    for c in range(D_FF // FF_CHUNK):
        for i in range(POST_SPLIT):
            f = _dot(h[i], w1_s[:, c * FF_CHUNK:(c + 1) * FF_CHUNK])
            pending.append((c, i, jnp.square(jnp.maximum(f, 0.0)).astype(BF16)))
            if len(pending) > 1:
                second_matmul(*pending.pop(0))
    second_matmul(*pending.pop(0))
    res = [x1[i] + mod[:, 5 * D_MODEL:6 * D_MODEL] * _rms(acc[i], g_mlp) for i in range(POST_SPLIT)]
    if n_out == 1:
        for i, r in enumerate(rows):
            out_refs[0][r, :] = res[i]
    else:
        @pl.when(is_ctx)
        def _():
            for i, r in enumerate(rows):
                out_refs[0][r, :] = res[i]

        @pl.when(jnp.logical_not(is_ctx))
        def _():
            for i, r in enumerate(rows):
                out_refs[1][r, :] = res[i]


def _post_mixer(mix_ctx, mix_lat, xs, mod, g_post, g_pre, g_mlp, w_out, w1, w2, li, *, split_out):
    tile = lambda s: s
    ctx_tile = lambda s: jnp.minimum(tile(s), CTX_TILES - 1)
    lat_tile = lambda s: jnp.maximum(tile(s) - CTX_TILES, 0)
    token_spec = lambda index, width=D_MODEL: pl.BlockSpec((TM, width), lambda s: (index(s), 0))
    mix_specs = [token_spec(ctx_tile)] + [token_spec(lat_tile, m.shape[1]) for m in mix_lat]
    x_specs = [token_spec(tile)] if len(xs) == 1 else [token_spec(ctx_tile), token_spec(lat_tile)]
    if split_out:
        out_specs = [token_spec(ctx_tile), token_spec(lat_tile)]
        out_shape = [jax.ShapeDtypeStruct((CTX_TOKENS, D_MODEL), F32), jax.ShapeDtypeStruct((LAT_TOKENS, D_MODEL), F32)]
    else:
        out_specs = [token_spec(tile)]
        out_shape = [jax.ShapeDtypeStruct((ALL_TOKENS, D_MODEL), F32)]
    gain = pl.BlockSpec((DEPTH, D_MODEL), lambda s: (0, 0))
    return pl.pallas_call(
        functools.partial(_post_mixer_kernel, li=li, n_lat_mix=len(mix_lat), n_x=len(xs), n_out=len(out_specs)),
        grid=(ALL_TOKENS // TM,),
        in_specs=mix_specs + x_specs + [
            pl.BlockSpec((1, MOD_ROWS, 6 * D_MODEL), lambda s: (li, 0, 0)),
            gain, gain, gain] + [pl.BlockSpec(memory_space=pl.ANY)] * 3,
        out_specs=out_specs,
        out_shape=out_shape,
        scratch_shapes=[pltpu.VMEM((D_MODEL, D_MODEL), BF16), pltpu.VMEM((D_MODEL, D_FF), BF16),
                        pltpu.VMEM((D_FF, D_MODEL), BF16),
                        pltpu.VMEM((STAGE_SLOTS, WO_ROWS, D_MODEL), F32), pltpu.VMEM((STAGE_SLOTS, W1_ROWS, D_FF), F32),
                        pltpu.SemaphoreType.DMA((2 * STAGE_SLOTS,))],
        compiler_params=_cparams(1),
        name="post_mixer",
    )(mix_ctx, *mix_lat, *xs, mod, g_post, g_pre, g_mlp, w_out, w1, w2)


def _rope_tables():
    t = jnp.arange(DEC_SEQ)
    rows = (t // GRID_W).astype(F32)
    cols = (t % GRID_W).astype(F32)
    q4 = DH // 4
    inv = 1.0 / (ROPE_BASE ** (jnp.arange(q4, dtype=F32) / q4))
    ar, ac = rows[:, None] * inv, cols[:, None] * inv
    cos64 = jnp.concatenate([jnp.cos(ar), jnp.cos(ar), jnp.cos(ac), jnp.cos(ac)], axis=1)
    sin64 = jnp.concatenate([-jnp.sin(ar), jnp.sin(ar), -jnp.sin(ac), jnp.sin(ac)], axis=1)
    return jnp.tile(cos64, (1, 2)), jnp.tile(sin64, (1, 2))


def kernel(x_prompt, x_sample, cache_diff_k, cache_diff_v, cache_na_k, cache_na_v, cache_swa_k, cache_swa_v, c, c_ctx, mod_w, mod_b, norm_mix_pre, norm_mix_post, norm_mlp_pre, norm_mlp_post, w_in_even, conv_w, lambda_q1, lambda_k1, lambda_q2, lambda_k2, subln, w_in_odd, rpb, sink, w_out, mlp_w1, mlp_w2):
    cond = jnp.concatenate([c, c_ctx[None, :], jnp.zeros((MOD_ROWS - DEC_BATCH - 1, D_MODEL), F32)], axis=0)
    mod, *w_in = _modulation(cond, mod_w, mod_b, w_in_even, w_in_odd)
    rope = _rope_tables()

    t_minor = lambda a: jnp.swapaxes(a, -1, -2)

    xp = x_prompt.reshape(CTX_TOKENS, D_MODEL)
    xs = x_sample.reshape(LAT_TOKENS, D_MODEL)
    lam_init0 = 0.8 - 0.6 * math.exp(-0.3 * 0)
    lam_args = (conv_w, lambda_q1, lambda_k1, lambda_q2, lambda_k2, subln)
    sink_flat = sink.reshape(-1)

    def post(li, mix_ctx, mix_lat, x_parts, split_out):
        return _post_mixer(mix_ctx, mix_lat, x_parts, mod, norm_mix_post, norm_mlp_pre, norm_mlp_post, w_out, mlp_w1,
                           mlp_w2, li, split_out=split_out)

    proj, new_diff_kt, new_diff_v = _pre_mixer(xp, mod, norm_mix_pre, w_in[0], 0, latent=False, even=True, rope=None,
                                               n_tok=CTX_TOKENS)
    mix_ctx = _even_mixer(proj, *lam_args, new_diff_kt, new_diff_v, seq=SEQ, n_seq=CTX_SEQS_PER_STEP_EVEN,
                          own_keys=False, lam_init=lam_init0)
    proj = _pre_mixer(xs, mod, norm_mix_pre, w_in[0], 0, latent=True, even=True, rope=rope, n_tok=LAT_TOKENS)
    mix_lat = _even_mixer(proj, *lam_args, t_minor(cache_diff_k), cache_diff_v, seq=DEC_SEQ, n_seq=1, own_keys=True,
                          lam_init=lam_init0)
    (x_all,) = post(0, mix_ctx, [mix_lat], [xp, xs], False)

    q, new_na_kt, new_na_vt, new_swa_kt, new_swa_vt = _pre_mixer(
        x_all, mod, norm_mix_pre, w_in[1], 1, latent=False, even=False, rope=None, n_tok=CTX_TOKENS)
    mix_ctx = _odd_ctx_mixer(q, new_na_kt, new_na_vt, new_swa_kt, new_swa_vt, sink_flat)
    proj = _pre_mixer(x_all, mod, norm_mix_pre, w_in[1], 1, latent=True, even=False, rope=rope, n_tok=LAT_TOKENS,
                      tile0=CTX_TOKENS // TM_PRE)
    mix_c = _na_mixer(proj, t_minor(cache_na_k), t_minor(cache_na_v), rpb.reshape(-1))
    mix_d = _swa_mixer(proj, t_minor(cache_swa_k), t_minor(cache_swa_v), sink_flat)
    xp, xs = post(1, mix_ctx, [mix_c, mix_d], [x_all], True)

    return (xp.reshape(BATCH, SEQ, D_MODEL), xs.reshape(DEC_BATCH, DEC_SEQ, D_MODEL),
            t_minor(new_diff_kt), new_diff_v, t_minor(new_na_kt), t_minor(new_na_vt),
            t_minor(new_swa_kt), t_minor(new_swa_vt))
```

```python
import functools
import math

import jax
import jax.numpy as jnp
from jax import lax
from jax.experimental import pallas as pl
from jax.experimental.pallas import tpu as pltpu

F32 = jnp.float32
BF16 = jnp.bfloat16

D_MODEL = 1024
BATCH = 32
SEQ = 256
DEPTH = 2
DEC_BATCH = 8
DEC_SEQ = 1024
PAST_LEN = 256
GRID_W = 64
GRID_H = DEC_SEQ // GRID_W
MIX_A = 512
H_B = 4
DH = 64
PAIR = 2 * DH
H_C = 8
H_D = 8
KV_D = 2
GQA = H_D // KV_D
IN_EVEN = 3072
IN_ODD = 2304
D_FF = 4096
NA_WIN_R = 8
NA_WIN_C = 16
SWA_BLOCK = 128
SWA_WINDOW = 128
ROPE_BASE = 10000.0
EPS = 1e-6
NEG = -1e30
LOG2E = math.log2(math.e)
Q_SCALE = DH ** -0.5 * LOG2E

MOD_ROWS = 16
CTX_ROW = DEC_BATCH
V7X_VMEM_BYTES = 64 * 1024 * 1024
VMEM_LIMIT = V7X_VMEM_BYTES - 12 * 1024 * 1024

CTX_TOKENS = BATCH * SEQ
LAT_TOKENS = DEC_BATCH * DEC_SEQ
ALL_TOKENS = CTX_TOKENS + LAT_TOKENS

TM = 512
TM_PRE = 1024
FF_CHUNK = 1024
POST_SPLIT = 2
WCHUNK_ELEMS = 128 * 1024
WO_ROWS, W1_ROWS, W2_ROWS = WCHUNK_ELEMS // D_MODEL, WCHUNK_ELEMS // D_FF, WCHUNK_ELEMS // D_MODEL
N_WO, N_W1, N_W2 = D_MODEL // WO_ROWS, D_MODEL // W1_ROWS, D_FF // W2_ROWS
STAGE_SLOTS = 4
CTX_TILES = CTX_TOKENS // TM
CTX_SEQS_PER_STEP = 4
CTX_SEQS_PER_STEP_EVEN = 2
MOD_TN = 3072
EVEN_BATCH_ROWS = 2048
NA_SEQS_PER_STEP = 2
NA_BLOCK_ROWS = 4
NA_MAX_KEYS = 768
SWA_QROWS = SWA_BLOCK
SWA_BAND = 3 * SWA_BLOCK
LAT_ODD_COLS = IN_ODD + 2 * PAIR


def _cparams(n_axes):
    return pltpu.CompilerParams(dimension_semantics=("arbitrary",) * n_axes, vmem_limit_bytes=VMEM_LIMIT)


def _rms(x, g):
    return x * lax.rsqrt(jnp.mean(x * x, axis=-1, keepdims=True) + EPS) * g


def _dot(a, b):
    return jnp.dot(a, b, preferred_element_type=F32)


def _dot_nt(a, b):
    return lax.dot_general(a, b, (((1,), (1,)), ((), ())), preferred_element_type=F32)


def _layer_block(shape, li):
    zeros = (0,) * len(shape)
    return pl.BlockSpec((1,) + tuple(shape), lambda *_: (li,) + zeros, pipeline_mode=pl.Buffered(1))


def _low_half(shape):
    return (lax.broadcasted_iota(jnp.int32, shape, len(shape) - 1) & DH) == 0


def _keep_mask_bf16(rows, hh):
    lane = lax.broadcasted_iota(jnp.int32, (rows, PAIR), 1)
    half = (lane & DH).astype(F32).astype(BF16)
    return (half == 0) if hh == 0 else (half != 0)


def _pad_rows(t, hh, fill):
    other = jnp.full(t.shape, fill, t.dtype)
    return jnp.concatenate([t, other] if hh == 0 else [other, t], axis=0)


class _Memo(dict):
    def get_or_build(self, key, build):
        if key not in self:
            self[key] = build()
        return self[key]


def _merge_pair(o0, o1, extra_den=None):
    low = _low_half(o0.shape)
    num = jnp.where(low, o0, o1)
    den = pltpu.roll(jnp.where(low, o1, o0), DH, axis=1)
    if extra_den is not None:
        den = den + jnp.where(low, extra_den[0], extra_den[1])
    return num * (1.0 / den)


class _Head:
    def __init__(self, scores, finish, sink=None):
        self.scores, self.finish, self.sink = scores, finish, sink


def _attend(heads, batch):
    batches = [heads[i:i + batch] for i in range(0, len(heads), batch)]
    cur = [hd.scores() for hd in batches[0]]
    for bi, group in enumerate(batches):
        nxt = [hd.scores() for hd in batches[bi + 1]] if bi + 1 < len(batches) else None
        exps = []
        for hd, parts in zip(group, cur):
            chunks = [s[:, j * 128:(j + 1) * 128] for s in parts for j in range(s.shape[1] // 128)]
            m_el = functools.reduce(jnp.maximum, chunks)
            if hd.sink is not None:
                m_el = jnp.maximum(m_el, hd.sink)
            mx = jnp.broadcast_to(jnp.max(m_el, axis=-1, keepdims=True), m_el.shape)
            term = None if hd.sink is None else jnp.exp2(hd.sink - mx)
            e_parts = [jnp.concatenate([jnp.exp2(s[:, j * 128:(j + 1) * 128] - mx)
                                        for j in range(s.shape[1] // 128)], axis=1).astype(BF16) for s in parts]
            exps.append((e_parts, term))
        for hd, (e_parts, term) in zip(group, exps):
            hd.finish(e_parts, term)
        cur = nxt


class _PairSink:
    def __init__(self, y_ref, rows, cols):
        self.y_ref, self.rows, self.cols, self.first = y_ref, rows, cols, None

    def put(self, hh, o, term):
        if hh == 0:
            self.first = (o, term)
        else:
            o0, term0 = self.first
            extra = None if term is None else (term0, term)
            self.y_ref[self.rows, self.cols] = _merge_pair(o0, o, extra).astype(BF16)


def _mod_kernel(cond_ref, w_ref, b_ref, win_e_ref, win_o_ref, o_ref, win_e_bf, win_o_bf):
    c = cond_ref[...]
    s = c * (1.0 / (1.0 + jnp.exp(-c)))
    o_ref[0] = _dot(s.astype(BF16), w_ref[0].astype(BF16)) + b_ref[pl.ds(pl.program_id(0), 1), :]
    win_e_bf[...] = win_e_ref[...].astype(BF16)
    win_o_bf[...] = win_o_ref[...].astype(BF16)


def _modulation(cond, mod_w, mod_b, w_in_even, w_in_odd):
    n_col = 6 * D_MODEL // MOD_TN
    slab = D_MODEL // (DEPTH * n_col)
    w_slab = lambda width: pl.BlockSpec((1, slab, width), lambda l, j: (0, l * n_col + j, 0))
    return pl.pallas_call(
        _mod_kernel,
        grid=(DEPTH, n_col),
        in_specs=[
            pl.BlockSpec((MOD_ROWS, D_MODEL), lambda l, j: (0, 0)),
            pl.BlockSpec((1, D_MODEL, MOD_TN), lambda l, j: (l, 0, j)),
            pl.BlockSpec((DEPTH, MOD_TN), lambda l, j: (0, j)),
            w_slab(IN_EVEN), w_slab(IN_ODD),
        ],
        out_specs=[pl.BlockSpec((1, MOD_ROWS, MOD_TN), lambda l, j: (l, 0, j)), w_slab(IN_EVEN), w_slab(IN_ODD)],
        out_shape=[jax.ShapeDtypeStruct((DEPTH, MOD_ROWS, 6 * D_MODEL), F32),
                   jax.ShapeDtypeStruct((1, D_MODEL, IN_EVEN), BF16), jax.ShapeDtypeStruct((1, D_MODEL, IN_ODD), BF16)],
        compiler_params=_cparams(2),
        name="adaln_modulation",
    )(cond, mod_w, mod_b, w_in_even, w_in_odd)


def _norm_mod(x, g, mod, shift_col):
    sh = mod[:, shift_col:shift_col + D_MODEL]
    sc = mod[:, shift_col + D_MODEL:shift_col + 2 * D_MODEL]
    return _rms(x, g) * (1.0 + sc) + sh


def _rope(z, cos, sin):
    outs = []
    for j in range(z.shape[1] // 128):
        zj = z[:, j * 128:(j + 1) * 128]
        lane = lax.broadcasted_iota(jnp.int32, zj.shape, 1)
        partner = jnp.where((lane & 16) == 0, pltpu.roll(zj, 128 - 16, axis=1), pltpu.roll(zj, 16, axis=1))
        outs.append(zj * cos + partner * sin)
    return outs[0] if len(outs) == 1 else jnp.concatenate(outs, axis=1)


def _store_heads_transposed(dst, bi, pc_rows, n_pairs):
    for p in range(n_pairs):
        t = pc_rows[:, p * PAIR:(p + 1) * PAIR].T
        dst[bi, 0, 2 * p] = t[0:DH]
        dst[bi, 0, 2 * p + 1] = t[DH:PAIR]


def _mod_row(mod_ref, latent):
    if latent:
        return mod_ref[0, pl.ds(pl.program_id(0) // (DEC_SEQ // TM_PRE), 1), :]
    return mod_ref[0, CTX_ROW:CTX_ROW + 1, :]


def _normed_groups(x_ref, mod_ref, g_ref, li, latent):
    rows = [slice(i * SEQ, (i + 1) * SEQ) for i in range(TM_PRE // SEQ)]
    mod, g = _mod_row(mod_ref, latent), g_ref[li:li + 1, :]
    return rows, [_norm_mod(x_ref[r, :], g, mod, 0).astype(BF16) for r in rows]


def _k1_ctx_even(x_ref, mod_ref, g_ref, w_ref, proj_ref, dkt_ref, dv_ref, *, li):
    rows, hs = _normed_groups(x_ref, mod_ref, g_ref, li, False)
    for c in range(IN_EVEN // 512):
        for bi, (r, h) in enumerate(zip(rows, hs)):
            pc = _dot(h, w_ref[0, :, c * 512:(c + 1) * 512])
            if c < 3:
                proj_ref[r, c * 512:(c + 1) * 512] = pc.astype(BF16)
            elif c == 3:
                proj_ref[r, c * 512:(c + 1) * 512] = (pc * Q_SCALE).astype(BF16)
            elif c == 4:
                for hh in range(H_B):
                    t = pc[:, hh * PAIR:(hh + 1) * PAIR].T
                    dkt_ref[bi, 0, hh, 0] = t[0:DH]
                    dkt_ref[bi, 0, hh, 1] = t[DH:PAIR]
            else:
                for hh in range(H_B):
                    dv_ref[bi, 0, hh] = pc[:, hh * PAIR:(hh + 1) * PAIR]


def _k1_lat_even(x_ref, mod_ref, g_ref, w_ref, cos_ref, sin_ref, proj_ref, *, li):
    rows, hs = _normed_groups(x_ref, mod_ref, g_ref, li, True)
    for c in range(IN_EVEN // 512):
        for r, h in zip(rows, hs):
            pc = _dot(h, w_ref[0, :, c * 512:(c + 1) * 512])
            if c == 3:
                pc = _rope(pc * Q_SCALE, cos_ref[r, :], sin_ref[r, :])
            elif c == 4:
                pc = _rope(pc, cos_ref[r, :], sin_ref[r, :])
            proj_ref[r, c * 512:(c + 1) * 512] = pc.astype(BF16)


def _k1_ctx_odd(x_ref, mod_ref, g_ref, w_ref, q_ref, nkt_ref, nvt_ref, skt_ref, svt_ref, *, li):
    rows, hs = _normed_groups(x_ref, mod_ref, g_ref, li, False)
    for c in range(4):
        for bi, (r, h) in enumerate(zip(rows, hs)):
            pc = _dot(h, w_ref[0, :, c * 512:(c + 1) * 512])
            if c == 0 or c == 3:
                q_ref[r, (c // 3) * 512:(c // 3 + 1) * 512] = (pc * Q_SCALE).astype(BF16)
            else:
                _store_heads_transposed(nkt_ref if c == 1 else nvt_ref, bi, pc, H_C // 2)
    for bi, h in enumerate(hs):
        pc = _dot(h, w_ref[0, :, 2048:IN_ODD])
        _store_heads_transposed(skt_ref, bi, pc[:, 0:PAIR], 1)
        _store_heads_transposed(svt_ref, bi, pc[:, PAIR:2 * PAIR], 1)


def _k1_lat_odd(x_ref, mod_ref, g_ref, w_ref, cos_ref, sin_ref, proj_ref, *, li):
    rows, hs = _normed_groups(x_ref, mod_ref, g_ref, li, True)
    for c in range(4):
        for r, h in zip(rows, hs):
            pc = _dot(h, w_ref[0, :, c * 512:(c + 1) * 512])
            if c == 0:
                pc = pc * Q_SCALE
            elif c == 3:
                pc = _rope(pc * Q_SCALE, cos_ref[r, :], sin_ref[r, :])
            proj_ref[r, c * 512:(c + 1) * 512] = pc.astype(BF16)
    low = _low_half((SEQ, PAIR))
    for r, h in zip(rows, hs):
        pc = _dot(h, w_ref[0, :, 2048:IN_ODD])
        for j, z in enumerate((_rope(pc[:, 0:PAIR], cos_ref[r, :], sin_ref[r, :]), pc[:, PAIR:2 * PAIR])):
            zr = pltpu.roll(z, DH, axis=1)
            base = 2048 + j * 2 * PAIR
            proj_ref[r, base:base + PAIR] = jnp.where(low, z, zr).astype(BF16)
            proj_ref[r, base + PAIR:base + 2 * PAIR] = jnp.where(low, zr, z).astype(BF16)


def _pre_mixer(x, mod, gains, w, li, *, latent, even, rope, n_tok, tile0=0):
    n_in = w.shape[2]
    tm = TM_PRE
    tiles_per_seq = DEC_SEQ // tm
    in_specs = [
        pl.BlockSpec((tm, D_MODEL), lambda i: (i + tile0, 0)),
        pl.BlockSpec((1, MOD_ROWS, 6 * D_MODEL), lambda i: (li, 0, 0)),
        pl.BlockSpec((DEPTH, D_MODEL), lambda i: (0, 0)),
        _layer_block((D_MODEL, n_in), 0),
    ]
    args = [x, mod, gains, w]
    nb = tm // SEQ
    if latent:
        in_specs += [pl.BlockSpec((tm, 128), lambda i: (i % tiles_per_seq, 0))] * 2
        args += list(rope)
        body = _k1_lat_even if even else _k1_lat_odd
        n_out = IN_EVEN if even else LAT_ODD_COLS
        out_specs = pl.BlockSpec((tm, n_out), lambda i: (i, 0))
        out_shape = jax.ShapeDtypeStruct((n_tok, n_out), BF16)
    elif even:
        body = _k1_ctx_even
        out_specs = [pl.BlockSpec((tm, 4 * MIX_A), lambda i: (i, 0)),
                     pl.BlockSpec((nb, 1, H_B, 2, DH, SEQ), lambda i: (i, 0, 0, 0, 0, 0)),
                     pl.BlockSpec((nb, 1, H_B, SEQ, PAIR), lambda i: (i, 0, 0, 0, 0))]
        out_shape = [jax.ShapeDtypeStruct((n_tok, 4 * MIX_A), BF16),
                     jax.ShapeDtypeStruct((BATCH, 1, H_B, 2, DH, SEQ), F32),
                     jax.ShapeDtypeStruct((BATCH, 1, H_B, SEQ, PAIR), F32)]
    else:
        body = _k1_ctx_odd
        c_spec = pl.BlockSpec((nb, 1, H_C, DH, SEQ), lambda i: (i, 0, 0, 0, 0))
        d_spec = pl.BlockSpec((nb, 1, KV_D, DH, SEQ), lambda i: (i, 0, 0, 0, 0))
        c_shape = jax.ShapeDtypeStruct((BATCH, 1, H_C, DH, SEQ), F32)
        d_shape = jax.ShapeDtypeStruct((BATCH, 1, KV_D, DH, SEQ), F32)
        out_specs = [pl.BlockSpec((tm, D_MODEL), lambda i: (i, 0)), c_spec, c_spec, d_spec, d_spec]
        out_shape = [jax.ShapeDtypeStruct((n_tok, D_MODEL), BF16), c_shape, c_shape, d_shape, d_shape]
    return pl.pallas_call(
        functools.partial(body, li=li),
        grid=(n_tok // tm,),
        in_specs=in_specs,
        out_specs=out_specs,
        out_shape=out_shape,
        compiler_params=_cparams(1),
        name=f"pre_mixer_{'lat' if latent else 'ctx'}_{'even' if even else 'odd'}",
    )(*args)


def _even_mixer_kernel(*refs, seq, n_seq, tq, own_keys, lam_init):
    proj_ref, cw_ref, lq1_ref, lk1_ref, lq2_ref, lk2_ref, subln_ref, ckt_ref, cv_ref = refs[:9]
    y_ref = refs[-1]
    n_rows = n_seq * seq

    pos = lax.broadcasted_iota(jnp.int32, (n_rows, 128), 0) % seq
    for j in range(MIX_A // 128):
        cols = slice(j * 128, (j + 1) * 128)
        a_b = proj_ref[:, j * 128:(j + 1) * 128].astype(F32)
        u = (proj_ref[:, MIX_A + j * 128:MIX_A + (j + 1) * 128].astype(F32)
             * proj_ref[:, 2 * MIX_A + j * 128:2 * MIX_A + (j + 1) * 128].astype(F32))
        u_prev = jnp.where(pos == 0, 0.0, pltpu.roll(u, 1, axis=0))
        u_next = jnp.where(pos == seq - 1, 0.0, pltpu.roll(u, n_rows - 1, axis=0))
        w = cw_ref[0, :, cols]
        y_ref[:, cols] = (a_b * (w[0:1] * u_prev + w[1:2] * u + w[2:3] * u_next)).astype(BF16)

    lam = (jnp.exp(jnp.sum(lq1_ref[...] * lk1_ref[...], axis=-1, keepdims=True))
           - jnp.exp(jnp.sum(lq2_ref[...] * lk2_ref[...], axis=-1, keepdims=True)) + lam_init)
    subln = subln_ref[...]
    q_col, k_col, v_col = 3 * MIX_A, 3 * MIX_A + 512, 3 * MIX_A + 1024
    ones = jnp.ones((PAST_LEN, PAIR), BF16)

    memo = _Memo()

    def own_k(b, h, m):
        kp = proj_ref[b * seq:(b + 1) * seq, k_col + h * PAIR:k_col + (h + 1) * PAIR]
        return jnp.where(_keep_mask_bf16(seq, m), kp, jnp.zeros_like(kp))

    def value_ops(b, h):
        v_ops = []
        if own_keys:
            v_own = proj_ref[b * seq:(b + 1) * seq, v_col + h * PAIR:v_col + (h + 1) * PAIR]
            v_ops.append(jnp.concatenate([v_own, jnp.ones((seq, PAIR), BF16)], axis=1))
        v_ops.append(jnp.concatenate([cv_ref[b, 0, h].astype(BF16), ones], axis=1))
        return v_ops

    def scores(b, qrows, h, m):
        qp = proj_ref[qrows, q_col + h * PAIR:q_col + (h + 1) * PAIR]
        parts = []
        if own_keys:
            parts.append(_dot_nt(qp, memo.get_or_build(("k", b, h, m), functools.partial(own_k, b, h, m))))
        ckt = memo.get_or_build(("ck", b, h, m), lambda: _pad_rows(ckt_ref[b, 0, h, m].astype(BF16), m, 0.0))
        parts.append(_dot(qp, ckt))
        return parts

    first_map = {}

    def pv(b, qrows, h, m, e_parts, _):
        v_ops = memo.get_or_build(("v", b, h), functools.partial(value_ops, b, h))
        o = None
        for e, v_op in zip(e_parts, v_ops):
            t = _dot(e, v_op)
            o = t if o is None else o + t
        attn = o[:, 0:PAIR] * (1.0 / o[:, PAIR:2 * PAIR])
        if m == 0:
            first_map[(qrows.start, h)] = attn
        else:
            y = _rms(first_map.pop((qrows.start, h)) - lam * attn, subln) * (1.0 - lam_init)
            y_ref[qrows, MIX_A + h * PAIR:MIX_A + (h + 1) * PAIR] = y.astype(BF16)

    heads = []
    for b in range(n_seq):
        for i in range(seq // tq):
            qrows = slice(b * seq + i * tq, b * seq + (i + 1) * tq)
            for h in range(H_B):
                for m in range(2):
                    heads.append(_Head(functools.partial(scores, b, qrows, h, m),
                                       functools.partial(pv, b, qrows, h, m)))
    _attend(heads, batch=EVEN_BATCH_ROWS // tq)


def _even_mixer(proj, conv_w, lq1, lk1, lq2, lk2, subln, kt, v, *, seq, n_seq, own_keys, lam_init):
    n_tok = proj.shape[0]
    rows = n_seq * seq
    small = lambda a: pl.BlockSpec((1, a.shape[1]), lambda b: (0, 0))
    in_specs = [pl.BlockSpec((rows, proj.shape[1]), lambda b: (b, 0)),
                pl.BlockSpec((1, 3, MIX_A), lambda b: (0, 0, 0)),
                small(lq1), small(lk1), small(lq2), small(lk2), small(subln),
                pl.BlockSpec((n_seq, 1, H_B, 2, DH, PAST_LEN), lambda b: (b, 0, 0, 0, 0, 0)),
                pl.BlockSpec((n_seq, 1, H_B, PAST_LEN, PAIR), lambda b: (b, 0, 0, 0, 0))]
    return pl.pallas_call(
        functools.partial(_even_mixer_kernel, seq=seq, n_seq=n_seq, tq=min(seq, 256), own_keys=own_keys,
                          lam_init=lam_init),
        grid=(n_tok // rows,),
        in_specs=in_specs,
        out_specs=pl.BlockSpec((rows, D_MODEL), lambda b: (b, 0)),
        out_shape=jax.ShapeDtypeStruct((n_tok, D_MODEL), BF16),
        compiler_params=_cparams(1),
        name=f"even_mixer_{'lat' if own_keys else 'ctx'}",
    )(proj, conv_w, lq1, lk1, lq2, lk2, subln, kt, v)


def _odd_ctx_kernel(q_ref, nkt_ref, nvt_ref, skt_ref, svt_ref, sink_ref, y_ref):
    def kv_refs(b, head):
        if head < H_C:
            return nkt_ref.at[b, 0, head], nvt_ref.at[b, 0, head]
        g = (head - H_C) // GQA
        return skt_ref.at[b, 0, g], svt_ref.at[b, 0, g]

    memo = _Memo()

    def kv_key(head):
        return head if head < H_C else H_C + (head - H_C) // GQA

    def scores(b, p, hh):
        head = 2 * p + hh
        kt = memo.get_or_build(("k", b, kv_key(head), hh),
                               lambda: _pad_rows(kv_refs(b, head)[0][...].astype(BF16), hh, 0.0))
        return [_dot(q_ref[b * SEQ:(b + 1) * SEQ, p * PAIR:(p + 1) * PAIR], kt)]

    def finish(out, b, p, hh, e_parts, term):
        head = 2 * p + hh
        vt = memo.get_or_build(("v", b, kv_key(head), hh),
                               lambda: _pad_rows(kv_refs(b, head)[1][...].astype(BF16), hh, 1.0))
        out.put(hh, _dot_nt(e_parts[0], vt), term)

    heads = []
    for b in range(CTX_SEQS_PER_STEP):
        for p in range((H_C + H_D) // 2):
            out = _PairSink(y_ref, slice(b * SEQ, (b + 1) * SEQ), slice(p * PAIR, (p + 1) * PAIR))
            for hh in range(2):
                d_head = 2 * p + hh - H_C
                sink = sink_ref[d_head] * LOG2E if d_head >= 0 else None
                heads.append(_Head(functools.partial(scores, b, p, hh), functools.partial(finish, out, b, p, hh), sink))
    _attend(heads, batch=16)


def _odd_ctx_mixer(q, nkt, nvt, skt, svt, sink):
    n_tok = q.shape[0]
    n_seq = CTX_SEQS_PER_STEP
    c_spec = pl.BlockSpec((n_seq, 1, H_C, DH, SEQ), lambda b: (b, 0, 0, 0, 0))
    d_spec = pl.BlockSpec((n_seq, 1, KV_D, DH, SEQ), lambda b: (b, 0, 0, 0, 0))
    return pl.pallas_call(
        _odd_ctx_kernel,
        grid=(n_tok // (n_seq * SEQ),),
        in_specs=[pl.BlockSpec((n_seq * SEQ, D_MODEL), lambda b: (b, 0)), c_spec, c_spec, d_spec, d_spec,
                  pl.BlockSpec(memory_space=pltpu.SMEM)],
        out_specs=pl.BlockSpec((n_seq * SEQ, D_MODEL), lambda b: (b, 0)),
        out_shape=jax.ShapeDtypeStruct((n_tok, D_MODEL), BF16),
        compiler_params=_cparams(1),
        name="odd_mixer_ctx",
    )(q, nkt, nvt, skt, svt, sink)


def _na_block(blk):
    r0 = blk * NA_BLOCK_ROWS
    first = min(max(r0 - NA_WIN_R // 2, 0), GRID_H - NA_WIN_R)
    last = min(max(r0 + NA_BLOCK_ROWS - 1 - NA_WIN_R // 2, 0), GRID_H - NA_WIN_R) + NA_WIN_R
    n_rows = -(-(last - first) // 4) * 4
    return min(first, GRID_H - n_rows), n_rows


def _build_na_bias(rpb_ref, head, nb_ref, slot):
    qc = lax.broadcasted_iota(jnp.int32, (GRID_W, GRID_W), 0)
    kc = lax.broadcasted_iota(jnp.int32, (GRID_W, GRID_W), 1)
    col_start = jnp.clip(qc - NA_WIN_C // 2, 0, GRID_W - NA_WIN_C)
    col_ok = (kc >= col_start) & (kc < col_start + NA_WIN_C)
    dc = kc - qc + NA_WIN_C - 1
    n_dr, n_dc = 2 * NA_WIN_R - 1, 2 * NA_WIN_C - 1
    neg = jnp.full((GRID_W, GRID_W), NEG, F32)
    toeplitz = []
    for dr in range(n_dr):
        t = neg
        for d in range(n_dc):
            t = jnp.where(dc == d, rpb_ref[(head * n_dr + dr) * n_dc + d] * LOG2E, t)
        toeplitz.append(jnp.where(col_ok, t, NEG))
    for blk in range(GRID_H // NA_BLOCK_ROWS):
        first, n_rows = _na_block(blk)
        for rl in range(NA_BLOCK_ROWS):
            r = blk * NA_BLOCK_ROWS + rl
            r_start = min(max(r - NA_WIN_R // 2, 0), GRID_H - NA_WIN_R)
            blocks = []
            for rk in range(first, first + n_rows):
                inside = r_start <= rk < r_start + NA_WIN_R
                blocks.append(toeplitz[rk - r + NA_WIN_R - 1] if inside else neg)
            nb_ref[slot, blk, rl * GRID_W:(rl + 1) * GRID_W, 0:n_rows * GRID_W] = jnp.concatenate(blocks, axis=1)


def _na_kernel(rpb_ref, q_ref, k_ref, v_ref, kct_ref, vct_ref, y_ref, nb_ref):
    @pl.when(pl.program_id(1) == 0)
    def _():
        for hh in range(2):
            _build_na_bias(rpb_ref, 2 * pl.program_id(0) + hh, nb_ref, hh)

    n_q = NA_BLOCK_ROWS * GRID_W

    def key_rows(b, blk):
        first, n_rows = _na_block(blk)
        return slice(b * DEC_SEQ + first * GRID_W, b * DEC_SEQ + (first + n_rows) * GRID_W), n_rows * GRID_W

    def scores(b, blk, hh):
        keys, n_keys = key_rows(b, blk)
        qp = q_ref[b * DEC_SEQ + blk * n_q:b * DEC_SEQ + (blk + 1) * n_q, :]
        kw = k_ref[keys, :]
        s1 = (_dot_nt(qp, jnp.where(_keep_mask_bf16(n_keys, hh), kw, jnp.zeros_like(kw)))
              + nb_ref[hh, blk, :, 0:n_keys])
        s2 = _dot(qp, _pad_rows(kct_ref[b, 0, hh].astype(BF16), hh, 0.0))
        return [s1, s2]

    def finish(out, b, blk, hh, e_parts, term):
        keys, n_keys = key_rows(b, blk)
        vw = v_ref[keys, :]
        o = (_dot(e_parts[0], jnp.where(_keep_mask_bf16(n_keys, hh), vw, jnp.ones_like(vw)))
             + _dot_nt(e_parts[1], _pad_rows(vct_ref[b, 0, hh].astype(BF16), hh, 1.0)))
        out.put(hh, o, term)

    heads = []
    for b in range(NA_SEQS_PER_STEP):
        for blk in range(GRID_H // NA_BLOCK_ROWS):
            rows = slice(b * DEC_SEQ + blk * n_q, b * DEC_SEQ + (blk + 1) * n_q)
            out = _PairSink(y_ref, rows, slice(0, PAIR))
            for hh in range(2):
                heads.append(_Head(functools.partial(scores, b, blk, hh), functools.partial(finish, out, b, blk, hh)))
    _attend(heads, batch=8)


def _na_mixer(proj, cache_kt, cache_vt, rpb_flat):
    pairs = H_C // 2
    n_seq = NA_SEQS_PER_STEP
    col = lambda base: pl.BlockSpec((n_seq * DEC_SEQ, PAIR), lambda hp, b: (b, base + hp))
    cache = pl.BlockSpec((n_seq, 1, 2, DH, PAST_LEN), lambda hp, b: (b, 0, hp, 0, 0))
    return pl.pallas_call(
        _na_kernel,
        grid=(pairs, DEC_BATCH // n_seq),
        in_specs=[pl.BlockSpec(memory_space=pltpu.SMEM), col(0), col(pairs), col(2 * pairs), cache, cache],
        out_specs=pl.BlockSpec((n_seq * DEC_SEQ, PAIR), lambda hp, b: (b, hp)),
        out_shape=jax.ShapeDtypeStruct((LAT_TOKENS, 512), BF16),
        scratch_shapes=[pltpu.VMEM((2, GRID_H // NA_BLOCK_ROWS, NA_BLOCK_ROWS * GRID_W, NA_MAX_KEYS), F32)],
        compiler_params=_cparams(2),
        name="na_mixer",
    )(rpb_flat, proj, proj, proj, cache_kt, cache_vt)


def _swa_kernel(q_ref, kv_ref, kct_ref, vct_ref, sink_ref, y_ref):
    n_groups = DEC_SEQ // SWA_QROWS
    ql = lax.broadcasted_iota(jnp.int32, (SWA_QROWS, SWA_BAND), 0)
    kj = lax.broadcasted_iota(jnp.int32, (SWA_QROWS, SWA_BAND), 1)

    def band_bias(first_key_minus_first_query):
        return jnp.where(jnp.abs(kj + first_key_minus_first_query - ql) <= SWA_WINDOW, 0.0, NEG)

    bias_first, bias_mid, bias_last = band_bias(0), band_bias(-SWA_BLOCK), band_bias(-2 * SWA_BLOCK)

    def band_start(grp):
        return min(max(grp * SWA_QROWS - SWA_BLOCK, 0), DEC_SEQ - SWA_BAND)

    memo = _Memo()

    def band_k(g, grp, hh):
        start = band_start(grp)
        kb = kv_ref[start:start + SWA_BAND, g * PAIR:(g + 1) * PAIR]
        return jnp.where(_keep_mask_bf16(SWA_BAND, hh), kb, jnp.zeros_like(kb))

    def band_v(g, grp, hh):
        start = band_start(grp)
        vb = kv_ref[start:start + SWA_BAND, (KV_D + g) * PAIR:(KV_D + g + 1) * PAIR]
        return jnp.where(_keep_mask_bf16(SWA_BAND, hh), vb, jnp.ones_like(vb))

    def scores(g, grp, p, hh):
        bias = bias_first if grp == 0 else (bias_last if grp == n_groups - 1 else bias_mid)
        qp = q_ref[grp * SWA_QROWS:(grp + 1) * SWA_QROWS, p * PAIR:(p + 1) * PAIR]
        s1 = _dot_nt(qp, memo.get_or_build(("k", g, grp, hh), functools.partial(band_k, g, grp, hh))) + bias
        ckt = memo.get_or_build(("ck", g, hh), lambda: _pad_rows(kct_ref[0, 0, g].astype(BF16), hh, 0.0))
        return [s1, _dot(qp, ckt)]

    def finish(out, g, grp, hh, e_parts, term):
        cvt = memo.get_or_build(("cv", g, hh), lambda: _pad_rows(vct_ref[0, 0, g].astype(BF16), hh, 1.0))
        o = (_dot(e_parts[0], memo.get_or_build(("v", g, grp, hh), functools.partial(band_v, g, grp, hh)))
             + _dot_nt(e_parts[1], cvt))
        out.put(hh, o, term)

    heads = []
    for g in range(KV_D):
        for grp in range(n_groups):
            for j in range(GQA // 2):
                p = g * (GQA // 2) + j
                out = _PairSink(y_ref, slice(grp * SWA_QROWS, (grp + 1) * SWA_QROWS), slice(p * PAIR, (p + 1) * PAIR))
                for hh in range(2):
                    sink = sink_ref[2 * p + hh] * LOG2E
                    heads.append(_Head(functools.partial(scores, g, grp, p, hh),
                                       functools.partial(finish, out, g, grp, hh), sink))
    _attend(heads, batch=16)


def _swa_mixer(proj, cache_kt, cache_vt, sink):
    cache = pl.BlockSpec((1, 1, KV_D, DH, PAST_LEN), lambda b: (b, 0, 0, 0, 0))
    return pl.pallas_call(
        _swa_kernel,
        grid=(DEC_BATCH,),
        in_specs=[pl.BlockSpec((DEC_SEQ, 512), lambda b: (b, 3)),
                  pl.BlockSpec((DEC_SEQ, 512), lambda b: (b, 4)),
                  cache, cache,
                  pl.BlockSpec(memory_space=pltpu.SMEM)],
        out_specs=pl.BlockSpec((DEC_SEQ, 512), lambda b: (b, 0)),
        out_shape=jax.ShapeDtypeStruct((LAT_TOKENS, 512), BF16),
        compiler_params=_cparams(1),
        name="swa_mixer",
    )(proj, proj, cache_kt, cache_vt, sink)


def _load_weights_bf16(li, wo_hbm, w1_hbm, w2_hbm, wo_s, w1_s, w2_s, stage_sq, stage_wide, sems):
    square = ([(wo_hbm, wo_s, j, WO_ROWS) for j in range(N_WO)] + [(w2_hbm, w2_s, j, W2_ROWS) for j in range(N_W2)])
    wide = [(w1_hbm, w1_s, j, W1_ROWS) for j in range(N_W1)]
    queues = {"sq": (square, stage_sq, 0), "wide": (wide, stage_wide, STAGE_SLOTS)}

    def copy(kind, idx):
        chunks, stage, sem0 = queues[kind]
        src, _, j, n_rows = chunks[idx]
        slot = idx % STAGE_SLOTS
        return pltpu.make_async_copy(src.at[li, j * n_rows:(j + 1) * n_rows, :], stage.at[slot], sems.at[sem0 + slot])

    for kind in queues:
        for idx in range(min(STAGE_SLOTS, len(queues[kind][0]))):
            copy(kind, idx).start()
    order = []
    for i in range(max(len(square), len(wide))):
        order += [("sq", i)] * (i < len(square)) + [("wide", i)] * (i < len(wide))
    for kind, idx in order:
        chunks, stage, _ = queues[kind]
        _, dst, j, n_rows = chunks[idx]
        copy(kind, idx).wait()
        dst[j * n_rows:(j + 1) * n_rows, :] = stage[idx % STAGE_SLOTS].astype(BF16)
        if idx + STAGE_SLOTS < len(chunks):
            copy(kind, idx + STAGE_SLOTS).start()


def _post_mixer_kernel(*refs, li, n_lat_mix, n_x, n_out):
    mix_ctx_ref, mix_lat_refs = refs[0], refs[1:1 + n_lat_mix]
    refs = refs[1 + n_lat_mix:]
    x_refs = refs[:n_x]
    mod_ref, gpost_ref, gpre_ref, gmlp_ref, wo_hbm, w1_hbm, w2_hbm = refs[n_x:7 + n_x]
    out_refs = refs[7 + n_x:7 + n_x + n_out]
    wo_s, w1_s, w2_s, stage_sq, stage_wide, sems = refs[7 + n_x + n_out:]
    step = pl.program_id(0)

    @pl.when(step == 0)
    def _():
        _load_weights_bf16(li, wo_hbm, w1_hbm, w2_hbm, wo_s, w1_s, w2_s, stage_sq, stage_wide, sems)

    is_ctx = step < CTX_TILES
    mod = mod_ref[0, pl.ds(jnp.where(is_ctx, CTX_ROW, (step - CTX_TILES) // (DEC_SEQ // TM)), 1), :]
    g_post, g_pre, g_mlp = gpost_ref[li:li + 1, :], gpre_ref[li:li + 1, :], gmlp_ref[li:li + 1, :]
    rows = [slice(i * (TM // POST_SPLIT), (i + 1) * (TM // POST_SPLIT)) for i in range(POST_SPLIT)]

    def residual(r):
        if n_x == 1:
            return x_refs[0][r, :]
        return jnp.where(is_ctx, x_refs[0][r, :], x_refs[1][r, :])

    def mixed(r):
        lat = [m[r, :] for m in mix_lat_refs]
        lat = lat[0] if n_lat_mix == 1 else jnp.concatenate(lat, axis=1)
        return jnp.where(is_ctx, mix_ctx_ref[r, :], lat)

    ys = [_dot(mixed(r), wo_s[...]) for r in rows]
    x1 = [residual(r) + mod[:, 2 * D_MODEL:3 * D_MODEL] * _rms(y, g_post) for r, y in zip(rows, ys)]
    h = [_norm_mod(x, g_pre, mod, 3 * D_MODEL).astype(BF16) for x in x1]
    acc = [None] * POST_SPLIT
    pending = []

    def second_matmul(c, i, f):
        t2 = _dot(f, w2_s[c * FF_CHUNK:(c + 1) * FF_CHUNK, :])
        acc[i] = t2 if acc[i] is None else acc[i] + t2

    for c in range(D_FF // FF_CHUNK):
        for i in range(POST_SPLIT):
            f = _dot(h[i], w1_s[:, c * FF_CHUNK:(c + 1) * FF_CHUNK])
            pending.append((c, i, jnp.square(jnp.maximum(f, 0.0)).astype(BF16)))
            if len(pending) > 1:
                second_matmul(*pending.pop(0))
    second_matmul(*pending.pop(0))
    res = [x1[i] + mod[:, 5 * D_MODEL:6 * D_MODEL] * _rms(acc[i], g_mlp) for i in range(POST_SPLIT)]
    if n_out == 1:
        for i, r in enumerate(rows):
            out_refs[0][r, :] = res[i]
    else:
        @pl.when(is_ctx)
        def _():
            for i, r in enumerate(rows):
                out_refs[0][r, :] = res[i]

        @pl.when(jnp.logical_not(is_ctx))
        def _():
            for i, r in enumerate(rows):
                out_refs[1][r, :] = res[i]


def _post_mixer(mix_ctx, mix_lat, xs, mod, g_post, g_pre, g_mlp, w_out, w1, w2, li, *, split_out):
    tile = lambda s: s
    ctx_tile = lambda s: jnp.minimum(tile(s), CTX_TILES - 1)
    lat_tile = lambda s: jnp.maximum(tile(s) - CTX_TILES, 0)
    token_spec = lambda index, width=D_MODEL: pl.BlockSpec((TM, width), lambda s: (index(s), 0))
    mix_specs = [token_spec(ctx_tile)] + [token_spec(lat_tile, m.shape[1]) for m in mix_lat]
    x_specs = [token_spec(tile)] if len(xs) == 1 else [token_spec(ctx_tile), token_spec(lat_tile)]
    if split_out:
        out_specs = [token_spec(ctx_tile), token_spec(lat_tile)]
        out_shape = [jax.ShapeDtypeStruct((CTX_TOKENS, D_MODEL), F32), jax.ShapeDtypeStruct((LAT_TOKENS, D_MODEL), F32)]
    else:
        out_specs = [token_spec(tile)]
        out_shape = [jax.ShapeDtypeStruct((ALL_TOKENS, D_MODEL), F32)]
    gain = pl.BlockSpec((DEPTH, D_MODEL), lambda s: (0, 0))
    return pl.pallas_call(
        functools.partial(_post_mixer_kernel, li=li, n_lat_mix=len(mix_lat), n_x=len(xs), n_out=len(out_specs)),
        grid=(ALL_TOKENS // TM,),
        in_specs=mix_specs + x_specs + [
            pl.BlockSpec((1, MOD_ROWS, 6 * D_MODEL), lambda s: (li, 0, 0)),
            gain, gain, gain] + [pl.BlockSpec(memory_space=pl.ANY)] * 3,
        out_specs=out_specs,
        out_shape=out_shape,
        scratch_shapes=[pltpu.VMEM((D_MODEL, D_MODEL), BF16), pltpu.VMEM((D_MODEL, D_FF), BF16),
                        pltpu.VMEM((D_FF, D_MODEL), BF16),
                        pltpu.VMEM((STAGE_SLOTS, WO_ROWS, D_MODEL), F32), pltpu.VMEM((STAGE_SLOTS, W1_ROWS, D_FF), F32),
                        pltpu.SemaphoreType.DMA((2 * STAGE_SLOTS,))],
        compiler_params=_cparams(1),
        name="post_mixer",
    )(mix_ctx, *mix_lat, *xs, mod, g_post, g_pre, g_mlp, w_out, w1, w2)


def _rope_tables():
    t = jnp.arange(DEC_SEQ)
    rows = (t // GRID_W).astype(F32)
    cols = (t % GRID_W).astype(F32)
    q4 = DH // 4
    inv = 1.0 / (ROPE_BASE ** (jnp.arange(q4, dtype=F32) / q4))
    ar, ac = rows[:, None] * inv, cols[:, None] * inv
    cos64 = jnp.concatenate([jnp.cos(ar), jnp.cos(ar), jnp.cos(ac), jnp.cos(ac)], axis=1)
    sin64 = jnp.concatenate([-jnp.sin(ar), jnp.sin(ar), -jnp.sin(ac), jnp.sin(ac)], axis=1)
    return jnp.tile(cos64, (1, 2)), jnp.tile(sin64, (1, 2))


def kernel(x_prompt, x_sample, cache_diff_k, cache_diff_v, cache_na_k, cache_na_v, cache_swa_k, cache_swa_v, c, c_ctx, mod_w, mod_b, norm_mix_pre, norm_mix_post, norm_mlp_pre, norm_mlp_post, w_in_even, conv_w, lambda_q1, lambda_k1, lambda_q2, lambda_k2, subln, w_in_odd, rpb, sink, w_out, mlp_w1, mlp_w2):
    cond = jnp.concatenate([c, c_ctx[None, :], jnp.zeros((MOD_ROWS - DEC_BATCH - 1, D_MODEL), F32)], axis=0)
    mod, *w_in = _modulation(cond, mod_w, mod_b, w_in_even, w_in_odd)
    rope = _rope_tables()

    t_minor = lambda a: jnp.swapaxes(a, -1, -2)

    xp = x_prompt.reshape(CTX_TOKENS, D_MODEL)
    xs = x_sample.reshape(LAT_TOKENS, D_MODEL)
    lam_init0 = 0.8 - 0.6 * math.exp(-0.3 * 0)
    lam_args = (conv_w, lambda_q1, lambda_k1, lambda_q2, lambda_k2, subln)
    sink_flat = sink.reshape(-1)

    def post(li, mix_ctx, mix_lat, x_parts, split_out):
        return _post_mixer(mix_ctx, mix_lat, x_parts, mod, norm_mix_post, norm_mlp_pre, norm_mlp_post, w_out, mlp_w1,
                           mlp_w2, li, split_out=split_out)

    proj, new_diff_kt, new_diff_v = _pre_mixer(xp, mod, norm_mix_pre, w_in[0], 0, latent=False, even=True, rope=None,
                                               n_tok=CTX_TOKENS)
    mix_ctx = _even_mixer(proj, *lam_args, new_diff_kt, new_diff_v, seq=SEQ, n_seq=CTX_SEQS_PER_STEP_EVEN,
                          own_keys=False, lam_init=lam_init0)
    proj = _pre_mixer(xs, mod, norm_mix_pre, w_in[0], 0, latent=True, even=True, rope=rope, n_tok=LAT_TOKENS)
    mix_lat = _even_mixer(proj, *lam_args, t_minor(cache_diff_k), cache_diff_v, seq=DEC_SEQ, n_seq=1, own_keys=True,
                          lam_init=lam_init0)
    (x_all,) = post(0, mix_ctx, [mix_lat], [xp, xs], False)

    q, new_na_kt, new_na_vt, new_swa_kt, new_swa_vt = _pre_mixer(
        x_all, mod, norm_mix_pre, w_in[1], 1, latent=False, even=False, rope=None, n_tok=CTX_TOKENS)
    mix_ctx = _odd_ctx_mixer(q, new_na_kt, new_na_vt, new_swa_kt, new_swa_vt, sink_flat)
    proj = _pre_mixer(x_all, mod, norm_mix_pre, w_in[1], 1, latent=True, even=False, rope=rope, n_tok=LAT_TOKENS,
                      tile0=CTX_TOKENS // TM_PRE)
    mix_c = _na_mixer(proj, t_minor(cache_na_k), t_minor(cache_na_v), rpb.reshape(-1))
    mix_d = _swa_mixer(proj, t_minor(cache_swa_k), t_minor(cache_swa_v), sink_flat)
    xp, xs = post(1, mix_ctx, [mix_c, mix_d], [x_all], True)

    return (xp.reshape(BATCH, SEQ, D_MODEL), xs.reshape(DEC_BATCH, DEC_SEQ, D_MODEL),
            t_minor(new_diff_kt), new_diff_v, t_minor(new_na_kt), t_minor(new_na_vt),
            t_minor(new_swa_kt), t_minor(new_swa_vt))
```

```python
import functools
import math

import jax
import jax.numpy as jnp
from jax import lax
from jax.experimental import pallas as pl
from jax.experimental.pallas import tpu as pltpu

F32 = jnp.float32
BF16 = jnp.bfloat16

D_MODEL = 1024
BATCH = 32
SEQ = 256
DEPTH = 2
DEC_BATCH = 8
DEC_SEQ = 1024
PAST_LEN = 256
GRID_W = 64
GRID_H = DEC_SEQ // GRID_W
MIX_A = 512
H_B = 4
DH = 64
PAIR = 2 * DH
H_C = 8
H_D = 8
KV_D = 2
GQA = H_D // KV_D
IN_EVEN = 3072
IN_ODD = 2304
D_FF = 4096
NA_WIN_R = 8
NA_WIN_C = 16
SWA_BLOCK = 128
SWA_WINDOW = 128
ROPE_BASE = 10000.0
EPS = 1e-6
NEG = -1e30
LOG2E = math.log2(math.e)
Q_SCALE = DH ** -0.5 * LOG2E

MOD_ROWS = 16
CTX_ROW = DEC_BATCH
V7X_VMEM_BYTES = 64 * 1024 * 1024
VMEM_LIMIT = V7X_VMEM_BYTES - 12 * 1024 * 1024

CTX_TOKENS = BATCH * SEQ
LAT_TOKENS = DEC_BATCH * DEC_SEQ
ALL_TOKENS = CTX_TOKENS + LAT_TOKENS

TM = 512
TM_PRE = 1024
FF_CHUNK = 1024
POST_SPLIT = 2
WCHUNK_ELEMS = 128 * 1024
WO_ROWS, W1_ROWS, W2_ROWS = WCHUNK_ELEMS // D_MODEL, WCHUNK_ELEMS // D_FF, WCHUNK_ELEMS // D_MODEL
N_WO, N_W1, N_W2 = D_MODEL // WO_ROWS, D_MODEL // W1_ROWS, D_FF // W2_ROWS
STAGE_SLOTS = 4
CTX_TILES = CTX_TOKENS // TM
CTX_SEQS_PER_STEP = 4
CTX_SEQS_PER_STEP_EVEN = 2
MOD_TN = 3072
EVEN_BATCH_ROWS = 2048
NA_SEQS_PER_STEP = 2
NA_BLOCK_ROWS = 4
NA_MAX_KEYS = 768
SWA_QROWS = SWA_BLOCK
SWA_BAND = 3 * SWA_BLOCK
LAT_ODD_COLS = IN_ODD + 2 * PAIR


def _cparams(n_axes):
    return pltpu.CompilerParams(dimension_semantics=("arbitrary",) * n_axes, vmem_limit_bytes=VMEM_LIMIT)


def _rms(x, g):
    return x * lax.rsqrt(jnp.mean(x * x, axis=-1, keepdims=True) + EPS) * g


def _dot(a, b):
    return jnp.dot(a, b, preferred_element_type=F32)


def _dot_nt(a, b):
    return lax.dot_general(a, b, (((1,), (1,)), ((), ())), preferred_element_type=F32)


def _layer_block(shape, li):
    zeros = (0,) * len(shape)
    return pl.BlockSpec((1,) + tuple(shape), lambda *_: (li,) + zeros, pipeline_mode=pl.Buffered(1))


def _low_half(shape):
    return (lax.broadcasted_iota(jnp.int32, shape, len(shape) - 1) & DH) == 0


def _keep_mask_bf16(rows, hh):
    lane = lax.broadcasted_iota(jnp.int32, (rows, PAIR), 1)
    half = (lane & DH).astype(F32).astype(BF16)
    return (half == 0) if hh == 0 else (half != 0)


def _pad_rows(t, hh, fill):
    other = jnp.full(t.shape, fill, t.dtype)
    return jnp.concatenate([t, other] if hh == 0 else [other, t], axis=0)


class _Memo(dict):
    def get_or_build(self, key, build):
        if key not in self:
            self[key] = build()
        return self[key]


def _merge_pair(o0, o1, extra_den=None):
    low = _low_half(o0.shape)
    num = jnp.where(low, o0, o1)
    den = pltpu.roll(jnp.where(low, o1, o0), DH, axis=1)
    if extra_den is not None:
        den = den + jnp.where(low, extra_den[0], extra_den[1])
    return num * (1.0 / den)


class _Head:
    def __init__(self, scores, finish, sink=None):
        self.scores, self.finish, self.sink = scores, finish, sink


def _attend(heads, batch):
    batches = [heads[i:i + batch] for i in range(0, len(heads), batch)]
    cur = [hd.scores() for hd in batches[0]]
    for bi, group in enumerate(batches):
        nxt = [hd.scores() for hd in batches[bi + 1]] if bi + 1 < len(batches) else None
        exps = []
        for hd, parts in zip(group, cur):
            chunks = [s[:, j * 128:(j + 1) * 128] for s in parts for j in range(s.shape[1] // 128)]
            m_el = functools.reduce(jnp.maximum, chunks)
            if hd.sink is not None:
                m_el = jnp.maximum(m_el, hd.sink)
            mx = jnp.broadcast_to(jnp.max(m_el, axis=-1, keepdims=True), m_el.shape)
            term = None if hd.sink is None else jnp.exp2(hd.sink - mx)
            e_parts = [jnp.concatenate([jnp.exp2(s[:, j * 128:(j + 1) * 128] - mx)
                                        for j in range(s.shape[1] // 128)], axis=1).astype(BF16) for s in parts]
            exps.append((e_parts, term))
        for hd, (e_parts, term) in zip(group, exps):
            hd.finish(e_parts, term)
        cur = nxt


class _PairSink:
    def __init__(self, y_ref, rows, cols):
        self.y_ref, self.rows, self.cols, self.first = y_ref, rows, cols, None

    def put(self, hh, o, term):
        if hh == 0:
            self.first = (o, term)
        else:
            o0, term0 = self.first
            extra = None if term is None else (term0, term)
            self.y_ref[self.rows, self.cols] = _merge_pair(o0, o, extra).astype(BF16)


def _mod_kernel(cond_ref, w_ref, b_ref, win_e_ref, win_o_ref, o_ref, win_e_bf, win_o_bf):
    c = cond_ref[...]
    s = c * (1.0 / (1.0 + jnp.exp(-c)))
    o_ref[0] = _dot(s.astype(BF16), w_ref[0].astype(BF16)) + b_ref[pl.ds(pl.program_id(0), 1), :]
    win_e_bf[...] = win_e_ref[...].astype(BF16)
    win_o_bf[...] = win_o_ref[...].astype(BF16)


def _modulation(cond, mod_w, mod_b, w_in_even, w_in_odd):
    n_col = 6 * D_MODEL // MOD_TN
    slab = D_MODEL // (DEPTH * n_col)
    w_slab = lambda width: pl.BlockSpec((1, slab, width), lambda l, j: (0, l * n_col + j, 0))
    return pl.pallas_call(
        _mod_kernel,
        grid=(DEPTH, n_col),
        in_specs=[
            pl.BlockSpec((MOD_ROWS, D_MODEL), lambda l, j: (0, 0)),
            pl.BlockSpec((1, D_MODEL, MOD_TN), lambda l, j: (l, 0, j)),
            pl.BlockSpec((DEPTH, MOD_TN), lambda l, j: (0, j)),
            w_slab(IN_EVEN), w_slab(IN_ODD),
        ],
        out_specs=[pl.BlockSpec((1, MOD_ROWS, MOD_TN), lambda l, j: (l, 0, j)), w_slab(IN_EVEN), w_slab(IN_ODD)],
        out_shape=[jax.ShapeDtypeStruct((DEPTH, MOD_ROWS, 6 * D_MODEL), F32),
                   jax.ShapeDtypeStruct((1, D_MODEL, IN_EVEN), BF16), jax.ShapeDtypeStruct((1, D_MODEL, IN_ODD), BF16)],
        compiler_params=_cparams(2),
        name="adaln_modulation",
    )(cond, mod_w, mod_b, w_in_even, w_in_odd)


def _norm_mod(x, g, mod, shift_col):
    sh = mod[:, shift_col:shift_col + D_MODEL]
    sc = mod[:, shift_col + D_MODEL:shift_col + 2 * D_MODEL]
    return _rms(x, g) * (1.0 + sc) + sh


def _rope(z, cos, sin):
    outs = []
    for j in range(z.shape[1] // 128):
        zj = z[:, j * 128:(j + 1) * 128]
        lane = lax.broadcasted_iota(jnp.int32, zj.shape, 1)
        partner = jnp.where((lane & 16) == 0, pltpu.roll(zj, 128 - 16, axis=1), pltpu.roll(zj, 16, axis=1))
        outs.append(zj * cos + partner * sin)
    return outs[0] if len(outs) == 1 else jnp.concatenate(outs, axis=1)


def _store_heads_transposed(dst, bi, pc_rows, n_pairs):
    for p in range(n_pairs):
        t = pc_rows[:, p * PAIR:(p + 1) * PAIR].T
        dst[bi, 0, 2 * p] = t[0:DH]
        dst[bi, 0, 2 * p + 1] = t[DH:PAIR]


def _mod_row(mod_ref, latent):
    if latent:
        return mod_ref[0, pl.ds(pl.program_id(0) // (DEC_SEQ // TM_PRE), 1), :]
    return mod_ref[0, CTX_ROW:CTX_ROW + 1, :]


def _normed_groups(x_ref, mod_ref, g_ref, li, latent):
    rows = [slice(i * SEQ, (i + 1) * SEQ) for i in range(TM_PRE // SEQ)]
    mod, g = _mod_row(mod_ref, latent), g_ref[li:li + 1, :]
    return rows, [_norm_mod(x_ref[r, :], g, mod, 0).astype(BF16) for r in rows]


def _k1_ctx_even(x_ref, mod_ref, g_ref, w_ref, proj_ref, dkt_ref, dv_ref, *, li):
    rows, hs = _normed_groups(x_ref, mod_ref, g_ref, li, False)
    for c in range(IN_EVEN // 512):
        for bi, (r, h) in enumerate(zip(rows, hs)):
            pc = _dot(h, w_ref[0, :, c * 512:(c + 1) * 512])
            if c < 3:
                proj_ref[r, c * 512:(c + 1) * 512] = pc.astype(BF16)
            elif c == 3:
                proj_ref[r, c * 512:(c + 1) * 512] = (pc * Q_SCALE).astype(BF16)
            elif c == 4:
                for hh in range(H_B):
                    t = pc[:, hh * PAIR:(hh + 1) * PAIR].T
                    dkt_ref[bi, 0, hh, 0] = t[0:DH]
                    dkt_ref[bi, 0, hh, 1] = t[DH:PAIR]
            else:
                for hh in range(H_B):
                    dv_ref[bi, 0, hh] = pc[:, hh * PAIR:(hh + 1) * PAIR]


def _k1_lat_even(x_ref, mod_ref, g_ref, w_ref, cos_ref, sin_ref, proj_ref, *, li):
    rows, hs = _normed_groups(x_ref, mod_ref, g_ref, li, True)
    for c in range(IN_EVEN // 512):
        for r, h in zip(rows, hs):
            pc = _dot(h, w_ref[0, :, c * 512:(c + 1) * 512])
            if c == 3:
                pc = _rope(pc * Q_SCALE, cos_ref[r, :], sin_ref[r, :])
            elif c == 4:
                pc = _rope(pc, cos_ref[r, :], sin_ref[r, :])
            proj_ref[r, c * 512:(c + 1) * 512] = pc.astype(BF16)


def _k1_ctx_odd(x_ref, mod_ref, g_ref, w_ref, q_ref, nkt_ref, nvt_ref, skt_ref, svt_ref, *, li):
    rows, hs = _normed_groups(x_ref, mod_ref, g_ref, li, False)
    for c in range(4):
        for bi, (r, h) in enumerate(zip(rows, hs)):
            pc = _dot(h, w_ref[0, :, c * 512:(c + 1) * 512])
            if c == 0 or c == 3:
                q_ref[r, (c // 3) * 512:(c // 3 + 1) * 512] = (pc * Q_SCALE).astype(BF16)
            else:
                _store_heads_transposed(nkt_ref if c == 1 else nvt_ref, bi, pc, H_C // 2)
    for bi, h in enumerate(hs):
        pc = _dot(h, w_ref[0, :, 2048:IN_ODD])
        _store_heads_transposed(skt_ref, bi, pc[:, 0:PAIR], 1)
        _store_heads_transposed(svt_ref, bi, pc[:, PAIR:2 * PAIR], 1)


def _k1_lat_odd(x_ref, mod_ref, g_ref, w_ref, cos_ref, sin_ref, proj_ref, *, li):
    rows, hs = _normed_groups(x_ref, mod_ref, g_ref, li, True)
    for c in range(4):
        for r, h in zip(rows, hs):
            pc = _dot(h, w_ref[0, :, c * 512:(c + 1) * 512])
            if c == 0:
                pc = pc * Q_SCALE
            elif c == 3:
                pc = _rope(pc * Q_SCALE, cos_ref[r, :], sin_ref[r, :])
            proj_ref[r, c * 512:(c + 1) * 512] = pc.astype(BF16)
    low = _low_half((SEQ, PAIR))
    for r, h in zip(rows, hs):
        pc = _dot(h, w_ref[0, :, 2048:IN_ODD])
        for j, z in enumerate((_rope(pc[:, 0:PAIR], cos_ref[r, :], sin_ref[r, :]), pc[:, PAIR:2 * PAIR])):
            zr = pltpu.roll(z, DH, axis=1)
            base = 2048 + j * 2 * PAIR
            proj_ref[r, base:base + PAIR] = jnp.where(low, z, zr).astype(BF16)
            proj_ref[r, base + PAIR:base + 2 * PAIR] = jnp.where(low, zr, z).astype(BF16)


def _pre_mixer(x, mod, gains, w, li, *, latent, even, rope, n_tok, tile0=0):
    n_in = w.shape[2]
    tm = TM_PRE
    tiles_per_seq = DEC_SEQ // tm
    in_specs = [
        pl.BlockSpec((tm, D_MODEL), lambda i: (i + tile0, 0)),
        pl.BlockSpec((1, MOD_ROWS, 6 * D_MODEL), lambda i: (li, 0, 0)),
        pl.BlockSpec((DEPTH, D_MODEL), lambda i: (0, 0)),
        _layer_block((D_MODEL, n_in), 0),
    ]
    args = [x, mod, gains, w]
    nb = tm // SEQ
    if latent:
        in_specs += [pl.BlockSpec((tm, 128), lambda i: (i % tiles_per_seq, 0))] * 2
        args += list(rope)
        body = _k1_lat_even if even else _k1_lat_odd
        n_out = IN_EVEN if even else LAT_ODD_COLS
        out_specs = pl.BlockSpec((tm, n_out), lambda i: (i, 0))
        out_shape = jax.ShapeDtypeStruct((n_tok, n_out), BF16)
    elif even:
        body = _k1_ctx_even
        out_specs = [pl.BlockSpec((tm, 4 * MIX_A), lambda i: (i, 0)),
                     pl.BlockSpec((nb, 1, H_B, 2, DH, SEQ), lambda i: (i, 0, 0, 0, 0, 0)),
                     pl.BlockSpec((nb, 1, H_B, SEQ, PAIR), lambda i: (i, 0, 0, 0, 0))]
        out_shape = [jax.ShapeDtypeStruct((n_tok, 4 * MIX_A), BF16),
                     jax.ShapeDtypeStruct((BATCH, 1, H_B, 2, DH, SEQ), F32),
                     jax.ShapeDtypeStruct((BATCH, 1, H_B, SEQ, PAIR), F32)]
    else:
        body = _k1_ctx_odd
        c_spec = pl.BlockSpec((nb, 1, H_C, DH, SEQ), lambda i: (i, 0, 0, 0, 0))
        d_spec = pl.BlockSpec((nb, 1, KV_D, DH, SEQ), lambda i: (i, 0, 0, 0, 0))
        c_shape = jax.ShapeDtypeStruct((BATCH, 1, H_C, DH, SEQ), F32)
        d_shape = jax.ShapeDtypeStruct((BATCH, 1, KV_D, DH, SEQ), F32)
        out_specs = [pl.BlockSpec((tm, D_MODEL), lambda i: (i, 0)), c_spec, c_spec, d_spec, d_spec]
        out_shape = [jax.ShapeDtypeStruct((n_tok, D_MODEL), BF16), c_shape, c_shape, d_shape, d_shape]
    return pl.pallas_call(
        functools.partial(body, li=li),
        grid=(n_tok // tm,),
        in_specs=in_specs,
        out_specs=out_specs,
        out_shape=out_shape,
        compiler_params=_cparams(1),
        name=f"pre_mixer_{'lat' if latent else 'ctx'}_{'even' if even else 'odd'}",
    )(*args)


def _even_mixer_kernel(*refs, seq, n_seq, tq, own_keys, lam_init):
    proj_ref, cw_ref, lq1_ref, lk1_ref, lq2_ref, lk2_ref, subln_ref, ckt_ref, cv_ref = refs[:9]
    y_ref = refs[-1]
    n_rows = n_seq * seq

    pos = lax.broadcasted_iota(jnp.int32, (n_rows, 128), 0) % seq
    for j in range(MIX_A // 128):
        cols = slice(j * 128, (j + 1) * 128)
        a_b = proj_ref[:, j * 128:(j + 1) * 128].astype(F32)
        u = (proj_ref[:, MIX_A + j * 128:MIX_A + (j + 1) * 128].astype(F32)
             * proj_ref[:, 2 * MIX_A + j * 128:2 * MIX_A + (j + 1) * 128].astype(F32))
        u_prev = jnp.where(pos == 0, 0.0, pltpu.roll(u, 1, axis=0))
        u_next = jnp.where(pos == seq - 1, 0.0, pltpu.roll(u, n_rows - 1, axis=0))
        w = cw_ref[0, :, cols]
        y_ref[:, cols] = (a_b * (w[0:1] * u_prev + w[1:2] * u + w[2:3] * u_next)).astype(BF16)

    lam = (jnp.exp(jnp.sum(lq1_ref[...] * lk1_ref[...], axis=-1, keepdims=True))
           - jnp.exp(jnp.sum(lq2_ref[...] * lk2_ref[...], axis=-1, keepdims=True)) + lam_init)
    subln = subln_ref[...]
    q_col, k_col, v_col = 3 * MIX_A, 3 * MIX_A + 512, 3 * MIX_A + 1024
    ones = jnp.ones((PAST_LEN, PAIR), BF16)

    memo = _Memo()

    def own_k(b, h, m):
        kp = proj_ref[b * seq:(b + 1) * seq, k_col + h * PAIR:k_col + (h + 1) * PAIR]
        return jnp.where(_keep_mask_bf16(seq, m), kp, jnp.zeros_like(kp))

    def value_ops(b, h):
        v_ops = []
        if own_keys:
            v_own = proj_ref[b * seq:(b + 1) * seq, v_col + h * PAIR:v_col + (h + 1) * PAIR]
            v_ops.append(jnp.concatenate([v_own, jnp.ones((seq, PAIR), BF16)], axis=1))
        v_ops.append(jnp.concatenate([cv_ref[b, 0, h].astype(BF16), ones], axis=1))
        return v_ops

    def scores(b, qrows, h, m):
        qp = proj_ref[qrows, q_col + h * PAIR:q_col + (h + 1) * PAIR]
        parts = []
        if own_keys:
            parts.append(_dot_nt(qp, memo.get_or_build(("k", b, h, m), functools.partial(own_k, b, h, m))))
        ckt = memo.get_or_build(("ck", b, h, m), lambda: _pad_rows(ckt_ref[b, 0, h, m].astype(BF16), m, 0.0))
        parts.append(_dot(qp, ckt))
        return parts

    first_map = {}

    def pv(b, qrows, h, m, e_parts, _):
        v_ops = memo.get_or_build(("v", b, h), functools.partial(value_ops, b, h))
        o = None
        for e, v_op in zip(e_parts, v_ops):
            t = _dot(e, v_op)
            o = t if o is None else o + t
        attn = o[:, 0:PAIR] * (1.0 / o[:, PAIR:2 * PAIR])
        if m == 0:
            first_map[(qrows.start, h)] = attn
        else:
            y = _rms(first_map.pop((qrows.start, h)) - lam * attn, subln) * (1.0 - lam_init)
            y_ref[qrows, MIX_A + h * PAIR:MIX_A + (h + 1) * PAIR] = y.astype(BF16)

    heads = []
    for b in range(n_seq):
        for i in range(seq // tq):
            qrows = slice(b * seq + i * tq, b * seq + (i + 1) * tq)
            for h in range(H_B):
                for m in range(2):
                    heads.append(_Head(functools.partial(scores, b, qrows, h, m),
                                       functools.partial(pv, b, qrows, h, m)))
    _attend(heads, batch=EVEN_BATCH_ROWS // tq)


def _even_mixer(proj, conv_w, lq1, lk1, lq2, lk2, subln, kt, v, *, seq, n_seq, own_keys, lam_init):
    n_tok = proj.shape[0]
    rows = n_seq * seq
    small = lambda a: pl.BlockSpec((1, a.shape[1]), lambda b: (0, 0))
    in_specs = [pl.BlockSpec((rows, proj.shape[1]), lambda b: (b, 0)),
                pl.BlockSpec((1, 3, MIX_A), lambda b: (0, 0, 0)),
                small(lq1), small(lk1), small(lq2), small(lk2), small(subln),
                pl.BlockSpec((n_seq, 1, H_B, 2, DH, PAST_LEN), lambda b: (b, 0, 0, 0, 0, 0)),
                pl.BlockSpec((n_seq, 1, H_B, PAST_LEN, PAIR), lambda b: (b, 0, 0, 0, 0))]
    return pl.pallas_call(
        functools.partial(_even_mixer_kernel, seq=seq, n_seq=n_seq, tq=min(seq, 256), own_keys=own_keys,
                          lam_init=lam_init),
        grid=(n_tok // rows,),
        in_specs=in_specs,
        out_specs=pl.BlockSpec((rows, D_MODEL), lambda b: (b, 0)),
        out_shape=jax.ShapeDtypeStruct((n_tok, D_MODEL), BF16),
        compiler_params=_cparams(1),
        name=f"even_mixer_{'lat' if own_keys else 'ctx'}",
    )(proj, conv_w, lq1, lk1, lq2, lk2, subln, kt, v)


def _odd_ctx_kernel(q_ref, nkt_ref, nvt_ref, skt_ref, svt_ref, sink_ref, y_ref):
    def kv_refs(b, head):
        if head < H_C:
            return nkt_ref.at[b, 0, head], nvt_ref.at[b, 0, head]
        g = (head - H_C) // GQA
        return skt_ref.at[b, 0, g], svt_ref.at[b, 0, g]

    memo = _Memo()

    def kv_key(head):
        return head if head < H_C else H_C + (head - H_C) // GQA

    def scores(b, p, hh):
        head = 2 * p + hh
        kt = memo.get_or_build(("k", b, kv_key(head), hh),
                               lambda: _pad_rows(kv_refs(b, head)[0][...].astype(BF16), hh, 0.0))
        return [_dot(q_ref[b * SEQ:(b + 1) * SEQ, p * PAIR:(p + 1) * PAIR], kt)]

    def finish(out, b, p, hh, e_parts, term):
        head = 2 * p + hh
        vt = memo.get_or_build(("v", b, kv_key(head), hh),
                               lambda: _pad_rows(kv_refs(b, head)[1][...].astype(BF16), hh, 1.0))
        out.put(hh, _dot_nt(e_parts[0], vt), term)

    heads = []
    for b in range(CTX_SEQS_PER_STEP):
        for p in range((H_C + H_D) // 2):
            out = _PairSink(y_ref, slice(b * SEQ, (b + 1) * SEQ), slice(p * PAIR, (p + 1) * PAIR))
            for hh in range(2):
                d_head = 2 * p + hh - H_C
                sink = sink_ref[d_head] * LOG2E if d_head >= 0 else None
                heads.append(_Head(functools.partial(scores, b, p, hh), functools.partial(finish, out, b, p, hh), sink))
    _attend(heads, batch=16)


def _odd_ctx_mixer(q, nkt, nvt, skt, svt, sink):
    n_tok = q.shape[0]
    n_seq = CTX_SEQS_PER_STEP
    c_spec = pl.BlockSpec((n_seq, 1, H_C, DH, SEQ), lambda b: (b, 0, 0, 0, 0))
    d_spec = pl.BlockSpec((n_seq, 1, KV_D, DH, SEQ), lambda b: (b, 0, 0, 0, 0))
    return pl.pallas_call(
        _odd_ctx_kernel,
        grid=(n_tok // (n_seq * SEQ),),
        in_specs=[pl.BlockSpec((n_seq * SEQ, D_MODEL), lambda b: (b, 0)), c_spec, c_spec, d_spec, d_spec,
                  pl.BlockSpec(memory_space=pltpu.SMEM)],
        out_specs=pl.BlockSpec((n_seq * SEQ, D_MODEL), lambda b: (b, 0)),
        out_shape=jax.ShapeDtypeStruct((n_tok, D_MODEL), BF16),
        compiler_params=_cparams(1),
        name="odd_mixer_ctx",
    )(q, nkt, nvt, skt, svt, sink)


def _na_block(blk):
    r0 = blk * NA_BLOCK_ROWS
    first = min(max(r0 - NA_WIN_R // 2, 0), GRID_H - NA_WIN_R)
    last = min(max(r0 + NA_BLOCK_ROWS - 1 - NA_WIN_R // 2, 0), GRID_H - NA_WIN_R) + NA_WIN_R
    n_rows = -(-(last - first) // 4) * 4
    return min(first, GRID_H - n_rows), n_rows


def _build_na_bias(rpb_ref, head, nb_ref, slot):
    qc = lax.broadcasted_iota(jnp.int32, (GRID_W, GRID_W), 0)
    kc = lax.broadcasted_iota(jnp.int32, (GRID_W, GRID_W), 1)
    col_start = jnp.clip(qc - NA_WIN_C // 2, 0, GRID_W - NA_WIN_C)
    col_ok = (kc >= col_start) & (kc < col_start + NA_WIN_C)
    dc = kc - qc + NA_WIN_C - 1
    n_dr, n_dc = 2 * NA_WIN_R - 1, 2 * NA_WIN_C - 1
    neg = jnp.full((GRID_W, GRID_W), NEG, F32)
    toeplitz = []
    for dr in range(n_dr):
        t = neg
        for d in range(n_dc):
            t = jnp.where(dc == d, rpb_ref[(head * n_dr + dr) * n_dc + d] * LOG2E, t)
        toeplitz.append(jnp.where(col_ok, t, NEG))
    for blk in range(GRID_H // NA_BLOCK_ROWS):
        first, n_rows = _na_block(blk)
        for rl in range(NA_BLOCK_ROWS):
            r = blk * NA_BLOCK_ROWS + rl
            r_start = min(max(r - NA_WIN_R // 2, 0), GRID_H - NA_WIN_R)
            blocks = []
            for rk in range(first, first + n_rows):
                inside = r_start <= rk < r_start + NA_WIN_R
                blocks.append(toeplitz[rk - r + NA_WIN_R - 1] if inside else neg)
            nb_ref[slot, blk, rl * GRID_W:(rl + 1) * GRID_W, 0:n_rows * GRID_W] = jnp.concatenate(blocks, axis=1)


def _na_kernel(rpb_ref, q_ref, k_ref, v_ref, kct_ref, vct_ref, y_ref, nb_ref):
    @pl.when(pl.program_id(1) == 0)
    def _():
        for hh in range(2):
            _build_na_bias(rpb_ref, 2 * pl.program_id(0) + hh, nb_ref, hh)

    n_q = NA_BLOCK_ROWS * GRID_W

    def key_rows(b, blk):
        first, n_rows = _na_block(blk)
        return slice(b * DEC_SEQ + first * GRID_W, b * DEC_SEQ + (first + n_rows) * GRID_W), n_rows * GRID_W

    def scores(b, blk, hh):
        keys, n_keys = key_rows(b, blk)
        qp = q_ref[b * DEC_SEQ + blk * n_q:b * DEC_SEQ + (blk + 1) * n_q, :]
        kw = k_ref[keys, :]
        s1 = (_dot_nt(qp, jnp.where(_keep_mask_bf16(n_keys, hh), kw, jnp.zeros_like(kw)))
              + nb_ref[hh, blk, :, 0:n_keys])
        s2 = _dot(qp, _pad_rows(kct_ref[b, 0, hh].astype(BF16), hh, 0.0))
        return [s1, s2]

    def finish(out, b, blk, hh, e_parts, term):
        keys, n_keys = key_rows(b, blk)
        vw = v_ref[keys, :]
        o = (_dot(e_parts[0], jnp.where(_keep_mask_bf16(n_keys, hh), vw, jnp.ones_like(vw)))
             + _dot_nt(e_parts[1], _pad_rows(vct_ref[b, 0, hh].astype(BF16), hh, 1.0)))
        out.put(hh, o, term)

    heads = []
    for b in range(NA_SEQS_PER_STEP):
        for blk in range(GRID_H // NA_BLOCK_ROWS):
            rows = slice(b * DEC_SEQ + blk * n_q, b * DEC_SEQ + (blk + 1) * n_q)
            out = _PairSink(y_ref, rows, slice(0, PAIR))
            for hh in range(2):
                heads.append(_Head(functools.partial(scores, b, blk, hh), functools.partial(finish, out, b, blk, hh)))
    _attend(heads, batch=16)


def _na_mixer(proj, cache_kt, cache_vt, rpb_flat):
    pairs = H_C // 2
    n_seq = NA_SEQS_PER_STEP
    col = lambda base: pl.BlockSpec((n_seq * DEC_SEQ, PAIR), lambda hp, b: (b, base + hp))
    cache = pl.BlockSpec((n_seq, 1, 2, DH, PAST_LEN), lambda hp, b: (b, 0, hp, 0, 0))
    return pl.pallas_call(
        _na_kernel,
        grid=(pairs, DEC_BATCH // n_seq),
        in_specs=[pl.BlockSpec(memory_space=pltpu.SMEM), col(0), col(pairs), col(2 * pairs), cache, cache],
        out_specs=pl.BlockSpec((n_seq * DEC_SEQ, PAIR), lambda hp, b: (b, hp)),
        out_shape=jax.ShapeDtypeStruct((LAT_TOKENS, 512), BF16),
        scratch_shapes=[pltpu.VMEM((2, GRID_H // NA_BLOCK_ROWS, NA_BLOCK_ROWS * GRID_W, NA_MAX_KEYS), F32)],
        compiler_params=_cparams(2),
        name="na_mixer",
    )(rpb_flat, proj, proj, proj, cache_kt, cache_vt)


def _swa_kernel(q_ref, kv_ref, kct_ref, vct_ref, sink_ref, y_ref):
    n_groups = DEC_SEQ // SWA_QROWS
    ql = lax.broadcasted_iota(jnp.int32, (SWA_QROWS, SWA_BAND), 0)
    kj = lax.broadcasted_iota(jnp.int32, (SWA_QROWS, SWA_BAND), 1)

    def band_bias(first_key_minus_first_query):
        return jnp.where(jnp.abs(kj + first_key_minus_first_query - ql) <= SWA_WINDOW, 0.0, NEG)

    bias_first, bias_mid, bias_last = band_bias(0), band_bias(-SWA_BLOCK), band_bias(-2 * SWA_BLOCK)

    def band_start(grp):
        return min(max(grp * SWA_QROWS - SWA_BLOCK, 0), DEC_SEQ - SWA_BAND)

    memo = _Memo()

    def band_k(g, grp, hh):
        start = band_start(grp)
        kb = kv_ref[start:start + SWA_BAND, g * PAIR:(g + 1) * PAIR]
        return jnp.where(_keep_mask_bf16(SWA_BAND, hh), kb, jnp.zeros_like(kb))

    def band_v(g, grp, hh):
        start = band_start(grp)
        vb = kv_ref[start:start + SWA_BAND, (KV_D + g) * PAIR:(KV_D + g + 1) * PAIR]
        return jnp.where(_keep_mask_bf16(SWA_BAND, hh), vb, jnp.ones_like(vb))

    def scores(g, grp, p, hh):
        bias = bias_first if grp == 0 else (bias_last if grp == n_groups - 1 else bias_mid)
        qp = q_ref[grp * SWA_QROWS:(grp + 1) * SWA_QROWS, p * PAIR:(p + 1) * PAIR]
        s1 = _dot_nt(qp, memo.get_or_build(("k", g, grp, hh), functools.partial(band_k, g, grp, hh))) + bias
        ckt = memo.get_or_build(("ck", g, hh), lambda: _pad_rows(kct_ref[0, 0, g].astype(BF16), hh, 0.0))
        return [s1, _dot(qp, ckt)]

    def finish(out, g, grp, hh, e_parts, term):
        cvt = memo.get_or_build(("cv", g, hh), lambda: _pad_rows(vct_ref[0, 0, g].astype(BF16), hh, 1.0))
        o = (_dot(e_parts[0], memo.get_or_build(("v", g, grp, hh), functools.partial(band_v, g, grp, hh)))
             + _dot_nt(e_parts[1], cvt))
        out.put(hh, o, term)

    heads = []
    for g in range(KV_D):
        for grp in range(n_groups):
            for j in range(GQA // 2):
                p = g * (GQA // 2) + j
                out = _PairSink(y_ref, slice(grp * SWA_QROWS, (grp + 1) * SWA_QROWS), slice(p * PAIR, (p + 1) * PAIR))
                for hh in range(2):
                    sink = sink_ref[2 * p + hh] * LOG2E
                    heads.append(_Head(functools.partial(scores, g, grp, p, hh),
                                       functools.partial(finish, out, g, grp, hh), sink))
    _attend(heads, batch=32)


def _swa_mixer(proj, cache_kt, cache_vt, sink):
    cache = pl.BlockSpec((1, 1, KV_D, DH, PAST_LEN), lambda b: (b, 0, 0, 0, 0))
    return pl.pallas_call(
        _swa_kernel,
        grid=(DEC_BATCH,),
        in_specs=[pl.BlockSpec((DEC_SEQ, 512), lambda b: (b, 3)),
                  pl.BlockSpec((DEC_SEQ, 512), lambda b: (b, 4)),
                  cache, cache,
                  pl.BlockSpec(memory_space=pltpu.SMEM)],
        out_specs=pl.BlockSpec((DEC_SEQ, 512), lambda b: (b, 0)),
        out_shape=jax.ShapeDtypeStruct((LAT_TOKENS, 512), BF16),
        compiler_params=_cparams(1),
        name="swa_mixer",
    )(proj, proj, cache_kt, cache_vt, sink)


def _load_weights_bf16(li, wo_hbm, w1_hbm, w2_hbm, wo_s, w1_s, w2_s, stage_sq, stage_wide, sems):
    square = ([(wo_hbm, wo_s, j, WO_ROWS) for j in range(N_WO)] + [(w2_hbm, w2_s, j, W2_ROWS) for j in range(N_W2)])
    wide = [(w1_hbm, w1_s, j, W1_ROWS) for j in range(N_W1)]
    queues = {"sq": (square, stage_sq, 0), "wide": (wide, stage_wide, STAGE_SLOTS)}

    def copy(kind, idx):
        chunks, stage, sem0 = queues[kind]
        src, _, j, n_rows = chunks[idx]
        slot = idx % STAGE_SLOTS
        return pltpu.make_async_copy(src.at[li, j * n_rows:(j + 1) * n_rows, :], stage.at[slot], sems.at[sem0 + slot])

    for kind in queues:
        for idx in range(min(STAGE_SLOTS, len(queues[kind][0]))):
            copy(kind, idx).start()
    order = []
    for i in range(max(len(square), len(wide))):
        order += [("sq", i)] * (i < len(square)) + [("wide", i)] * (i < len(wide))
    for kind, idx in order:
        chunks, stage, _ = queues[kind]
        _, dst, j, n_rows = chunks[idx]
        copy(kind, idx).wait()
        dst[j * n_rows:(j + 1) * n_rows, :] = stage[idx % STAGE_SLOTS].astype(BF16)
        if idx + STAGE_SLOTS < len(chunks):
            copy(kind, idx + STAGE_SLOTS).start()


def _post_mixer_kernel(*refs, li, n_lat_mix, n_x, n_out):
    mix_ctx_ref, mix_lat_refs = refs[0], refs[1:1 + n_lat_mix]
    refs = refs[1 + n_lat_mix:]
    x_refs = refs[:n_x]
    mod_ref, gpost_ref, gpre_ref, gmlp_ref, wo_hbm, w1_hbm, w2_hbm = refs[n_x:7 + n_x]
    out_refs = refs[7 + n_x:7 + n_x + n_out]
    wo_s, w1_s, w2_s, stage_sq, stage_wide, sems = refs[7 + n_x + n_out:]
    step = pl.program_id(0)

    @pl.when(step == 0)
    def _():
        _load_weights_bf16(li, wo_hbm, w1_hbm, w2_hbm, wo_s, w1_s, w2_s, stage_sq, stage_wide, sems)

    is_ctx = step < CTX_TILES
    mod = mod_ref[0, pl.ds(jnp.where(is_ctx, CTX_ROW, (step - CTX_TILES) // (DEC_SEQ // TM)), 1), :]
    g_post, g_pre, g_mlp = gpost_ref[li:li + 1, :], gpre_ref[li:li + 1, :], gmlp_ref[li:li + 1, :]
    rows = [slice(i * (TM // POST_SPLIT), (i + 1) * (TM // POST_SPLIT)) for i in range(POST_SPLIT)]

    def residual(r):
        if n_x == 1:
            return x_refs[0][r, :]
        return jnp.where(is_ctx, x_refs[0][r, :], x_refs[1][r, :])

    def mixed(r):
        lat = [m[r, :] for m in mix_lat_refs]
        lat = lat[0] if n_lat_mix == 1 else jnp.concatenate(lat, axis=1)
        return jnp.where(is_ctx, mix_ctx_ref[r, :], lat)

    ys = [_dot(mixed(r), wo_s[...]) for r in rows]
    x1 = [residual(r) + mod[:, 2 * D_MODEL:3 * D_MODEL] * _rms(y, g_post) for r, y in zip(rows, ys)]
    h = [_norm_mod(x, g_pre, mod, 3 * D_MODEL).astype(BF16) for x in x1]
    acc = [None] * POST_SPLIT
    pending = []

    def second_matmul(c, i, f):
        t2 = _dot(f, w2_s[c * FF_CHUNK:(c + 1) * FF_CHUNK, :])
        acc[i] = t2 if acc[i] is None else acc[i] + t2

    for c in range(D_FF // FF_CHUNK):
        for i in range(POST_SPLIT):
            f = _dot(h[i], w1_s[:, c * FF_CHUNK:(c + 1) * FF_CHUNK])
            pending.append((c, i, jnp.square(jnp.maximum(f, 0.0)).astype(BF16)))
            if len(pending) > 1:
                second_matmul(*pending.pop(0))
    second_matmul(*pending.pop(0))
    res = [x1[i] + mod[:, 5 * D_MODEL:6 * D_MODEL] * _rms(acc[i], g_mlp) for i in range(POST_SPLIT)]
    if n_out == 1:
        for i, r in enumerate(rows):
            out_refs[0][r, :] = res[i]
    else:
        @pl.when(is_ctx)
        def _():
            for i, r in enumerate(rows):
                out_refs[0][r, :] = res[i]

        @pl.when(jnp.logical_not(is_ctx))
        def _():
            for i, r in enumerate(rows):
                out_refs[1][r, :] = res[i]


def _post_mixer(mix_ctx, mix_lat, xs, mod, g_post, g_pre, g_mlp, w_out, w1, w2, li, *, split_out):
    tile = lambda s: s
    ctx_tile = lambda s: jnp.minimum(tile(s), CTX_TILES - 1)
    lat_tile = lambda s: jnp.maximum(tile(s) - CTX_TILES, 0)
    token_spec = lambda index, width=D_MODEL: pl.BlockSpec((TM, width), lambda s: (index(s), 0))
    mix_specs = [token_spec(ctx_tile)] + [token_spec(lat_tile, m.shape[1]) for m in mix_lat]
    x_specs = [token_spec(tile)] if len(xs) == 1 else [token_spec(ctx_tile), token_spec(lat_tile)]
    if split_out:
        out_specs = [token_spec(ctx_tile), token_spec(lat_tile)]
        out_shape = [jax.ShapeDtypeStruct((CTX_TOKENS, D_MODEL), F32), jax.ShapeDtypeStruct((LAT_TOKENS, D_MODEL), F32)]
    else:
        out_specs = [token_spec(tile)]
        out_shape = [jax.ShapeDtypeStruct((ALL_TOKENS, D_MODEL), F32)]
    gain = pl.BlockSpec((DEPTH, D_MODEL), lambda s: (0, 0))
    return pl.pallas_call(
        functools.partial(_post_mixer_kernel, li=li, n_lat_mix=len(mix_lat), n_x=len(xs), n_out=len(out_specs)),
        grid=(ALL_TOKENS // TM,),
        in_specs=mix_specs + x_specs + [
            pl.BlockSpec((1, MOD_ROWS, 6 * D_MODEL), lambda s: (li, 0, 0)),
            gain, gain, gain] + [pl.BlockSpec(memory_space=pl.ANY)] * 3,
        out_specs=out_specs,
        out_shape=out_shape,
        scratch_shapes=[pltpu.VMEM((D_MODEL, D_MODEL), BF16), pltpu.VMEM((D_MODEL, D_FF), BF16),
                        pltpu.VMEM((D_FF, D_MODEL), BF16),
                        pltpu.VMEM((STAGE_SLOTS, WO_ROWS, D_MODEL), F32), pltpu.VMEM((STAGE_SLOTS, W1_ROWS, D_FF), F32),
                        pltpu.SemaphoreType.DMA((2 * STAGE_SLOTS,))],
        compiler_params=_cparams(1),
        name="post_mixer",
    )(mix_ctx, *mix_lat, *xs, mod, g_post, g_pre, g_mlp, w_out, w1, w2)


def _rope_tables():
    t = jnp.arange(DEC_SEQ)
    rows = (t // GRID_W).astype(F32)
    cols = (t % GRID_W).astype(F32)
    q4 = DH // 4
    inv = 1.0 / (ROPE_BASE ** (jnp.arange(q4, dtype=F32) / q4))
    ar, ac = rows[:, None] * inv, cols[:, None] * inv
    cos64 = jnp.concatenate([jnp.cos(ar), jnp.cos(ar), jnp.cos(ac), jnp.cos(ac)], axis=1)
    sin64 = jnp.concatenate([-jnp.sin(ar), jnp.sin(ar), -jnp.sin(ac), jnp.sin(ac)], axis=1)
    return jnp.tile(cos64, (1, 2)), jnp.tile(sin64, (1, 2))


def kernel(x_prompt, x_sample, cache_diff_k, cache_diff_v, cache_na_k, cache_na_v, cache_swa_k, cache_swa_v, c, c_ctx, mod_w, mod_b, norm_mix_pre, norm_mix_post, norm_mlp_pre, norm_mlp_post, w_in_even, conv_w, lambda_q1, lambda_k1, lambda_q2, lambda_k2, subln, w_in_odd, rpb, sink, w_out, mlp_w1, mlp_w2):
    cond = jnp.concatenate([c, c_ctx[None, :], jnp.zeros((MOD_ROWS - DEC_BATCH - 1, D_MODEL), F32)], axis=0)
    mod, *w_in = _modulation(cond, mod_w, mod_b, w_in_even, w_in_odd)
    rope = _rope_tables()

    t_minor = lambda a: jnp.swapaxes(a, -1, -2)

    xp = x_prompt.reshape(CTX_TOKENS, D_MODEL)
    xs = x_sample.reshape(LAT_TOKENS, D_MODEL)
    lam_init0 = 0.8 - 0.6 * math.exp(-0.3 * 0)
    lam_args = (conv_w, lambda_q1, lambda_k1, lambda_q2, lambda_k2, subln)
    sink_flat = sink.reshape(-1)

    def post(li, mix_ctx, mix_lat, x_parts, split_out):
        return _post_mixer(mix_ctx, mix_lat, x_parts, mod, norm_mix_post, norm_mlp_pre, norm_mlp_post, w_out, mlp_w1,
                           mlp_w2, li, split_out=split_out)

    proj, new_diff_kt, new_diff_v = _pre_mixer(xp, mod, norm_mix_pre, w_in[0], 0, latent=False, even=True, rope=None,
                                               n_tok=CTX_TOKENS)
    mix_ctx = _even_mixer(proj, *lam_args, new_diff_kt, new_diff_v, seq=SEQ, n_seq=CTX_SEQS_PER_STEP_EVEN,
                          own_keys=False, lam_init=lam_init0)
    proj = _pre_mixer(xs, mod, norm_mix_pre, w_in[0], 0, latent=True, even=True, rope=rope, n_tok=LAT_TOKENS)
    mix_lat = _even_mixer(proj, *lam_args, t_minor(cache_diff_k), cache_diff_v, seq=DEC_SEQ, n_seq=1, own_keys=True,
                          lam_init=lam_init0)
    (x_all,) = post(0, mix_ctx, [mix_lat], [xp, xs], False)

    q, new_na_kt, new_na_vt, new_swa_kt, new_swa_vt = _pre_mixer(
        x_all, mod, norm_mix_pre, w_in[1], 1, latent=False, even=False, rope=None, n_tok=CTX_TOKENS)
    mix_ctx = _odd_ctx_mixer(q, new_na_kt, new_na_vt, new_swa_kt, new_swa_vt, sink_flat)
    proj = _pre_mixer(x_all, mod, norm_mix_pre, w_in[1], 1, latent=True, even=False, rope=rope, n_tok=LAT_TOKENS,
                      tile0=CTX_TOKENS // TM_PRE)
    mix_c = _na_mixer(proj, t_minor(cache_na_k), t_minor(cache_na_v), rpb.reshape(-1))
    mix_d = _swa_mixer(proj, t_minor(cache_swa_k), t_minor(cache_swa_v), sink_flat)
    xp, xs = post(1, mix_ctx, [mix_c, mix_d], [x_all], True)

    return (xp.reshape(BATCH, SEQ, D_MODEL), xs.reshape(DEC_BATCH, DEC_SEQ, D_MODEL),
            t_minor(new_diff_kt), new_diff_v, t_minor(new_na_kt), t_minor(new_na_vt),
            t_minor(new_swa_kt), t_minor(new_swa_vt))
```

```python
import functools
import math

import jax
import jax.numpy as jnp
from jax import lax
from jax.experimental import pallas as pl
from jax.experimental.pallas import tpu as pltpu

F32 = jnp.float32
BF16 = jnp.bfloat16

D_MODEL = 1024
BATCH = 32
SEQ = 256
DEPTH = 2
DEC_BATCH = 8
DEC_SEQ = 1024
PAST_LEN = 256
GRID_W = 64
GRID_H = DEC_SEQ // GRID_W
MIX_A = 512
H_B = 4
DH = 64
PAIR = 2 * DH
H_C = 8
H_D = 8
KV_D = 2
GQA = H_D // KV_D
IN_EVEN = 3072
IN_ODD = 2304
D_FF = 4096
NA_WIN_R = 8
NA_WIN_C = 16
SWA_BLOCK = 128
SWA_WINDOW = 128
ROPE_BASE = 10000.0
EPS = 1e-6
NEG = -1e30
LOG2E = math.log2(math.e)
Q_SCALE = DH ** -0.5 * LOG2E

MOD_ROWS = 16
CTX_ROW = DEC_BATCH
V7X_VMEM_BYTES = 64 * 1024 * 1024
VMEM_LIMIT = V7X_VMEM_BYTES - 12 * 1024 * 1024

CTX_TOKENS = BATCH * SEQ
LAT_TOKENS = DEC_BATCH * DEC_SEQ
ALL_TOKENS = CTX_TOKENS + LAT_TOKENS

TM = 512
TM_PRE = 1024
FF_CHUNK = 1024
POST_SPLIT = 2
WCHUNK_ELEMS = 128 * 1024
WO_ROWS, W1_ROWS, W2_ROWS = WCHUNK_ELEMS // D_MODEL, WCHUNK_ELEMS // D_FF, WCHUNK_ELEMS // D_MODEL
N_WO, N_W1, N_W2 = D_MODEL // WO_ROWS, D_MODEL // W1_ROWS, D_FF // W2_ROWS
STAGE_SLOTS = 4
CTX_TILES = CTX_TOKENS // TM
CTX_SEQS_PER_STEP = 4
CTX_SEQS_PER_STEP_EVEN = 2
MOD_TN = 3072
EVEN_BATCH_ROWS = 2048
NA_SEQS_PER_STEP = 2
NA_BLOCK_ROWS = 4
NA_MAX_KEYS = 768
SWA_QROWS = SWA_BLOCK
SWA_BAND = 3 * SWA_BLOCK
LAT_ODD_COLS = IN_ODD + 2 * PAIR


def _cparams(n_axes):
    return pltpu.CompilerParams(dimension_semantics=("arbitrary",) * n_axes, vmem_limit_bytes=VMEM_LIMIT)


def _rms(x, g):
    return x * lax.rsqrt(jnp.mean(x * x, axis=-1, keepdims=True) + EPS) * g


def _dot(a, b):
    return jnp.dot(a, b, preferred_element_type=F32)


def _dot_nt(a, b):
    return lax.dot_general(a, b, (((1,), (1,)), ((), ())), preferred_element_type=F32)


def _layer_block(shape, li):
    zeros = (0,) * len(shape)
    return pl.BlockSpec((1,) + tuple(shape), lambda *_: (li,) + zeros, pipeline_mode=pl.Buffered(1))


def _low_half(shape):
    return (lax.broadcasted_iota(jnp.int32, shape, len(shape) - 1) & DH) == 0


def _keep_mask_bf16(rows, hh):
    lane = lax.broadcasted_iota(jnp.int32, (rows, PAIR), 1)
    half = (lane & DH).astype(F32).astype(BF16)
    return (half == 0) if hh == 0 else (half != 0)


def _pad_rows(t, hh, fill):
    other = jnp.full(t.shape, fill, t.dtype)
    return jnp.concatenate([t, other] if hh == 0 else [other, t], axis=0)


class _Memo(dict):
    def get_or_build(self, key, build):
        if key not in self:
            self[key] = build()
        return self[key]


def _merge_pair(o0, o1, extra_den=None):
    low = _low_half(o0.shape)
    num = jnp.where(low, o0, o1)
    den = pltpu.roll(jnp.where(low, o1, o0), DH, axis=1)
    if extra_den is not None:
        den = den + jnp.where(low, extra_den[0], extra_den[1])
    return num * (1.0 / den)


class _Head:
    def __init__(self, scores, finish, sink=None):
        self.scores, self.finish, self.sink = scores, finish, sink


def _attend(heads, batch):
    batches = [heads[i:i + batch] for i in range(0, len(heads), batch)]
    cur = [hd.scores() for hd in batches[0]]
    for bi, group in enumerate(batches):
        nxt = [hd.scores() for hd in batches[bi + 1]] if bi + 1 < len(batches) else None
        exps = []
        for hd, parts in zip(group, cur):
            chunks = [s[:, j * 128:(j + 1) * 128] for s in parts for j in range(s.shape[1] // 128)]
            m_el = functools.reduce(jnp.maximum, chunks)
            if hd.sink is not None:
                m_el = jnp.maximum(m_el, hd.sink)
            mx = jnp.broadcast_to(jnp.max(m_el, axis=-1, keepdims=True), m_el.shape)
            term = None if hd.sink is None else jnp.exp2(hd.sink - mx)
            e_parts = [jnp.concatenate([jnp.exp2(s[:, j * 128:(j + 1) * 128] - mx)
                                        for j in range(s.shape[1] // 128)], axis=1).astype(BF16) for s in parts]
            exps.append((e_parts, term))
        for hd, (e_parts, term) in zip(group, exps):
            hd.finish(e_parts, term)
        cur = nxt


class _PairSink:
    def __init__(self, y_ref, rows, cols):
        self.y_ref, self.rows, self.cols, self.first = y_ref, rows, cols, None

    def put(self, hh, o, term):
        if hh == 0:
            self.first = (o, term)
        else:
            o0, term0 = self.first
            extra = None if term is None else (term0, term)
            self.y_ref[self.rows, self.cols] = _merge_pair(o0, o, extra).astype(BF16)


def _mod_kernel(cond_ref, w_ref, b_ref, win_e_ref, win_o_ref, o_ref, win_e_bf, win_o_bf):
    c = cond_ref[...]
    s = c * (1.0 / (1.0 + jnp.exp(-c)))
    o_ref[0] = _dot(s.astype(BF16), w_ref[0].astype(BF16)) + b_ref[pl.ds(pl.program_id(0), 1), :]
    win_e_bf[...] = win_e_ref[...].astype(BF16)
    win_o_bf[...] = win_o_ref[...].astype(BF16)


def _modulation(cond, mod_w, mod_b, w_in_even, w_in_odd):
    n_col = 6 * D_MODEL // MOD_TN
    slab = D_MODEL // (DEPTH * n_col)
    w_slab = lambda width: pl.BlockSpec((1, slab, width), lambda l, j: (0, l * n_col + j, 0))
    return pl.pallas_call(
        _mod_kernel,
        grid=(DEPTH, n_col),
        in_specs=[
            pl.BlockSpec((MOD_ROWS, D_MODEL), lambda l, j: (0, 0)),
            pl.BlockSpec((1, D_MODEL, MOD_TN), lambda l, j: (l, 0, j)),
            pl.BlockSpec((DEPTH, MOD_TN), lambda l, j: (0, j)),
            w_slab(IN_EVEN), w_slab(IN_ODD),
        ],
        out_specs=[pl.BlockSpec((1, MOD_ROWS, MOD_TN), lambda l, j: (l, 0, j)), w_slab(IN_EVEN), w_slab(IN_ODD)],
        out_shape=[jax.ShapeDtypeStruct((DEPTH, MOD_ROWS, 6 * D_MODEL), F32),
                   jax.ShapeDtypeStruct((1, D_MODEL, IN_EVEN), BF16), jax.ShapeDtypeStruct((1, D_MODEL, IN_ODD), BF16)],
        compiler_params=_cparams(2),
        name="adaln_modulation",
    )(cond, mod_w, mod_b, w_in_even, w_in_odd)


def _norm_mod(x, g, mod, shift_col):
    sh = mod[:, shift_col:shift_col + D_MODEL]
    sc = mod[:, shift_col + D_MODEL:shift_col + 2 * D_MODEL]
    return _rms(x, g) * (1.0 + sc) + sh


def _rope(z, cos, sin):
    outs = []
    for j in range(z.shape[1] // 128):
        zj = z[:, j * 128:(j + 1) * 128]
        lane = lax.broadcasted_iota(jnp.int32, zj.shape, 1)
        partner = jnp.where((lane & 16) == 0, pltpu.roll(zj, 128 - 16, axis=1), pltpu.roll(zj, 16, axis=1))
        outs.append(zj * cos + partner * sin)
    return outs[0] if len(outs) == 1 else jnp.concatenate(outs, axis=1)


def _store_heads_transposed(dst, bi, pc_rows, n_pairs):
    for p in range(n_pairs):
        t = pc_rows[:, p * PAIR:(p + 1) * PAIR].T
        dst[bi, 0, 2 * p] = t[0:DH]
        dst[bi, 0, 2 * p + 1] = t[DH:PAIR]


def _mod_row(mod_ref, latent):
    if latent:
        return mod_ref[0, pl.ds(pl.program_id(0) // (DEC_SEQ // TM_PRE), 1), :]
    return mod_ref[0, CTX_ROW:CTX_ROW + 1, :]


def _normed_groups(x_ref, mod_ref, g_ref, li, latent):
    rows = [slice(i * SEQ, (i + 1) * SEQ) for i in range(TM_PRE // SEQ)]
    mod, g = _mod_row(mod_ref, latent), g_ref[li:li + 1, :]
    return rows, [_norm_mod(x_ref[r, :], g, mod, 0).astype(BF16) for r in rows]


def _k1_ctx_even(x_ref, mod_ref, g_ref, w_ref, proj_ref, dkt_ref, dv_ref, *, li):
    rows, hs = _normed_groups(x_ref, mod_ref, g_ref, li, False)
    for c in range(IN_EVEN // 512):
        for bi, (r, h) in enumerate(zip(rows, hs)):
            pc = _dot(h, w_ref[0, :, c * 512:(c + 1) * 512])
            if c < 3:
                proj_ref[r, c * 512:(c + 1) * 512] = pc.astype(BF16)
            elif c == 3:
                proj_ref[r, c * 512:(c + 1) * 512] = (pc * Q_SCALE).astype(BF16)
            elif c == 4:
                for hh in range(H_B):
                    t = pc[:, hh * PAIR:(hh + 1) * PAIR].T
                    dkt_ref[bi, 0, hh, 0] = t[0:DH]
                    dkt_ref[bi, 0, hh, 1] = t[DH:PAIR]
            else:
                for hh in range(H_B):
                    dv_ref[bi, 0, hh] = pc[:, hh * PAIR:(hh + 1) * PAIR]


def _k1_lat_even(x_ref, mod_ref, g_ref, w_ref, cos_ref, sin_ref, proj_ref, *, li):
    rows, hs = _normed_groups(x_ref, mod_ref, g_ref, li, True)
    for c in range(IN_EVEN // 512):
        for r, h in zip(rows, hs):
            pc = _dot(h, w_ref[0, :, c * 512:(c + 1) * 512])
            if c == 3:
                pc = _rope(pc * Q_SCALE, cos_ref[r, :], sin_ref[r, :])
            elif c == 4:
                pc = _rope(pc, cos_ref[r, :], sin_ref[r, :])
            proj_ref[r, c * 512:(c + 1) * 512] = pc.astype(BF16)


def _k1_ctx_odd(x_ref, mod_ref, g_ref, w_ref, q_ref, nkt_ref, nvt_ref, skt_ref, svt_ref, *, li):
    rows, hs = _normed_groups(x_ref, mod_ref, g_ref, li, False)
    for c in range(4):
        for bi, (r, h) in enumerate(zip(rows, hs)):
            pc = _dot(h, w_ref[0, :, c * 512:(c + 1) * 512])
            if c == 0 or c == 3:
                q_ref[r, (c // 3) * 512:(c // 3 + 1) * 512] = (pc * Q_SCALE).astype(BF16)
            else:
                _store_heads_transposed(nkt_ref if c == 1 else nvt_ref, bi, pc, H_C // 2)
    for bi, h in enumerate(hs):
        pc = _dot(h, w_ref[0, :, 2048:IN_ODD])
        _store_heads_transposed(skt_ref, bi, pc[:, 0:PAIR], 1)
        _store_heads_transposed(svt_ref, bi, pc[:, PAIR:2 * PAIR], 1)


def _k1_lat_odd(x_ref, mod_ref, g_ref, w_ref, cos_ref, sin_ref, proj_ref, *, li):
    rows, hs = _normed_groups(x_ref, mod_ref, g_ref, li, True)
    for c in range(4):
        for r, h in zip(rows, hs):
            pc = _dot(h, w_ref[0, :, c * 512:(c + 1) * 512])
            if c == 0:
                pc = pc * Q_SCALE
            elif c == 3:
                pc = _rope(pc * Q_SCALE, cos_ref[r, :], sin_ref[r, :])
            proj_ref[r, c * 512:(c + 1) * 512] = pc.astype(BF16)
    low = _low_half((SEQ, PAIR))
    for r, h in zip(rows, hs):
        pc = _dot(h, w_ref[0, :, 2048:IN_ODD])
        for j, z in enumerate((_rope(pc[:, 0:PAIR], cos_ref[r, :], sin_ref[r, :]), pc[:, PAIR:2 * PAIR])):
            zr = pltpu.roll(z, DH, axis=1)
            base = 2048 + j * 2 * PAIR
            proj_ref[r, base:base + PAIR] = jnp.where(low, z, zr).astype(BF16)
            proj_ref[r, base + PAIR:base + 2 * PAIR] = jnp.where(low, zr, z).astype(BF16)


def _pre_mixer(x, mod, gains, w, li, *, latent, even, rope, n_tok, tile0=0):
    n_in = w.shape[2]
    tm = TM_PRE
    tiles_per_seq = DEC_SEQ // tm
    in_specs = [
        pl.BlockSpec((tm, D_MODEL), lambda i: (i + tile0, 0)),
        pl.BlockSpec((1, MOD_ROWS, 6 * D_MODEL), lambda i: (li, 0, 0)),
        pl.BlockSpec((DEPTH, D_MODEL), lambda i: (0, 0)),
        _layer_block((D_MODEL, n_in), 0),
    ]
    args = [x, mod, gains, w]
    nb = tm // SEQ
    if latent:
        in_specs += [pl.BlockSpec((tm, 128), lambda i: (i % tiles_per_seq, 0))] * 2
        args += list(rope)
        body = _k1_lat_even if even else _k1_lat_odd
        n_out = IN_EVEN if even else LAT_ODD_COLS
        out_specs = pl.BlockSpec((tm, n_out), lambda i: (i, 0))
        out_shape = jax.ShapeDtypeStruct((n_tok, n_out), BF16)
    elif even:
        body = _k1_ctx_even
        out_specs = [pl.BlockSpec((tm, 4 * MIX_A), lambda i: (i, 0)),
                     pl.BlockSpec((nb, 1, H_B, 2, DH, SEQ), lambda i: (i, 0, 0, 0, 0, 0)),
                     pl.BlockSpec((nb, 1, H_B, SEQ, PAIR), lambda i: (i, 0, 0, 0, 0))]
        out_shape = [jax.ShapeDtypeStruct((n_tok, 4 * MIX_A), BF16),
                     jax.ShapeDtypeStruct((BATCH, 1, H_B, 2, DH, SEQ), F32),
                     jax.ShapeDtypeStruct((BATCH, 1, H_B, SEQ, PAIR), F32)]
    else:
        body = _k1_ctx_odd
        c_spec = pl.BlockSpec((nb, 1, H_C, DH, SEQ), lambda i: (i, 0, 0, 0, 0))
        d_spec = pl.BlockSpec((nb, 1, KV_D, DH, SEQ), lambda i: (i, 0, 0, 0, 0))
        c_shape = jax.ShapeDtypeStruct((BATCH, 1, H_C, DH, SEQ), F32)
        d_shape = jax.ShapeDtypeStruct((BATCH, 1, KV_D, DH, SEQ), F32)
        out_specs = [pl.BlockSpec((tm, D_MODEL), lambda i: (i, 0)), c_spec, c_spec, d_spec, d_spec]
        out_shape = [jax.ShapeDtypeStruct((n_tok, D_MODEL), BF16), c_shape, c_shape, d_shape, d_shape]
    return pl.pallas_call(
        functools.partial(body, li=li),
        grid=(n_tok // tm,),
        in_specs=in_specs,
        out_specs=out_specs,
        out_shape=out_shape,
        compiler_params=_cparams(1),
        name=f"pre_mixer_{'lat' if latent else 'ctx'}_{'even' if even else 'odd'}",
    )(*args)


def _even_mixer_kernel(*refs, seq, n_seq, tq, own_keys, lam_init):
    proj_ref, cw_ref, lq1_ref, lk1_ref, lq2_ref, lk2_ref, subln_ref, ckt_ref, cv_ref = refs[:9]
    y_ref = refs[-1]
    n_rows = n_seq * seq

    pos = lax.broadcasted_iota(jnp.int32, (n_rows, 128), 0) % seq
    for j in range(MIX_A // 128):
        cols = slice(j * 128, (j + 1) * 128)
        a_b = proj_ref[:, j * 128:(j + 1) * 128].astype(F32)
        u = (proj_ref[:, MIX_A + j * 128:MIX_A + (j + 1) * 128].astype(F32)
             * proj_ref[:, 2 * MIX_A + j * 128:2 * MIX_A + (j + 1) * 128].astype(F32))
        u_prev = jnp.where(pos == 0, 0.0, pltpu.roll(u, 1, axis=0))
        u_next = jnp.where(pos == seq - 1, 0.0, pltpu.roll(u, n_rows - 1, axis=0))
        w = cw_ref[0, :, cols]
        y_ref[:, cols] = (a_b * (w[0:1] * u_prev + w[1:2] * u + w[2:3] * u_next)).astype(BF16)

    lam = (jnp.exp(jnp.sum(lq1_ref[...] * lk1_ref[...], axis=-1, keepdims=True))
           - jnp.exp(jnp.sum(lq2_ref[...] * lk2_ref[...], axis=-1, keepdims=True)) + lam_init)
    subln = subln_ref[...]
    q_col, k_col, v_col = 3 * MIX_A, 3 * MIX_A + 512, 3 * MIX_A + 1024
    ones = jnp.ones((PAST_LEN, PAIR), BF16)

    memo = _Memo()

    def own_k(b, h, m):
        kp = proj_ref[b * seq:(b + 1) * seq, k_col + h * PAIR:k_col + (h + 1) * PAIR]
        return jnp.where(_keep_mask_bf16(seq, m), kp, jnp.zeros_like(kp))

    def value_ops(b, h):
        v_ops = []
        if own_keys:
            v_own = proj_ref[b * seq:(b + 1) * seq, v_col + h * PAIR:v_col + (h + 1) * PAIR]
            v_ops.append(jnp.concatenate([v_own, jnp.ones((seq, PAIR), BF16)], axis=1))
        v_ops.append(jnp.concatenate([cv_ref[b, 0, h].astype(BF16), ones], axis=1))
        return v_ops

    def scores(b, qrows, h, m):
        qp = proj_ref[qrows, q_col + h * PAIR:q_col + (h + 1) * PAIR]
        parts = []
        if own_keys:
            parts.append(_dot_nt(qp, memo.get_or_build(("k", b, h, m), functools.partial(own_k, b, h, m))))
        ckt = memo.get_or_build(("ck", b, h, m), lambda: _pad_rows(ckt_ref[b, 0, h, m].astype(BF16), m, 0.0))
        parts.append(_dot(qp, ckt))
        return parts

    first_map = {}

    def pv(b, qrows, h, m, e_parts, _):
        v_ops = memo.get_or_build(("v", b, h), functools.partial(value_ops, b, h))
        o = None
        for e, v_op in zip(e_parts, v_ops):
            t = _dot(e, v_op)
            o = t if o is None else o + t
        attn = o[:, 0:PAIR] * (1.0 / o[:, PAIR:2 * PAIR])
        if m == 0:
            first_map[(qrows.start, h)] = attn
        else:
            y = _rms(first_map.pop((qrows.start, h)) - lam * attn, subln) * (1.0 - lam_init)
            y_ref[qrows, MIX_A + h * PAIR:MIX_A + (h + 1) * PAIR] = y.astype(BF16)

    heads = []
    for b in range(n_seq):
        for i in range(seq // tq):
            qrows = slice(b * seq + i * tq, b * seq + (i + 1) * tq)
            for h in range(H_B):
                for m in range(2):
                    heads.append(_Head(functools.partial(scores, b, qrows, h, m),
                                       functools.partial(pv, b, qrows, h, m)))
    _attend(heads, batch=EVEN_BATCH_ROWS // tq)


def _even_mixer(proj, conv_w, lq1, lk1, lq2, lk2, subln, kt, v, *, seq, n_seq, own_keys, lam_init):
    n_tok = proj.shape[0]
    rows = n_seq * seq
    small = lambda a: pl.BlockSpec((1, a.shape[1]), lambda b: (0, 0))
    in_specs = [pl.BlockSpec((rows, proj.shape[1]), lambda b: (b, 0)),
                pl.BlockSpec((1, 3, MIX_A), lambda b: (0, 0, 0)),
                small(lq1), small(lk1), small(lq2), small(lk2), small(subln),
                pl.BlockSpec((n_seq, 1, H_B, 2, DH, PAST_LEN), lambda b: (b, 0, 0, 0, 0, 0)),
                pl.BlockSpec((n_seq, 1, H_B, PAST_LEN, PAIR), lambda b: (b, 0, 0, 0, 0))]
    return pl.pallas_call(
        functools.partial(_even_mixer_kernel, seq=seq, n_seq=n_seq, tq=min(seq, 256), own_keys=own_keys,
                          lam_init=lam_init),
        grid=(n_tok // rows,),
        in_specs=in_specs,
        out_specs=pl.BlockSpec((rows, D_MODEL), lambda b: (b, 0)),
        out_shape=jax.ShapeDtypeStruct((n_tok, D_MODEL), BF16),
        compiler_params=_cparams(1),
        name=f"even_mixer_{'lat' if own_keys else 'ctx'}",
    )(proj, conv_w, lq1, lk1, lq2, lk2, subln, kt, v)


def _odd_ctx_kernel(q_ref, nkt_ref, nvt_ref, skt_ref, svt_ref, sink_ref, y_ref):
    def kv_refs(b, head):
        if head < H_C:
            return nkt_ref.at[b, 0, head], nvt_ref.at[b, 0, head]
        g = (head - H_C) // GQA
        return skt_ref.at[b, 0, g], svt_ref.at[b, 0, g]

    memo = _Memo()

    def kv_key(head):
        return head if head < H_C else H_C + (head - H_C) // GQA

    def scores(b, p, hh):
        head = 2 * p + hh
        kt = memo.get_or_build(("k", b, kv_key(head), hh),
                               lambda: _pad_rows(kv_refs(b, head)[0][...].astype(BF16), hh, 0.0))
        return [_dot(q_ref[b * SEQ:(b + 1) * SEQ, p * PAIR:(p + 1) * PAIR], kt)]

    def finish(out, b, p, hh, e_parts, term):
        head = 2 * p + hh
        vt = memo.get_or_build(("v", b, kv_key(head), hh),
                               lambda: _pad_rows(kv_refs(b, head)[1][...].astype(BF16), hh, 1.0))
        out.put(hh, _dot_nt(e_parts[0], vt), term)

    heads = []
    for b in range(CTX_SEQS_PER_STEP):
        for p in range((H_C + H_D) // 2):
            out = _PairSink(y_ref, slice(b * SEQ, (b + 1) * SEQ), slice(p * PAIR, (p + 1) * PAIR))
            for hh in range(2):
                d_head = 2 * p + hh - H_C
                sink = sink_ref[d_head] * LOG2E if d_head >= 0 else None
                heads.append(_Head(functools.partial(scores, b, p, hh), functools.partial(finish, out, b, p, hh), sink))
    _attend(heads, batch=16)


def _odd_ctx_mixer(q, nkt, nvt, skt, svt, sink):
    n_tok = q.shape[0]
    n_seq = CTX_SEQS_PER_STEP
    c_spec = pl.BlockSpec((n_seq, 1, H_C, DH, SEQ), lambda b: (b, 0, 0, 0, 0))
    d_spec = pl.BlockSpec((n_seq, 1, KV_D, DH, SEQ), lambda b: (b, 0, 0, 0, 0))
    return pl.pallas_call(
        _odd_ctx_kernel,
        grid=(n_tok // (n_seq * SEQ),),
        in_specs=[pl.BlockSpec((n_seq * SEQ, D_MODEL), lambda b: (b, 0)), c_spec, c_spec, d_spec, d_spec,
                  pl.BlockSpec(memory_space=pltpu.SMEM)],
        out_specs=pl.BlockSpec((n_seq * SEQ, D_MODEL), lambda b: (b, 0)),
        out_shape=jax.ShapeDtypeStruct((n_tok, D_MODEL), BF16),
        compiler_params=_cparams(1),
        name="odd_mixer_ctx",
    )(q, nkt, nvt, skt, svt, sink)


def _na_block(blk):
    r0 = blk * NA_BLOCK_ROWS
    first = min(max(r0 - NA_WIN_R // 2, 0), GRID_H - NA_WIN_R)
    last = min(max(r0 + NA_BLOCK_ROWS - 1 - NA_WIN_R // 2, 0), GRID_H - NA_WIN_R) + NA_WIN_R
    n_rows = -(-(last - first) // 4) * 4
    return min(first, GRID_H - n_rows), n_rows


def _build_na_bias(rpb_ref, head, nb_ref, slot):
    qc = lax.broadcasted_iota(jnp.int32, (GRID_W, GRID_W), 0)
    kc = lax.broadcasted_iota(jnp.int32, (GRID_W, GRID_W), 1)
    col_start = jnp.clip(qc - NA_WIN_C // 2, 0, GRID_W - NA_WIN_C)
    col_ok = (kc >= col_start) & (kc < col_start + NA_WIN_C)
    dc = kc - qc + NA_WIN_C - 1
    n_dr, n_dc = 2 * NA_WIN_R - 1, 2 * NA_WIN_C - 1
    neg = jnp.full((GRID_W, GRID_W), NEG, F32)
    toeplitz = []
    for dr in range(n_dr):
        t = neg
        for d in range(n_dc):
            t = jnp.where(dc == d, rpb_ref[(head * n_dr + dr) * n_dc + d] * LOG2E, t)
        toeplitz.append(jnp.where(col_ok, t, NEG))
    for blk in range(GRID_H // NA_BLOCK_ROWS):
        first, n_rows = _na_block(blk)
        for rl in range(NA_BLOCK_ROWS):
            r = blk * NA_BLOCK_ROWS + rl
            r_start = min(max(r - NA_WIN_R // 2, 0), GRID_H - NA_WIN_R)
            blocks = []
            for rk in range(first, first + n_rows):
                inside = r_start <= rk < r_start + NA_WIN_R
                blocks.append(toeplitz[rk - r + NA_WIN_R - 1] if inside else neg)
            nb_ref[slot, blk, rl * GRID_W:(rl + 1) * GRID_W, 0:n_rows * GRID_W] = jnp.concatenate(blocks, axis=1)


def _na_kernel(rpb_ref, q_ref, k_ref, v_ref, kct_ref, vct_ref, y_ref, nb_ref):
    @pl.when(pl.program_id(1) == 0)
    def _():
        for hh in range(2):
            _build_na_bias(rpb_ref, 2 * pl.program_id(0) + hh, nb_ref, hh)

    n_q = NA_BLOCK_ROWS * GRID_W

    def key_rows(b, blk):
        first, n_rows = _na_block(blk)
        return slice(b * DEC_SEQ + first * GRID_W, b * DEC_SEQ + (first + n_rows) * GRID_W), n_rows * GRID_W

    def scores(b, blk, hh):
        keys, n_keys = key_rows(b, blk)
        qp = q_ref[b * DEC_SEQ + blk * n_q:b * DEC_SEQ + (blk + 1) * n_q, :]
        kw = k_ref[keys, :]
        s1 = (_dot_nt(qp, jnp.where(_keep_mask_bf16(n_keys, hh), kw, jnp.zeros_like(kw)))
              + nb_ref[hh, blk, :, 0:n_keys])
        s2 = _dot(qp, _pad_rows(kct_ref[b, 0, hh].astype(BF16), hh, 0.0))
        return [s1, s2]

    def finish(out, b, blk, hh, e_parts, term):
        keys, n_keys = key_rows(b, blk)
        vw = v_ref[keys, :]
        o = (_dot(e_parts[0], jnp.where(_keep_mask_bf16(n_keys, hh), vw, jnp.ones_like(vw)))
             + _dot_nt(e_parts[1], _pad_rows(vct_ref[b, 0, hh].astype(BF16), hh, 1.0)))
        out.put(hh, o, term)

    heads = []
    for b in range(NA_SEQS_PER_STEP):
        for blk in range(GRID_H // NA_BLOCK_ROWS):
            rows = slice(b * DEC_SEQ + blk * n_q, b * DEC_SEQ + (blk + 1) * n_q)
            out = _PairSink(y_ref, rows, slice(0, PAIR))
            for hh in range(2):
                heads.append(_Head(functools.partial(scores, b, blk, hh), functools.partial(finish, out, b, blk, hh)))
    _attend(heads, batch=16)


def _na_mixer(proj, cache_kt, cache_vt, rpb_flat):
    pairs = H_C // 2
    n_seq = NA_SEQS_PER_STEP
    col = lambda base: pl.BlockSpec((n_seq * DEC_SEQ, PAIR), lambda hp, b: (b, base + hp))
    cache = pl.BlockSpec((n_seq, 1, 2, DH, PAST_LEN), lambda hp, b: (b, 0, hp, 0, 0))
    return pl.pallas_call(
        _na_kernel,
        grid=(pairs, DEC_BATCH // n_seq),
        in_specs=[pl.BlockSpec(memory_space=pltpu.SMEM), col(0), col(pairs), col(2 * pairs), cache, cache],
        out_specs=pl.BlockSpec((n_seq * DEC_SEQ, PAIR), lambda hp, b: (b, hp)),
        out_shape=jax.ShapeDtypeStruct((LAT_TOKENS, 512), BF16),
        scratch_shapes=[pltpu.VMEM((2, GRID_H // NA_BLOCK_ROWS, NA_BLOCK_ROWS * GRID_W, NA_MAX_KEYS), F32)],
        compiler_params=_cparams(2),
        name="na_mixer",
    )(rpb_flat, proj, proj, proj, cache_kt, cache_vt)


def _swa_kernel(q_ref, kv_ref, kct_ref, vct_ref, sink_ref, y_ref):
    n_groups = DEC_SEQ // SWA_QROWS
    ql = lax.broadcasted_iota(jnp.int32, (SWA_QROWS, SWA_BAND), 0)
    kj = lax.broadcasted_iota(jnp.int32, (SWA_QROWS, SWA_BAND), 1)

    def band_bias(first_key_minus_first_query):
        return jnp.where(jnp.abs(kj + first_key_minus_first_query - ql) <= SWA_WINDOW, 0.0, NEG)

    bias_first, bias_mid, bias_last = band_bias(0), band_bias(-SWA_BLOCK), band_bias(-2 * SWA_BLOCK)

    def band_start(grp):
        return min(max(grp * SWA_QROWS - SWA_BLOCK, 0), DEC_SEQ - SWA_BAND)

    memo = _Memo()

    def band_k(g, grp, hh):
        start = band_start(grp)
        kb = kv_ref[start:start + SWA_BAND, g * PAIR:(g + 1) * PAIR]
        return jnp.where(_keep_mask_bf16(SWA_BAND, hh), kb, jnp.zeros_like(kb))

    def band_v(g, grp, hh):
        start = band_start(grp)
        vb = kv_ref[start:start + SWA_BAND, (KV_D + g) * PAIR:(KV_D + g + 1) * PAIR]
        return jnp.where(_keep_mask_bf16(SWA_BAND, hh), vb, jnp.ones_like(vb))

    def scores(g, grp, p, hh):
        bias = bias_first if grp == 0 else (bias_last if grp == n_groups - 1 else bias_mid)
        qp = q_ref[grp * SWA_QROWS:(grp + 1) * SWA_QROWS, p * PAIR:(p + 1) * PAIR]
        s1 = _dot_nt(qp, memo.get_or_build(("k", g, grp, hh), functools.partial(band_k, g, grp, hh))) + bias
        ckt = memo.get_or_build(("ck", g, hh), lambda: _pad_rows(kct_ref[0, 0, g].astype(BF16), hh, 0.0))
        return [s1, _dot(qp, ckt)]

    def finish(out, g, grp, hh, e_parts, term):
        cvt = memo.get_or_build(("cv", g, hh), lambda: _pad_rows(vct_ref[0, 0, g].astype(BF16), hh, 1.0))
        o = (_dot(e_parts[0], memo.get_or_build(("v", g, grp, hh), functools.partial(band_v, g, grp, hh)))
             + _dot_nt(e_parts[1], cvt))
        out.put(hh, o, term)

    heads = []
    for g in range(KV_D):
        for grp in range(n_groups):
            for j in range(GQA // 2):
                p = g * (GQA // 2) + j
                out = _PairSink(y_ref, slice(grp * SWA_QROWS, (grp + 1) * SWA_QROWS), slice(p * PAIR, (p + 1) * PAIR))
                for hh in range(2):
                    sink = sink_ref[2 * p + hh] * LOG2E
                    heads.append(_Head(functools.partial(scores, g, grp, p, hh),
                                       functools.partial(finish, out, g, grp, hh), sink))
    _attend(heads, batch=32)


def _swa_mixer(proj, cache_kt, cache_vt, sink):
    cache = pl.BlockSpec((1, 1, KV_D, DH, PAST_LEN), lambda b: (b, 0, 0, 0, 0))
    return pl.pallas_call(
        _swa_kernel,
        grid=(DEC_BATCH,),
        in_specs=[pl.BlockSpec((DEC_SEQ, 512), lambda b: (b, 3)),
                  pl.BlockSpec((DEC_SEQ, 512), lambda b: (b, 4)),
                  cache, cache,
                  pl.BlockSpec(memory_space=pltpu.SMEM)],
        out_specs=pl.BlockSpec((DEC_SEQ, 512), lambda b: (b, 0)),
        out_shape=jax.ShapeDtypeStruct((LAT_TOKENS, 512), BF16),
        compiler_params=_cparams(1),
        name="swa_mixer",
    )(proj, proj, cache_kt, cache_vt, sink)


def _load_weights_bf16(li, wo_hbm, w1_hbm, w2_hbm, wo_s, w1_s, w2_s, stage_sq, stage_wide, sems):
    square = ([(wo_hbm, wo_s, j, WO_ROWS) for j in range(N_WO)] + [(w2_hbm, w2_s, j, W2_ROWS) for j in range(N_W2)])
    wide = [(w1_hbm, w1_s, j, W1_ROWS) for j in range(N_W1)]
    queues = {"sq": (square, stage_sq, 0), "wide": (wide, stage_wide, STAGE_SLOTS)}

    def copy(kind, idx):
        chunks, stage, sem0 = queues[kind]
        src, _, j, n_rows = chunks[idx]
        slot = idx % STAGE_SLOTS
        return pltpu.make_async_copy(src.at[li, j * n_rows:(j + 1) * n_rows, :], stage.at[slot], sems.at[sem0 + slot])

    for kind in queues:
        for idx in range(min(STAGE_SLOTS, len(queues[kind][0]))):
            copy(kind, idx).start(priority=idx % 2)
    order = []
    for i in range(max(len(square), len(wide))):
        order += [("sq", i)] * (i < len(square)) + [("wide", i)] * (i < len(wide))
    for kind, idx in order:
        chunks, stage, _ = queues[kind]
        _, dst, j, n_rows = chunks[idx]
        copy(kind, idx).wait()
        dst[j * n_rows:(j + 1) * n_rows, :] = stage[idx % STAGE_SLOTS].astype(BF16)
        if idx + STAGE_SLOTS < len(chunks):
            copy(kind, idx + STAGE_SLOTS).start(priority=idx % 2)


def _post_mixer_kernel(*refs, li, n_lat_mix, n_x, n_out):
    mix_ctx_ref, mix_lat_refs = refs[0], refs[1:1 + n_lat_mix]
    refs = refs[1 + n_lat_mix:]
    x_refs = refs[:n_x]
    mod_ref, gpost_ref, gpre_ref, gmlp_ref, wo_hbm, w1_hbm, w2_hbm = refs[n_x:7 + n_x]
    out_refs = refs[7 + n_x:7 + n_x + n_out]
    wo_s, w1_s, w2_s, stage_sq, stage_wide, sems = refs[7 + n_x + n_out:]
    step = pl.program_id(0)

    @pl.when(step == 0)
    def _():
        _load_weights_bf16(li, wo_hbm, w1_hbm, w2_hbm, wo_s, w1_s, w2_s, stage_sq, stage_wide, sems)

    is_ctx = step < CTX_TILES
    mod = mod_ref[0, pl.ds(jnp.where(is_ctx, CTX_ROW, (step - CTX_TILES) // (DEC_SEQ // TM)), 1), :]
    g_post, g_pre, g_mlp = gpost_ref[li:li + 1, :], gpre_ref[li:li + 1, :], gmlp_ref[li:li + 1, :]
    rows = [slice(i * (TM // POST_SPLIT), (i + 1) * (TM // POST_SPLIT)) for i in range(POST_SPLIT)]

    def residual(r):
        if n_x == 1:
            return x_refs[0][r, :]
        return jnp.where(is_ctx, x_refs[0][r, :], x_refs[1][r, :])

    def mixed(r):
        lat = [m[r, :] for m in mix_lat_refs]
        lat = lat[0] if n_lat_mix == 1 else jnp.concatenate(lat, axis=1)
        return jnp.where(is_ctx, mix_ctx_ref[r, :], lat)

    ys = [_dot(mixed(r), wo_s[...]) for r in rows]
    x1 = [residual(r) + mod[:, 2 * D_MODEL:3 * D_MODEL] * _rms(y, g_post) for r, y in zip(rows, ys)]
    h = [_norm_mod(x, g_pre, mod, 3 * D_MODEL).astype(BF16) for x in x1]
    acc = [None] * POST_SPLIT
    pending = []

    def second_matmul(c, i, f):
        t2 = _dot(f, w2_s[c * FF_CHUNK:(c + 1) * FF_CHUNK, :])
        acc[i] = t2 if acc[i] is None else acc[i] + t2

    for c in range(D_FF // FF_CHUNK):
        for i in range(POST_SPLIT):
            f = _dot(h[i], w1_s[:, c * FF_CHUNK:(c + 1) * FF_CHUNK])
            pending.append((c, i, jnp.square(jnp.maximum(f, 0.0)).astype(BF16)))
            if len(pending) > 1:
                second_matmul(*pending.pop(0))
    second_matmul(*pending.pop(0))
    res = [x1[i] + mod[:, 5 * D_MODEL:6 * D_MODEL] * _rms(acc[i], g_mlp) for i in range(POST_SPLIT)]
    if n_out == 1:
        for i, r in enumerate(rows):
            out_refs[0][r, :] = res[i]
    else:
        @pl.when(is_ctx)
        def _():
            for i, r in enumerate(rows):
                out_refs[0][r, :] = res[i]

        @pl.when(jnp.logical_not(is_ctx))
        def _():
            for i, r in enumerate(rows):
                out_refs[1][r, :] = res[i]


def _post_mixer(mix_ctx, mix_lat, xs, mod, g_post, g_pre, g_mlp, w_out, w1, w2, li, *, split_out):
    tile = lambda s: s
    ctx_tile = lambda s: jnp.minimum(tile(s), CTX_TILES - 1)
    lat_tile = lambda s: jnp.maximum(tile(s) - CTX_TILES, 0)
    token_spec = lambda index, width=D_MODEL: pl.BlockSpec((TM, width), lambda s: (index(s), 0))
    mix_specs = [token_spec(ctx_tile)] + [token_spec(lat_tile, m.shape[1]) for m in mix_lat]
    x_specs = [token_spec(tile)] if len(xs) == 1 else [token_spec(ctx_tile), token_spec(lat_tile)]
    if split_out:
        out_specs = [token_spec(ctx_tile), token_spec(lat_tile)]
        out_shape = [jax.ShapeDtypeStruct((CTX_TOKENS, D_MODEL), F32), jax.ShapeDtypeStruct((LAT_TOKENS, D_MODEL), F32)]
    else:
        out_specs = [token_spec(tile)]
        out_shape = [jax.ShapeDtypeStruct((ALL_TOKENS, D_MODEL), F32)]
    gain = pl.BlockSpec((DEPTH, D_MODEL), lambda s: (0, 0))
    return pl.pallas_call(
        functools.partial(_post_mixer_kernel, li=li, n_lat_mix=len(mix_lat), n_x=len(xs), n_out=len(out_specs)),
        grid=(ALL_TOKENS // TM,),
        in_specs=mix_specs + x_specs + [
            pl.BlockSpec((1, MOD_ROWS, 6 * D_MODEL), lambda s: (li, 0, 0)),
            gain, gain, gain] + [pl.BlockSpec(memory_space=pl.ANY)] * 3,
        out_specs=out_specs,
        out_shape=out_shape,
        scratch_shapes=[pltpu.VMEM((D_MODEL, D_MODEL), BF16), pltpu.VMEM((D_MODEL, D_FF), BF16),
                        pltpu.VMEM((D_FF, D_MODEL), BF16),
                        pltpu.VMEM((STAGE_SLOTS, WO_ROWS, D_MODEL), F32), pltpu.VMEM((STAGE_SLOTS, W1_ROWS, D_FF), F32),
                        pltpu.SemaphoreType.DMA((2 * STAGE_SLOTS,))],
        compiler_params=_cparams(1),
        name="post_mixer",
    )(mix_ctx, *mix_lat, *xs, mod, g_post, g_pre, g_mlp, w_out, w1, w2)


def _rope_tables():
    t = jnp.arange(DEC_SEQ)
    rows = (t // GRID_W).astype(F32)
    cols = (t % GRID_W).astype(F32)
    q4 = DH // 4
    inv = 1.0 / (ROPE_BASE ** (jnp.arange(q4, dtype=F32) / q4))
    ar, ac = rows[:, None] * inv, cols[:, None] * inv
    cos64 = jnp.concatenate([jnp.cos(ar), jnp.cos(ar), jnp.cos(ac), jnp.cos(ac)], axis=1)
    sin64 = jnp.concatenate([-jnp.sin(ar), jnp.sin(ar), -jnp.sin(ac), jnp.sin(ac)], axis=1)
    return jnp.tile(cos64, (1, 2)), jnp.tile(sin64, (1, 2))


def kernel(x_prompt, x_sample, cache_diff_k, cache_diff_v, cache_na_k, cache_na_v, cache_swa_k, cache_swa_v, c, c_ctx, mod_w, mod_b, norm_mix_pre, norm_mix_post, norm_mlp_pre, norm_mlp_post, w_in_even, conv_w, lambda_q1, lambda_k1, lambda_q2, lambda_k2, subln, w_in_odd, rpb, sink, w_out, mlp_w1, mlp_w2):
    cond = jnp.concatenate([c, c_ctx[None, :], jnp.zeros((MOD_ROWS - DEC_BATCH - 1, D_MODEL), F32)], axis=0)
    mod, *w_in = _modulation(cond, mod_w, mod_b, w_in_even, w_in_odd)
    rope = _rope_tables()

    t_minor = lambda a: jnp.swapaxes(a, -1, -2)

    xp = x_prompt.reshape(CTX_TOKENS, D_MODEL)
    xs = x_sample.reshape(LAT_TOKENS, D_MODEL)
    lam_init0 = 0.8 - 0.6 * math.exp(-0.3 * 0)
    lam_args = (conv_w, lambda_q1, lambda_k1, lambda_q2, lambda_k2, subln)
    sink_flat = sink.reshape(-1)

    def post(li, mix_ctx, mix_lat, x_parts, split_out):
        return _post_mixer(mix_ctx, mix_lat, x_parts, mod, norm_mix_post, norm_mlp_pre, norm_mlp_post, w_out, mlp_w1,
                           mlp_w2, li, split_out=split_out)

    proj, new_diff_kt, new_diff_v = _pre_mixer(xp, mod, norm_mix_pre, w_in[0], 0, latent=False, even=True, rope=None,
                                               n_tok=CTX_TOKENS)
    mix_ctx = _even_mixer(proj, *lam_args, new_diff_kt, new_diff_v, seq=SEQ, n_seq=CTX_SEQS_PER_STEP_EVEN,
                          own_keys=False, lam_init=lam_init0)
    proj = _pre_mixer(xs, mod, norm_mix_pre, w_in[0], 0, latent=True, even=True, rope=rope, n_tok=LAT_TOKENS)
    mix_lat = _even_mixer(proj, *lam_args, t_minor(cache_diff_k), cache_diff_v, seq=DEC_SEQ, n_seq=1, own_keys=True,
                          lam_init=lam_init0)
    (x_all,) = post(0, mix_ctx, [mix_lat], [xp, xs], False)

    q, new_na_kt, new_na_vt, new_swa_kt, new_swa_vt = _pre_mixer(
        x_all, mod, norm_mix_pre, w_in[1], 1, latent=False, even=False, rope=None, n_tok=CTX_TOKENS)
    mix_ctx = _odd_ctx_mixer(q, new_na_kt, new_na_vt, new_swa_kt, new_swa_vt, sink_flat)
    proj = _pre_mixer(x_all, mod, norm_mix_pre, w_in[1], 1, latent=True, even=False, rope=rope, n_tok=LAT_TOKENS,
                      tile0=CTX_TOKENS // TM_PRE)
    mix_c = _na_mixer(proj, t_minor(cache_na_k), t_minor(cache_na_v), rpb.reshape(-1))
    mix_d = _swa_mixer(proj, t_minor(cache_swa_k), t_minor(cache_swa_v), sink_flat)
    xp, xs = post(1, mix_ctx, [mix_c, mix_d], [x_all], True)

    return (xp.reshape(BATCH, SEQ, D_MODEL), xs.reshape(DEC_BATCH, DEC_SEQ, D_MODEL),
            t_minor(new_diff_kt), new_diff_v, t_minor(new_na_kt), t_minor(new_na_vt),
            t_minor(new_swa_kt), t_minor(new_swa_vt))
```
